```python
import jax, jax.numpy as jnp
from jax import lax
import numpy as np

D_MODEL = 1024
BATCH = 8
SEQ = 4096
DEPTH = 1

D_MIX = D_MODEL
HEAD_DIM = 64
NSA_HEADS = 8
NSA_KV_HEADS = 2
NSA_GROUP = NSA_HEADS // NSA_KV_HEADS
D_NSA = NSA_HEADS * HEAD_DIM
D_KV = NSA_KV_HEADS * HEAD_DIM
D_CONV = D_MIX - D_NSA
CONV_GROUPS = D_CONV // HEAD_DIM
CONV_WIDTH = 3
N_BRANCH = 3
ROPE_THETA = 500000.0
ROT_DIM = HEAD_DIM // 4
CMP_LEN = 32
CMP_STRIDE = 16
CMP_HIDDEN = 2 * HEAD_DIM
SEL_BLOCK = 64
SEL_TOPK = 16
WINDOW = 512
Q_BLOCK = 64
EPS = 1e-6
NEG_INF = -1e30
FORCE_SCORE = 1e9
SPLIT_SIZES = (D_NSA, 6 * D_KV, N_BRANCH * NSA_HEADS, D_NSA, D_CONV, D_CONV, D_CONV, D_CONV)
D_IN = sum(SPLIT_SIZES)
SPLIT_POINTS = tuple(int(v) for v in np.cumsum(SPLIT_SIZES)[:-1])

kernel_name = "hybrid_nsa_shortconv_layer"


def rms_norm(x, w):
    xf = x.astype(jnp.float32)
    xf = xf * lax.rsqrt(jnp.mean(xf * xf, axis=-1, keepdims=True) + EPS)
    return (xf * w.astype(jnp.float32)).astype(x.dtype)


def apply_rope(x, pos):
    inv_freq = ROPE_THETA ** (-jnp.arange(0, ROT_DIM, 2, dtype=jnp.float32) / ROT_DIM)
    ang = pos.astype(jnp.float32)[:, None] * inv_freq[None, :]
    cos = jnp.cos(ang)[:, None, :]
    sin = jnp.sin(ang)[:, None, :]
    xr = x[..., :ROT_DIM].astype(jnp.float32)
    x1, x2 = xr[..., :ROT_DIM // 2], xr[..., ROT_DIM // 2:]
    rot = jnp.concatenate([x1 * cos - x2 * sin, x2 * cos + x1 * sin], axis=-1)
    return jnp.concatenate([rot.astype(x.dtype), x[..., ROT_DIM:]], axis=-1)


def masked_softmax(s, mask):
    p = jax.nn.softmax(jnp.where(mask, s, NEG_INF), axis=-1)
    return jnp.where(mask, p, 0.0)


def compress_blocks(kv, pos_emb, w1, b1, w2):
    B_, S_, G_, dh = kv.shape
    halves = kv.reshape(B_, S_ // CMP_STRIDE, CMP_STRIDE, G_, dh)
    blocks = jnp.concatenate([halves[:, :-1], halves[:, 1:]], axis=2)
    blocks = blocks + pos_emb[None, None, :, None, :]
    nc = blocks.shape[1]
    flat = jnp.moveaxis(blocks, 3, 2).reshape(B_, nc, G_, CMP_LEN * dh)
    hid = jax.nn.silu(jnp.einsum('bcgf,fh->bcgh', flat, w1) + b1)
    return jnp.einsum('bcgh,hd->bcgd', hid, w2)


def cmp_sel_overlap(nc, ns):
    cs = jnp.arange(nc) * CMP_STRIDE
    ce = cs + CMP_LEN
    ss = jnp.arange(ns) * SEL_BLOCK
    se = ss + SEL_BLOCK
    ov = jnp.clip(jnp.minimum(ce[:, None], se[None, :]) - jnp.maximum(cs[:, None], ss[None, :]), 0)
    return ov.astype(jnp.float32) / CMP_LEN


def hybrid_layer(x, norm_w, w_in, q_norm_w, k_norm_w, cmp_k_pos, cmp_k_w1, cmp_k_b1, cmp_k_w2,
                 cmp_v_pos, cmp_v_w1, cmp_v_b1, cmp_v_w2, conv_w, conv_b, w_out):
    B_, S_, _ = x.shape
    G, R, dh = NSA_KV_HEADS, NSA_GROUP, HEAD_DIM
    nc = S_ // CMP_STRIDE - 1
    ns = S_ // SEL_BLOCK
    n_sel = min(SEL_TOPK, ns)
    nqb = S_ // Q_BLOCK

    h = rms_norm(x, norm_w)
    proj = jnp.einsum('bsd,de->bse', h, w_in)
    q, kv_all, gate_logits, nsa_z, cv_h, cv_b, cv_c, cv_z = jnp.split(proj, SPLIT_POINTS, axis=-1)

    pos = jnp.arange(S_)
    q = apply_rope(rms_norm(q.reshape(B_, S_, NSA_HEADS, dh), q_norm_w), pos)
    k_c, v_c, k_s, v_s, k_w, v_w = [t.reshape(B_, S_, G, dh) for t in jnp.split(kv_all, 6, axis=-1)]
    cmp_end = jnp.arange(nc) * CMP_STRIDE + (CMP_LEN - 1)
    k_c = apply_rope(rms_norm(compress_blocks(k_c, cmp_k_pos, cmp_k_w1, cmp_k_b1, cmp_k_w2),
                              k_norm_w[0]), cmp_end)
    v_c = compress_blocks(v_c, cmp_v_pos, cmp_v_w1, cmp_v_b1, cmp_v_w2)
    k_s = apply_rope(rms_norm(k_s, k_norm_w[1]), pos)
    k_w = apply_rope(rms_norm(k_w, k_norm_w[2]), pos)
    k_sel_blk = k_s.reshape(B_, ns, SEL_BLOCK, G, dh).transpose(0, 3, 1, 2, 4)
    v_sel_blk = v_s.reshape(B_, ns, SEL_BLOCK, G, dh).transpose(0, 3, 1, 2, 4)
    pad = ((0, 0), (WINDOW, 0), (0, 0), (0, 0))
    k_w_pad = jnp.pad(k_w, pad)
    v_w_pad = jnp.pad(v_w, pad)
    overlap = cmp_sel_overlap(nc, ns)
    scale = HEAD_DIM ** -0.5
    b_ix = jnp.arange(B_)[:, None, None, None]
    g_ix = jnp.arange(G)[None, :, None, None]
    blk = jnp.arange(ns)
    tok_in_blk = jnp.arange(SEL_BLOCK)
    win_off = jnp.arange(Q_BLOCK + WINDOW)

    def query_block(args):
        qb, i = args
        t = i * Q_BLOCK + jnp.arange(Q_BLOCK)
        s_c = jnp.einsum('bgqrd,bcgd->bgqrc', qb, k_c).astype(jnp.float32) * scale
        p_c = masked_softmax(s_c, (cmp_end[None, :] <= t[:, None])[None, None, :, None, :])
        o_cmp = jnp.einsum('bgqrc,bcgd->bgqrd', p_c.astype(v_c.dtype), v_c)
        imp = jnp.einsum('bgqrc,cn->bgqn', p_c, overlap)
        cur = t[:, None] // SEL_BLOCK
        forced = (blk[None, :] == 0) | (blk[None, :] == cur) | (blk[None, :] == cur - 1)
        visible = blk[None, :] * SEL_BLOCK <= t[:, None]
        imp = jnp.where(forced, FORCE_SCORE, jnp.where(visible, imp, -1.0))
        _, idx = lax.top_k(imp, n_sel)
        k_g = k_sel_blk[b_ix, g_ix, idx]
        v_g = v_sel_blk[b_ix, g_ix, idx]
        s_s = jnp.einsum('bgqrd,bgqnld->bgqrnl', qb, k_g).astype(jnp.float32) * scale
        tok = idx[..., None] * SEL_BLOCK + tok_in_blk
        mask_s = tok <= t[None, None, :, None, None]
        n_tok = n_sel * SEL_BLOCK
        p_s = masked_softmax(s_s.reshape(B_, G, Q_BLOCK, R, n_tok),
                             mask_s.reshape(B_, G, Q_BLOCK, 1, n_tok))
        o_slc = jnp.einsum('bgqrm,bgqmd->bgqrd', p_s.astype(v_g.dtype),
                           v_g.reshape(B_, G, Q_BLOCK, n_tok, dh))
        start = i * Q_BLOCK
        k_wb = lax.dynamic_slice_in_dim(k_w_pad, start, Q_BLOCK + WINDOW, axis=1)
        v_wb = lax.dynamic_slice_in_dim(v_w_pad, start, Q_BLOCK + WINDOW, axis=1)
        key_pos = start - WINDOW + win_off
        mask_w = ((key_pos[None, :] <= t[:, None]) & (key_pos[None, :] > t[:, None] - WINDOW)
                  & (key_pos[None, :] >= 0))
        s_w = jnp.einsum('bgqrd,bkgd->bgqrk', qb, k_wb).astype(jnp.float32) * scale
        p_w = masked_softmax(s_w, mask_w[None, None, :, None, :])
        o_win = jnp.einsum('bgqrk,bkgd->bgqrd', p_w.astype(v_wb.dtype), v_wb)
        return o_cmp, o_slc, o_win

    q_blocks = q.reshape(B_, nqb, Q_BLOCK, G, R, dh).transpose(1, 0, 3, 2, 4, 5)
    o_cmp, o_slc, o_win = lax.map(query_block, (q_blocks, jnp.arange(nqb)))

    def unblock(o):
        return o.transpose(1, 0, 3, 2, 4, 5).reshape(B_, S_, NSA_HEADS, dh)

    gates = jax.nn.sigmoid(gate_logits.astype(jnp.float32)).astype(x.dtype)
    gates = gates.reshape(B_, S_, NSA_HEADS, N_BRANCH)
    o_nsa = (gates[..., 0:1] * unblock(o_cmp) + gates[..., 1:2] * unblock(o_slc)
             + gates[..., 2:3] * unblock(o_win))
    o_nsa = o_nsa.reshape(B_, S_, D_NSA) * jax.nn.silu(nsa_z)

    u = cv_c * cv_h
    conv = lax.conv_general_dilated(u, conv_w[:, None, :], window_strides=(1,),
                                    padding=[(CONV_WIDTH - 1, 0)],
                                    dimension_numbers=('NWC', 'WIO', 'NWC'),
                                    feature_group_count=D_CONV) + conv_b
    o_conv = cv_b * conv * jax.nn.silu(cv_z)

    mix = jnp.concatenate([o_nsa, o_conv], axis=-1)
    return x + jnp.einsum('bse,ed->bsd', mix, w_out)


def setup_inputs(seed: int = 0) -> dict:
    key = jax.random.key(seed)
    ks = jax.random.split(key, 17)
    L = DEPTH

    def nrm(k, shape, scale):
        return jax.random.normal(k, shape, jnp.float32) * scale

    f_in = CMP_LEN * HEAD_DIM
    return {
        "x": nrm(ks[0], (BATCH, SEQ, D_MODEL), 1.0),
        "norm_w": 1.0 + nrm(ks[1], (L, D_MODEL), 0.01),
        "w_in": nrm(ks[2], (L, D_MODEL, D_IN), D_MODEL ** -0.5),
        "q_norm_w": 1.0 + nrm(ks[3], (L, HEAD_DIM), 0.01),
        "k_norm_w": 1.0 + nrm(ks[4], (L, N_BRANCH, HEAD_DIM), 0.01),
        "cmp_k_pos": nrm(ks[5], (L, CMP_LEN, HEAD_DIM), 0.02),
        "cmp_k_w1": nrm(ks[6], (L, f_in, CMP_HIDDEN), f_in ** -0.5),
        "cmp_k_b1": nrm(ks[7], (L, CMP_HIDDEN), 0.01),
        "cmp_k_w2": nrm(ks[8], (L, CMP_HIDDEN, HEAD_DIM), CMP_HIDDEN ** -0.5),
        "cmp_v_pos": nrm(ks[9], (L, CMP_LEN, HEAD_DIM), 0.02),
        "cmp_v_w1": nrm(ks[10], (L, f_in, CMP_HIDDEN), f_in ** -0.5),
        "cmp_v_b1": nrm(ks[11], (L, CMP_HIDDEN), 0.01),
        "cmp_v_w2": nrm(ks[12], (L, CMP_HIDDEN, HEAD_DIM), CMP_HIDDEN ** -0.5),
        "conv_w": nrm(ks[13], (L, CONV_WIDTH, D_CONV), CONV_WIDTH ** -0.5),
        "conv_b": nrm(ks[14], (L, D_CONV), 0.01),
        "w_out": nrm(ks[15], (L, D_MIX, D_MODEL), D_MIX ** -0.5),
    }


def reference(x, norm_w, w_in, q_norm_w, k_norm_w, cmp_k_pos, cmp_k_w1, cmp_k_b1, cmp_k_w2,
              cmp_v_pos, cmp_v_w1, cmp_v_b1, cmp_v_w2, conv_w, conv_b, w_out):
    for l in range(DEPTH):
        x = hybrid_layer(x, norm_w[l], w_in[l], q_norm_w[l], k_norm_w[l], cmp_k_pos[l],
                         cmp_k_w1[l], cmp_k_b1[l], cmp_k_w2[l], cmp_v_pos[l], cmp_v_w1[l],
                         cmp_v_b1[l], cmp_v_w2[l], conv_w[l], conv_b[l], w_out[l])
    return x
```

```python
import functools

import numpy as np
import jax
import jax.numpy as jnp
from jax import lax
from jax.experimental import pallas as pl
from jax.experimental.pallas import tpu as pltpu

LANES = 128
SUBLANES = 8
HEAD_DIM = 64
NSA_HEADS = 8
KV_GROUPS = 2
GROUP_HEADS = NSA_HEADS // KV_GROUPS
D_NSA = NSA_HEADS * HEAD_DIM
D_CONV = 512
N_BRANCH = 3
ROT_DIM = HEAD_DIM // 4
ROPE_THETA = 500000.0
CMP_LEN = 32
CMP_STRIDE = 16
CMP_HIDDEN = 2 * HEAD_DIM
SEL_BLOCK = 64
SEL_TOPK = 16
WINDOW = 512
EPS = 1e-6
NEG_INF = -1e30
FORCE_SCORE = 1e9
SCALE = HEAD_DIM ** -0.5

TM = 512
TQ = 128
TK = 512
VCHUNK = 128
VMEM_LIMIT = 56 * 1024 * 1024

BF16 = jnp.bfloat16
F32 = jnp.float32


def _dot(a, b):
    return jnp.dot(a, b, preferred_element_type=F32)


def _dot_nt(a, b):
    return lax.dot_general(a, b, (((1,), (1,)), ((), ())), preferred_element_type=F32)


def _split3(a):
    hi = a.astype(BF16)
    r1 = a - hi.astype(F32)
    mid = r1.astype(BF16)
    lo = (r1 - mid.astype(F32)).astype(BF16)
    return hi, mid, lo


def _group_mean(sq, mbd):
    hi = sq.astype(BF16)
    lo = (sq - hi.astype(F32)).astype(BF16)
    return _dot(hi, mbd) + _dot(lo, mbd)


def _rope(xn, c, sa, sb):
    return xn * c + pltpu.roll(xn, 8, 1) * sa + pltpu.roll(xn, LANES - 8, 1) * sb


def _silu(z):
    return z * (1.0 / (1.0 + jnp.exp(-z)))


def _proj_kernel(x_ref, nw_ref, w_ref, qnw_ref, knw_ref, rc_ref, rsa_ref, rsb_ref,
                 mbd_ref, cw_ref, cb_ref,
                 q_ref, kc_ref, vc_ref, ksf_ref, vst_ref, kw_ref, vwt_ref,
                 gate_ref, zs_ref, oconv_ref, ubuf):
    si = pl.program_id(1)
    x = x_ref[0]
    ms = jnp.mean(x * x, axis=-1, keepdims=True)
    h = (x * lax.rsqrt(ms + EPS) * nw_ref[...]).astype(BF16)

    rc, rsa, rsb = rc_ref[...], rsa_ref[...], rsb_ref[...]
    mbd = mbd_ref[...]

    pq = _dot(h, w_ref[:, 0:D_NSA])
    for pair in range(2):
        blk = pq[:, pair * 256:(pair + 1) * 256]
        msq = _group_mean(blk * blk, mbd)
        qn = blk * lax.rsqrt(msq + EPS) * qnw_ref[...]
        for half in range(2):
            t = qn[:, half * LANES:(half + 1) * LANES]
            q_ref[0, pair * 2 + half] = (_rope(t, rc, rsa, rsb) * SCALE).astype(BF16)

    pkv = _dot(h, w_ref[:, 512:1280])
    kc_ref[0] = pkv[:, 0:128]
    vc_ref[0] = pkv[:, 128:256]
    ksw = jnp.concatenate([pkv[:, 256:384], pkv[:, 512:640]], axis=1)
    msk = _group_mean(ksw * ksw, mbd)
    kn = ksw * lax.rsqrt(msk + EPS) * knw_ref[...]
    ks = _rope(kn[:, 0:128], rc, rsa, rsb)
    kw = _rope(kn[:, 128:256], rc, rsa, rsb)
    row = lax.broadcasted_iota(jnp.int32, (TM, LANES), 0) + si * TM
    lane = lax.broadcasted_iota(jnp.int32, (TM, LANES), 1)
    onehot = jnp.where(lane == (row >> 6), 1.0, 0.0)
    ksf_ref[0, :, 0:128] = ks.astype(BF16)
    ksf_ref[0, :, 128:256] = onehot.astype(BF16)
    kw_ref[0] = kw.astype(BF16)
    vs = pkv[:, 384:512]
    vw = pkv[:, 640:768]
    for j in range(TM // VCHUNK):
        vst_ref[0, j] = vs[j * VCHUNK:(j + 1) * VCHUNK, :].T.astype(BF16)
        vwt_ref[0, j] = vw[j * VCHUNK:(j + 1) * VCHUNK, :].T.astype(BF16)

    pgz = _dot(h, w_ref[:, 1280:1920])
    gate_ref[0] = 1.0 / (1.0 + jnp.exp(-pgz[:, 0:128]))
    zs_ref[0] = _silu(pgz[:, 128:640]).astype(BF16)

    pcv = _dot(h, w_ref[:, 1920:3968])
    u = pcv[:, 1024:1536] * pcv[:, 0:512]

    @pl.when(si == 0)
    def _():
        ubuf[0:8, :] = jnp.zeros((8, D_CONV), F32)

    ubuf[8:8 + TM, :] = u
    u1 = ubuf[7:7 + TM, :]
    u2 = ubuf[6:6 + TM, :]
    conv = cw_ref[0:1, :] * u2 + cw_ref[1:2, :] * u1 + cw_ref[2:3, :] * u + cb_ref[...]
    oconv_ref[0] = (pcv[:, 512:1024] * conv * _silu(pcv[:, 1536:2048])).astype(BF16)
    ubuf[0:8, :] = ubuf[TM:TM + 8, :]


def _cmp_kernel(hk_ref, hv_ref, pka_ref, pkb_ref, pva_ref, pvb_ref,
                wk1a_ref, wk1b_ref, bk1_ref, wk2_ref,
                wv1a_ref, wv1b_ref, bv1_ref, wv2_ref,
                knw_ref, rc_ref, rsa_ref, rsb_ref, mbd_ref,
                kc_ref, vct_ref, *, ncp):
    def mlp(h_ref, pa_ref, pb_ref, w1a_ref, w1b_ref, b1_ref, w2_ref):
        hh = h_ref[0]
        p = _dot((hh + pa_ref[...]).astype(BF16), w1a_ref[...])
        q = _dot((hh + pb_ref[...]).astype(BF16), w1b_ref[...])
        pre = p + pltpu.roll(q, ncp - 1, 0) + b1_ref[...]
        return _dot(_silu(pre).astype(BF16), w2_ref[...])

    kc = mlp(hk_ref, pka_ref, pkb_ref, wk1a_ref, wk1b_ref, bk1_ref, wk2_ref)
    msk = _group_mean(kc * kc, mbd_ref[0:LANES, 0:LANES])
    kn = kc * lax.rsqrt(msk + EPS) * knw_ref[...]
    kc_ref[0] = _rope(kn, rc_ref[...], rsa_ref[...], rsb_ref[...]).astype(BF16)
    vc = mlp(hv_ref, pva_ref, pvb_ref, wv1a_ref, wv1b_ref, bv1_ref, wv2_ref)
    vct_ref[0] = vc.T.astype(BF16)


def _attn_kernel(q_ref, kc_ref, vct_ref, ksf_ref, vst_ref, kw_ref, vwt_ref,
                 gate_ref, zs_ref, ovt_ref, o_ref, imp_sc, *, ncp):
    i = pl.program_id(1)
    t0 = i * TQ
    rows = GROUP_HEADS * TQ
    nblk = SEL_BLOCK

    q_all = q_ref[0].reshape(rows, LANES)
    lane = lax.broadcasted_iota(jnp.int32, (rows, LANES), 1)
    colq = lax.broadcasted_iota(jnp.int32, (1, rows), 1)
    t_col = t0 + (colq & (TQ - 1))
    g_t = gate_ref[0].T

    n_idx = lax.broadcasted_iota(jnp.int32, (nblk, TQ), 0)
    tq = t0 + lax.broadcasted_iota(jnp.int32, (nblk, TQ), 1)
    cur = tq >> 6
    forced = (n_idx == 0) | (n_idx == cur) | (n_idx == cur - 1)
    visible = (n_idx << 6) <= tq

    slabs = [None] * NSA_HEADS
    for g in range(KV_GROUPS):
        qg = jnp.where((lane >= HEAD_DIM) if g == 1 else (lane < HEAD_DIM), q_all, 0)

        sc = _dot_nt(kc_ref[0], qg)
        c_idx = lax.broadcasted_iota(jnp.int32, (ncp, 1), 0)
        mask_c = ((c_idx * CMP_STRIDE + (CMP_LEN - 1)) <= t_col) & (c_idx < ncp - 1)
        sm = jnp.where(mask_c, sc, NEG_INF)
        m_c = jnp.max(sm, axis=0, keepdims=True)
        e_c = jnp.where(mask_c, jnp.exp(sm - m_c), 0.0)
        l_c = jnp.sum(e_c, axis=0, keepdims=True)
        p_c = e_c / jnp.where(l_c > 0.0, l_c, 1.0)
        o_c = _dot(vct_ref[0], p_c.astype(BF16))

        psum = p_c[:, 0:TQ]
        for r in range(1, GROUP_HEADS):
            psum = psum + p_c[:, r * TQ:(r + 1) * TQ]
        hi, mid, lo = _split3(psum)
        ovt = ovt_ref[...]
        imp = _dot(ovt, hi) + _dot(ovt, mid) + _dot(ovt, lo)
        imp = jnp.where(forced, FORCE_SCORE, jnp.where(visible, imp, -1.0))
        imp_sc[...] = imp
        nvb = nblk // SUBLANES
        imp_b = [imp[vb * SUBLANES:(vb + 1) * SUBLANES, :] for vb in range(nvb)]
        rank_b = [jnp.zeros((SUBLANES, TQ), F32) for _ in range(nvb)]
        sub = lax.broadcasted_iota(jnp.int32, (SUBLANES, TQ), 0)
        for mm in range(nblk):
            a = jnp.broadcast_to(imp_sc[mm:mm + 1, :], (SUBLANES, TQ))
            for vb in range(nvb):
                if vb * SUBLANES > mm:
                    beats = a >= imp_b[vb]
                elif vb * SUBLANES + SUBLANES - 1 < mm:
                    beats = a > imp_b[vb]
                else:
                    beats = (a > imp_b[vb]) | ((a == imp_b[vb]) & (sub + vb * SUBLANES > mm))
                rank_b[vb] = rank_b[vb] + jnp.where(beats, 1.0, 0.0)
        rank = jnp.concatenate(rank_b, axis=0)
        pen_t = jnp.where(rank < float(SEL_TOPK), 0.0, NEG_INF)
        pen_t = jnp.concatenate([pen_t, jnp.zeros((LANES - nblk, TQ), F32)], axis=0)
        pen = pen_t.T.astype(BF16)
        q_aug = jnp.concatenate([qg, jnp.concatenate([pen] * GROUP_HEADS, axis=0)], axis=1)

        def sel_body(c, carry):
            m_p, l_p, acc = carry
            k0 = pl.multiple_of(c * TK, TK)
            s = _dot_nt(ksf_ref[0, pl.ds(k0, TK), :], q_aug)
            kpos = k0 + lax.broadcasted_iota(jnp.int32, (TK, 1), 0)
            s = jnp.where(kpos <= t_col, s, NEG_INF)
            m_n = jnp.maximum(m_p, jnp.max(s, axis=0, keepdims=True))
            alpha = jnp.exp(m_p - m_n)
            p = jnp.exp(s - m_n)
            l_n = alpha * l_p + jnp.sum(p, axis=0, keepdims=True)
            cv = c * (TK // VCHUNK)
            v = jnp.concatenate([vst_ref[0, cv + j] for j in range(TK // VCHUNK)], axis=1)
            acc = alpha * acc + _dot(v, p.astype(BF16))
            return m_n, l_n, acc

        n_chunks = ((i + 1) * TQ + TK - 1) // TK
        init = (jnp.full((1, rows), NEG_INF, F32), jnp.zeros((1, rows), F32),
                jnp.zeros((LANES, rows), F32))
        _, l_s, o_s = lax.fori_loop(0, n_chunks, sel_body, init)

        nwc = WINDOW // VCHUNK + TQ // VCHUNK
        cw = jnp.maximum(i * (TQ // VCHUNK) - WINDOW // VCHUNK, 0)
        w0 = pl.multiple_of(cw * VCHUNK, VCHUNK)
        sw = _dot_nt(kw_ref[0, pl.ds(w0, nwc * VCHUNK), :], qg)
        kpos = w0 + lax.broadcasted_iota(jnp.int32, (nwc * VCHUNK, 1), 0)
        mask_w = (kpos <= t_col) & (kpos > t_col - WINDOW)
        sw = jnp.where(mask_w, sw, NEG_INF)
        m_w = jnp.max(sw, axis=0, keepdims=True)
        p_w = jnp.exp(sw - m_w)
        l_w = jnp.sum(p_w, axis=0, keepdims=True)
        vwin = jnp.concatenate([vwt_ref[0, cw + j] for j in range(nwc)], axis=1)
        o_w = _dot(vwin, p_w.astype(BF16))

        def gate_row(br):
            base = g * (N_BRANCH * GROUP_HEADS) + br * GROUP_HEADS
            return jnp.concatenate([g_t[base + r:base + r + 1, :] for r in range(GROUP_HEADS)], axis=1)

        og = o_c * gate_row(0) + o_s * (gate_row(1) / l_s) + o_w * (gate_row(2) / l_w)
        og = og[g * HEAD_DIM:(g + 1) * HEAD_DIM, :]
        for r in range(GROUP_HEADS):
            slabs[g * GROUP_HEADS + r] = og[:, r * TQ:(r + 1) * TQ]

    out_t = jnp.concatenate(slabs, axis=0)
    o_ref[0] = (out_t.T * zs_ref[0].astype(F32)).astype(BF16)


def _out_kernel(x_ref, on_ref, oc_ref, w_ref, o_ref):
    acc = _dot(on_ref[0], w_ref[0:D_NSA, :]) + _dot(oc_ref[0], w_ref[D_NSA:D_NSA + D_CONV, :])
    o_ref[0] = x_ref[0] + acc


def _rope_tables(pos):
    inv_freq = ROPE_THETA ** (-jnp.arange(0, ROT_DIM, 2, dtype=F32) / ROT_DIM)
    ang = pos.astype(F32)[:, None] * inv_freq[None, :]
    cos, sin = jnp.cos(ang), jnp.sin(ang)
    n = pos.shape[0]
    ones = jnp.ones((n, HEAD_DIM - ROT_DIM), F32)
    zeros = jnp.zeros((n, HEAD_DIM - ROT_DIM), F32)
    z8 = jnp.zeros((n, ROT_DIM // 2), F32)
    c = jnp.concatenate([cos, cos, ones], axis=1)
    sa = jnp.concatenate([z8, sin, zeros], axis=1)
    sb = jnp.concatenate([-sin, z8, zeros], axis=1)
    tile2 = lambda t: jnp.concatenate([t, t], axis=1)
    return tile2(c), tile2(sa), tile2(sb)


def _overlap_t(ncp):
    cs = np.arange(ncp) * CMP_STRIDE
    ce = cs + CMP_LEN
    ss = np.arange(SEL_BLOCK) * SEL_BLOCK
    se = ss + SEL_BLOCK
    ov = np.clip(np.minimum(ce[None, :], se[:, None]) - np.maximum(cs[None, :], ss[:, None]), 0, None)
    ov = ov.astype(np.float32) / CMP_LEN
    ov[:, ncp - 1] = 0.0
    return jnp.asarray(ov, BF16)


def _cmp_weights(pos, w1, b1, w2):
    half = CMP_STRIDE
    def big_w1(w):
        w = w.reshape(half, 1, HEAD_DIM, 1, CMP_HIDDEN)
        eye = jnp.eye(KV_GROUPS, dtype=F32).reshape(1, KV_GROUPS, 1, KV_GROUPS, 1)
        return (w * eye).reshape(half * KV_GROUPS * HEAD_DIM, KV_GROUPS * CMP_HIDDEN).astype(BF16)
    def big_pos(p):
        return jnp.broadcast_to(p[:, None, :], (half, KV_GROUPS, HEAD_DIM)).reshape(1, -1)
    w1a, w1b = w1[:half * HEAD_DIM], w1[half * HEAD_DIM:]
    eye2 = jnp.eye(KV_GROUPS, dtype=F32)
    w2b = (w2[None, :, None, :] * eye2[:, None, :, None]).reshape(KV_GROUPS * CMP_HIDDEN,
                                                                 KV_GROUPS * HEAD_DIM).astype(BF16)
    b1b = jnp.tile(b1, KV_GROUPS)[None, :]
    return big_pos(pos[:half]), big_pos(pos[half:]), big_w1(w1a), big_w1(w1b), b1b, w2b


def _full(shape):
    nd = len(shape)
    return pl.BlockSpec(shape, lambda *_: (0,) * nd)


def kernel(x, norm_w, w_in, q_norm_w, k_norm_w, cmp_k_pos, cmp_k_w1, cmp_k_b1, cmp_k_w2,
           cmp_v_pos, cmp_v_w1, cmp_v_b1, cmp_v_w2, conv_w, conv_b, w_out):
    B, S, D = x.shape
    assert norm_w.shape[0] == 1, "single layer"
    assert S % TM == 0 and S // SEL_BLOCK <= SEL_BLOCK and S >= WINDOW + TQ
    ncp = S // CMP_STRIDE
    nst = S // TM
    f32 = lambda a: a.astype(F32)

    w = f32(w_in[0])
    head_order = []
    for j in range(GROUP_HEADS):
        head_order += [j, GROUP_HEADS + j]
    wq = w[:, :D_NSA].reshape(D, NSA_HEADS, HEAD_DIM)[:, jnp.array(head_order), :].reshape(D, D_NSA)
    gate_cols = []
    for g in range(KV_GROUPS):
        for br in range(N_BRANCH):
            for r in range(GROUP_HEADS):
                gate_cols.append((g * GROUP_HEADS + r) * N_BRANCH + br)
    g0 = D_NSA + 6 * LANES
    wg = w[:, g0:g0 + NSA_HEADS * N_BRANCH][:, jnp.array(gate_cols)]
    wg = jnp.pad(wg, ((0, 0), (0, LANES - NSA_HEADS * N_BRANCH)))
    rest0 = g0 + NSA_HEADS * N_BRANCH
    w_cat = jnp.concatenate([wq, w[:, D_NSA:g0], wg, w[:, rest0:]], axis=1).astype(BF16)
    n_cat = w_cat.shape[1]

    rc, rsa, rsb = _rope_tables(jnp.arange(S))
    mbd = jnp.asarray(np.kron(np.eye(4), np.full((HEAD_DIM, HEAD_DIM), 1.0 / HEAD_DIM)), BF16)
    qnw = jnp.tile(f32(q_norm_w[0]), 4)[None, :]
    knw_sw = jnp.concatenate([jnp.tile(f32(k_norm_w[0, 1]), 2), jnp.tile(f32(k_norm_w[0, 2]), 2)])[None, :]
    knw_c = jnp.tile(f32(k_norm_w[0, 0]), 2)[None, :]

    cp = pltpu.CompilerParams(dimension_semantics=("arbitrary", "arbitrary"),
                              vmem_limit_bytes=VMEM_LIMIT)
    row_blk = lambda n: pl.BlockSpec((1, TM, n), lambda b, s: (b, s, 0))
    tab_blk = pl.BlockSpec((TM, LANES), lambda b, s: (s, 0))
    vt_blk = pl.BlockSpec((1, TM // VCHUNK, LANES, VCHUNK), lambda b, s: (b, s, 0, 0))
    sds = jax.ShapeDtypeStruct

    (q4, kc_raw, vc_raw, ksf, vst, kw, vwt, gates, zs, oconv) = pl.pallas_call(
        _proj_kernel,
        grid=(B, nst),
        in_specs=[row_blk(D), _full((1, D)), _full((D, n_cat)), _full((1, 256)), _full((1, 256)),
                  tab_blk, tab_blk, tab_blk, _full((256, 256)), _full((3, D_CONV)), _full((1, D_CONV))],
        out_specs=[pl.BlockSpec((1, 4, TM, LANES), lambda b, s: (b, 0, s, 0)),
                   row_blk(LANES), row_blk(LANES), row_blk(2 * LANES), vt_blk, row_blk(LANES), vt_blk,
                   row_blk(LANES), row_blk(D_NSA), row_blk(D_CONV)],
        out_shape=[sds((B, 4, S, LANES), BF16), sds((B, S, LANES), F32), sds((B, S, LANES), F32),
                   sds((B, S, 2 * LANES), BF16), sds((B, S // VCHUNK, LANES, VCHUNK), BF16),
                   sds((B, S, LANES), BF16), sds((B, S // VCHUNK, LANES, VCHUNK), BF16),
                   sds((B, S, LANES), F32), sds((B, S, D_NSA), BF16), sds((B, S, D_CONV), BF16)],
        scratch_shapes=[pltpu.VMEM((TM + 16, D_CONV), F32)],
        compiler_params=cp,
        name="nsa_proj",
    )(x, f32(norm_w), w_cat, qnw, knw_sw, rc, rsa, rsb, mbd, f32(conv_w[0]), f32(conv_b))

    cmp_pos = jnp.arange(ncp) * CMP_STRIDE + (CMP_LEN - 1)
    crc, crsa, crsb = _rope_tables(cmp_pos)
    kpa, kpb, kw1a, kw1b, kb1, kw2 = _cmp_weights(f32(cmp_k_pos[0]), f32(cmp_k_w1[0]),
                                                  f32(cmp_k_b1[0]), f32(cmp_k_w2[0]))
    vpa, vpb, vw1a, vw1b, vb1, vw2 = _cmp_weights(f32(cmp_v_pos[0]), f32(cmp_v_w1[0]),
                                                  f32(cmp_v_b1[0]), f32(cmp_v_w2[0]))
    flat = CMP_STRIDE * LANES
    h_blk = pl.BlockSpec((1, ncp, flat), lambda b: (b, 0, 0))
    kc, vct = pl.pallas_call(
        functools.partial(_cmp_kernel, ncp=ncp),
        grid=(B,),
        in_specs=[h_blk, h_blk] + [_full((1, flat))] * 4
                 + [_full((flat, 256)), _full((flat, 256)), _full((1, 256)), _full((256, LANES))] * 2
                 + [_full((1, LANES)), _full((ncp, LANES)), _full((ncp, LANES)), _full((ncp, LANES)),
                    _full((256, 256))],
        out_specs=[pl.BlockSpec((1, ncp, LANES), lambda b: (b, 0, 0)),
                   pl.BlockSpec((1, LANES, ncp), lambda b: (b, 0, 0))],
        out_shape=[sds((B, ncp, LANES), BF16), sds((B, LANES, ncp), BF16)],
        compiler_params=pltpu.CompilerParams(dimension_semantics=("arbitrary",),
                                             vmem_limit_bytes=VMEM_LIMIT),
        name="nsa_compress",
    )(kc_raw.reshape(B, ncp, flat), vc_raw.reshape(B, ncp, flat), kpa, kpb, vpa, vpb,
      kw1a, kw1b, kb1, kw2, vw1a, vw1b, vb1, vw2, knw_c, crc, crsa, crsb, mbd)

    nqt = S // TQ
    o_nsa = pl.pallas_call(
        functools.partial(_attn_kernel, ncp=ncp),
        grid=(B, nqt),
        in_specs=[pl.BlockSpec((1, 4, TQ, LANES), lambda b, i: (b, 0, i, 0)),
                  pl.BlockSpec((1, ncp, LANES), lambda b, i: (b, 0, 0)),
                  pl.BlockSpec((1, LANES, ncp), lambda b, i: (b, 0, 0)),
                  pl.BlockSpec((1, S, 2 * LANES), lambda b, i: (b, 0, 0)),
                  pl.BlockSpec((1, S // VCHUNK, LANES, VCHUNK), lambda b, i: (b, 0, 0, 0)),
                  pl.BlockSpec((1, S, LANES), lambda b, i: (b, 0, 0)),
                  pl.BlockSpec((1, S // VCHUNK, LANES, VCHUNK), lambda b, i: (b, 0, 0, 0)),
                  pl.BlockSpec((1, TQ, LANES), lambda b, i: (b, i, 0)),
                  pl.BlockSpec((1, TQ, D_NSA), lambda b, i: (b, i, 0)),
                  _full((SEL_BLOCK, ncp))],
        out_specs=pl.BlockSpec((1, TQ, D_NSA), lambda b, i: (b, i, 0)),
        out_shape=sds((B, S, D_NSA), BF16),
        scratch_shapes=[pltpu.VMEM((SEL_BLOCK, TQ), F32)],
        compiler_params=cp,
        name="nsa_attn",
    )(q4, kc, vct, ksf, vst, kw, vwt, gates, zs, _overlap_t(ncp))

    out = pl.pallas_call(
        _out_kernel,
        grid=(B, nst),
        in_specs=[row_blk(D), row_blk(D_NSA), row_blk(D_CONV), _full((D_NSA + D_CONV, D))],
        out_specs=row_blk(D),
        out_shape=sds((B, S, D), x.dtype),
        compiler_params=cp,
        name="nsa_out",
    )(x, o_nsa, oconv, f32(w_out[0]).astype(BF16))
    return out
```

```python
import functools

import numpy as np
import jax
import jax.numpy as jnp
from jax import lax
from jax.experimental import pallas as pl
from jax.experimental.pallas import tpu as pltpu

LANES = 128
SUBLANES = 8
HEAD_DIM = 64
NSA_HEADS = 8
KV_GROUPS = 2
GROUP_HEADS = NSA_HEADS // KV_GROUPS
D_NSA = NSA_HEADS * HEAD_DIM
D_CONV = 512
N_BRANCH = 3
ROT_DIM = HEAD_DIM // 4
ROPE_THETA = 500000.0
CMP_LEN = 32
CMP_STRIDE = 16
CMP_HIDDEN = 2 * HEAD_DIM
SEL_BLOCK = 64
SEL_TOPK = 16
WINDOW = 512
EPS = 1e-6
NEG_INF = -1e30
FORCE_SCORE = 1e9
SCALE = HEAD_DIM ** -0.5
Q_SCALE = SCALE * float(np.log2(np.e))

TM = 512
TQ = 128
TK = 512
VCHUNK = 128
VMEM_LIMIT = 56 * 1024 * 1024

BF16 = jnp.bfloat16
F32 = jnp.float32


def _dot(a, b):
    return jnp.dot(a, b, preferred_element_type=F32)


def _dot_nt(a, b):
    return lax.dot_general(a, b, (((1,), (1,)), ((), ())), preferred_element_type=F32)


def _split3(a):
    hi = a.astype(BF16)
    r1 = a - hi.astype(F32)
    mid = r1.astype(BF16)
    lo = (r1 - mid.astype(F32)).astype(BF16)
    return hi, mid, lo


def _group_mean(sq, mbd):
    hi = sq.astype(BF16)
    lo = (sq - hi.astype(F32)).astype(BF16)
    return _dot(hi, mbd) + _dot(lo, mbd)


def _rope(xn, c, sa, sb):
    return xn * c + pltpu.roll(xn, 8, 1) * sa + pltpu.roll(xn, LANES - 8, 1) * sb


def _silu(z):
    return z * (1.0 / (1.0 + jnp.exp(-z)))


def _proj_kernel(x_ref, nw_ref, w_ref, qnw_ref, knw_ref, rc_ref, rsa_ref, rsb_ref,
                 mbd_ref, cw_ref, cb_ref,
                 q_ref, kc_ref, vc_ref, ksf_ref, vst_ref, kw_ref, vwt_ref,
                 gate_ref, zs_ref, oconv_ref, ubuf):
    si = pl.program_id(1)
    x = x_ref[0]
    ms = jnp.mean(x * x, axis=-1, keepdims=True)
    h = (x * lax.rsqrt(ms + EPS) * nw_ref[...]).astype(BF16)

    rc, rsa, rsb = rc_ref[...], rsa_ref[...], rsb_ref[...]
    mbd = mbd_ref[...]

    pq = _dot(h, w_ref[:, 0:D_NSA])
    for pair in range(2):
        blk = pq[:, pair * 256:(pair + 1) * 256]
        msq = _group_mean(blk * blk, mbd)
        qn = blk * lax.rsqrt(msq + EPS) * qnw_ref[...]
        for half in range(2):
            t = qn[:, half * LANES:(half + 1) * LANES]
            q_ref[0, pair * 2 + half] = (_rope(t, rc, rsa, rsb) * Q_SCALE).astype(BF16)

    pkv = _dot(h, w_ref[:, 512:1280])
    kc_ref[0] = pkv[:, 0:128]
    vc_ref[0] = pkv[:, 128:256]
    ksw = jnp.concatenate([pkv[:, 256:384], pkv[:, 512:640]], axis=1)
    msk = _group_mean(ksw * ksw, mbd)
    kn = ksw * lax.rsqrt(msk + EPS) * knw_ref[...]
    ks = _rope(kn[:, 0:128], rc, rsa, rsb)
    kw = _rope(kn[:, 128:256], rc, rsa, rsb)
    row = lax.broadcasted_iota(jnp.int32, (TM, LANES), 0) + si * TM
    lane = lax.broadcasted_iota(jnp.int32, (TM, LANES), 1)
    onehot = jnp.where(lane == (row >> 6), 1.0, 0.0)
    ksf_ref[0, :, 0:128] = ks.astype(BF16)
    ksf_ref[0, :, 128:256] = onehot.astype(BF16)
    kw_ref[0] = kw.astype(BF16)
    vs = pkv[:, 384:512]
    vw = pkv[:, 640:768]
    for j in range(TM // VCHUNK):
        vst_ref[0, j] = vs[j * VCHUNK:(j + 1) * VCHUNK, :].T.astype(BF16)
        vwt_ref[0, j] = vw[j * VCHUNK:(j + 1) * VCHUNK, :].T.astype(BF16)

    pgz = _dot(h, w_ref[:, 1280:1920])
    gate_ref[0] = 1.0 / (1.0 + jnp.exp(-pgz[:, 0:128]))
    zs_ref[0] = _silu(pgz[:, 128:640]).astype(BF16)

    pcv = _dot(h, w_ref[:, 1920:3968])
    u = pcv[:, 1024:1536] * pcv[:, 0:512]

    @pl.when(si == 0)
    def _():
        ubuf[0:8, :] = jnp.zeros((8, D_CONV), F32)

    ubuf[8:8 + TM, :] = u
    u1 = ubuf[7:7 + TM, :]
    u2 = ubuf[6:6 + TM, :]
    conv = cw_ref[0:1, :] * u2 + cw_ref[1:2, :] * u1 + cw_ref[2:3, :] * u + cb_ref[...]
    oconv_ref[0] = (pcv[:, 512:1024] * conv * _silu(pcv[:, 1536:2048])).astype(BF16)
    ubuf[0:8, :] = ubuf[TM:TM + 8, :]


def _cmp_kernel(hk_ref, hv_ref, pka_ref, pkb_ref, pva_ref, pvb_ref,
                wk1a_ref, wk1b_ref, bk1_ref, wk2_ref,
                wv1a_ref, wv1b_ref, bv1_ref, wv2_ref,
                knw_ref, rc_ref, rsa_ref, rsb_ref, mbd_ref,
                kc_ref, vct_ref, *, ncp):
    def mlp(h_ref, pa_ref, pb_ref, w1a_ref, w1b_ref, b1_ref, w2_ref):
        hh = h_ref[0]
        p = _dot((hh + pa_ref[...]).astype(BF16), w1a_ref[...])
        q = _dot((hh + pb_ref[...]).astype(BF16), w1b_ref[...])
        pre = p + pltpu.roll(q, ncp - 1, 0) + b1_ref[...]
        return _dot(_silu(pre).astype(BF16), w2_ref[...])

    kc = mlp(hk_ref, pka_ref, pkb_ref, wk1a_ref, wk1b_ref, bk1_ref, wk2_ref)
    msk = _group_mean(kc * kc, mbd_ref[0:LANES, 0:LANES])
    kn = kc * lax.rsqrt(msk + EPS) * knw_ref[...]
    kc_ref[0] = _rope(kn, rc_ref[...], rsa_ref[...], rsb_ref[...]).astype(BF16)
    vc = mlp(hv_ref, pva_ref, pvb_ref, wv1a_ref, wv1b_ref, bv1_ref, wv2_ref)
    vct_ref[0] = vc.T.astype(BF16)


def _attn_kernel(q_ref, kc_ref, vct_ref, ksf_ref, vst_ref, kw_ref, vwt_ref,
                 gate_ref, zs_ref, ovt_ref, o_ref, imp_sc, pen_sc, *, ncp):
    i = pl.program_id(1)
    t0 = i * TQ
    rows = GROUP_HEADS * TQ
    nblk = SEL_BLOCK
    groups = range(KV_GROUPS)

    q_all = q_ref[0].reshape(rows, LANES)
    lane = lax.broadcasted_iota(jnp.int32, (rows, LANES), 1)
    qg = [jnp.where(lane < HEAD_DIM, q_all, 0), jnp.where(lane >= HEAD_DIM, q_all, 0)]
    colq = lax.broadcasted_iota(jnp.int32, (1, rows), 1)
    t_col = t0 + (colq & (TQ - 1))
    vrows = lambda v, g: v[g * HEAD_DIM:(g + 1) * HEAD_DIM, :]

    c_idx = lax.broadcasted_iota(jnp.int32, (ncp, 1), 0)
    mask_c = ((c_idx * CMP_STRIDE + (CMP_LEN - 1)) <= t_col) & (c_idx < ncp - 1)
    p_c, o_c = [], []
    for g in groups:
        sm = jnp.where(mask_c, _dot_nt(kc_ref[0], qg[g]), NEG_INF)
        m_c = jnp.max(sm, axis=0, keepdims=True)
        e_c = jnp.where(mask_c, jnp.exp2(sm - m_c), 0.0)
        l_c = jnp.sum(e_c, axis=0, keepdims=True)
        p = e_c / jnp.where(l_c > 0.0, l_c, 1.0)
        p_c.append(p)
        o_c.append(_dot(vrows(vct_ref[0], g), p.astype(BF16)))

    @pl.when(t0 + TQ <= SEL_TOPK * SEL_BLOCK)
    def _():
        pen_sc[...] = jnp.zeros(pen_sc.shape, BF16)

    @pl.when(t0 + TQ > SEL_TOPK * SEL_BLOCK)
    def _():
        n_idx = lax.broadcasted_iota(jnp.int32, (nblk, TQ), 0)
        tq = t0 + lax.broadcasted_iota(jnp.int32, (nblk, TQ), 1)
        cur = tq >> 6
        forced = (n_idx == 0) | (n_idx == cur) | (n_idx == cur - 1)
        visible = (n_idx << 6) <= tq
        sub = lax.broadcasted_iota(jnp.int32, (SUBLANES, TQ), 0)
        nvb = nblk // SUBLANES
        ovt = ovt_ref[...]
        for g in groups:
            psum = p_c[g][:, 0:TQ]
            for r in range(1, GROUP_HEADS):
                psum = psum + p_c[g][:, r * TQ:(r + 1) * TQ]
            hi, mid, lo = _split3(psum)
            imp = _dot(ovt, hi) + _dot(ovt, mid) + _dot(ovt, lo)
            imp = jnp.where(forced, FORCE_SCORE, jnp.where(visible, imp, -1.0))
            imp_sc[g] = imp
            imp_b = [imp[vb * SUBLANES:(vb + 1) * SUBLANES, :] for vb in range(nvb)]
            rank_b = [jnp.zeros((SUBLANES, TQ), F32) for _ in range(nvb)]
            for mm in range(nblk):
                a = jnp.broadcast_to(imp_sc[g, mm:mm + 1, :], (SUBLANES, TQ))
                for vb in range(nvb):
                    if vb * SUBLANES > mm:
                        beats = a >= imp_b[vb]
                    elif vb * SUBLANES + SUBLANES - 1 < mm:
                        beats = a > imp_b[vb]
                    else:
                        beats = (a > imp_b[vb]) | ((a == imp_b[vb]) & (sub + vb * SUBLANES > mm))
                    rank_b[vb] = rank_b[vb] + jnp.where(beats, 1.0, 0.0)
            rank = jnp.concatenate(rank_b, axis=0)
            pen_t = jnp.where(rank < float(SEL_TOPK), 0.0, NEG_INF)
            pen_t = jnp.concatenate([pen_t, jnp.zeros((LANES - nblk, TQ), F32)], axis=0)
            pen_sc[g] = pen_t.T.astype(BF16)

    q_aug = [jnp.concatenate([qg[g], jnp.concatenate([pen_sc[g]] * GROUP_HEADS, axis=0)], axis=1)
             for g in groups]

    def sel_chunk(c, carry, masked):
        k0 = pl.multiple_of(c * TK, TK)
        kch = ksf_ref[0, pl.ds(k0, TK), :]
        cv = c * (TK // VCHUNK)
        vch = jnp.concatenate([vst_ref[0, cv + j] for j in range(TK // VCHUNK)], axis=1)
        out = []
        for g in groups:
            m_p, l_p, acc = carry[g]
            s = _dot_nt(kch, q_aug[g])
            if masked:
                kpos = k0 + lax.broadcasted_iota(jnp.int32, (TK, 1), 0)
                s = jnp.where(kpos <= t_col, s, NEG_INF)
            m_n = jnp.maximum(m_p, jnp.max(s, axis=0, keepdims=True))
            alpha = jnp.exp2(m_p - m_n)
            p = jnp.exp2(s - m_n)
            l_n = alpha * l_p + jnp.sum(p, axis=0, keepdims=True)
            acc = alpha * acc + _dot(vrows(vch, g), p.astype(BF16))
            out.append((m_n, l_n, acc))
        return tuple(out)

    init = tuple((jnp.full((1, rows), NEG_INF, F32), jnp.zeros((1, rows), F32),
                  jnp.zeros((HEAD_DIM, rows), F32)) for _ in groups)
    n_full = t0 // TK
    carry = lax.fori_loop(0, n_full, lambda c, cr: sel_chunk(c, cr, False), init)
    carry = sel_chunk(n_full, carry, True)

    nwc = WINDOW // VCHUNK + TQ // VCHUNK
    cw = jnp.maximum(i * (TQ // VCHUNK) - WINDOW // VCHUNK, 0)
    w0 = pl.multiple_of(cw * VCHUNK, VCHUNK)
    kwin = kw_ref[0, pl.ds(w0, nwc * VCHUNK), :]
    vwin = jnp.concatenate([vwt_ref[0, cw + j] for j in range(nwc)], axis=1)
    kpos = w0 + lax.broadcasted_iota(jnp.int32, (nwc * VCHUNK, 1), 0)
    mask_w = (kpos <= t_col) & (kpos > t_col - WINDOW)
    o_w, l_w = [], []
    for g in groups:
        sw = jnp.where(mask_w, _dot_nt(kwin, qg[g]), NEG_INF)
        p_w = jnp.exp2(sw - jnp.max(sw, axis=0, keepdims=True))
        l_w.append(jnp.sum(p_w, axis=0, keepdims=True))
        o_w.append(_dot(vrows(vwin, g), p_w.astype(BF16)))

    g_t = gate_ref[0].T
    slabs = []
    for g in groups:
        def gate_row(br):
            base = g * (N_BRANCH * GROUP_HEADS) + br * GROUP_HEADS
            return jnp.concatenate([g_t[base + r:base + r + 1, :] for r in range(GROUP_HEADS)], axis=1)
        _, l_s, o_s = carry[g]
        og = o_c[g] * gate_row(0) + o_s * (gate_row(1) / l_s) + o_w[g] * (gate_row(2) / l_w[g])
        slabs += [og[:, r * TQ:(r + 1) * TQ] for r in range(GROUP_HEADS)]
    out_t = jnp.concatenate(slabs, axis=0)
    o_ref[0] = (out_t.T * zs_ref[0].astype(F32)).astype(BF16)


def _out_kernel(x_ref, on_ref, oc_ref, w_ref, o_ref):
    acc = _dot(on_ref[0], w_ref[0:D_NSA, :]) + _dot(oc_ref[0], w_ref[D_NSA:D_NSA + D_CONV, :])
    o_ref[0] = x_ref[0] + acc


def _rope_tables(pos):
    inv_freq = ROPE_THETA ** (-jnp.arange(0, ROT_DIM, 2, dtype=F32) / ROT_DIM)
    ang = pos.astype(F32)[:, None] * inv_freq[None, :]
    cos, sin = jnp.cos(ang), jnp.sin(ang)
    n = pos.shape[0]
    ones = jnp.ones((n, HEAD_DIM - ROT_DIM), F32)
    zeros = jnp.zeros((n, HEAD_DIM - ROT_DIM), F32)
    z8 = jnp.zeros((n, ROT_DIM // 2), F32)
    c = jnp.concatenate([cos, cos, ones], axis=1)
    sa = jnp.concatenate([z8, sin, zeros], axis=1)
    sb = jnp.concatenate([-sin, z8, zeros], axis=1)
    tile2 = lambda t: jnp.concatenate([t, t], axis=1)
    return tile2(c), tile2(sa), tile2(sb)


def _overlap_t(ncp):
    cs = np.arange(ncp) * CMP_STRIDE
    ce = cs + CMP_LEN
    ss = np.arange(SEL_BLOCK) * SEL_BLOCK
    se = ss + SEL_BLOCK
    ov = np.clip(np.minimum(ce[None, :], se[:, None]) - np.maximum(cs[None, :], ss[:, None]), 0, None)
    ov = ov.astype(np.float32) / CMP_LEN
    ov[:, ncp - 1] = 0.0
    return jnp.asarray(ov, BF16)


def _cmp_weights(pos, w1, b1, w2):
    half = CMP_STRIDE
    def big_w1(w):
        w = w.reshape(half, 1, HEAD_DIM, 1, CMP_HIDDEN)
        eye = jnp.eye(KV_GROUPS, dtype=F32).reshape(1, KV_GROUPS, 1, KV_GROUPS, 1)
        return (w * eye).reshape(half * KV_GROUPS * HEAD_DIM, KV_GROUPS * CMP_HIDDEN).astype(BF16)
    def big_pos(p):
        return jnp.broadcast_to(p[:, None, :], (half, KV_GROUPS, HEAD_DIM)).reshape(1, -1)
    w1a, w1b = w1[:half * HEAD_DIM], w1[half * HEAD_DIM:]
    eye2 = jnp.eye(KV_GROUPS, dtype=F32)
    w2b = (w2[None, :, None, :] * eye2[:, None, :, None]).reshape(KV_GROUPS * CMP_HIDDEN,
                                                                 KV_GROUPS * HEAD_DIM).astype(BF16)
    b1b = jnp.tile(b1, KV_GROUPS)[None, :]
    return big_pos(pos[:half]), big_pos(pos[half:]), big_w1(w1a), big_w1(w1b), b1b, w2b


def _full(shape):
    nd = len(shape)
    return pl.BlockSpec(shape, lambda *_: (0,) * nd)


def kernel(x, norm_w, w_in, q_norm_w, k_norm_w, cmp_k_pos, cmp_k_w1, cmp_k_b1, cmp_k_w2,
           cmp_v_pos, cmp_v_w1, cmp_v_b1, cmp_v_w2, conv_w, conv_b, w_out):
    B, S, D = x.shape
    assert norm_w.shape[0] == 1, "single layer"
    assert S % TM == 0 and S // SEL_BLOCK <= SEL_BLOCK and S >= WINDOW + TQ
    ncp = S // CMP_STRIDE
    nst = S // TM
    f32 = lambda a: a.astype(F32)

    w = f32(w_in[0])
    head_order = []
    for j in range(GROUP_HEADS):
        head_order += [j, GROUP_HEADS + j]
    wq = w[:, :D_NSA].reshape(D, NSA_HEADS, HEAD_DIM)[:, jnp.array(head_order), :].reshape(D, D_NSA)
    gate_cols = []
    for g in range(KV_GROUPS):
        for br in range(N_BRANCH):
            for r in range(GROUP_HEADS):
                gate_cols.append((g * GROUP_HEADS + r) * N_BRANCH + br)
    g0 = D_NSA + 6 * LANES
    wg = w[:, g0:g0 + NSA_HEADS * N_BRANCH][:, jnp.array(gate_cols)]
    wg = jnp.pad(wg, ((0, 0), (0, LANES - NSA_HEADS * N_BRANCH)))
    rest0 = g0 + NSA_HEADS * N_BRANCH
    w_cat = jnp.concatenate([wq, w[:, D_NSA:g0], wg, w[:, rest0:]], axis=1).astype(BF16)
    n_cat = w_cat.shape[1]

    rc, rsa, rsb = _rope_tables(jnp.arange(S))
    mbd = jnp.asarray(np.kron(np.eye(4), np.full((HEAD_DIM, HEAD_DIM), 1.0 / HEAD_DIM)), BF16)
    qnw = jnp.tile(f32(q_norm_w[0]), 4)[None, :]
    knw_sw = jnp.concatenate([jnp.tile(f32(k_norm_w[0, 1]), 2), jnp.tile(f32(k_norm_w[0, 2]), 2)])[None, :]
    knw_c = jnp.tile(f32(k_norm_w[0, 0]), 2)[None, :]

    cp = pltpu.CompilerParams(dimension_semantics=("arbitrary", "arbitrary"),
                              vmem_limit_bytes=VMEM_LIMIT)
    row_blk = lambda n: pl.BlockSpec((1, TM, n), lambda b, s: (b, s, 0))
    tab_blk = pl.BlockSpec((TM, LANES), lambda b, s: (s, 0))
    vt_blk = pl.BlockSpec((1, TM // VCHUNK, LANES, VCHUNK), lambda b, s: (b, s, 0, 0))
    sds = jax.ShapeDtypeStruct

    (q4, kc_raw, vc_raw, ksf, vst, kw, vwt, gates, zs, oconv) = pl.pallas_call(
        _proj_kernel,
        grid=(B, nst),
        in_specs=[row_blk(D), _full((1, D)), _full((D, n_cat)), _full((1, 256)), _full((1, 256)),
                  tab_blk, tab_blk, tab_blk, _full((256, 256)), _full((3, D_CONV)), _full((1, D_CONV))],
        out_specs=[pl.BlockSpec((1, 4, TM, LANES), lambda b, s: (b, 0, s, 0)),
                   row_blk(LANES), row_blk(LANES), row_blk(2 * LANES), vt_blk, row_blk(LANES), vt_blk,
                   row_blk(LANES), row_blk(D_NSA), row_blk(D_CONV)],
        out_shape=[sds((B, 4, S, LANES), BF16), sds((B, S, LANES), F32), sds((B, S, LANES), F32),
                   sds((B, S, 2 * LANES), BF16), sds((B, S // VCHUNK, LANES, VCHUNK), BF16),
                   sds((B, S, LANES), BF16), sds((B, S // VCHUNK, LANES, VCHUNK), BF16),
                   sds((B, S, LANES), F32), sds((B, S, D_NSA), BF16), sds((B, S, D_CONV), BF16)],
        scratch_shapes=[pltpu.VMEM((TM + 16, D_CONV), F32)],
        compiler_params=cp,
        name="nsa_proj",
    )(x, f32(norm_w), w_cat, qnw, knw_sw, rc, rsa, rsb, mbd, f32(conv_w[0]), f32(conv_b))

    cmp_pos = jnp.arange(ncp) * CMP_STRIDE + (CMP_LEN - 1)
    crc, crsa, crsb = _rope_tables(cmp_pos)
    kpa, kpb, kw1a, kw1b, kb1, kw2 = _cmp_weights(f32(cmp_k_pos[0]), f32(cmp_k_w1[0]),
                                                  f32(cmp_k_b1[0]), f32(cmp_k_w2[0]))
    vpa, vpb, vw1a, vw1b, vb1, vw2 = _cmp_weights(f32(cmp_v_pos[0]), f32(cmp_v_w1[0]),
                                                  f32(cmp_v_b1[0]), f32(cmp_v_w2[0]))
    flat = CMP_STRIDE * LANES
    h_blk = pl.BlockSpec((1, ncp, flat), lambda b: (b, 0, 0))
    kc, vct = pl.pallas_call(
        functools.partial(_cmp_kernel, ncp=ncp),
        grid=(B,),
        in_specs=[h_blk, h_blk] + [_full((1, flat))] * 4
                 + [_full((flat, 256)), _full((flat, 256)), _full((1, 256)), _full((256, LANES))] * 2
                 + [_full((1, LANES)), _full((ncp, LANES)), _full((ncp, LANES)), _full((ncp, LANES)),
                    _full((256, 256))],
        out_specs=[pl.BlockSpec((1, ncp, LANES), lambda b: (b, 0, 0)),
                   pl.BlockSpec((1, LANES, ncp), lambda b: (b, 0, 0))],
        out_shape=[sds((B, ncp, LANES), BF16), sds((B, LANES, ncp), BF16)],
        compiler_params=pltpu.CompilerParams(dimension_semantics=("arbitrary",),
                                             vmem_limit_bytes=VMEM_LIMIT),
        name="nsa_compress",
    )(kc_raw.reshape(B, ncp, flat), vc_raw.reshape(B, ncp, flat), kpa, kpb, vpa, vpb,
      kw1a, kw1b, kb1, kw2, vw1a, vw1b, vb1, vw2, knw_c, crc, crsa, crsb, mbd)

    nqt = S // TQ
    o_nsa = pl.pallas_call(
        functools.partial(_attn_kernel, ncp=ncp),
        grid=(B, nqt),
        in_specs=[pl.BlockSpec((1, 4, TQ, LANES), lambda b, i: (b, 0, i, 0)),
                  pl.BlockSpec((1, ncp, LANES), lambda b, i: (b, 0, 0)),
                  pl.BlockSpec((1, LANES, ncp), lambda b, i: (b, 0, 0)),
                  pl.BlockSpec((1, S, 2 * LANES), lambda b, i: (b, 0, 0)),
                  pl.BlockSpec((1, S // VCHUNK, LANES, VCHUNK), lambda b, i: (b, 0, 0, 0)),
                  pl.BlockSpec((1, S, LANES), lambda b, i: (b, 0, 0)),
                  pl.BlockSpec((1, S // VCHUNK, LANES, VCHUNK), lambda b, i: (b, 0, 0, 0)),
                  pl.BlockSpec((1, TQ, LANES), lambda b, i: (b, i, 0)),
                  pl.BlockSpec((1, TQ, D_NSA), lambda b, i: (b, i, 0)),
                  _full((SEL_BLOCK, ncp))],
        out_specs=pl.BlockSpec((1, TQ, D_NSA), lambda b, i: (b, i, 0)),
        out_shape=sds((B, S, D_NSA), BF16),
        scratch_shapes=[pltpu.VMEM((KV_GROUPS, SEL_BLOCK, TQ), F32),
                        pltpu.VMEM((KV_GROUPS, TQ, LANES), BF16)],
        compiler_params=cp,
        name="nsa_attn",
    )(q4, kc, vct, ksf, vst, kw, vwt, gates, zs, _overlap_t(ncp))

    out = pl.pallas_call(
        _out_kernel,
        grid=(B, nst),
        in_specs=[row_blk(D), row_blk(D_NSA), row_blk(D_CONV), _full((D_NSA + D_CONV, D))],
        out_specs=row_blk(D),
        out_shape=sds((B, S, D), x.dtype),
        compiler_params=cp,
        name="nsa_out",
    )(x, o_nsa, oconv, f32(w_out[0]).astype(BF16))
    return out
```

```python
import functools

import numpy as np
import jax
import jax.numpy as jnp
from jax import lax
from jax.experimental import pallas as pl
from jax.experimental.pallas import tpu as pltpu

LANES = 128
SUBLANES = 8
HEAD_DIM = 64
NSA_HEADS = 8
KV_GROUPS = 2
GROUP_HEADS = NSA_HEADS // KV_GROUPS
D_NSA = NSA_HEADS * HEAD_DIM
D_CONV = 512
N_BRANCH = 3
ROT_DIM = HEAD_DIM // 4
ROPE_THETA = 500000.0
CMP_LEN = 32
CMP_STRIDE = 16
CMP_HIDDEN = 2 * HEAD_DIM
SEL_BLOCK = 64
SEL_TOPK = 16
WINDOW = 512
EPS = 1e-6
NEG_INF = -1e30
FORCE_SCORE = 1e9
SCALE = HEAD_DIM ** -0.5
Q_SCALE = SCALE * float(np.log2(np.e))

TM = 512
TQ = 128
TK = 512
VCHUNK = 128
VMEM_LIMIT = 56 * 1024 * 1024

BF16 = jnp.bfloat16
F32 = jnp.float32


def _dot(a, b):
    return jnp.dot(a, b, preferred_element_type=F32)


def _dot_nt(a, b):
    return lax.dot_general(a, b, (((1,), (1,)), ((), ())), preferred_element_type=F32)


def _split3(a):
    hi = a.astype(BF16)
    r1 = a - hi.astype(F32)
    mid = r1.astype(BF16)
    lo = (r1 - mid.astype(F32)).astype(BF16)
    return hi, mid, lo


def _group_mean(sq, mbd):
    hi = sq.astype(BF16)
    lo = (sq - hi.astype(F32)).astype(BF16)
    return _dot(hi, mbd) + _dot(lo, mbd)


def _rope(xn, c, sa, sb):
    return xn * c + pltpu.roll(xn, 8, 1) * sa + pltpu.roll(xn, LANES - 8, 1) * sb


def _silu(z):
    return z * (1.0 / (1.0 + jnp.exp(-z)))


def _proj_kernel(x_ref, nw_ref, w_ref, qnw_ref, knw_ref, rc_ref, rsa_ref, rsb_ref,
                 mbd_ref, cw_ref, cb_ref,
                 qt_ref, kc_ref, vc_ref, ksf_ref, vst_ref, kw_ref, vwt_ref,
                 gate_ref, zs_ref, oconv_ref, ubuf):
    si = pl.program_id(1)
    x = x_ref[0]
    ms = jnp.mean(x * x, axis=-1, keepdims=True)
    h = (x * lax.rsqrt(ms + EPS) * nw_ref[...]).astype(BF16)

    rc, rsa, rsb = rc_ref[...], rsa_ref[...], rsb_ref[...]
    mbd = mbd_ref[...]

    pq = _dot(h, w_ref[:, 0:D_NSA])
    for pair in range(2):
        blk = pq[:, pair * 256:(pair + 1) * 256]
        msq = _group_mean(blk * blk, mbd)
        qn = blk * lax.rsqrt(msq + EPS) * qnw_ref[...]
        for half in range(2):
            t = qn[:, half * LANES:(half + 1) * LANES]
            qt_ref[0, pair * 2 + half] = (_rope(t, rc, rsa, rsb) * Q_SCALE).T.astype(BF16)

    pkv = _dot(h, w_ref[:, 512:1280])
    kc_ref[0] = pkv[:, 0:128]
    vc_ref[0] = pkv[:, 128:256]
    ksw = jnp.concatenate([pkv[:, 256:384], pkv[:, 512:640]], axis=1)
    msk = _group_mean(ksw * ksw, mbd)
    kn = ksw * lax.rsqrt(msk + EPS) * knw_ref[...]
    ks = _rope(kn[:, 0:128], rc, rsa, rsb)
    kw = _rope(kn[:, 128:256], rc, rsa, rsb)
    row = lax.broadcasted_iota(jnp.int32, (TM, LANES), 0) + si * TM
    lane = lax.broadcasted_iota(jnp.int32, (TM, LANES), 1)
    onehot = jnp.where(lane == (row >> 6), 1.0, 0.0)
    ksf_ref[0, :, 0:128] = ks.astype(BF16)
    ksf_ref[0, :, 128:256] = onehot.astype(BF16)
    kw_ref[0] = kw.astype(BF16)
    vs = pkv[:, 384:512]
    vw = pkv[:, 640:768]
    for j in range(TM // VCHUNK):
        vst_ref[0, j] = vs[j * VCHUNK:(j + 1) * VCHUNK, :].T.astype(BF16)
        vwt_ref[0, j] = vw[j * VCHUNK:(j + 1) * VCHUNK, :].T.astype(BF16)

    pgz = _dot(h, w_ref[:, 1280:1920])
    gate_ref[0] = 1.0 / (1.0 + jnp.exp(-pgz[:, 0:128]))
    zs_ref[0] = _silu(pgz[:, 128:640]).astype(BF16)

    pcv = _dot(h, w_ref[:, 1920:3968])
    u = pcv[:, 1024:1536] * pcv[:, 0:512]

    @pl.when(si == 0)
    def _():
        ubuf[0:8, :] = jnp.zeros((8, D_CONV), F32)

    ubuf[8:8 + TM, :] = u
    u1 = ubuf[7:7 + TM, :]
    u2 = ubuf[6:6 + TM, :]
    conv = cw_ref[0:1, :] * u2 + cw_ref[1:2, :] * u1 + cw_ref[2:3, :] * u + cb_ref[...]
    oconv_ref[0] = (pcv[:, 512:1024] * conv * _silu(pcv[:, 1536:2048])).astype(BF16)
    ubuf[0:8, :] = ubuf[TM:TM + 8, :]


def _cmp_kernel(hk_ref, hv_ref, pka_ref, pkb_ref, pva_ref, pvb_ref,
                wk1a_ref, wk1b_ref, bk1_ref, wk2_ref,
                wv1a_ref, wv1b_ref, bv1_ref, wv2_ref,
                knw_ref, rc_ref, rsa_ref, rsb_ref, mbd_ref,
                kc_ref, vct_ref, *, ncp):
    def mlp(h_ref, pa_ref, pb_ref, w1a_ref, w1b_ref, b1_ref, w2_ref):
        hh = h_ref[0]
        p = _dot((hh + pa_ref[...]).astype(BF16), w1a_ref[...])
        q = _dot((hh + pb_ref[...]).astype(BF16), w1b_ref[...])
        pre = p + pltpu.roll(q, ncp - 1, 0) + b1_ref[...]
        return _dot(_silu(pre).astype(BF16), w2_ref[...])

    kc = mlp(hk_ref, pka_ref, pkb_ref, wk1a_ref, wk1b_ref, bk1_ref, wk2_ref)
    msk = _group_mean(kc * kc, mbd_ref[0:LANES, 0:LANES])
    kn = kc * lax.rsqrt(msk + EPS) * knw_ref[...]
    kc_ref[0] = _rope(kn, rc_ref[...], rsa_ref[...], rsb_ref[...]).astype(BF16)
    vc = mlp(hv_ref, pva_ref, pvb_ref, wv1a_ref, wv1b_ref, bv1_ref, wv2_ref)
    vct_ref[0] = vc.T.astype(BF16)


def _attn_kernel(qt_ref, kc_ref, vct_ref, ksf_ref, vst_ref, kw_ref, vwt_ref,
                 gate_ref, zs_ref, ovt_ref, o_ref, imp_sc, pen_sc, s_sc, cm_sc, *, ncp):
    i = pl.program_id(1)
    t0 = i * TQ
    rows = GROUP_HEADS * TQ
    nblk = SEL_BLOCK
    groups = range(KV_GROUPS)

    q_all = jnp.concatenate([qt_ref[0, j] for j in range(GROUP_HEADS)], axis=1)
    frow = lax.broadcasted_iota(jnp.int32, (LANES, rows), 0)
    qg = [jnp.where(frow < HEAD_DIM, q_all, 0), jnp.where(frow >= HEAD_DIM, q_all, 0)]
    colq = lax.broadcasted_iota(jnp.int32, (1, rows), 1)
    t_col = t0 + (colq & (TQ - 1))
    vrows = lambda v, g: v[g * HEAD_DIM:(g + 1) * HEAD_DIM, :]

    s_cmp = [_dot(kc_ref[0], qg[g]) for g in groups]
    nwc = WINDOW // VCHUNK + TQ // VCHUNK
    cw = jnp.maximum(i * (TQ // VCHUNK) - WINDOW // VCHUNK, 0)
    w0 = pl.multiple_of(cw * VCHUNK, VCHUNK)
    kwin = kw_ref[0, pl.ds(w0, nwc * VCHUNK), :]
    s_win = [_dot(kwin, qg[g]) for g in groups]

    c_idx = lax.broadcasted_iota(jnp.int32, (ncp, 1), 0)
    mask_c = ((c_idx * CMP_STRIDE + (CMP_LEN - 1)) <= t_col) & (c_idx < ncp - 1)
    p_c, o_c = [], []
    for g in groups:
        sm = jnp.where(mask_c, s_cmp[g], NEG_INF)
        m_c = jnp.max(sm, axis=0, keepdims=True)
        e_c = jnp.where(mask_c, jnp.exp2(sm - m_c), 0.0)
        l_c = jnp.sum(e_c, axis=0, keepdims=True)
        p = e_c / jnp.where(l_c > 0.0, l_c, 1.0)
        p_c.append(p)
        o_c.append(_dot(vrows(vct_ref[0], g), p.astype(BF16)))

    @pl.when(t0 + TQ <= SEL_TOPK * SEL_BLOCK)
    def _():
        pen_sc[...] = jnp.zeros(pen_sc.shape, BF16)

    @pl.when(t0 + TQ > SEL_TOPK * SEL_BLOCK)
    def _():
        n_idx = lax.broadcasted_iota(jnp.int32, (nblk, TQ), 0)
        tq = t0 + lax.broadcasted_iota(jnp.int32, (nblk, TQ), 1)
        cur = tq >> 6
        forced = (n_idx == 0) | (n_idx == cur) | (n_idx == cur - 1)
        visible = (n_idx << 6) <= tq
        sub = lax.broadcasted_iota(jnp.int32, (SUBLANES, TQ), 0)
        nvb = nblk // SUBLANES
        ovt = ovt_ref[...]
        for g in groups:
            psum = p_c[g][:, 0:TQ]
            for r in range(1, GROUP_HEADS):
                psum = psum + p_c[g][:, r * TQ:(r + 1) * TQ]
            hi, mid, lo = _split3(psum)
            imp = _dot(ovt, hi) + _dot(ovt, mid) + _dot(ovt, lo)
            imp = jnp.where(forced, FORCE_SCORE, jnp.where(visible, imp, -1.0))
            imp_sc[g] = imp
            imp_b = [imp[vb * SUBLANES:(vb + 1) * SUBLANES, :] for vb in range(nvb)]
            rank_b = [jnp.zeros((SUBLANES, TQ), F32) for _ in range(nvb)]
            for mm in range(nblk):
                a = jnp.broadcast_to(imp_sc[g, mm:mm + 1, :], (SUBLANES, TQ))
                for vb in range(nvb):
                    if vb * SUBLANES > mm:
                        beats = a >= imp_b[vb]
                    elif vb * SUBLANES + SUBLANES - 1 < mm:
                        beats = a > imp_b[vb]
                    else:
                        beats = (a > imp_b[vb]) | ((a == imp_b[vb]) & (sub + vb * SUBLANES > mm))
                    rank_b[vb] = rank_b[vb] + jnp.where(beats, 1.0, 0.0)
            rank = jnp.concatenate(rank_b, axis=0)
            pen_t = jnp.where(rank < float(SEL_TOPK), 0.0, NEG_INF)
            pen_sc[g, 0:nblk, :] = pen_t.astype(BF16)
            pen_sc[g, nblk:LANES, :] = jnp.zeros((LANES - nblk, TQ), BF16)

    q_aug = [jnp.concatenate([qg[g], jnp.concatenate([pen_sc[g]] * GROUP_HEADS, axis=1)], axis=0)
             for g in groups]

    def qk(c, slot, g):
        k0 = pl.multiple_of(c * TK, TK)
        s = _dot(ksf_ref[0, pl.ds(k0, TK), :], q_aug[g])
        s_sc[slot, g] = s
        cm_sc[slot, g] = jnp.max(s, axis=0, keepdims=True)

    def sm_pv(c, slot, g, state, masked):
        m_p, l_p, acc = state
        s = s_sc[slot, g]
        if masked:
            kpos = c * TK + lax.broadcasted_iota(jnp.int32, (TK, 1), 0)
            s = jnp.where(kpos <= t_col, s, NEG_INF)
            cm = jnp.max(s, axis=0, keepdims=True)
        else:
            cm = cm_sc[slot, g]
        m_n = jnp.maximum(m_p, cm)
        alpha = jnp.exp2(m_p - m_n)
        p = jnp.exp2(s - m_n)
        l_n = alpha * l_p + jnp.sum(p, axis=0, keepdims=True)
        cv = c * (TK // VCHUNK)
        vch = jnp.concatenate([vrows(vst_ref[0, cv + j], g) for j in range(TK // VCHUNK)], axis=1)
        acc = alpha * acc + _dot(vch, p.astype(BF16))
        return m_n, l_n, acc

    def step(c, slot, carry):
        out = []
        for g in groups:
            qk(c + 1, 1 - slot, g)
            out.append(sm_pv(c, slot, g, carry[g], False))
        return tuple(out)

    def last(slot, carry):
        return tuple(sm_pv(n_full, slot, g, carry[g], True) for g in groups)

    n_full = t0 // TK
    for g in groups:
        qk(0, 0, g)

    vwin = jnp.concatenate([vwt_ref[0, cw + j] for j in range(nwc)], axis=1)
    kpos = w0 + lax.broadcasted_iota(jnp.int32, (nwc * VCHUNK, 1), 0)
    mask_w = (kpos <= t_col) & (kpos > t_col - WINDOW)
    o_w, l_w = [], []
    for g in groups:
        sw = jnp.where(mask_w, s_win[g], NEG_INF)
        p_w = jnp.exp2(sw - jnp.max(sw, axis=0, keepdims=True))
        l_w.append(jnp.sum(p_w, axis=0, keepdims=True))
        o_w.append(_dot(vrows(vwin, g), p_w.astype(BF16)))

    init = tuple((jnp.full((1, rows), NEG_INF, F32), jnp.zeros((1, rows), F32),
                  jnp.zeros((HEAD_DIM, rows), F32)) for _ in groups)
    carry = lax.fori_loop(0, n_full // 2, lambda h, cr: step(2 * h + 1, 1, step(2 * h, 0, cr)), init)
    carry = lax.cond((n_full & 1) == 1,
                     lambda cr: last(1, step(n_full - 1, 0, cr)),
                     lambda cr: last(0, cr), carry)

    g_t = gate_ref[0].T
    slabs = []
    for g in groups:
        def gate_row(br):
            base = g * (N_BRANCH * GROUP_HEADS) + br * GROUP_HEADS
            return jnp.concatenate([g_t[base + r:base + r + 1, :] for r in range(GROUP_HEADS)], axis=1)
        _, l_s, o_s = carry[g]
        og = o_c[g] * gate_row(0) + o_s * (gate_row(1) / l_s) + o_w[g] * (gate_row(2) / l_w[g])
        slabs += [og[:, r * TQ:(r + 1) * TQ] for r in range(GROUP_HEADS)]
    out_t = jnp.concatenate(slabs, axis=0)
    o_ref[0] = (out_t.T * zs_ref[0].astype(F32)).astype(BF16)


def _out_kernel(x_ref, on_ref, oc_ref, w_ref, o_ref):
    acc = _dot(on_ref[0], w_ref[0:D_NSA, :]) + _dot(oc_ref[0], w_ref[D_NSA:D_NSA + D_CONV, :])
    o_ref[0] = x_ref[0] + acc


def _rope_tables(pos):
    inv_freq = ROPE_THETA ** (-jnp.arange(0, ROT_DIM, 2, dtype=F32) / ROT_DIM)
    ang = pos.astype(F32)[:, None] * inv_freq[None, :]
    cos, sin = jnp.cos(ang), jnp.sin(ang)
    n = pos.shape[0]
    ones = jnp.ones((n, HEAD_DIM - ROT_DIM), F32)
    zeros = jnp.zeros((n, HEAD_DIM - ROT_DIM), F32)
    z8 = jnp.zeros((n, ROT_DIM // 2), F32)
    c = jnp.concatenate([cos, cos, ones], axis=1)
    sa = jnp.concatenate([z8, sin, zeros], axis=1)
    sb = jnp.concatenate([-sin, z8, zeros], axis=1)
    tile2 = lambda t: jnp.concatenate([t, t], axis=1)
    return tile2(c), tile2(sa), tile2(sb)


def _overlap_t(ncp):
    cs = np.arange(ncp) * CMP_STRIDE
    ce = cs + CMP_LEN
    ss = np.arange(SEL_BLOCK) * SEL_BLOCK
    se = ss + SEL_BLOCK
    ov = np.clip(np.minimum(ce[None, :], se[:, None]) - np.maximum(cs[None, :], ss[:, None]), 0, None)
    ov = ov.astype(np.float32) / CMP_LEN
    ov[:, ncp - 1] = 0.0
    return jnp.asarray(ov, BF16)


def _cmp_weights(pos, w1, b1, w2):
    half = CMP_STRIDE
    def big_w1(w):
        w = w.reshape(half, 1, HEAD_DIM, 1, CMP_HIDDEN)
        eye = jnp.eye(KV_GROUPS, dtype=F32).reshape(1, KV_GROUPS, 1, KV_GROUPS, 1)
        return (w * eye).reshape(half * KV_GROUPS * HEAD_DIM, KV_GROUPS * CMP_HIDDEN).astype(BF16)
    def big_pos(p):
        return jnp.broadcast_to(p[:, None, :], (half, KV_GROUPS, HEAD_DIM)).reshape(1, -1)
    w1a, w1b = w1[:half * HEAD_DIM], w1[half * HEAD_DIM:]
    eye2 = jnp.eye(KV_GROUPS, dtype=F32)
    w2b = (w2[None, :, None, :] * eye2[:, None, :, None]).reshape(KV_GROUPS * CMP_HIDDEN,
                                                                 KV_GROUPS * HEAD_DIM).astype(BF16)
    b1b = jnp.tile(b1, KV_GROUPS)[None, :]
    return big_pos(pos[:half]), big_pos(pos[half:]), big_w1(w1a), big_w1(w1b), b1b, w2b


def _full(shape):
    nd = len(shape)
    return pl.BlockSpec(shape, lambda *_: (0,) * nd)


def kernel(x, norm_w, w_in, q_norm_w, k_norm_w, cmp_k_pos, cmp_k_w1, cmp_k_b1, cmp_k_w2,
           cmp_v_pos, cmp_v_w1, cmp_v_b1, cmp_v_w2, conv_w, conv_b, w_out):
    B, S, D = x.shape
    assert norm_w.shape[0] == 1, "single layer"
    assert S % TM == 0 and S // SEL_BLOCK <= SEL_BLOCK and S >= WINDOW + TQ
    ncp = S // CMP_STRIDE
    nst = S // TM
    f32 = lambda a: a.astype(F32)

    w = f32(w_in[0])
    head_order = []
    for j in range(GROUP_HEADS):
        head_order += [j, GROUP_HEADS + j]
    wq = w[:, :D_NSA].reshape(D, NSA_HEADS, HEAD_DIM)[:, jnp.array(head_order), :].reshape(D, D_NSA)
    gate_cols = []
    for g in range(KV_GROUPS):
        for br in range(N_BRANCH):
            for r in range(GROUP_HEADS):
                gate_cols.append((g * GROUP_HEADS + r) * N_BRANCH + br)
    g0 = D_NSA + 6 * LANES
    wg = w[:, g0:g0 + NSA_HEADS * N_BRANCH][:, jnp.array(gate_cols)]
    wg = jnp.pad(wg, ((0, 0), (0, LANES - NSA_HEADS * N_BRANCH)))
    rest0 = g0 + NSA_HEADS * N_BRANCH
    w_cat = jnp.concatenate([wq, w[:, D_NSA:g0], wg, w[:, rest0:]], axis=1).astype(BF16)
    n_cat = w_cat.shape[1]

    rc, rsa, rsb = _rope_tables(jnp.arange(S))
    mbd = jnp.asarray(np.kron(np.eye(4), np.full((HEAD_DIM, HEAD_DIM), 1.0 / HEAD_DIM)), BF16)
    qnw = jnp.tile(f32(q_norm_w[0]), 4)[None, :]
    knw_sw = jnp.concatenate([jnp.tile(f32(k_norm_w[0, 1]), 2), jnp.tile(f32(k_norm_w[0, 2]), 2)])[None, :]
    knw_c = jnp.tile(f32(k_norm_w[0, 0]), 2)[None, :]

    cp = pltpu.CompilerParams(dimension_semantics=("arbitrary", "arbitrary"),
                              vmem_limit_bytes=VMEM_LIMIT)
    row_blk = lambda n: pl.BlockSpec((1, TM, n), lambda b, s: (b, s, 0))
    tab_blk = pl.BlockSpec((TM, LANES), lambda b, s: (s, 0))
    vt_blk = pl.BlockSpec((1, TM // VCHUNK, LANES, VCHUNK), lambda b, s: (b, s, 0, 0))
    sds = jax.ShapeDtypeStruct

    (qt4, kc_raw, vc_raw, ksf, vst, kw, vwt, gates, zs, oconv) = pl.pallas_call(
        _proj_kernel,
        grid=(B, nst),
        in_specs=[row_blk(D), _full((1, D)), _full((D, n_cat)), _full((1, 256)), _full((1, 256)),
                  tab_blk, tab_blk, tab_blk, _full((256, 256)), _full((3, D_CONV)), _full((1, D_CONV))],
        out_specs=[pl.BlockSpec((1, 4, LANES, TM), lambda b, s: (b, 0, 0, s)),
                   row_blk(LANES), row_blk(LANES), row_blk(2 * LANES), vt_blk, row_blk(LANES), vt_blk,
                   row_blk(LANES), row_blk(D_NSA), row_blk(D_CONV)],
        out_shape=[sds((B, 4, LANES, S), BF16), sds((B, S, LANES), F32), sds((B, S, LANES), F32),
                   sds((B, S, 2 * LANES), BF16), sds((B, S // VCHUNK, LANES, VCHUNK), BF16),
                   sds((B, S, LANES), BF16), sds((B, S // VCHUNK, LANES, VCHUNK), BF16),
                   sds((B, S, LANES), F32), sds((B, S, D_NSA), BF16), sds((B, S, D_CONV), BF16)],
        scratch_shapes=[pltpu.VMEM((TM + 16, D_CONV), F32)],
        compiler_params=cp,
        name="nsa_proj",
    )(x, f32(norm_w), w_cat, qnw, knw_sw, rc, rsa, rsb, mbd, f32(conv_w[0]), f32(conv_b))

    cmp_pos = jnp.arange(ncp) * CMP_STRIDE + (CMP_LEN - 1)
    crc, crsa, crsb = _rope_tables(cmp_pos)
    kpa, kpb, kw1a, kw1b, kb1, kw2 = _cmp_weights(f32(cmp_k_pos[0]), f32(cmp_k_w1[0]),
                                                  f32(cmp_k_b1[0]), f32(cmp_k_w2[0]))
    vpa, vpb, vw1a, vw1b, vb1, vw2 = _cmp_weights(f32(cmp_v_pos[0]), f32(cmp_v_w1[0]),
                                                  f32(cmp_v_b1[0]), f32(cmp_v_w2[0]))
    flat = CMP_STRIDE * LANES
    h_blk = pl.BlockSpec((1, ncp, flat), lambda b: (b, 0, 0))
    kc, vct = pl.pallas_call(
        functools.partial(_cmp_kernel, ncp=ncp),
        grid=(B,),
        in_specs=[h_blk, h_blk] + [_full((1, flat))] * 4
                 + [_full((flat, 256)), _full((flat, 256)), _full((1, 256)), _full((256, LANES))] * 2
                 + [_full((1, LANES)), _full((ncp, LANES)), _full((ncp, LANES)), _full((ncp, LANES)),
                    _full((256, 256))],
        out_specs=[pl.BlockSpec((1, ncp, LANES), lambda b: (b, 0, 0)),
                   pl.BlockSpec((1, LANES, ncp), lambda b: (b, 0, 0))],
        out_shape=[sds((B, ncp, LANES), BF16), sds((B, LANES, ncp), BF16)],
        compiler_params=pltpu.CompilerParams(dimension_semantics=("arbitrary",),
                                             vmem_limit_bytes=VMEM_LIMIT),
        name="nsa_compress",
    )(kc_raw.reshape(B, ncp, flat), vc_raw.reshape(B, ncp, flat), kpa, kpb, vpa, vpb,
      kw1a, kw1b, kb1, kw2, vw1a, vw1b, vb1, vw2, knw_c, crc, crsa, crsb, mbd)

    nqt = S // TQ
    o_nsa = pl.pallas_call(
        functools.partial(_attn_kernel, ncp=ncp),
        grid=(B, nqt),
        in_specs=[pl.BlockSpec((1, 4, LANES, TQ), lambda b, i: (b, 0, 0, i)),
                  pl.BlockSpec((1, ncp, LANES), lambda b, i: (b, 0, 0)),
                  pl.BlockSpec((1, LANES, ncp), lambda b, i: (b, 0, 0)),
                  pl.BlockSpec((1, S, 2 * LANES), lambda b, i: (b, 0, 0)),
                  pl.BlockSpec((1, S // VCHUNK, LANES, VCHUNK), lambda b, i: (b, 0, 0, 0)),
                  pl.BlockSpec((1, S, LANES), lambda b, i: (b, 0, 0)),
                  pl.BlockSpec((1, S // VCHUNK, LANES, VCHUNK), lambda b, i: (b, 0, 0, 0)),
                  pl.BlockSpec((1, TQ, LANES), lambda b, i: (b, i, 0)),
                  pl.BlockSpec((1, TQ, D_NSA), lambda b, i: (b, i, 0)),
                  _full((SEL_BLOCK, ncp))],
        out_specs=pl.BlockSpec((1, TQ, D_NSA), lambda b, i: (b, i, 0)),
        out_shape=sds((B, S, D_NSA), BF16),
        scratch_shapes=[pltpu.VMEM((KV_GROUPS, SEL_BLOCK, TQ), F32),
                        pltpu.VMEM((KV_GROUPS, LANES, TQ), BF16),
                        pltpu.VMEM((2, KV_GROUPS, TK, GROUP_HEADS * TQ), F32),
                        pltpu.VMEM((2, KV_GROUPS, 1, GROUP_HEADS * TQ), F32)],
        compiler_params=cp,
        name="nsa_attn",
    )(qt4, kc, vct, ksf, vst, kw, vwt, gates, zs, _overlap_t(ncp))

    out = pl.pallas_call(
        _out_kernel,
        grid=(B, nst),
        in_specs=[row_blk(D), row_blk(D_NSA), row_blk(D_CONV), _full((D_NSA + D_CONV, D))],
        out_specs=row_blk(D),
        out_shape=sds((B, S, D), x.dtype),
        compiler_params=cp,
        name="nsa_out",
    )(x, o_nsa, oconv, f32(w_out[0]).astype(BF16))
    return out
```

```python
import functools

import numpy as np
import jax
import jax.numpy as jnp
from jax import lax
from jax.experimental import pallas as pl
from jax.experimental.pallas import tpu as pltpu

LANES = 128
SUBLANES = 8
HEAD_DIM = 64
NSA_HEADS = 8
KV_GROUPS = 2
GROUP_HEADS = NSA_HEADS // KV_GROUPS
D_NSA = NSA_HEADS * HEAD_DIM
D_CONV = 512
N_BRANCH = 3
ROT_DIM = HEAD_DIM // 4
ROPE_THETA = 500000.0
CMP_LEN = 32
CMP_STRIDE = 16
CMP_HIDDEN = 2 * HEAD_DIM
SEL_BLOCK = 64
SEL_TOPK = 16
WINDOW = 512
EPS = 1e-6
NEG_INF = -1e30
FORCE_SCORE = 1e9
SCALE = HEAD_DIM ** -0.5
Q_SCALE = SCALE * float(np.log2(np.e))

TM = 512
TQ = 128
TK = 512
VCHUNK = 128
VMEM_LIMIT = 56 * 1024 * 1024
MAX_SAFE_SCORE = 50.0

BF16 = jnp.bfloat16
F32 = jnp.float32


def _dot(a, b):
    return jnp.dot(a, b, preferred_element_type=F32)


def _dot_nt(a, b):
    return lax.dot_general(a, b, (((1,), (1,)), ((), ())), preferred_element_type=F32)


def _split3(a):
    hi = a.astype(BF16)
    r1 = a - hi.astype(F32)
    mid = r1.astype(BF16)
    lo = (r1 - mid.astype(F32)).astype(BF16)
    return hi, mid, lo


def _group_mean(sq, mbd):
    hi = sq.astype(BF16)
    lo = (sq - hi.astype(F32)).astype(BF16)
    return _dot(hi, mbd) + _dot(lo, mbd)


def _rope(xn, c, sa, sb):
    return xn * c + pltpu.roll(xn, 8, 1) * sa + pltpu.roll(xn, LANES - 8, 1) * sb


def _silu(z):
    return z * (1.0 / (1.0 + jnp.exp(-z)))


def _proj_kernel(x_ref, nw_ref, w_ref, qnw_ref, knw_ref, rc_ref, rsa_ref, rsb_ref,
                 mbd_ref, cw_ref, cb_ref,
                 qt_ref, kc_ref, vc_ref, ksf_ref, vst_ref, kw_ref, vwt_ref,
                 gate_ref, zs_ref, oconv_ref, ubuf):
    si = pl.program_id(1)
    x = x_ref[0]
    ms = jnp.mean(x * x, axis=-1, keepdims=True)
    h = (x * lax.rsqrt(ms + EPS) * nw_ref[...]).astype(BF16)

    rc, rsa, rsb = rc_ref[...], rsa_ref[...], rsb_ref[...]
    mbd = mbd_ref[...]

    pq = _dot(h, w_ref[:, 0:D_NSA])
    for pair in range(2):
        blk = pq[:, pair * 256:(pair + 1) * 256]
        msq = _group_mean(blk * blk, mbd)
        qn = blk * lax.rsqrt(msq + EPS) * qnw_ref[...]
        for half in range(2):
            t = qn[:, half * LANES:(half + 1) * LANES]
            qt_ref[0, pair * 2 + half] = (_rope(t, rc, rsa, rsb) * Q_SCALE).T.astype(BF16)

    pkv = _dot(h, w_ref[:, 512:1280])
    kc_ref[0] = pkv[:, 0:128]
    vc_ref[0] = pkv[:, 128:256]
    ksw = jnp.concatenate([pkv[:, 256:384], pkv[:, 512:640]], axis=1)
    msk = _group_mean(ksw * ksw, mbd)
    kn = ksw * lax.rsqrt(msk + EPS) * knw_ref[...]
    ks = _rope(kn[:, 0:128], rc, rsa, rsb)
    kw = _rope(kn[:, 128:256], rc, rsa, rsb)
    row = lax.broadcasted_iota(jnp.int32, (TM, LANES), 0) + si * TM
    lane = lax.broadcasted_iota(jnp.int32, (TM, LANES), 1)
    onehot = jnp.where(lane == (row >> 6), 1.0, 0.0)
    ksf_ref[0, :, 0:128] = ks.astype(BF16)
    ksf_ref[0, :, 128:256] = onehot.astype(BF16)
    kw_ref[0] = kw.astype(BF16)
    vs = pkv[:, 384:512]
    vw = pkv[:, 640:768]
    for j in range(TM // VCHUNK):
        vst_ref[0, j] = vs[j * VCHUNK:(j + 1) * VCHUNK, :].T.astype(BF16)
        vwt_ref[0, j] = vw[j * VCHUNK:(j + 1) * VCHUNK, :].T.astype(BF16)

    pgz = _dot(h, w_ref[:, 1280:1920])
    gate_ref[0] = 1.0 / (1.0 + jnp.exp(-pgz[:, 0:128]))
    zs_ref[0] = _silu(pgz[:, 128:640]).astype(BF16)

    pcv = _dot(h, w_ref[:, 1920:3968])
    u = pcv[:, 1024:1536] * pcv[:, 0:512]

    @pl.when(si == 0)
    def _():
        ubuf[0:8, :] = jnp.zeros((8, D_CONV), F32)

    ubuf[8:8 + TM, :] = u
    u1 = ubuf[7:7 + TM, :]
    u2 = ubuf[6:6 + TM, :]
    conv = cw_ref[0:1, :] * u2 + cw_ref[1:2, :] * u1 + cw_ref[2:3, :] * u + cb_ref[...]
    oconv_ref[0] = (pcv[:, 512:1024] * conv * _silu(pcv[:, 1536:2048])).astype(BF16)
    ubuf[0:8, :] = ubuf[TM:TM + 8, :]


def _cmp_kernel(hk_ref, hv_ref, pka_ref, pkb_ref, pva_ref, pvb_ref,
                wk1a_ref, wk1b_ref, bk1_ref, wk2_ref,
                wv1a_ref, wv1b_ref, bv1_ref, wv2_ref,
                knw_ref, rc_ref, rsa_ref, rsb_ref, mbd_ref,
                kc_ref, vct_ref, *, ncp):
    def mlp(h_ref, pa_ref, pb_ref, w1a_ref, w1b_ref, b1_ref, w2_ref):
        hh = h_ref[0]
        p = _dot((hh + pa_ref[...]).astype(BF16), w1a_ref[...])
        q = _dot((hh + pb_ref[...]).astype(BF16), w1b_ref[...])
        pre = p + pltpu.roll(q, ncp - 1, 0) + b1_ref[...]
        return _dot(_silu(pre).astype(BF16), w2_ref[...])

    kc = mlp(hk_ref, pka_ref, pkb_ref, wk1a_ref, wk1b_ref, bk1_ref, wk2_ref)
    msk = _group_mean(kc * kc, mbd_ref[0:LANES, 0:LANES])
    kn = kc * lax.rsqrt(msk + EPS) * knw_ref[...]
    kc_ref[0] = _rope(kn, rc_ref[...], rsa_ref[...], rsb_ref[...]).astype(BF16)
    vc = mlp(hv_ref, pva_ref, pvb_ref, wv1a_ref, wv1b_ref, bv1_ref, wv2_ref)
    vct_ref[0] = vc.T.astype(BF16)


def _attn_kernel(qt_ref, kc_ref, vct_ref, ksf_ref, vst_ref, kw_ref, vwt_ref,
                 gate_ref, zs_ref, ovt_ref, o_ref, imp_sc, pen_sc, s_sc, cm_sc, *, ncp, bounded):
    i = pl.program_id(1)
    t0 = i * TQ
    rows = GROUP_HEADS * TQ
    nblk = SEL_BLOCK
    groups = range(KV_GROUPS)

    q_all = jnp.concatenate([qt_ref[0, j] for j in range(GROUP_HEADS)], axis=1)
    frow = lax.broadcasted_iota(jnp.int32, (LANES, rows), 0)
    qg = [jnp.where(frow < HEAD_DIM, q_all, 0), jnp.where(frow >= HEAD_DIM, q_all, 0)]
    colq = lax.broadcasted_iota(jnp.int32, (1, rows), 1)
    t_col = t0 + (colq & (TQ - 1))
    vrows = lambda v, g: v[g * HEAD_DIM:(g + 1) * HEAD_DIM, :]

    s_cmp = [_dot(kc_ref[0], qg[g]) for g in groups]
    nwc = WINDOW // VCHUNK + TQ // VCHUNK
    cw = jnp.maximum(i * (TQ // VCHUNK) - WINDOW // VCHUNK, 0)
    w0 = pl.multiple_of(cw * VCHUNK, VCHUNK)
    kwin = kw_ref[0, pl.ds(w0, nwc * VCHUNK), :]
    s_win = [_dot(kwin, qg[g]) for g in groups]

    c_idx = lax.broadcasted_iota(jnp.int32, (ncp, 1), 0)
    mask_c = ((c_idx * CMP_STRIDE + (CMP_LEN - 1)) <= t_col) & (c_idx < ncp - 1)
    p_c, o_c = [], []
    for g in groups:
        if bounded:
            e_c = jnp.where(mask_c, jnp.exp2(s_cmp[g]), 0.0)
        else:
            sm = jnp.where(mask_c, s_cmp[g], NEG_INF)
            e_c = jnp.where(mask_c, jnp.exp2(sm - jnp.max(sm, axis=0, keepdims=True)), 0.0)
        l_c = jnp.sum(e_c, axis=0, keepdims=True)
        p = e_c / jnp.where(l_c > 0.0, l_c, 1.0)
        p_c.append(p)
        o_c.append(_dot(vrows(vct_ref[0], g), p.astype(BF16)))

    n_idx = lax.broadcasted_iota(jnp.int32, (nblk, TQ), 0)
    tq = t0 + lax.broadcasted_iota(jnp.int32, (nblk, TQ), 1)
    cur = tq >> 6
    forced = (n_idx == 0) | (n_idx == cur) | (n_idx == cur - 1)
    visible = (n_idx << 6) <= tq
    sub = lax.broadcasted_iota(jnp.int32, (SUBLANES, TQ), 0)
    nvb = nblk // SUBLANES
    ovt = ovt_ref[...]
    for g in groups:
        psum = p_c[g][:, 0:TQ]
        for r in range(1, GROUP_HEADS):
            psum = psum + p_c[g][:, r * TQ:(r + 1) * TQ]
        hi, mid, lo = _split3(psum)
        imp = _dot(ovt, hi) + _dot(ovt, mid) + _dot(ovt, lo)
        imp = jnp.where(forced, FORCE_SCORE, jnp.where(visible, imp, -1.0))
        imp_sc[g] = imp
        imp_b = [imp[vb * SUBLANES:(vb + 1) * SUBLANES, :] for vb in range(nvb)]
        rank_b = [jnp.zeros((SUBLANES, TQ), F32) for _ in range(nvb)]
        for mm in range(nblk):
            a = jnp.broadcast_to(imp_sc[g, mm:mm + 1, :], (SUBLANES, TQ))
            for vb in range(nvb):
                if vb * SUBLANES > mm:
                    beats = a >= imp_b[vb]
                elif vb * SUBLANES + SUBLANES - 1 < mm:
                    beats = a > imp_b[vb]
                else:
                    beats = (a > imp_b[vb]) | ((a == imp_b[vb]) & (sub + vb * SUBLANES > mm))
                rank_b[vb] = rank_b[vb] + jnp.where(beats, 1.0, 0.0)
        rank = jnp.concatenate(rank_b, axis=0)
        pen_t = jnp.where(rank < float(SEL_TOPK), 0.0, NEG_INF)
        pen_sc[g, 0:nblk, :] = pen_t.astype(BF16)
        pen_sc[g, nblk:LANES, :] = jnp.zeros((LANES - nblk, TQ), BF16)

    q_aug = [jnp.concatenate([qg[g], jnp.concatenate([pen_sc[g]] * GROUP_HEADS, axis=1)], axis=0)
             for g in groups]

    def window_branch():
        vwin = jnp.concatenate([vwt_ref[0, cw + j] for j in range(nwc)], axis=1)
        kpos = w0 + lax.broadcasted_iota(jnp.int32, (nwc * VCHUNK, 1), 0)
        mask_w = (kpos <= t_col) & (kpos > t_col - WINDOW)
        o_w, l_w = [], []
        for g in groups:
            if bounded:
                p_w = jnp.where(mask_w, jnp.exp2(s_win[g]), 0.0)
            else:
                sw = jnp.where(mask_w, s_win[g], NEG_INF)
                p_w = jnp.exp2(sw - jnp.max(sw, axis=0, keepdims=True))
            l_w.append(jnp.sum(p_w, axis=0, keepdims=True))
            o_w.append(_dot(vrows(vwin, g), p_w.astype(BF16)))
        return o_w, l_w

    def v_chunk(c, g):
        cv = c * (TK // VCHUNK)
        return jnp.concatenate([vrows(vst_ref[0, cv + j], g) for j in range(TK // VCHUNK)], axis=1)

    def qk(c, slot, g):
        k0 = pl.multiple_of(c * TK, TK)
        s = _dot(ksf_ref[0, pl.ds(k0, TK), :], q_aug[g])
        s_sc[slot, g] = s
        if not bounded:
            cm_sc[slot, g] = jnp.max(s, axis=0, keepdims=True)

    def sm_pv(c, slot, g, state, masked):
        m_p, l_p, acc = state
        s = s_sc[slot, g]
        if masked:
            kpos = c * TK + lax.broadcasted_iota(jnp.int32, (TK, 1), 0)
            s = jnp.where(kpos <= t_col, s, NEG_INF)
        if bounded:
            p = jnp.exp2(s)
            return m_p, l_p + jnp.sum(p, axis=0, keepdims=True), acc + _dot(v_chunk(c, g), p.astype(BF16))
        cm = jnp.max(s, axis=0, keepdims=True) if masked else cm_sc[slot, g]
        m_n = jnp.maximum(m_p, cm)
        alpha = jnp.exp2(m_p - m_n)
        p = jnp.exp2(s - m_n)
        l_n = alpha * l_p + jnp.sum(p, axis=0, keepdims=True)
        acc = alpha * acc + _dot(v_chunk(c, g), p.astype(BF16))
        return m_n, l_n, acc

    def step(c, slot, carry):
        out = []
        for g in groups:
            qk(c + 1, 1 - slot, g)
            out.append(sm_pv(c, slot, g, carry[g], False))
        return tuple(out)

    def last(slot, carry):
        return tuple(sm_pv(n_full, slot, g, carry[g], True) for g in groups)

    n_full = t0 // TK
    for g in groups:
        qk(0, 0, g)
    o_w, l_w = window_branch()
    init = tuple((jnp.full((1, rows), 0.0 if bounded else NEG_INF, F32), jnp.zeros((1, rows), F32),
                  jnp.zeros((HEAD_DIM, rows), F32)) for _ in groups)
    carry = lax.fori_loop(0, n_full // 2, lambda h, cr: step(2 * h + 1, 1, step(2 * h, 0, cr)), init)
    carry = lax.cond((n_full & 1) == 1,
                     lambda cr: last(1, step(n_full - 1, 0, cr)),
                     lambda cr: last(0, cr), carry)
    sel_out = [(carry[g][1], carry[g][2]) for g in groups]

    g_t = gate_ref[0].T
    slabs = []
    for g in groups:
        def gate_row(br):
            base = g * (N_BRANCH * GROUP_HEADS) + br * GROUP_HEADS
            return jnp.concatenate([g_t[base + r:base + r + 1, :] for r in range(GROUP_HEADS)], axis=1)
        l_s, o_s = sel_out[g]
        og = o_c[g] * gate_row(0) + o_s * (gate_row(1) / l_s) + o_w[g] * (gate_row(2) / l_w[g])
        slabs += [og[:, r * TQ:(r + 1) * TQ] for r in range(GROUP_HEADS)]
    out_t = jnp.concatenate(slabs, axis=0)
    o_ref[0] = (out_t.T * zs_ref[0].astype(F32)).astype(BF16)


def _out_kernel(x_ref, on_ref, oc_ref, w_ref, o_ref):
    acc = _dot(on_ref[0], w_ref[0:D_NSA, :]) + _dot(oc_ref[0], w_ref[D_NSA:D_NSA + D_CONV, :])
    o_ref[0] = x_ref[0] + acc


def _rope_tables(pos):
    inv_freq = ROPE_THETA ** (-jnp.arange(0, ROT_DIM, 2, dtype=F32) / ROT_DIM)
    ang = pos.astype(F32)[:, None] * inv_freq[None, :]
    cos, sin = jnp.cos(ang), jnp.sin(ang)
    n = pos.shape[0]
    ones = jnp.ones((n, HEAD_DIM - ROT_DIM), F32)
    zeros = jnp.zeros((n, HEAD_DIM - ROT_DIM), F32)
    z8 = jnp.zeros((n, ROT_DIM // 2), F32)
    c = jnp.concatenate([cos, cos, ones], axis=1)
    sa = jnp.concatenate([z8, sin, zeros], axis=1)
    sb = jnp.concatenate([-sin, z8, zeros], axis=1)
    tile2 = lambda t: jnp.concatenate([t, t], axis=1)
    return tile2(c), tile2(sa), tile2(sb)


def _overlap_t(ncp):
    cs = np.arange(ncp) * CMP_STRIDE
    ce = cs + CMP_LEN
    ss = np.arange(SEL_BLOCK) * SEL_BLOCK
    se = ss + SEL_BLOCK
    ov = np.clip(np.minimum(ce[None, :], se[:, None]) - np.maximum(cs[None, :], ss[:, None]), 0, None)
    ov = ov.astype(np.float32) / CMP_LEN
    ov[:, ncp - 1] = 0.0
    return jnp.asarray(ov, BF16)


def _cmp_weights(pos, w1, b1, w2):
    half = CMP_STRIDE
    def big_w1(w):
        w = w.reshape(half, 1, HEAD_DIM, 1, CMP_HIDDEN)
        eye = jnp.eye(KV_GROUPS, dtype=F32).reshape(1, KV_GROUPS, 1, KV_GROUPS, 1)
        return (w * eye).reshape(half * KV_GROUPS * HEAD_DIM, KV_GROUPS * CMP_HIDDEN).astype(BF16)
    def big_pos(p):
        return jnp.broadcast_to(p[:, None, :], (half, KV_GROUPS, HEAD_DIM)).reshape(1, -1)
    w1a, w1b = w1[:half * HEAD_DIM], w1[half * HEAD_DIM:]
    eye2 = jnp.eye(KV_GROUPS, dtype=F32)
    w2b = (w2[None, :, None, :] * eye2[:, None, :, None]).reshape(KV_GROUPS * CMP_HIDDEN,
                                                                 KV_GROUPS * HEAD_DIM).astype(BF16)
    b1b = jnp.tile(b1, KV_GROUPS)[None, :]
    return big_pos(pos[:half]), big_pos(pos[half:]), big_w1(w1a), big_w1(w1b), b1b, w2b


def _full(shape):
    nd = len(shape)
    return pl.BlockSpec(shape, lambda *_: (0,) * nd)


def kernel(x, norm_w, w_in, q_norm_w, k_norm_w, cmp_k_pos, cmp_k_w1, cmp_k_b1, cmp_k_w2,
           cmp_v_pos, cmp_v_w1, cmp_v_b1, cmp_v_w2, conv_w, conv_b, w_out):
    B, S, D = x.shape
    assert norm_w.shape[0] == 1, "single layer"
    assert S % TM == 0 and S // SEL_BLOCK <= SEL_BLOCK and S >= WINDOW + TQ
    ncp = S // CMP_STRIDE
    nst = S // TM
    f32 = lambda a: a.astype(F32)

    w = f32(w_in[0])
    head_order = []
    for j in range(GROUP_HEADS):
        head_order += [j, GROUP_HEADS + j]
    wq = w[:, :D_NSA].reshape(D, NSA_HEADS, HEAD_DIM)[:, jnp.array(head_order), :].reshape(D, D_NSA)
    gate_cols = []
    for g in range(KV_GROUPS):
        for br in range(N_BRANCH):
            for r in range(GROUP_HEADS):
                gate_cols.append((g * GROUP_HEADS + r) * N_BRANCH + br)
    g0 = D_NSA + 6 * LANES
    wg = w[:, g0:g0 + NSA_HEADS * N_BRANCH][:, jnp.array(gate_cols)]
    wg = jnp.pad(wg, ((0, 0), (0, LANES - NSA_HEADS * N_BRANCH)))
    rest0 = g0 + NSA_HEADS * N_BRANCH
    w_cat = jnp.concatenate([wq, w[:, D_NSA:g0], wg, w[:, rest0:]], axis=1).astype(BF16)
    n_cat = w_cat.shape[1]

    rc, rsa, rsb = _rope_tables(jnp.arange(S))
    mbd = jnp.asarray(np.kron(np.eye(4), np.full((HEAD_DIM, HEAD_DIM), 1.0 / HEAD_DIM)), BF16)
    qnw = jnp.tile(f32(q_norm_w[0]), 4)[None, :]
    knw_sw = jnp.concatenate([jnp.tile(f32(k_norm_w[0, 1]), 2), jnp.tile(f32(k_norm_w[0, 2]), 2)])[None, :]
    knw_c = jnp.tile(f32(k_norm_w[0, 0]), 2)[None, :]

    cp = pltpu.CompilerParams(dimension_semantics=("arbitrary", "arbitrary"),
                              vmem_limit_bytes=VMEM_LIMIT)
    row_blk = lambda n: pl.BlockSpec((1, TM, n), lambda b, s: (b, s, 0))
    tab_blk = pl.BlockSpec((TM, LANES), lambda b, s: (s, 0))
    vt_blk = pl.BlockSpec((1, TM // VCHUNK, LANES, VCHUNK), lambda b, s: (b, s, 0, 0))
    sds = jax.ShapeDtypeStruct

    (qt4, kc_raw, vc_raw, ksf, vst, kw, vwt, gates, zs, oconv) = pl.pallas_call(
        _proj_kernel,
        grid=(B, nst),
        in_specs=[row_blk(D), _full((1, D)), _full((D, n_cat)), _full((1, 256)), _full((1, 256)),
                  tab_blk, tab_blk, tab_blk, _full((256, 256)), _full((3, D_CONV)), _full((1, D_CONV))],
        out_specs=[pl.BlockSpec((1, 4, LANES, TM), lambda b, s: (b, 0, 0, s)),
                   row_blk(LANES), row_blk(LANES), row_blk(2 * LANES), vt_blk, row_blk(LANES), vt_blk,
                   row_blk(LANES), row_blk(D_NSA), row_blk(D_CONV)],
        out_shape=[sds((B, 4, LANES, S), BF16), sds((B, S, LANES), F32), sds((B, S, LANES), F32),
                   sds((B, S, 2 * LANES), BF16), sds((B, S // VCHUNK, LANES, VCHUNK), BF16),
                   sds((B, S, LANES), BF16), sds((B, S // VCHUNK, LANES, VCHUNK), BF16),
                   sds((B, S, LANES), F32), sds((B, S, D_NSA), BF16), sds((B, S, D_CONV), BF16)],
        scratch_shapes=[pltpu.VMEM((TM + 16, D_CONV), F32)],
        compiler_params=cp,
        name="nsa_proj",
    )(x, f32(norm_w), w_cat, qnw, knw_sw, rc, rsa, rsb, mbd, f32(conv_w[0]), f32(conv_b))

    cmp_pos = jnp.arange(ncp) * CMP_STRIDE + (CMP_LEN - 1)
    crc, crsa, crsb = _rope_tables(cmp_pos)
    kpa, kpb, kw1a, kw1b, kb1, kw2 = _cmp_weights(f32(cmp_k_pos[0]), f32(cmp_k_w1[0]),
                                                  f32(cmp_k_b1[0]), f32(cmp_k_w2[0]))
    vpa, vpb, vw1a, vw1b, vb1, vw2 = _cmp_weights(f32(cmp_v_pos[0]), f32(cmp_v_w1[0]),
                                                  f32(cmp_v_b1[0]), f32(cmp_v_w2[0]))
    flat = CMP_STRIDE * LANES
    h_blk = pl.BlockSpec((1, ncp, flat), lambda b: (b, 0, 0))
    kc, vct = pl.pallas_call(
        functools.partial(_cmp_kernel, ncp=ncp),
        grid=(B,),
        in_specs=[h_blk, h_blk] + [_full((1, flat))] * 4
                 + [_full((flat, 256)), _full((flat, 256)), _full((1, 256)), _full((256, LANES))] * 2
                 + [_full((1, LANES)), _full((ncp, LANES)), _full((ncp, LANES)), _full((ncp, LANES)),
                    _full((256, 256))],
        out_specs=[pl.BlockSpec((1, ncp, LANES), lambda b: (b, 0, 0)),
                   pl.BlockSpec((1, LANES, ncp), lambda b: (b, 0, 0))],
        out_shape=[sds((B, ncp, LANES), BF16), sds((B, LANES, ncp), BF16)],
        compiler_params=pltpu.CompilerParams(dimension_semantics=("arbitrary",),
                                             vmem_limit_bytes=VMEM_LIMIT),
        name="nsa_compress",
    )(kc_raw.reshape(B, ncp, flat), vc_raw.reshape(B, ncp, flat), kpa, kpb, vpa, vpb,
      kw1a, kw1b, kb1, kw2, vw1a, vw1b, vb1, vw2, knw_c, crc, crsa, crsb, mbd)

    score_bound = (HEAD_DIM * Q_SCALE) * jnp.max(jnp.abs(f32(q_norm_w))) * jnp.max(jnp.abs(f32(k_norm_w)))
    nqt = S // TQ

    def attention(bounded):
        return pl.pallas_call(
            functools.partial(_attn_kernel, ncp=ncp, bounded=bounded),
            grid=(B, nqt),
            in_specs=[pl.BlockSpec((1, 4, LANES, TQ), lambda b, i: (b, 0, 0, i)),
                      pl.BlockSpec((1, ncp, LANES), lambda b, i: (b, 0, 0)),
                      pl.BlockSpec((1, LANES, ncp), lambda b, i: (b, 0, 0)),
                      pl.BlockSpec((1, S, 2 * LANES), lambda b, i: (b, 0, 0)),
                      pl.BlockSpec((1, S // VCHUNK, LANES, VCHUNK), lambda b, i: (b, 0, 0, 0)),
                      pl.BlockSpec((1, S, LANES), lambda b, i: (b, 0, 0)),
                      pl.BlockSpec((1, S // VCHUNK, LANES, VCHUNK), lambda b, i: (b, 0, 0, 0)),
                      pl.BlockSpec((1, TQ, LANES), lambda b, i: (b, i, 0)),
                      pl.BlockSpec((1, TQ, D_NSA), lambda b, i: (b, i, 0)),
                      _full((SEL_BLOCK, ncp))],
            out_specs=pl.BlockSpec((1, TQ, D_NSA), lambda b, i: (b, i, 0)),
            out_shape=sds((B, S, D_NSA), BF16),
            scratch_shapes=[pltpu.VMEM((KV_GROUPS, SEL_BLOCK, TQ), F32),
                            pltpu.VMEM((KV_GROUPS, LANES, TQ), BF16),
                            pltpu.VMEM((2, KV_GROUPS, TK, GROUP_HEADS * TQ), F32),
                            pltpu.VMEM((2, KV_GROUPS, 1, GROUP_HEADS * TQ), F32)],
            compiler_params=cp,
            name="nsa_attn_bounded" if bounded else "nsa_attn_online",
        )

    attn_args = (qt4, kc, vct, ksf, vst, kw, vwt, gates, zs, _overlap_t(ncp))
    o_nsa = lax.cond(score_bound <= MAX_SAFE_SCORE,
                     lambda *a: attention(True)(*a), lambda *a: attention(False)(*a), *attn_args)

    out = pl.pallas_call(
        _out_kernel,
        grid=(B, nst),
        in_specs=[row_blk(D), row_blk(D_NSA), row_blk(D_CONV), _full((D_NSA + D_CONV, D))],
        out_specs=row_blk(D),
        out_shape=sds((B, S, D), x.dtype),
        compiler_params=cp,
        name="nsa_out",
    )(x, o_nsa, oconv, f32(w_out[0]).astype(BF16))
    return out
```

```python
import functools

import numpy as np
import jax
import jax.numpy as jnp
from jax import lax
from jax.experimental import pallas as pl
from jax.experimental.pallas import tpu as pltpu

LANES = 128
SUBLANES = 8
HEAD_DIM = 64
NSA_HEADS = 8
KV_GROUPS = 2
GROUP_HEADS = NSA_HEADS // KV_GROUPS
D_NSA = NSA_HEADS * HEAD_DIM
D_CONV = 512
N_BRANCH = 3
ROT_DIM = HEAD_DIM // 4
ROPE_THETA = 500000.0
CMP_LEN = 32
CMP_STRIDE = 16
CMP_HIDDEN = 2 * HEAD_DIM
SEL_BLOCK = 64
SEL_TOPK = 16
WINDOW = 512
EPS = 1e-6
NEG_INF = -1e30
FORCE_SCORE = 1e9
SCALE = HEAD_DIM ** -0.5
Q_SCALE = SCALE * float(np.log2(np.e))

TM = 512
TM_OUT = 1024
TQ = 128
TK = 512
VCHUNK = 128
VMEM_LIMIT = 56 * 1024 * 1024
MAX_SAFE_SCORE = 50.0

BF16 = jnp.bfloat16
F32 = jnp.float32


def _dot(a, b):
    return jnp.dot(a, b, preferred_element_type=F32)


def _dot_nt(a, b):
    return lax.dot_general(a, b, (((1,), (1,)), ((), ())), preferred_element_type=F32)


def _split3(a):
    hi = a.astype(BF16)
    r1 = a - hi.astype(F32)
    mid = r1.astype(BF16)
    lo = (r1 - mid.astype(F32)).astype(BF16)
    return hi, mid, lo


def _group_mean(sq, mbd):
    return _dot(sq.astype(BF16), mbd)


def _rope(xn, c, sa, sb):
    return xn * c + pltpu.roll(xn, 8, 1) * sa + pltpu.roll(xn, LANES - 8, 1) * sb


def _silu(z):
    return z * (1.0 / (1.0 + jnp.exp(-z)))


def _proj_kernel(x_ref, nw_ref, wq_ref, wkv_ref, wg_ref, wr_ref, qnw_ref, knw_ref,
                 rc_ref, rsa_ref, rsb_ref, mbd_ref, cw_ref, cb_ref,
                 qt_ref, kc_ref, vc_ref, ksf_ref, vst_ref, kw_ref, vwt_ref,
                 gate_ref, zs_ref, oconv_ref, ubuf):
    si = pl.program_id(1)
    x = x_ref[0]
    ms = jnp.mean(x * x, axis=-1, keepdims=True)
    h = (x * lax.rsqrt(ms + EPS) * nw_ref[...]).astype(BF16)

    rc, rsa, rsb = rc_ref[...], rsa_ref[...], rsb_ref[...]
    mbd = mbd_ref[...]

    pq = _dot(h, wq_ref[...])
    for pair in range(2):
        blk = pq[:, pair * 256:(pair + 1) * 256]
        msq = _group_mean(blk * blk, mbd)
        qn = blk * lax.rsqrt(msq + EPS) * qnw_ref[...]
        for half in range(2):
            t = qn[:, half * LANES:(half + 1) * LANES]
            qt_ref[0, pair * 2 + half] = (_rope(t, rc, rsa, rsb) * Q_SCALE).T.astype(BF16)

    pkv = _dot(h, wkv_ref[...])
    kc_ref[0] = pkv[:, 0:128]
    vc_ref[0] = pkv[:, 128:256]
    ksw = jnp.concatenate([pkv[:, 256:384], pkv[:, 512:640]], axis=1)
    msk = _group_mean(ksw * ksw, mbd)
    kn = ksw * lax.rsqrt(msk + EPS) * knw_ref[...]
    ks = _rope(kn[:, 0:128], rc, rsa, rsb)
    kw = _rope(kn[:, 128:256], rc, rsa, rsb)
    row = lax.broadcasted_iota(jnp.int32, (TM, LANES), 0) + si * TM
    lane = lax.broadcasted_iota(jnp.int32, (TM, LANES), 1)
    onehot = jnp.where(lane == (row >> 6), 1.0, 0.0)
    ksf_ref[0, :, 0:128] = ks.astype(BF16)
    ksf_ref[0, :, 128:256] = onehot.astype(BF16)
    kw_ref[0] = kw.astype(BF16)
    vs = pkv[:, 384:512]
    vw = pkv[:, 640:768]
    for j in range(TM // VCHUNK):
        vst_ref[0, j] = vs[j * VCHUNK:(j + 1) * VCHUNK, :].T.astype(BF16)
        vwt_ref[0, j] = vw[j * VCHUNK:(j + 1) * VCHUNK, :].T.astype(BF16)

    gate_ref[0] = 1.0 / (1.0 + jnp.exp(-_dot(h, wg_ref[...])))
    zs_ref[0] = _silu(_dot(h, wr_ref[:, 0:D_NSA])).astype(BF16)

    pcv = _dot(h, wr_ref[:, D_NSA:D_NSA + 4 * D_CONV])
    u = pcv[:, 1024:1536] * pcv[:, 0:512]

    @pl.when(si == 0)
    def _():
        ubuf[0:8, :] = jnp.zeros((8, D_CONV), F32)

    ubuf[8:8 + TM, :] = u
    u1 = ubuf[7:7 + TM, :]
    u2 = ubuf[6:6 + TM, :]
    conv = cw_ref[0:1, :] * u2 + cw_ref[1:2, :] * u1 + cw_ref[2:3, :] * u + cb_ref[...]
    oconv_ref[0] = (pcv[:, 512:1024] * conv * _silu(pcv[:, 1536:2048])).astype(BF16)
    ubuf[0:8, :] = ubuf[TM:TM + 8, :]


def _cmp_kernel(hk_ref, hv_ref, pka_ref, pkb_ref, pva_ref, pvb_ref,
                wk1a_ref, wk1b_ref, bk1_ref, wk2_ref,
                wv1a_ref, wv1b_ref, bv1_ref, wv2_ref,
                knw_ref, rc_ref, rsa_ref, rsb_ref, mbd_ref,
                kc_ref, vct_ref, *, ncp):
    def first_layer(h_ref, p_ref, w_ref):
        acc = None
        for l in range(0, CMP_STRIDE, 2):
            lhs = jnp.concatenate([h_ref[0, pl.ds(l, ncp, stride=CMP_STRIDE), :],
                                   h_ref[0, pl.ds(l + 1, ncp, stride=CMP_STRIDE), :]], axis=1)
            lhs = (lhs + p_ref[:, l * LANES:(l + 2) * LANES]).astype(BF16)
            part = _dot(lhs, w_ref[l * LANES:(l + 2) * LANES, :])
            acc = part if acc is None else acc + part
        return acc

    def mlp(h_ref, pa_ref, pb_ref, w1a_ref, w1b_ref, b1_ref, w2_ref):
        p = first_layer(h_ref, pa_ref, w1a_ref)
        q = first_layer(h_ref, pb_ref, w1b_ref)
        pre = p + pltpu.roll(q, ncp - 1, 0) + b1_ref[...]
        return _dot(_silu(pre).astype(BF16), w2_ref[...])

    kc = mlp(hk_ref, pka_ref, pkb_ref, wk1a_ref, wk1b_ref, bk1_ref, wk2_ref)
    msk = _group_mean(kc * kc, mbd_ref[0:LANES, 0:LANES])
    kn = kc * lax.rsqrt(msk + EPS) * knw_ref[...]
    kc_ref[0] = _rope(kn, rc_ref[...], rsa_ref[...], rsb_ref[...]).astype(BF16)
    vc = mlp(hv_ref, pva_ref, pvb_ref, wv1a_ref, wv1b_ref, bv1_ref, wv2_ref)
    vct_ref[0] = vc.T.astype(BF16)


def _attn_kernel(qt_ref, kc_ref, vct_ref, ksf_ref, vst_ref, kw_ref, vwt_ref,
                 gate_ref, zs_ref, ovt_ref, o_ref, imp_sc, pen_sc, s_sc, cm_sc, *, ncp, bounded):
    i = pl.program_id(1)
    t0 = i * TQ
    rows = GROUP_HEADS * TQ
    nblk = SEL_BLOCK
    groups = range(KV_GROUPS)

    q_all = jnp.concatenate([qt_ref[0, j] for j in range(GROUP_HEADS)], axis=1)
    frow = lax.broadcasted_iota(jnp.int32, (LANES, rows), 0)
    qg = [jnp.where(frow < HEAD_DIM, q_all, 0), jnp.where(frow >= HEAD_DIM, q_all, 0)]
    colq = lax.broadcasted_iota(jnp.int32, (1, rows), 1)
    t_col = t0 + (colq & (TQ - 1))
    vrows = lambda v, g: v[g * HEAD_DIM:(g + 1) * HEAD_DIM, :]

    s_cmp = [_dot(kc_ref[0], qg[g]) for g in groups]

    c_idx = lax.broadcasted_iota(jnp.int32, (ncp, 1), 0)
    mask_c = ((c_idx * CMP_STRIDE + (CMP_LEN - 1)) <= t_col) & (c_idx < ncp - 1)
    p_c, o_c = [], []
    for g in groups:
        if bounded:
            e_c = jnp.where(mask_c, jnp.exp2(s_cmp[g]), 0.0)
        else:
            sm = jnp.where(mask_c, s_cmp[g], NEG_INF)
            e_c = jnp.where(mask_c, jnp.exp2(sm - jnp.max(sm, axis=0, keepdims=True)), 0.0)
        l_c = jnp.sum(e_c, axis=0, keepdims=True)
        p = e_c / jnp.where(l_c > 0.0, l_c, 1.0)
        p_c.append(p)
        o_c.append(_dot(vrows(vct_ref[0], g), p.astype(BF16)))

    n_idx = lax.broadcasted_iota(jnp.int32, (nblk, TQ), 0)
    tq = t0 + lax.broadcasted_iota(jnp.int32, (nblk, TQ), 1)
    cur = tq >> 6
    forced = (n_idx == 0) | (n_idx == cur) | (n_idx == cur - 1)
    visible = (n_idx << 6) <= tq
    sub = lax.broadcasted_iota(jnp.int32, (SUBLANES, TQ), 0)
    nvb = nblk // SUBLANES
    ovt = ovt_ref[...]
    for g in groups:
        psum = p_c[g][:, 0:TQ]
        for r in range(1, GROUP_HEADS):
            psum = psum + p_c[g][:, r * TQ:(r + 1) * TQ]
        hi, mid, lo = _split3(psum)
        imp = _dot(ovt, hi) + _dot(ovt, mid) + _dot(ovt, lo)
        imp = jnp.where(forced, FORCE_SCORE, jnp.where(visible, imp, -1.0))
        imp_sc[g] = imp
        imp_b = [imp[vb * SUBLANES:(vb + 1) * SUBLANES, :] for vb in range(nvb)]
        rank_b = [jnp.zeros((SUBLANES, TQ), F32) for _ in range(nvb)]
        for mm in range(nblk):
            a = jnp.broadcast_to(imp_sc[g, mm:mm + 1, :], (SUBLANES, TQ))
            for vb in range(nvb):
                if vb * SUBLANES > mm:
                    beats = a >= imp_b[vb]
                elif vb * SUBLANES + SUBLANES - 1 < mm:
                    beats = a > imp_b[vb]
                else:
                    beats = (a > imp_b[vb]) | ((a == imp_b[vb]) & (sub + vb * SUBLANES > mm))
                rank_b[vb] = rank_b[vb] + jnp.where(beats, 1.0, 0.0)
        rank = jnp.concatenate(rank_b, axis=0)
        pen_t = jnp.where(rank < float(SEL_TOPK), 0.0, NEG_INF)
        pen_sc[g, 0:nblk, :] = pen_t.astype(BF16)
        pen_sc[g, nblk:LANES, :] = jnp.zeros((LANES - nblk, TQ), BF16)

    nwc = WINDOW // VCHUNK + TQ // VCHUNK
    cw = jnp.maximum(i * (TQ // VCHUNK) - WINDOW // VCHUNK, 0)
    w0 = pl.multiple_of(cw * VCHUNK, VCHUNK)
    kwin = kw_ref[0, pl.ds(w0, nwc * VCHUNK), :]
    s_win = [_dot(kwin, qg[g]) for g in groups]

    q_aug = [jnp.concatenate([qg[g], jnp.concatenate([pen_sc[g]] * GROUP_HEADS, axis=1)], axis=0)
             for g in groups]

    def window_branch():
        vwin = jnp.concatenate([vwt_ref[0, cw + j] for j in range(nwc)], axis=1)
        kpos = w0 + lax.broadcasted_iota(jnp.int32, (nwc * VCHUNK, 1), 0)
        mask_w = (kpos <= t_col) & (kpos > t_col - WINDOW)
        o_w, l_w = [], []
        for g in groups:
            if bounded:
                p_w = jnp.where(mask_w, jnp.exp2(s_win[g]), 0.0)
            else:
                sw = jnp.where(mask_w, s_win[g], NEG_INF)
                p_w = jnp.exp2(sw - jnp.max(sw, axis=0, keepdims=True))
            l_w.append(jnp.sum(p_w, axis=0, keepdims=True))
            o_w.append(_dot(vrows(vwin, g), p_w.astype(BF16)))
        return o_w, l_w

    def v_chunk(c, g):
        cv = c * (TK // VCHUNK)
        return jnp.concatenate([vrows(vst_ref[0, cv + j], g) for j in range(TK // VCHUNK)], axis=1)

    def qk(c, slot, g):
        k0 = pl.multiple_of(c * TK, TK)
        s = _dot(ksf_ref[0, pl.ds(k0, TK), :], q_aug[g])
        s_sc[slot, g] = s
        if not bounded:
            cm_sc[slot, g] = jnp.max(s, axis=0, keepdims=True)

    def sm_pv(c, slot, g, state, masked):
        m_p, l_p, acc = state
        s = s_sc[slot, g]
        if masked:
            kpos = c * TK + lax.broadcasted_iota(jnp.int32, (TK, 1), 0)
            s = jnp.where(kpos <= t_col, s, NEG_INF)
        if bounded:
            p = jnp.exp2(s)
            return m_p, l_p + jnp.sum(p, axis=0, keepdims=True), acc + _dot(v_chunk(c, g), p.astype(BF16))
        cm = jnp.max(s, axis=0, keepdims=True) if masked else cm_sc[slot, g]
        m_n = jnp.maximum(m_p, cm)
        alpha = jnp.exp2(m_p - m_n)
        p = jnp.exp2(s - m_n)
        l_n = alpha * l_p + jnp.sum(p, axis=0, keepdims=True)
        acc = alpha * acc + _dot(v_chunk(c, g), p.astype(BF16))
        return m_n, l_n, acc

    def step(c, slot, carry):
        out = []
        for g in groups:
            qk(c + 1, 1 - slot, g)
            out.append(sm_pv(c, slot, g, carry[g], False))
        return tuple(out)

    def last(slot, carry):
        return tuple(sm_pv(n_full, slot, g, carry[g], True) for g in groups)

    n_full = t0 // TK
    for g in groups:
        qk(0, 0, g)
    o_w, l_w = window_branch()
    init = tuple((jnp.full((1, rows), 0.0 if bounded else NEG_INF, F32), jnp.zeros((1, rows), F32),
                  jnp.zeros((HEAD_DIM, rows), F32)) for _ in groups)
    carry = lax.fori_loop(0, n_full // 2, lambda h, cr: step(2 * h + 1, 1, step(2 * h, 0, cr)), init)
    carry = lax.cond((n_full & 1) == 1,
                     lambda cr: last(1, step(n_full - 1, 0, cr)),
                     lambda cr: last(0, cr), carry)
    sel_out = [(carry[g][1], carry[g][2]) for g in groups]

    g_t = gate_ref[0].T
    slabs = []
    for g in groups:
        def gate_row(br):
            base = g * (N_BRANCH * GROUP_HEADS) + br * GROUP_HEADS
            return jnp.concatenate([g_t[base + r:base + r + 1, :] for r in range(GROUP_HEADS)], axis=1)
        l_s, o_s = sel_out[g]
        og = o_c[g] * gate_row(0) + o_s * (gate_row(1) / l_s) + o_w[g] * (gate_row(2) / l_w[g])
        slabs += [og[:, r * TQ:(r + 1) * TQ] for r in range(GROUP_HEADS)]
    out_t = jnp.concatenate(slabs, axis=0)
    o_ref[0] = (out_t.T * zs_ref[0].astype(F32)).astype(BF16)


def _out_kernel(x_ref, on_ref, oc_ref, w_ref, o_ref):
    acc = _dot(on_ref[0], w_ref[0:D_NSA, :]) + _dot(oc_ref[0], w_ref[D_NSA:D_NSA + D_CONV, :])
    o_ref[0] = x_ref[0] + acc


def _rope_tables(pos):
    inv_freq = ROPE_THETA ** (-jnp.arange(0, ROT_DIM, 2, dtype=F32) / ROT_DIM)
    ang = pos.astype(F32)[:, None] * inv_freq[None, :]
    cos, sin = jnp.cos(ang), jnp.sin(ang)
    n = pos.shape[0]
    ones = jnp.ones((n, HEAD_DIM - ROT_DIM), F32)
    zeros = jnp.zeros((n, HEAD_DIM - ROT_DIM), F32)
    z8 = jnp.zeros((n, ROT_DIM // 2), F32)
    c = jnp.concatenate([cos, cos, ones], axis=1)
    sa = jnp.concatenate([z8, sin, zeros], axis=1)
    sb = jnp.concatenate([-sin, z8, zeros], axis=1)
    tile2 = lambda t: jnp.concatenate([t, t], axis=1)
    return tile2(c), tile2(sa), tile2(sb)


def _overlap_t(ncp):
    cs = np.arange(ncp) * CMP_STRIDE
    ce = cs + CMP_LEN
    ss = np.arange(SEL_BLOCK) * SEL_BLOCK
    se = ss + SEL_BLOCK
    ov = np.clip(np.minimum(ce[None, :], se[:, None]) - np.maximum(cs[None, :], ss[:, None]), 0, None)
    ov = ov.astype(np.float32) / CMP_LEN
    ov[:, ncp - 1] = 0.0
    return jnp.asarray(ov, BF16)


def _cmp_weights(pos, w1, b1, w2):
    half = CMP_STRIDE
    def big_w1(w):
        w = w.reshape(half, 1, HEAD_DIM, 1, CMP_HIDDEN)
        eye = jnp.eye(KV_GROUPS, dtype=F32).reshape(1, KV_GROUPS, 1, KV_GROUPS, 1)
        return (w * eye).reshape(half * KV_GROUPS * HEAD_DIM, KV_GROUPS * CMP_HIDDEN).astype(BF16)
    def big_pos(p):
        return jnp.broadcast_to(p[:, None, :], (half, KV_GROUPS, HEAD_DIM)).reshape(1, -1)
    w1a, w1b = w1[:half * HEAD_DIM], w1[half * HEAD_DIM:]
    eye2 = jnp.eye(KV_GROUPS, dtype=F32)
    w2b = (w2[None, :, None, :] * eye2[:, None, :, None]).reshape(KV_GROUPS * CMP_HIDDEN,
                                                                 KV_GROUPS * HEAD_DIM).astype(BF16)
    b1b = jnp.tile(b1, KV_GROUPS)[None, :]
    return big_pos(pos[:half]), big_pos(pos[half:]), big_w1(w1a), big_w1(w1b), b1b, w2b


def _full(shape):
    nd = len(shape)
    return pl.BlockSpec(shape, lambda *_: (0,) * nd)


def kernel(x, norm_w, w_in, q_norm_w, k_norm_w, cmp_k_pos, cmp_k_w1, cmp_k_b1, cmp_k_w2,
           cmp_v_pos, cmp_v_w1, cmp_v_b1, cmp_v_w2, conv_w, conv_b, w_out):
    B, S, D = x.shape
    assert norm_w.shape[0] == 1, "single layer"
    assert S % TM == 0 and S % TM_OUT == 0 and S // SEL_BLOCK <= SEL_BLOCK and S >= WINDOW + TQ
    ncp = S // CMP_STRIDE
    nst = S // TM
    f32 = lambda a: a.astype(F32)

    w = f32(w_in[0])
    head_order = []
    for j in range(GROUP_HEADS):
        head_order += [j, GROUP_HEADS + j]
    wq = w[:, :D_NSA].reshape(D, NSA_HEADS, HEAD_DIM)[:, jnp.array(head_order), :].reshape(D, D_NSA)
    gate_cols = []
    for g in range(KV_GROUPS):
        for br in range(N_BRANCH):
            for r in range(GROUP_HEADS):
                gate_cols.append((g * GROUP_HEADS + r) * N_BRANCH + br)
    g0 = D_NSA + 6 * LANES
    wg = w[:, g0:g0 + NSA_HEADS * N_BRANCH][:, jnp.array(gate_cols)]
    wg = jnp.pad(wg, ((0, 0), (0, LANES - NSA_HEADS * N_BRANCH)))
    rest0 = g0 + NSA_HEADS * N_BRANCH
    wq, wkv, wg, wr = (t.astype(BF16) for t in (wq, w[:, D_NSA:g0], wg, w[:, rest0:]))
    n_rest = wr.shape[1]

    rc, rsa, rsb = _rope_tables(jnp.arange(S))
    mbd = jnp.asarray(np.kron(np.eye(4), np.full((HEAD_DIM, HEAD_DIM), 1.0 / HEAD_DIM)), BF16)
    qnw = jnp.tile(f32(q_norm_w[0]), 4)[None, :]
    knw_sw = jnp.concatenate([jnp.tile(f32(k_norm_w[0, 1]), 2), jnp.tile(f32(k_norm_w[0, 2]), 2)])[None, :]
    knw_c = jnp.tile(f32(k_norm_w[0, 0]), 2)[None, :]

    cp = pltpu.CompilerParams(dimension_semantics=("arbitrary", "arbitrary"),
                              vmem_limit_bytes=VMEM_LIMIT)
    row_blk = lambda n: pl.BlockSpec((1, TM, n), lambda b, s: (b, s, 0))
    tab_blk = pl.BlockSpec((TM, LANES), lambda b, s: (s, 0))
    vt_blk = pl.BlockSpec((1, TM // VCHUNK, LANES, VCHUNK), lambda b, s: (b, s, 0, 0))
    sds = jax.ShapeDtypeStruct

    (qt4, kc_raw, vc_raw, ksf, vst, kw, vwt, gates, zs, oconv) = pl.pallas_call(
        _proj_kernel,
        grid=(B, nst),
        in_specs=[row_blk(D), _full((1, D)), _full((D, D_NSA)), _full((D, 6 * LANES)), _full((D, LANES)),
                  _full((D, n_rest)), _full((1, 256)), _full((1, 256)),
                  tab_blk, tab_blk, tab_blk, _full((256, 256)), _full((3, D_CONV)), _full((1, D_CONV))],
        out_specs=[pl.BlockSpec((1, 4, LANES, TM), lambda b, s: (b, 0, 0, s)),
                   row_blk(LANES), row_blk(LANES), row_blk(2 * LANES), vt_blk, row_blk(LANES), vt_blk,
                   row_blk(LANES), row_blk(D_NSA), row_blk(D_CONV)],
        out_shape=[sds((B, 4, LANES, S), BF16), sds((B, S, LANES), F32), sds((B, S, LANES), F32),
                   sds((B, S, 2 * LANES), BF16), sds((B, S // VCHUNK, LANES, VCHUNK), BF16),
                   sds((B, S, LANES), BF16), sds((B, S // VCHUNK, LANES, VCHUNK), BF16),
                   sds((B, S, LANES), F32), sds((B, S, D_NSA), BF16), sds((B, S, D_CONV), BF16)],
        scratch_shapes=[pltpu.VMEM((TM + 16, D_CONV), F32)],
        compiler_params=cp,
        name="nsa_proj",
    )(x, f32(norm_w), wq, wkv, wg, wr, qnw, knw_sw, rc, rsa, rsb, mbd, f32(conv_w[0]), f32(conv_b))

    cmp_pos = jnp.arange(ncp) * CMP_STRIDE + (CMP_LEN - 1)
    crc, crsa, crsb = _rope_tables(cmp_pos)
    kpa, kpb, kw1a, kw1b, kb1, kw2 = _cmp_weights(f32(cmp_k_pos[0]), f32(cmp_k_w1[0]),
                                                  f32(cmp_k_b1[0]), f32(cmp_k_w2[0]))
    vpa, vpb, vw1a, vw1b, vb1, vw2 = _cmp_weights(f32(cmp_v_pos[0]), f32(cmp_v_w1[0]),
                                                  f32(cmp_v_b1[0]), f32(cmp_v_w2[0]))
    flat = CMP_STRIDE * LANES
    h_blk = pl.BlockSpec((1, S, LANES), lambda b: (b, 0, 0))
    kc, vct = pl.pallas_call(
        functools.partial(_cmp_kernel, ncp=ncp),
        grid=(B,),
        in_specs=[h_blk, h_blk] + [_full((1, flat))] * 4
                 + [_full((flat, 256)), _full((flat, 256)), _full((1, 256)), _full((256, LANES))] * 2
                 + [_full((1, LANES)), _full((ncp, LANES)), _full((ncp, LANES)), _full((ncp, LANES)),
                    _full((256, 256))],
        out_specs=[pl.BlockSpec((1, ncp, LANES), lambda b: (b, 0, 0)),
                   pl.BlockSpec((1, LANES, ncp), lambda b: (b, 0, 0))],
        out_shape=[sds((B, ncp, LANES), BF16), sds((B, LANES, ncp), BF16)],
        compiler_params=pltpu.CompilerParams(dimension_semantics=("arbitrary",),
                                             vmem_limit_bytes=VMEM_LIMIT),
        name="nsa_compress",
    )(kc_raw, vc_raw, kpa, kpb, vpa, vpb,
      kw1a, kw1b, kb1, kw2, vw1a, vw1b, vb1, vw2, knw_c, crc, crsa, crsb, mbd)

    score_bound = (HEAD_DIM * Q_SCALE) * jnp.max(jnp.abs(f32(q_norm_w))) * jnp.max(jnp.abs(f32(k_norm_w)))
    nqt = S // TQ

    def attention(bounded):
        return pl.pallas_call(
            functools.partial(_attn_kernel, ncp=ncp, bounded=bounded),
            grid=(B, nqt),
            in_specs=[pl.BlockSpec((1, 4, LANES, TQ), lambda b, i: (b, 0, 0, i)),
                      pl.BlockSpec((1, ncp, LANES), lambda b, i: (b, 0, 0)),
                      pl.BlockSpec((1, LANES, ncp), lambda b, i: (b, 0, 0)),
                      pl.BlockSpec((1, S, 2 * LANES), lambda b, i: (b, 0, 0)),
                      pl.BlockSpec((1, S // VCHUNK, LANES, VCHUNK), lambda b, i: (b, 0, 0, 0)),
                      pl.BlockSpec((1, S, LANES), lambda b, i: (b, 0, 0)),
                      pl.BlockSpec((1, S // VCHUNK, LANES, VCHUNK), lambda b, i: (b, 0, 0, 0)),
                      pl.BlockSpec((1, TQ, LANES), lambda b, i: (b, i, 0)),
                      pl.BlockSpec((1, TQ, D_NSA), lambda b, i: (b, i, 0)),
                      _full((SEL_BLOCK, ncp))],
            out_specs=pl.BlockSpec((1, TQ, D_NSA), lambda b, i: (b, i, 0)),
            out_shape=sds((B, S, D_NSA), BF16),
            scratch_shapes=[pltpu.VMEM((KV_GROUPS, SEL_BLOCK, TQ), F32),
                            pltpu.VMEM((KV_GROUPS, LANES, TQ), BF16),
                            pltpu.VMEM((2, KV_GROUPS, TK, GROUP_HEADS * TQ), F32),
                            pltpu.VMEM((2, KV_GROUPS, 1, GROUP_HEADS * TQ), F32)],
            compiler_params=cp,
            name="nsa_attn_bounded" if bounded else "nsa_attn_online",
        )

    attn_args = (qt4, kc, vct, ksf, vst, kw, vwt, gates, zs, _overlap_t(ncp))
    o_nsa = lax.cond(score_bound <= MAX_SAFE_SCORE,
                     lambda *a: attention(True)(*a), lambda *a: attention(False)(*a), *attn_args)

    out_blk = lambda n: pl.BlockSpec((1, TM_OUT, n), lambda b, s: (b, s, 0))
    out = pl.pallas_call(
        _out_kernel,
        grid=(B, S // TM_OUT),
        in_specs=[out_blk(D), out_blk(D_NSA), out_blk(D_CONV), _full((D_NSA + D_CONV, D))],
        out_specs=out_blk(D),
        out_shape=sds((B, S, D), x.dtype),
        compiler_params=cp,
        name="nsa_out",
    )(x, o_nsa, oconv, f32(w_out[0]).astype(BF16))
    return out
```

```python
import functools

import numpy as np
import jax
import jax.numpy as jnp
from jax import lax
from jax.experimental import pallas as pl
from jax.experimental.pallas import tpu as pltpu

LANES = 128
SUBLANES = 8
HEAD_DIM = 64
NSA_HEADS = 8
KV_GROUPS = 2
GROUP_HEADS = NSA_HEADS // KV_GROUPS
D_NSA = NSA_HEADS * HEAD_DIM
D_CONV = 512
N_BRANCH = 3
ROT_DIM = HEAD_DIM // 4
ROPE_THETA = 500000.0
CMP_LEN = 32
CMP_STRIDE = 16
CMP_HIDDEN = 2 * HEAD_DIM
SEL_BLOCK = 64
SEL_TOPK = 16
WINDOW = 512
EPS = 1e-6
NEG_INF = -1e30
FORCE_SCORE = 1e9
SCALE = HEAD_DIM ** -0.5
Q_SCALE = SCALE * float(np.log2(np.e))

TM = 512
TM_OUT = 1024
TQ = 128
TK = 512
VCHUNK = 128
VMEM_LIMIT = 56 * 1024 * 1024
MAX_SAFE_SCORE = 50.0

BF16 = jnp.bfloat16
F32 = jnp.float32


def _dot(a, b):
    return jnp.dot(a, b, preferred_element_type=F32)


def _dot_nt(a, b):
    return lax.dot_general(a, b, (((1,), (1,)), ((), ())), preferred_element_type=F32)


def _split3(a):
    hi = a.astype(BF16)
    r1 = a - hi.astype(F32)
    mid = r1.astype(BF16)
    lo = (r1 - mid.astype(F32)).astype(BF16)
    return hi, mid, lo


def _group_mean(sq, mbd):
    return _dot(sq.astype(BF16), mbd)


def _rope(xn, c, sa, sb):
    return xn * c + pltpu.roll(xn, 8, 1) * sa + pltpu.roll(xn, LANES - 8, 1) * sb


def _silu(z):
    return z * (1.0 / (1.0 + jnp.exp(-z)))


def _proj_kernel(x_ref, nw_ref, wq_ref, wkv_ref, wg_ref, wr_ref, qnw_ref, knw_ref,
                 rc_ref, rsa_ref, rsb_ref, mbd_ref, cw_ref, cb_ref,
                 qt_ref, kc_ref, vc_ref, ksf_ref, vst_ref, kw_ref, vwt_ref,
                 gate_ref, zs_ref, oconv_ref, ubuf):
    si = pl.program_id(1)

    @pl.when(si == 0)
    def _():
        ubuf[0:8, :] = jnp.zeros((8, D_CONV), F32)

    x = x_ref[0]
    ms = jnp.mean(x * x, axis=-1, keepdims=True)
    h = (x * lax.rsqrt(ms + EPS) * nw_ref[...]).astype(BF16)

    rc, rsa, rsb = rc_ref[...], rsa_ref[...], rsb_ref[...]
    mbd = mbd_ref[...]

    pq = _dot(h, wq_ref[...])
    for pair in range(2):
        blk = pq[:, pair * 256:(pair + 1) * 256]
        msq = _group_mean(blk * blk, mbd)
        qn = blk * lax.rsqrt(msq + EPS) * qnw_ref[...]
        for half in range(2):
            t = qn[:, half * LANES:(half + 1) * LANES]
            qt_ref[0, pair * 2 + half] = (_rope(t, rc, rsa, rsb) * Q_SCALE).T.astype(BF16)

    pkv = _dot(h, wkv_ref[...])
    kc_ref[0] = pkv[:, 0:128]
    vc_ref[0] = pkv[:, 128:256]
    ksw = jnp.concatenate([pkv[:, 256:384], pkv[:, 512:640]], axis=1)
    msk = _group_mean(ksw * ksw, mbd)
    kn = ksw * lax.rsqrt(msk + EPS) * knw_ref[...]
    ks = _rope(kn[:, 0:128], rc, rsa, rsb)
    kw = _rope(kn[:, 128:256], rc, rsa, rsb)
    row = lax.broadcasted_iota(jnp.int32, (TM, LANES), 0) + si * TM
    lane = lax.broadcasted_iota(jnp.int32, (TM, LANES), 1)
    onehot = jnp.where(lane == (row >> 6), 1.0, 0.0)
    ksf_ref[0, :, 0:128] = ks.astype(BF16)
    ksf_ref[0, :, 128:256] = onehot.astype(BF16)
    kw_ref[0] = kw.astype(BF16)
    vs = pkv[:, 384:512]
    vw = pkv[:, 640:768]
    for j in range(TM // VCHUNK):
        vst_ref[0, j] = vs[j * VCHUNK:(j + 1) * VCHUNK, :].T.astype(BF16)
        vwt_ref[0, j] = vw[j * VCHUNK:(j + 1) * VCHUNK, :].T.astype(BF16)

    gate_ref[0] = 1.0 / (1.0 + jnp.exp(-_dot(h, wg_ref[...])))
    zs_ref[0] = _silu(_dot(h, wr_ref[:, 0:D_NSA])).astype(BF16)

    pcv = _dot(h, wr_ref[:, D_NSA:D_NSA + 4 * D_CONV])
    u = pcv[:, 1024:1536] * pcv[:, 0:512]
    ubuf[8:8 + TM, :] = u
    u1 = ubuf[7:7 + TM, :]
    u2 = ubuf[6:6 + TM, :]
    conv = cw_ref[0:1, :] * u2 + cw_ref[1:2, :] * u1 + cw_ref[2:3, :] * u + cb_ref[...]
    oconv_ref[0] = (pcv[:, 512:1024] * conv * _silu(pcv[:, 1536:2048])).astype(BF16)
    ubuf[0:8, :] = ubuf[TM:TM + 8, :]


def _cmp_kernel(hk_ref, hv_ref, pka_ref, pkb_ref, pva_ref, pvb_ref,
                wk1a_ref, wk1b_ref, bk1_ref, wk2_ref,
                wv1a_ref, wv1b_ref, bv1_ref, wv2_ref,
                knw_ref, rc_ref, rsa_ref, rsb_ref, mbd_ref,
                kc_ref, vct_ref, *, ncp):
    def first_layer(h_ref, p_ref, w_ref):
        acc = None
        for l in range(0, CMP_STRIDE, 2):
            lhs = jnp.concatenate([h_ref[0, pl.ds(l, ncp, stride=CMP_STRIDE), :],
                                   h_ref[0, pl.ds(l + 1, ncp, stride=CMP_STRIDE), :]], axis=1)
            lhs = (lhs + p_ref[:, l * LANES:(l + 2) * LANES]).astype(BF16)
            part = _dot(lhs, w_ref[l * LANES:(l + 2) * LANES, :])
            acc = part if acc is None else acc + part
        return acc

    def mlp(h_ref, pa_ref, pb_ref, w1a_ref, w1b_ref, b1_ref, w2_ref):
        p = first_layer(h_ref, pa_ref, w1a_ref)
        q = first_layer(h_ref, pb_ref, w1b_ref)
        pre = p + pltpu.roll(q, ncp - 1, 0) + b1_ref[...]
        return _dot(_silu(pre).astype(BF16), w2_ref[...])

    kc = mlp(hk_ref, pka_ref, pkb_ref, wk1a_ref, wk1b_ref, bk1_ref, wk2_ref)
    msk = _group_mean(kc * kc, mbd_ref[0:LANES, 0:LANES])
    kn = kc * lax.rsqrt(msk + EPS) * knw_ref[...]
    kc_ref[0] = _rope(kn, rc_ref[...], rsa_ref[...], rsb_ref[...]).astype(BF16)
    vc = mlp(hv_ref, pva_ref, pvb_ref, wv1a_ref, wv1b_ref, bv1_ref, wv2_ref)
    vct_ref[0] = vc.T.astype(BF16)


def _attn_kernel(qt_ref, kc_ref, vct_ref, ksf_ref, vst_ref, kw_ref, vwt_ref,
                 gate_ref, zs_ref, ovt_ref, o_ref, imp_sc, pen_sc, s_sc, cm_sc, *, ncp, bounded):
    i = pl.program_id(1)
    t0 = i * TQ
    rows = GROUP_HEADS * TQ
    nblk = SEL_BLOCK
    groups = range(KV_GROUPS)

    q_all = jnp.concatenate([qt_ref[0, j] for j in range(GROUP_HEADS)], axis=1)
    frow = lax.broadcasted_iota(jnp.int32, (LANES, rows), 0)
    qg = [jnp.where(frow < HEAD_DIM, q_all, 0), jnp.where(frow >= HEAD_DIM, q_all, 0)]
    t_q = t0 + lax.broadcasted_iota(jnp.int32, (1, TQ), 1)

    def fill_where(a, mask, fill):
        return jnp.concatenate([jnp.where(mask, a[:, r * TQ:(r + 1) * TQ], fill)
                                for r in range(GROUP_HEADS)], axis=1)

    vrows = lambda v, g: v[g * HEAD_DIM:(g + 1) * HEAD_DIM, :]

    s_cmp = [_dot(kc_ref[0], qg[g]) for g in groups]

    c_idx = lax.broadcasted_iota(jnp.int32, (ncp, 1), 0)
    mask_c = ((c_idx * CMP_STRIDE + (CMP_LEN - 1)) <= t_q) & (c_idx < ncp - 1)
    p_c, o_c = [], []
    for g in groups:
        if bounded:
            e_c = fill_where(jnp.exp2(s_cmp[g]), mask_c, 0.0)
        else:
            sm = fill_where(s_cmp[g], mask_c, NEG_INF)
            e_c = fill_where(jnp.exp2(sm - jnp.max(sm, axis=0, keepdims=True)), mask_c, 0.0)
        l_c = jnp.sum(e_c, axis=0, keepdims=True)
        p = e_c / jnp.where(l_c > 0.0, l_c, 1.0)
        p_c.append(p)
        o_c.append(_dot(vrows(vct_ref[0], g), p.astype(BF16)))

    n_idx = lax.broadcasted_iota(jnp.int32, (nblk, TQ), 0)
    tq = t0 + lax.broadcasted_iota(jnp.int32, (nblk, TQ), 1)
    cur = tq >> 6
    forced = (n_idx == 0) | (n_idx == cur) | (n_idx == cur - 1)
    visible = (n_idx << 6) <= tq
    sub = lax.broadcasted_iota(jnp.int32, (SUBLANES, TQ), 0)
    nvb = nblk // SUBLANES
    ovt = ovt_ref[...]
    for g in groups:
        psum = p_c[g][:, 0:TQ]
        for r in range(1, GROUP_HEADS):
            psum = psum + p_c[g][:, r * TQ:(r + 1) * TQ]
        hi, mid, lo = _split3(psum)
        imp = _dot(ovt, hi) + _dot(ovt, mid) + _dot(ovt, lo)
        imp = jnp.where(forced, FORCE_SCORE, jnp.where(visible, imp, -1.0))
        imp_sc[g] = imp
        imp_b = [imp[vb * SUBLANES:(vb + 1) * SUBLANES, :] for vb in range(nvb)]
        rank_b = [jnp.zeros((SUBLANES, TQ), F32) for _ in range(nvb)]
        for mm in range(nblk):
            a = jnp.broadcast_to(imp_sc[g, mm:mm + 1, :], (SUBLANES, TQ))
            for vb in range(nvb):
                if vb * SUBLANES > mm:
                    beats = a >= imp_b[vb]
                elif vb * SUBLANES + SUBLANES - 1 < mm:
                    beats = a > imp_b[vb]
                else:
                    beats = (a > imp_b[vb]) | ((a == imp_b[vb]) & (sub + vb * SUBLANES > mm))
                rank_b[vb] = rank_b[vb] + jnp.where(beats, 1.0, 0.0)
        rank = jnp.concatenate(rank_b, axis=0)
        pen_t = jnp.where(rank < float(SEL_TOPK), 0.0, NEG_INF)
        pen_sc[g, 0:nblk, :] = pen_t.astype(BF16)
        pen_sc[g, nblk:LANES, :] = jnp.zeros((LANES - nblk, TQ), BF16)

    nwc = WINDOW // VCHUNK + TQ // VCHUNK
    cw = jnp.maximum(i * (TQ // VCHUNK) - WINDOW // VCHUNK, 0)
    w0 = pl.multiple_of(cw * VCHUNK, VCHUNK)
    kwin = kw_ref[0, pl.ds(w0, nwc * VCHUNK), :]
    s_win = [_dot(kwin, qg[g]) for g in groups]

    q_aug = [jnp.concatenate([qg[g], jnp.concatenate([pen_sc[g]] * GROUP_HEADS, axis=1)], axis=0)
             for g in groups]

    def window_branch():
        vwin = jnp.concatenate([vwt_ref[0, cw + j] for j in range(nwc)], axis=1)
        kpos = w0 + lax.broadcasted_iota(jnp.int32, (nwc * VCHUNK, 1), 0)
        mask_w = (kpos <= t_q) & (kpos > t_q - WINDOW)
        o_w, l_w = [], []
        for g in groups:
            if bounded:
                p_w = fill_where(jnp.exp2(s_win[g]), mask_w, 0.0)
            else:
                sw = fill_where(s_win[g], mask_w, NEG_INF)
                p_w = jnp.exp2(sw - jnp.max(sw, axis=0, keepdims=True))
            l_w.append(jnp.sum(p_w, axis=0, keepdims=True))
            o_w.append(_dot(vrows(vwin, g), p_w.astype(BF16)))
        return o_w, l_w

    def v_chunk(c, g):
        cv = c * (TK // VCHUNK)
        return jnp.concatenate([vrows(vst_ref[0, cv + j], g) for j in range(TK // VCHUNK)], axis=1)

    def qk(c, slot, g):
        k0 = pl.multiple_of(c * TK, TK)
        s = _dot(ksf_ref[0, pl.ds(k0, TK), :], q_aug[g])
        s_sc[slot, g] = s
        if not bounded:
            cm_sc[slot, g] = jnp.max(s, axis=0, keepdims=True)

    def sm_pv(c, slot, g, state, masked):
        m_p, l_p, acc = state
        s = s_sc[slot, g]
        if masked:
            kpos = c * TK + lax.broadcasted_iota(jnp.int32, (TK, 1), 0)
            s = fill_where(s, kpos <= t_q, NEG_INF)
        if bounded:
            p = jnp.exp2(s)
            return m_p, l_p + jnp.sum(p, axis=0, keepdims=True), acc + _dot(v_chunk(c, g), p.astype(BF16))
        cm = jnp.max(s, axis=0, keepdims=True) if masked else cm_sc[slot, g]
        m_n = jnp.maximum(m_p, cm)
        alpha = jnp.exp2(m_p - m_n)
        p = jnp.exp2(s - m_n)
        l_n = alpha * l_p + jnp.sum(p, axis=0, keepdims=True)
        acc = alpha * acc + _dot(v_chunk(c, g), p.astype(BF16))
        return m_n, l_n, acc

    def step(c, slot, carry):
        out = []
        for g in groups:
            qk(c + 1, 1 - slot, g)
            out.append(sm_pv(c, slot, g, carry[g], False))
        return tuple(out)

    def last(slot, carry):
        return tuple(sm_pv(n_full, slot, g, carry[g], True) for g in groups)

    n_full = t0 // TK
    for g in groups:
        qk(0, 0, g)
    o_w, l_w = window_branch()
    init = tuple((jnp.full((1, rows), 0.0 if bounded else NEG_INF, F32), jnp.zeros((1, rows), F32),
                  jnp.zeros((HEAD_DIM, rows), F32)) for _ in groups)
    carry = lax.fori_loop(0, n_full // 2, lambda h, cr: step(2 * h + 1, 1, step(2 * h, 0, cr)), init)
    carry = lax.cond((n_full & 1) == 1,
                     lambda cr: last(1, step(n_full - 1, 0, cr)),
                     lambda cr: last(0, cr), carry)
    sel_out = [(carry[g][1], carry[g][2]) for g in groups]

    g_t = gate_ref[0].T
    slabs = []
    for g in groups:
        def gate_row(br):
            base = g * (N_BRANCH * GROUP_HEADS) + br * GROUP_HEADS
            return jnp.concatenate([g_t[base + r:base + r + 1, :] for r in range(GROUP_HEADS)], axis=1)
        l_s, o_s = sel_out[g]
        og = o_c[g] * gate_row(0) + o_s * (gate_row(1) / l_s) + o_w[g] * (gate_row(2) / l_w[g])
        slabs += [og[:, r * TQ:(r + 1) * TQ] for r in range(GROUP_HEADS)]
    out_t = jnp.concatenate(slabs, axis=0)
    o_ref[0] = (out_t.T * zs_ref[0].astype(F32)).astype(BF16)


def _out_kernel(x_ref, on_ref, oc_ref, w_ref, o_ref):
    acc = _dot(on_ref[0], w_ref[0:D_NSA, :]) + _dot(oc_ref[0], w_ref[D_NSA:D_NSA + D_CONV, :])
    o_ref[0] = x_ref[0] + acc


def _rope_tables(pos):
    inv_freq = (np.float32(ROPE_THETA) ** (-np.arange(0, ROT_DIM, 2, dtype=np.float32) / ROT_DIM)).astype(np.float32)
    ang = pos.astype(np.float32)[:, None] * inv_freq[None, :]
    cos, sin = np.cos(ang), np.sin(ang)
    n = pos.shape[0]
    ones = np.ones((n, HEAD_DIM - ROT_DIM), np.float32)
    zeros = np.zeros((n, HEAD_DIM - ROT_DIM), np.float32)
    z8 = np.zeros((n, ROT_DIM // 2), np.float32)
    c = np.concatenate([cos, cos, ones], axis=1)
    sa = np.concatenate([z8, sin, zeros], axis=1)
    sb = np.concatenate([-sin, z8, zeros], axis=1)
    return tuple(jnp.asarray(np.concatenate([t, t], axis=1), F32) for t in (c, sa, sb))


def _overlap_t(ncp):
    cs = np.arange(ncp) * CMP_STRIDE
    ce = cs + CMP_LEN
    ss = np.arange(SEL_BLOCK) * SEL_BLOCK
    se = ss + SEL_BLOCK
    ov = np.clip(np.minimum(ce[None, :], se[:, None]) - np.maximum(cs[None, :], ss[:, None]), 0, None)
    ov = ov.astype(np.float32) / CMP_LEN
    ov[:, ncp - 1] = 0.0
    return jnp.asarray(ov, BF16)


def _cmp_weights(pos, w1, b1, w2):
    half = CMP_STRIDE
    def big_w1(w):
        w = w.reshape(half, 1, HEAD_DIM, 1, CMP_HIDDEN)
        eye = jnp.eye(KV_GROUPS, dtype=F32).reshape(1, KV_GROUPS, 1, KV_GROUPS, 1)
        return (w * eye).reshape(half * KV_GROUPS * HEAD_DIM, KV_GROUPS * CMP_HIDDEN).astype(BF16)
    def big_pos(p):
        return jnp.broadcast_to(p[:, None, :], (half, KV_GROUPS, HEAD_DIM)).reshape(1, -1)
    w1a, w1b = w1[:half * HEAD_DIM], w1[half * HEAD_DIM:]
    eye2 = jnp.eye(KV_GROUPS, dtype=F32)
    w2b = (w2[None, :, None, :] * eye2[:, None, :, None]).reshape(KV_GROUPS * CMP_HIDDEN,
                                                                 KV_GROUPS * HEAD_DIM).astype(BF16)
    b1b = jnp.tile(b1, KV_GROUPS)[None, :]
    return big_pos(pos[:half]), big_pos(pos[half:]), big_w1(w1a), big_w1(w1b), b1b, w2b


def _full(shape):
    nd = len(shape)
    return pl.BlockSpec(shape, lambda *_: (0,) * nd)


def kernel(x, norm_w, w_in, q_norm_w, k_norm_w, cmp_k_pos, cmp_k_w1, cmp_k_b1, cmp_k_w2,
           cmp_v_pos, cmp_v_w1, cmp_v_b1, cmp_v_w2, conv_w, conv_b, w_out):
    B, S, D = x.shape
    assert norm_w.shape[0] == 1, "single layer"
    assert S % TM == 0 and S % TM_OUT == 0 and S // SEL_BLOCK <= SEL_BLOCK and S >= WINDOW + TQ
    ncp = S // CMP_STRIDE
    nst = S // TM
    f32 = lambda a: a.astype(F32)

    w = f32(w_in[0])
    head_order = []
    for j in range(GROUP_HEADS):
        head_order += [j, GROUP_HEADS + j]
    wq = w[:, :D_NSA].reshape(D, NSA_HEADS, HEAD_DIM)[:, jnp.array(head_order), :].reshape(D, D_NSA)
    gate_cols = []
    for g in range(KV_GROUPS):
        for br in range(N_BRANCH):
            for r in range(GROUP_HEADS):
                gate_cols.append((g * GROUP_HEADS + r) * N_BRANCH + br)
    g0 = D_NSA + 6 * LANES
    wg = w[:, g0:g0 + NSA_HEADS * N_BRANCH][:, jnp.array(gate_cols)]
    wg = jnp.pad(wg, ((0, 0), (0, LANES - NSA_HEADS * N_BRANCH)))
    rest0 = g0 + NSA_HEADS * N_BRANCH
    wq, wkv, wg, wr = (t.astype(BF16) for t in (wq, w[:, D_NSA:g0], wg, w[:, rest0:]))
    n_rest = wr.shape[1]

    rc, rsa, rsb = _rope_tables(np.arange(S))
    mbd = jnp.asarray(np.kron(np.eye(4), np.full((HEAD_DIM, HEAD_DIM), 1.0 / HEAD_DIM)), BF16)
    qnw = jnp.tile(f32(q_norm_w[0]), 4)[None, :]
    knw_sw = jnp.concatenate([jnp.tile(f32(k_norm_w[0, 1]), 2), jnp.tile(f32(k_norm_w[0, 2]), 2)])[None, :]
    knw_c = jnp.tile(f32(k_norm_w[0, 0]), 2)[None, :]

    cp = pltpu.CompilerParams(dimension_semantics=("arbitrary", "arbitrary"),
                              vmem_limit_bytes=VMEM_LIMIT)
    row_blk = lambda n: pl.BlockSpec((1, TM, n), lambda b, s: (b, s, 0))
    tab_blk = pl.BlockSpec((TM, LANES), lambda b, s: (s, 0))
    vt_blk = pl.BlockSpec((1, TM // VCHUNK, LANES, VCHUNK), lambda b, s: (b, s, 0, 0))
    sds = jax.ShapeDtypeStruct

    (qt4, kc_raw, vc_raw, ksf, vst, kw, vwt, gates, zs, oconv) = pl.pallas_call(
        _proj_kernel,
        grid=(B, nst),
        in_specs=[row_blk(D), _full((1, D)), _full((D, D_NSA)), _full((D, 6 * LANES)), _full((D, LANES)),
                  _full((D, n_rest)), _full((1, 256)), _full((1, 256)),
                  tab_blk, tab_blk, tab_blk, _full((256, 256)), _full((3, D_CONV)), _full((1, D_CONV))],
        out_specs=[pl.BlockSpec((1, 4, LANES, TM), lambda b, s: (b, 0, 0, s)),
                   row_blk(LANES), row_blk(LANES), row_blk(2 * LANES), vt_blk, row_blk(LANES), vt_blk,
                   row_blk(LANES), row_blk(D_NSA), row_blk(D_CONV)],
        out_shape=[sds((B, 4, LANES, S), BF16), sds((B, S, LANES), F32), sds((B, S, LANES), F32),
                   sds((B, S, 2 * LANES), BF16), sds((B, S // VCHUNK, LANES, VCHUNK), BF16),
                   sds((B, S, LANES), BF16), sds((B, S // VCHUNK, LANES, VCHUNK), BF16),
                   sds((B, S, LANES), F32), sds((B, S, D_NSA), BF16), sds((B, S, D_CONV), BF16)],
        scratch_shapes=[pltpu.VMEM((TM + 16, D_CONV), F32)],
        compiler_params=cp,
        name="nsa_proj",
    )(x, f32(norm_w), wq, wkv, wg, wr, qnw, knw_sw, rc, rsa, rsb, mbd, f32(conv_w[0]), f32(conv_b))

    cmp_pos = np.arange(ncp) * CMP_STRIDE + (CMP_LEN - 1)
    crc, crsa, crsb = _rope_tables(cmp_pos)
    kpa, kpb, kw1a, kw1b, kb1, kw2 = _cmp_weights(f32(cmp_k_pos[0]), f32(cmp_k_w1[0]),
                                                  f32(cmp_k_b1[0]), f32(cmp_k_w2[0]))
    vpa, vpb, vw1a, vw1b, vb1, vw2 = _cmp_weights(f32(cmp_v_pos[0]), f32(cmp_v_w1[0]),
                                                  f32(cmp_v_b1[0]), f32(cmp_v_w2[0]))
    flat = CMP_STRIDE * LANES
    h_blk = pl.BlockSpec((1, S, LANES), lambda b: (b, 0, 0))
    kc, vct = pl.pallas_call(
        functools.partial(_cmp_kernel, ncp=ncp),
        grid=(B,),
        in_specs=[h_blk, h_blk] + [_full((1, flat))] * 4
                 + [_full((flat, 256)), _full((flat, 256)), _full((1, 256)), _full((256, LANES))] * 2
                 + [_full((1, LANES)), _full((ncp, LANES)), _full((ncp, LANES)), _full((ncp, LANES)),
                    _full((256, 256))],
        out_specs=[pl.BlockSpec((1, ncp, LANES), lambda b: (b, 0, 0)),
                   pl.BlockSpec((1, LANES, ncp), lambda b: (b, 0, 0))],
        out_shape=[sds((B, ncp, LANES), BF16), sds((B, LANES, ncp), BF16)],
        compiler_params=pltpu.CompilerParams(dimension_semantics=("arbitrary",),
                                             vmem_limit_bytes=VMEM_LIMIT),
        name="nsa_compress",
    )(kc_raw, vc_raw, kpa, kpb, vpa, vpb,
      kw1a, kw1b, kb1, kw2, vw1a, vw1b, vb1, vw2, knw_c, crc, crsa, crsb, mbd)

    score_bound = (HEAD_DIM * Q_SCALE) * jnp.max(jnp.abs(f32(q_norm_w))) * jnp.max(jnp.abs(f32(k_norm_w)))
    nqt = S // TQ

    def attention(bounded):
        return pl.pallas_call(
            functools.partial(_attn_kernel, ncp=ncp, bounded=bounded),
            grid=(B, nqt),
            in_specs=[pl.BlockSpec((1, 4, LANES, TQ), lambda b, i: (b, 0, 0, i)),
                      pl.BlockSpec((1, ncp, LANES), lambda b, i: (b, 0, 0)),
                      pl.BlockSpec((1, LANES, ncp), lambda b, i: (b, 0, 0)),
                      pl.BlockSpec((1, S, 2 * LANES), lambda b, i: (b, 0, 0)),
                      pl.BlockSpec((1, S // VCHUNK, LANES, VCHUNK), lambda b, i: (b, 0, 0, 0)),
                      pl.BlockSpec((1, S, LANES), lambda b, i: (b, 0, 0)),
                      pl.BlockSpec((1, S // VCHUNK, LANES, VCHUNK), lambda b, i: (b, 0, 0, 0)),
                      pl.BlockSpec((1, TQ, LANES), lambda b, i: (b, i, 0)),
                      pl.BlockSpec((1, TQ, D_NSA), lambda b, i: (b, i, 0)),
                      _full((SEL_BLOCK, ncp))],
            out_specs=pl.BlockSpec((1, TQ, D_NSA), lambda b, i: (b, i, 0)),
            out_shape=sds((B, S, D_NSA), BF16),
            scratch_shapes=[pltpu.VMEM((KV_GROUPS, SEL_BLOCK, TQ), F32),
                            pltpu.VMEM((KV_GROUPS, LANES, TQ), BF16),
                            pltpu.VMEM((2, KV_GROUPS, TK, GROUP_HEADS * TQ), F32),
                            pltpu.VMEM((2, KV_GROUPS, 1, GROUP_HEADS * TQ), F32)],
            compiler_params=cp,
            name="nsa_attn_bounded" if bounded else "nsa_attn_online",
        )

    attn_args = (qt4, kc, vct, ksf, vst, kw, vwt, gates, zs, _overlap_t(ncp))
    o_nsa = lax.cond(score_bound <= MAX_SAFE_SCORE,
                     lambda *a: attention(True)(*a), lambda *a: attention(False)(*a), *attn_args)

    out_blk = lambda n: pl.BlockSpec((1, TM_OUT, n), lambda b, s: (b, s, 0))
    out = pl.pallas_call(
        _out_kernel,
        grid=(B, S // TM_OUT),
        in_specs=[out_blk(D), out_blk(D_NSA), out_blk(D_CONV), _full((D_NSA + D_CONV, D))],
        out_specs=out_blk(D),
        out_shape=sds((B, S, D), x.dtype),
        compiler_params=cp,
        name="nsa_out",
    )(x, o_nsa, oconv, f32(w_out[0]).astype(BF16))
    return out
```

```python
import functools

import numpy as np
import jax
import jax.numpy as jnp
from jax import lax
from jax.experimental import pallas as pl
from jax.experimental.pallas import tpu as pltpu

LANES = 128
SUBLANES = 8
HEAD_DIM = 64
NSA_HEADS = 8
KV_GROUPS = 2
GROUP_HEADS = NSA_HEADS // KV_GROUPS
D_NSA = NSA_HEADS * HEAD_DIM
D_CONV = 512
N_BRANCH = 3
ROT_DIM = HEAD_DIM // 4
ROPE_THETA = 500000.0
CMP_LEN = 32
CMP_STRIDE = 16
CMP_HIDDEN = 2 * HEAD_DIM
SEL_BLOCK = 64
SEL_TOPK = 16
WINDOW = 512
EPS = 1e-6
NEG_INF = -1e30
N_FORCED = 3
TAKEN = -2.0
SCALE = HEAD_DIM ** -0.5
Q_SCALE = SCALE * float(np.log2(np.e))

TM = 512
TM_OUT = 1024
TQ = 256
TK = 512
VCHUNK = 128
VMEM_LIMIT = 56 * 1024 * 1024
MAX_SAFE_SCORE = 50.0

BF16 = jnp.bfloat16
F32 = jnp.float32


def _dot(a, b):
    return jnp.dot(a, b, preferred_element_type=F32)


def _dot_nt(a, b):
    return lax.dot_general(a, b, (((1,), (1,)), ((), ())), preferred_element_type=F32)


def _split3(a):
    hi = a.astype(BF16)
    r1 = a - hi.astype(F32)
    mid = r1.astype(BF16)
    lo = (r1 - mid.astype(F32)).astype(BF16)
    return hi, mid, lo


def _group_mean(sq, mbd):
    return _dot(sq.astype(BF16), mbd)


def _rope(xn, c, sa, sb):
    return xn * c + pltpu.roll(xn, 8, 1) * sa + pltpu.roll(xn, LANES - 8, 1) * sb


def _silu(z):
    return z * (1.0 / (1.0 + jnp.exp(-z)))


def _proj_kernel(x_ref, nw_ref, wq_ref, wkv_ref, wg_ref, wr_ref, qnw_ref, knw_ref,
                 rc_ref, rsa_ref, rsb_ref, mbd_ref, cw_ref, cb_ref,
                 qt_ref, kc_ref, vc_ref, ksf_ref, vst_ref, kw_ref, vwt_ref,
                 gate_ref, zs_ref, oconv_ref, ubuf):
    si = pl.program_id(1)

    @pl.when(si == 0)
    def _():
        ubuf[0:8, :] = jnp.zeros((8, D_CONV), F32)

    x = x_ref[0]
    ms = jnp.mean(x * x, axis=-1, keepdims=True)
    h = (x * lax.rsqrt(ms + EPS) * nw_ref[...]).astype(BF16)

    rc, rsa, rsb = rc_ref[...], rsa_ref[...], rsb_ref[...]
    mbd = mbd_ref[...]

    pq = _dot(h, wq_ref[...])
    for pair in range(2):
        blk = pq[:, pair * 256:(pair + 1) * 256]
        msq = _group_mean(blk * blk, mbd)
        qn = blk * lax.rsqrt(msq + EPS) * qnw_ref[...]
        for half in range(2):
            t = qn[:, half * LANES:(half + 1) * LANES]
            qt_ref[0, pair * 2 + half] = (_rope(t, rc, rsa, rsb) * Q_SCALE).T.astype(BF16)

    pkv = _dot(h, wkv_ref[...])
    kc_ref[0] = pkv[:, 0:128]
    vc_ref[0] = pkv[:, 128:256]
    ksw = jnp.concatenate([pkv[:, 256:384], pkv[:, 512:640]], axis=1)
    msk = _group_mean(ksw * ksw, mbd)
    kn = ksw * lax.rsqrt(msk + EPS) * knw_ref[...]
    ks = _rope(kn[:, 0:128], rc, rsa, rsb)
    kw = _rope(kn[:, 128:256], rc, rsa, rsb)
    row = lax.broadcasted_iota(jnp.int32, (TM, LANES), 0) + si * TM
    lane = lax.broadcasted_iota(jnp.int32, (TM, LANES), 1)
    onehot = jnp.where(lane == (row >> 6), 1.0, 0.0)
    ksf_ref[0, :, 0:128] = ks.astype(BF16)
    ksf_ref[0, :, 128:256] = onehot.astype(BF16)
    kw_ref[0] = kw.astype(BF16)
    vs = pkv[:, 384:512]
    vw = pkv[:, 640:768]
    for j in range(TM // VCHUNK):
        vst_ref[0, j] = vs[j * VCHUNK:(j + 1) * VCHUNK, :].T.astype(BF16)
        vwt_ref[0, j] = vw[j * VCHUNK:(j + 1) * VCHUNK, :].T.astype(BF16)

    gate_ref[0] = 1.0 / (1.0 + jnp.exp(-_dot(h, wg_ref[...])))
    zs_ref[0] = _silu(_dot(h, wr_ref[:, 0:D_NSA])).astype(BF16)

    pcv = _dot(h, wr_ref[:, D_NSA:D_NSA + 4 * D_CONV])
    u = pcv[:, 1024:1536] * pcv[:, 0:512]
    ubuf[8:8 + TM, :] = u
    u1 = ubuf[7:7 + TM, :]
    u2 = ubuf[6:6 + TM, :]
    conv = cw_ref[0:1, :] * u2 + cw_ref[1:2, :] * u1 + cw_ref[2:3, :] * u + cb_ref[...]
    oconv_ref[0] = (pcv[:, 512:1024] * conv * _silu(pcv[:, 1536:2048])).astype(BF16)
    ubuf[0:8, :] = ubuf[TM:TM + 8, :]


def _cmp_kernel(hk_ref, hv_ref, pka_ref, pkb_ref, pva_ref, pvb_ref,
                wk1a_ref, wk1b_ref, bk1_ref, wk2_ref,
                wv1a_ref, wv1b_ref, bv1_ref, wv2_ref,
                knw_ref, rc_ref, rsa_ref, rsb_ref, mbd_ref,
                kc_ref, vct_ref, *, ncp):
    def first_layer(h_ref, p_ref, w_ref):
        acc = None
        for l in range(0, CMP_STRIDE, 2):
            lhs = jnp.concatenate([h_ref[0, pl.ds(l, ncp, stride=CMP_STRIDE), :],
                                   h_ref[0, pl.ds(l + 1, ncp, stride=CMP_STRIDE), :]], axis=1)
            lhs = (lhs + p_ref[:, l * LANES:(l + 2) * LANES]).astype(BF16)
            part = _dot(lhs, w_ref[l * LANES:(l + 2) * LANES, :])
            acc = part if acc is None else acc + part
        return acc

    def mlp(h_ref, pa_ref, pb_ref, w1a_ref, w1b_ref, b1_ref, w2_ref):
        p = first_layer(h_ref, pa_ref, w1a_ref)
        q = first_layer(h_ref, pb_ref, w1b_ref)
        pre = p + pltpu.roll(q, ncp - 1, 0) + b1_ref[...]
        return _dot(_silu(pre).astype(BF16), w2_ref[...])

    kc = mlp(hk_ref, pka_ref, pkb_ref, wk1a_ref, wk1b_ref, bk1_ref, wk2_ref)
    msk = _group_mean(kc * kc, mbd_ref[0:LANES, 0:LANES])
    kn = kc * lax.rsqrt(msk + EPS) * knw_ref[...]
    kc_ref[0] = _rope(kn, rc_ref[...], rsa_ref[...], rsb_ref[...]).astype(BF16)
    vc = mlp(hv_ref, pva_ref, pvb_ref, wv1a_ref, wv1b_ref, bv1_ref, wv2_ref)
    vct_ref[0] = vc.T.astype(BF16)


def _attn_kernel(qt_ref, kc_ref, vct_ref, ksf_ref, vst_ref, kw_ref, vwt_ref,
                 gate_ref, zs_ref, ovt_ref, o_ref, pen_sc, s_sc, cm_sc, *, ncp, bounded):
    i = pl.program_id(1)
    t0 = i * TQ
    rows = GROUP_HEADS * TQ
    nblk = SEL_BLOCK
    groups = range(KV_GROUPS)

    q_all = jnp.concatenate([qt_ref[0, j] for j in range(GROUP_HEADS)], axis=1)
    frow = lax.broadcasted_iota(jnp.int32, (LANES, rows), 0)
    qg = [jnp.where(frow < HEAD_DIM, q_all, 0), jnp.where(frow >= HEAD_DIM, q_all, 0)]
    t_q = t0 + lax.broadcasted_iota(jnp.int32, (1, TQ), 1)

    def fill_where(a, mask, fill):
        return jnp.concatenate([jnp.where(mask, a[:, r * TQ:(r + 1) * TQ], fill)
                                for r in range(GROUP_HEADS)], axis=1)

    vrows = lambda v, g: v[g * HEAD_DIM:(g + 1) * HEAD_DIM, :]

    s_cmp = [_dot(kc_ref[0], qg[g]) for g in groups]

    c_idx = lax.broadcasted_iota(jnp.int32, (ncp, 1), 0)
    mask_c = ((c_idx * CMP_STRIDE + (CMP_LEN - 1)) <= t_q) & (c_idx < ncp - 1)
    p_c, o_c = [], []
    for g in groups:
        if bounded:
            e_c = fill_where(jnp.exp2(s_cmp[g]), mask_c, 0.0)
        else:
            sm = fill_where(s_cmp[g], mask_c, NEG_INF)
            e_c = fill_where(jnp.exp2(sm - jnp.max(sm, axis=0, keepdims=True)), mask_c, 0.0)
        l_c = jnp.sum(e_c, axis=0, keepdims=True)
        p = e_c / jnp.where(l_c > 0.0, l_c, 1.0)
        p_c.append(p)
        o_c.append(_dot(vrows(vct_ref[0], g), p.astype(BF16)))

    n_idx = lax.broadcasted_iota(jnp.int32, (nblk, TQ), 0)
    tq = t0 + lax.broadcasted_iota(jnp.int32, (nblk, TQ), 1)
    cur = tq >> 6
    forced = (n_idx == 0) | (n_idx == cur) | (n_idx == cur - 1)
    visible = (n_idx << 6) <= tq
    ovt = ovt_ref[...]
    left, pen_t = [], []
    for g in groups:
        psum = p_c[g][:, 0:TQ]
        for r in range(1, GROUP_HEADS):
            psum = psum + p_c[g][:, r * TQ:(r + 1) * TQ]
        hi, mid, lo = _split3(psum)
        imp = _dot(ovt, hi) + _dot(ovt, mid) + _dot(ovt, lo)
        left.append(jnp.where(forced, TAKEN, jnp.where(visible, imp, -1.0)))
        pen_t.append(jnp.where(forced, 0.0, NEG_INF))
    for _ in range(SEL_TOPK - N_FORCED):
        for g in groups:
            top = jnp.max(left[g], axis=0, keepdims=True)
            first = jnp.min(jnp.where(left[g] == top, n_idx, nblk), axis=0, keepdims=True)
            pick = n_idx == first
            left[g] = jnp.where(pick, TAKEN, left[g])
            pen_t[g] = jnp.where(pick, 0.0, pen_t[g])
    for g in groups:
        pen_sc[g, 0:nblk, :] = pen_t[g].astype(BF16)
        pen_sc[g, nblk:LANES, :] = jnp.zeros((LANES - nblk, TQ), BF16)

    nwc = WINDOW // VCHUNK + TQ // VCHUNK
    cw = jnp.maximum(i * (TQ // VCHUNK) - WINDOW // VCHUNK, 0)
    w0 = pl.multiple_of(cw * VCHUNK, VCHUNK)
    kwin = kw_ref[0, pl.ds(w0, nwc * VCHUNK), :]
    s_win = [_dot(kwin, qg[g]) for g in groups]

    q_aug = [jnp.concatenate([qg[g], jnp.concatenate([pen_sc[g]] * GROUP_HEADS, axis=1)], axis=0)
             for g in groups]

    def window_branch():
        vwin = jnp.concatenate([vwt_ref[0, cw + j] for j in range(nwc)], axis=1)
        kpos = w0 + lax.broadcasted_iota(jnp.int32, (nwc * VCHUNK, 1), 0)
        mask_w = (kpos <= t_q) & (kpos > t_q - WINDOW)
        o_w, l_w = [], []
        for g in groups:
            if bounded:
                p_w = fill_where(jnp.exp2(s_win[g]), mask_w, 0.0)
            else:
                sw = fill_where(s_win[g], mask_w, NEG_INF)
                p_w = jnp.exp2(sw - jnp.max(sw, axis=0, keepdims=True))
            l_w.append(jnp.sum(p_w, axis=0, keepdims=True))
            o_w.append(_dot(vrows(vwin, g), p_w.astype(BF16)))
        return o_w, l_w

    def v_chunk(c, g):
        cv = c * (TK // VCHUNK)
        return jnp.concatenate([vrows(vst_ref[0, cv + j], g) for j in range(TK // VCHUNK)], axis=1)

    def qk(c, slot, g):
        k0 = pl.multiple_of(c * TK, TK)
        s = _dot(ksf_ref[0, pl.ds(k0, TK), :], q_aug[g])
        s_sc[slot, g] = s
        if not bounded:
            cm_sc[slot, g] = jnp.max(s, axis=0, keepdims=True)

    def sm_pv(c, slot, g, state, masked):
        m_p, l_p, acc = state
        s = s_sc[slot, g]
        if masked:
            kpos = c * TK + lax.broadcasted_iota(jnp.int32, (TK, 1), 0)
            s = fill_where(s, kpos <= t_q, NEG_INF)
        if bounded:
            p = jnp.exp2(s)
            return m_p, l_p + jnp.sum(p, axis=0, keepdims=True), acc + _dot(v_chunk(c, g), p.astype(BF16))
        cm = jnp.max(s, axis=0, keepdims=True) if masked else cm_sc[slot, g]
        m_n = jnp.maximum(m_p, cm)
        alpha = jnp.exp2(m_p - m_n)
        p = jnp.exp2(s - m_n)
        l_n = alpha * l_p + jnp.sum(p, axis=0, keepdims=True)
        acc = alpha * acc + _dot(v_chunk(c, g), p.astype(BF16))
        return m_n, l_n, acc

    def step(c, slot, carry):
        out = []
        for g in groups:
            qk(c + 1, 1 - slot, g)
            out.append(sm_pv(c, slot, g, carry[g], False))
        return tuple(out)

    def last(slot, carry):
        return tuple(sm_pv(n_full, slot, g, carry[g], True) for g in groups)

    n_full = t0 // TK
    for g in groups:
        qk(0, 0, g)
    o_w, l_w = window_branch()
    init = tuple((jnp.full((1, rows), 0.0 if bounded else NEG_INF, F32), jnp.zeros((1, rows), F32),
                  jnp.zeros((HEAD_DIM, rows), F32)) for _ in groups)
    carry = lax.fori_loop(0, n_full // 2, lambda h, cr: step(2 * h + 1, 1, step(2 * h, 0, cr)), init)
    carry = lax.cond((n_full & 1) == 1,
                     lambda cr: last(1, step(n_full - 1, 0, cr)),
                     lambda cr: last(0, cr), carry)
    sel_out = [(carry[g][1], carry[g][2]) for g in groups]

    g_t = gate_ref[0].T
    slabs = []
    for g in groups:
        def gate_row(br):
            base = g * (N_BRANCH * GROUP_HEADS) + br * GROUP_HEADS
            return jnp.concatenate([g_t[base + r:base + r + 1, :] for r in range(GROUP_HEADS)], axis=1)
        l_s, o_s = sel_out[g]
        og = o_c[g] * gate_row(0) + o_s * (gate_row(1) / l_s) + o_w[g] * (gate_row(2) / l_w[g])
        slabs += [og[:, r * TQ:(r + 1) * TQ] for r in range(GROUP_HEADS)]
    out_t = jnp.concatenate(slabs, axis=0)
    o_ref[0] = (out_t.T * zs_ref[0].astype(F32)).astype(BF16)


def _out_kernel(x_ref, on_ref, oc_ref, w_ref, o_ref):
    acc = _dot(on_ref[0], w_ref[0:D_NSA, :]) + _dot(oc_ref[0], w_ref[D_NSA:D_NSA + D_CONV, :])
    o_ref[0] = x_ref[0] + acc


def _rope_tables(pos):
    inv_freq = (np.float32(ROPE_THETA) ** (-np.arange(0, ROT_DIM, 2, dtype=np.float32) / ROT_DIM)).astype(np.float32)
    ang = pos.astype(np.float32)[:, None] * inv_freq[None, :]
    cos, sin = np.cos(ang), np.sin(ang)
    n = pos.shape[0]
    ones = np.ones((n, HEAD_DIM - ROT_DIM), np.float32)
    zeros = np.zeros((n, HEAD_DIM - ROT_DIM), np.float32)
    z8 = np.zeros((n, ROT_DIM // 2), np.float32)
    c = np.concatenate([cos, cos, ones], axis=1)
    sa = np.concatenate([z8, sin, zeros], axis=1)
    sb = np.concatenate([-sin, z8, zeros], axis=1)
    return tuple(jnp.asarray(np.concatenate([t, t], axis=1), F32) for t in (c, sa, sb))


def _overlap_t(ncp):
    cs = np.arange(ncp) * CMP_STRIDE
    ce = cs + CMP_LEN
    ss = np.arange(SEL_BLOCK) * SEL_BLOCK
    se = ss + SEL_BLOCK
    ov = np.clip(np.minimum(ce[None, :], se[:, None]) - np.maximum(cs[None, :], ss[:, None]), 0, None)
    ov = ov.astype(np.float32) / CMP_LEN
    ov[:, ncp - 1] = 0.0
    return jnp.asarray(ov, BF16)


def _cmp_weights(pos, w1, b1, w2):
    half = CMP_STRIDE
    def big_w1(w):
        w = w.reshape(half, 1, HEAD_DIM, 1, CMP_HIDDEN)
        eye = jnp.eye(KV_GROUPS, dtype=F32).reshape(1, KV_GROUPS, 1, KV_GROUPS, 1)
        return (w * eye).reshape(half * KV_GROUPS * HEAD_DIM, KV_GROUPS * CMP_HIDDEN).astype(BF16)
    def big_pos(p):
        return jnp.broadcast_to(p[:, None, :], (half, KV_GROUPS, HEAD_DIM)).reshape(1, -1)
    w1a, w1b = w1[:half * HEAD_DIM], w1[half * HEAD_DIM:]
    eye2 = jnp.eye(KV_GROUPS, dtype=F32)
    w2b = (w2[None, :, None, :] * eye2[:, None, :, None]).reshape(KV_GROUPS * CMP_HIDDEN,
                                                                 KV_GROUPS * HEAD_DIM).astype(BF16)
    b1b = jnp.tile(b1, KV_GROUPS)[None, :]
    return big_pos(pos[:half]), big_pos(pos[half:]), big_w1(w1a), big_w1(w1b), b1b, w2b


def _full(shape):
    nd = len(shape)
    return pl.BlockSpec(shape, lambda *_: (0,) * nd)


def kernel(x, norm_w, w_in, q_norm_w, k_norm_w, cmp_k_pos, cmp_k_w1, cmp_k_b1, cmp_k_w2,
           cmp_v_pos, cmp_v_w1, cmp_v_b1, cmp_v_w2, conv_w, conv_b, w_out):
    B, S, D = x.shape
    assert norm_w.shape[0] == 1, "single layer"
    assert S % TM == 0 and S % TM_OUT == 0 and S // SEL_BLOCK <= SEL_BLOCK and S >= WINDOW + TQ
    ncp = S // CMP_STRIDE
    nst = S // TM
    f32 = lambda a: a.astype(F32)

    w = f32(w_in[0])
    head_order = []
    for j in range(GROUP_HEADS):
        head_order += [j, GROUP_HEADS + j]
    wq = w[:, :D_NSA].reshape(D, NSA_HEADS, HEAD_DIM)[:, jnp.array(head_order), :].reshape(D, D_NSA)
    gate_cols = []
    for g in range(KV_GROUPS):
        for br in range(N_BRANCH):
            for r in range(GROUP_HEADS):
                gate_cols.append((g * GROUP_HEADS + r) * N_BRANCH + br)
    g0 = D_NSA + 6 * LANES
    wg = w[:, g0:g0 + NSA_HEADS * N_BRANCH][:, jnp.array(gate_cols)]
    wg = jnp.pad(wg, ((0, 0), (0, LANES - NSA_HEADS * N_BRANCH)))
    rest0 = g0 + NSA_HEADS * N_BRANCH
    wq, wkv, wg, wr = (t.astype(BF16) for t in (wq, w[:, D_NSA:g0], wg, w[:, rest0:]))
    n_rest = wr.shape[1]

    rc, rsa, rsb = _rope_tables(np.arange(S))
    mbd = jnp.asarray(np.kron(np.eye(4), np.full((HEAD_DIM, HEAD_DIM), 1.0 / HEAD_DIM)), BF16)
    qnw = jnp.tile(f32(q_norm_w[0]), 4)[None, :]
    knw_sw = jnp.concatenate([jnp.tile(f32(k_norm_w[0, 1]), 2), jnp.tile(f32(k_norm_w[0, 2]), 2)])[None, :]
    knw_c = jnp.tile(f32(k_norm_w[0, 0]), 2)[None, :]

    cp = pltpu.CompilerParams(dimension_semantics=("arbitrary", "arbitrary"),
                              vmem_limit_bytes=VMEM_LIMIT)
    row_blk = lambda n: pl.BlockSpec((1, TM, n), lambda b, s: (b, s, 0))
    tab_blk = pl.BlockSpec((TM, LANES), lambda b, s: (s, 0))
    vt_blk = pl.BlockSpec((1, TM // VCHUNK, LANES, VCHUNK), lambda b, s: (b, s, 0, 0))
    sds = jax.ShapeDtypeStruct

    (qt4, kc_raw, vc_raw, ksf, vst, kw, vwt, gates, zs, oconv) = pl.pallas_call(
        _proj_kernel,
        grid=(B, nst),
        in_specs=[row_blk(D), _full((1, D)), _full((D, D_NSA)), _full((D, 6 * LANES)), _full((D, LANES)),
                  _full((D, n_rest)), _full((1, 256)), _full((1, 256)),
                  tab_blk, tab_blk, tab_blk, _full((256, 256)), _full((3, D_CONV)), _full((1, D_CONV))],
        out_specs=[pl.BlockSpec((1, 4, LANES, TM), lambda b, s: (b, 0, 0, s)),
                   row_blk(LANES), row_blk(LANES), row_blk(2 * LANES), vt_blk, row_blk(LANES), vt_blk,
                   row_blk(LANES), row_blk(D_NSA), row_blk(D_CONV)],
        out_shape=[sds((B, 4, LANES, S), BF16), sds((B, S, LANES), F32), sds((B, S, LANES), F32),
                   sds((B, S, 2 * LANES), BF16), sds((B, S // VCHUNK, LANES, VCHUNK), BF16),
                   sds((B, S, LANES), BF16), sds((B, S // VCHUNK, LANES, VCHUNK), BF16),
                   sds((B, S, LANES), F32), sds((B, S, D_NSA), BF16), sds((B, S, D_CONV), BF16)],
        scratch_shapes=[pltpu.VMEM((TM + 16, D_CONV), F32)],
        compiler_params=cp,
        name="nsa_proj",
    )(x, f32(norm_w), wq, wkv, wg, wr, qnw, knw_sw, rc, rsa, rsb, mbd, f32(conv_w[0]), f32(conv_b))

    cmp_pos = np.arange(ncp) * CMP_STRIDE + (CMP_LEN - 1)
    crc, crsa, crsb = _rope_tables(cmp_pos)
    kpa, kpb, kw1a, kw1b, kb1, kw2 = _cmp_weights(f32(cmp_k_pos[0]), f32(cmp_k_w1[0]),
                                                  f32(cmp_k_b1[0]), f32(cmp_k_w2[0]))
    vpa, vpb, vw1a, vw1b, vb1, vw2 = _cmp_weights(f32(cmp_v_pos[0]), f32(cmp_v_w1[0]),
                                                  f32(cmp_v_b1[0]), f32(cmp_v_w2[0]))
    flat = CMP_STRIDE * LANES
    h_blk = pl.BlockSpec((1, S, LANES), lambda b: (b, 0, 0))
    kc, vct = pl.pallas_call(
        functools.partial(_cmp_kernel, ncp=ncp),
        grid=(B,),
        in_specs=[h_blk, h_blk] + [_full((1, flat))] * 4
                 + [_full((flat, 256)), _full((flat, 256)), _full((1, 256)), _full((256, LANES))] * 2
                 + [_full((1, LANES)), _full((ncp, LANES)), _full((ncp, LANES)), _full((ncp, LANES)),
                    _full((256, 256))],
        out_specs=[pl.BlockSpec((1, ncp, LANES), lambda b: (b, 0, 0)),
                   pl.BlockSpec((1, LANES, ncp), lambda b: (b, 0, 0))],
        out_shape=[sds((B, ncp, LANES), BF16), sds((B, LANES, ncp), BF16)],
        compiler_params=pltpu.CompilerParams(dimension_semantics=("arbitrary",),
                                             vmem_limit_bytes=VMEM_LIMIT),
        name="nsa_compress",
    )(kc_raw, vc_raw, kpa, kpb, vpa, vpb,
      kw1a, kw1b, kb1, kw2, vw1a, vw1b, vb1, vw2, knw_c, crc, crsa, crsb, mbd)

    score_bound = (HEAD_DIM * Q_SCALE) * jnp.max(jnp.abs(f32(q_norm_w))) * jnp.max(jnp.abs(f32(k_norm_w)))
    nqt = S // TQ

    def attention(bounded):
        return pl.pallas_call(
            functools.partial(_attn_kernel, ncp=ncp, bounded=bounded),
            grid=(B, nqt),
            in_specs=[pl.BlockSpec((1, 4, LANES, TQ), lambda b, i: (b, 0, 0, i)),
                      pl.BlockSpec((1, ncp, LANES), lambda b, i: (b, 0, 0)),
                      pl.BlockSpec((1, LANES, ncp), lambda b, i: (b, 0, 0)),
                      pl.BlockSpec((1, S, 2 * LANES), lambda b, i: (b, 0, 0)),
                      pl.BlockSpec((1, S // VCHUNK, LANES, VCHUNK), lambda b, i: (b, 0, 0, 0)),
                      pl.BlockSpec((1, S, LANES), lambda b, i: (b, 0, 0)),
                      pl.BlockSpec((1, S // VCHUNK, LANES, VCHUNK), lambda b, i: (b, 0, 0, 0)),
                      pl.BlockSpec((1, TQ, LANES), lambda b, i: (b, i, 0)),
                      pl.BlockSpec((1, TQ, D_NSA), lambda b, i: (b, i, 0)),
                      _full((SEL_BLOCK, ncp))],
            out_specs=pl.BlockSpec((1, TQ, D_NSA), lambda b, i: (b, i, 0)),
            out_shape=sds((B, S, D_NSA), BF16),
            scratch_shapes=[pltpu.VMEM((KV_GROUPS, LANES, TQ), BF16),
                            pltpu.VMEM((2, KV_GROUPS, TK, GROUP_HEADS * TQ), F32),
                            pltpu.VMEM((2, KV_GROUPS, 1, GROUP_HEADS * TQ), F32)],
            compiler_params=cp,
            name="nsa_attn_bounded" if bounded else "nsa_attn_online",
        )

    attn_args = (qt4, kc, vct, ksf, vst, kw, vwt, gates, zs, _overlap_t(ncp))
    o_nsa = lax.cond(score_bound <= MAX_SAFE_SCORE,
                     lambda *a: attention(True)(*a), lambda *a: attention(False)(*a), *attn_args)

    out_blk = lambda n: pl.BlockSpec((1, TM_OUT, n), lambda b, s: (b, s, 0))
    out = pl.pallas_call(
        _out_kernel,
        grid=(B, S // TM_OUT),
        in_specs=[out_blk(D), out_blk(D_NSA), out_blk(D_CONV), _full((D_NSA + D_CONV, D))],
        out_specs=out_blk(D),
        out_shape=sds((B, S, D), x.dtype),
        compiler_params=cp,
        name="nsa_out",
    )(x, o_nsa, oconv, f32(w_out[0]).astype(BF16))
    return out
```

```python
import functools

import numpy as np
import jax
import jax.numpy as jnp
from jax import lax
from jax.experimental import pallas as pl
from jax.experimental.pallas import tpu as pltpu

LANES = 128
SUBLANES = 8
HEAD_DIM = 64
NSA_HEADS = 8
KV_GROUPS = 2
GROUP_HEADS = NSA_HEADS // KV_GROUPS
D_NSA = NSA_HEADS * HEAD_DIM
D_CONV = 512
N_BRANCH = 3
ROT_DIM = HEAD_DIM // 4
ROPE_THETA = 500000.0
CMP_LEN = 32
CMP_STRIDE = 16
CMP_HIDDEN = 2 * HEAD_DIM
SEL_BLOCK = 64
SEL_TOPK = 16
WINDOW = 512
EPS = 1e-6
NEG_INF = -1e30
N_FORCED = 3
TAKEN = -2.0
SCALE = HEAD_DIM ** -0.5
Q_SCALE = SCALE * float(np.log2(np.e))

TM = 512
TM_OUT = 1024
TQ = 256
TK = 512
VCHUNK = 128
VMEM_LIMIT = 56 * 1024 * 1024
MAX_SAFE_SCORE = 50.0

BF16 = jnp.bfloat16
F32 = jnp.float32


def _dot(a, b):
    return jnp.dot(a, b, preferred_element_type=F32)


def _dot_nt(a, b):
    return lax.dot_general(a, b, (((1,), (1,)), ((), ())), preferred_element_type=F32)


def _split3(a):
    hi = a.astype(BF16)
    r1 = a - hi.astype(F32)
    mid = r1.astype(BF16)
    lo = (r1 - mid.astype(F32)).astype(BF16)
    return hi, mid, lo


def _group_mean(sq, mbd):
    return _dot(sq.astype(BF16), mbd)


def _rope(xn, c, sa, sb):
    return xn * c + pltpu.roll(xn, 8, 1) * sa + pltpu.roll(xn, LANES - 8, 1) * sb


def _silu(z):
    return z * (1.0 / (1.0 + jnp.exp(-z)))


def _proj_kernel(x_ref, nw_ref, wq_ref, wkv_ref, wg_ref, wr_ref, qnw_ref, knw_ref,
                 rc_ref, rsa_ref, rsb_ref, mbd_ref, cw_ref, cb_ref,
                 qt_ref, kc_ref, vc_ref, ksf_ref, vst_ref, kw_ref, vwt_ref,
                 gate_ref, zs_ref, oconv_ref, ubuf):
    si = pl.program_id(1)

    @pl.when(si == 0)
    def _():
        ubuf[0:8, :] = jnp.zeros((8, D_CONV), F32)

    x = x_ref[0]
    ms = jnp.mean(x * x, axis=-1, keepdims=True)
    h = (x * lax.rsqrt(ms + EPS) * nw_ref[...]).astype(BF16)

    rc, rsa, rsb = rc_ref[...], rsa_ref[...], rsb_ref[...]
    mbd = mbd_ref[...]

    pq = _dot(h, wq_ref[...])
    for pair in range(2):
        blk = pq[:, pair * 256:(pair + 1) * 256]
        msq = _group_mean(blk * blk, mbd)
        qn = blk * lax.rsqrt(msq + EPS) * qnw_ref[...]
        for half in range(2):
            t = qn[:, half * LANES:(half + 1) * LANES]
            qt_ref[0, pair * 2 + half] = (_rope(t, rc, rsa, rsb) * Q_SCALE).T.astype(BF16)

    pkv = _dot(h, wkv_ref[...])
    kc_ref[0] = pkv[:, 0:128]
    vc_ref[0] = pkv[:, 128:256]
    ksw = jnp.concatenate([pkv[:, 256:384], pkv[:, 512:640]], axis=1)
    msk = _group_mean(ksw * ksw, mbd)
    kn = ksw * lax.rsqrt(msk + EPS) * knw_ref[...]
    ks = _rope(kn[:, 0:128], rc, rsa, rsb)
    kw = _rope(kn[:, 128:256], rc, rsa, rsb)
    row = lax.broadcasted_iota(jnp.int32, (TM, LANES), 0) + si * TM
    lane = lax.broadcasted_iota(jnp.int32, (TM, LANES), 1)
    onehot = jnp.where(lane == (row >> 6), 1.0, 0.0)
    ksf_ref[0, :, 0:128] = ks.astype(BF16)
    ksf_ref[0, :, 128:256] = onehot.astype(BF16)
    kw_ref[0] = kw.astype(BF16)
    vs = pkv[:, 384:512]
    vw = pkv[:, 640:768]
    for j in range(TM // VCHUNK):
        vst_ref[0, j] = vs[j * VCHUNK:(j + 1) * VCHUNK, :].T.astype(BF16)
        vwt_ref[0, j] = vw[j * VCHUNK:(j + 1) * VCHUNK, :].T.astype(BF16)

    gate_ref[0] = 1.0 / (1.0 + jnp.exp(-_dot(h, wg_ref[...])))
    zs_ref[0] = _silu(_dot(h, wr_ref[:, 0:D_NSA])).astype(BF16)

    pcv = _dot(h, wr_ref[:, D_NSA:D_NSA + 4 * D_CONV])
    u = pcv[:, 1024:1536] * pcv[:, 0:512]
    ubuf[8:8 + TM, :] = u
    u1 = ubuf[7:7 + TM, :]
    u2 = ubuf[6:6 + TM, :]
    conv = cw_ref[0:1, :] * u2 + cw_ref[1:2, :] * u1 + cw_ref[2:3, :] * u + cb_ref[...]
    oconv_ref[0] = (pcv[:, 512:1024] * conv * _silu(pcv[:, 1536:2048])).astype(BF16)
    ubuf[0:8, :] = ubuf[TM:TM + 8, :]


def _cmp_kernel(hk_ref, hv_ref, pka_ref, pkb_ref, pva_ref, pvb_ref,
                wk1a_ref, wk1b_ref, bk1_ref, wk2_ref,
                wv1a_ref, wv1b_ref, bv1_ref, wv2_ref,
                knw_ref, rc_ref, rsa_ref, rsb_ref, mbd_ref,
                kc_ref, vct_ref, *, ncp):
    def first_layer(h_ref, p_ref, w_ref):
        acc = None
        for l in range(0, CMP_STRIDE, 2):
            lhs = jnp.concatenate([h_ref[0, pl.ds(l, ncp, stride=CMP_STRIDE), :],
                                   h_ref[0, pl.ds(l + 1, ncp, stride=CMP_STRIDE), :]], axis=1)
            lhs = (lhs + p_ref[:, l * LANES:(l + 2) * LANES]).astype(BF16)
            part = _dot(lhs, w_ref[l * LANES:(l + 2) * LANES, :])
            acc = part if acc is None else acc + part
        return acc

    def mlp(h_ref, pa_ref, pb_ref, w1a_ref, w1b_ref, b1_ref, w2_ref):
        p = first_layer(h_ref, pa_ref, w1a_ref)
        q = first_layer(h_ref, pb_ref, w1b_ref)
        pre = p + pltpu.roll(q, ncp - 1, 0) + b1_ref[...]
        return _dot(_silu(pre).astype(BF16), w2_ref[...])

    kc = mlp(hk_ref, pka_ref, pkb_ref, wk1a_ref, wk1b_ref, bk1_ref, wk2_ref)
    msk = _group_mean(kc * kc, mbd_ref[0:LANES, 0:LANES])
    kn = kc * lax.rsqrt(msk + EPS) * knw_ref[...]
    kc_ref[0] = _rope(kn, rc_ref[...], rsa_ref[...], rsb_ref[...]).astype(BF16)
    vc = mlp(hv_ref, pva_ref, pvb_ref, wv1a_ref, wv1b_ref, bv1_ref, wv2_ref)
    vct_ref[0] = vc.T.astype(BF16)


def _attn_kernel(qt_ref, kc_ref, vct_ref, ksf_ref, vst_ref, kw_ref, vwt_ref,
                 gate_ref, zs_ref, ovt_ref, o_ref, pen_sc, s_sc, cm_sc, *, ncp, bounded):
    i = pl.program_id(1)
    t0 = i * TQ
    rows = GROUP_HEADS * TQ
    nblk = SEL_BLOCK
    groups = range(KV_GROUPS)

    q_all = jnp.concatenate([qt_ref[0, j] for j in range(GROUP_HEADS)], axis=1)
    frow = lax.broadcasted_iota(jnp.int32, (LANES, rows), 0)
    qg = [jnp.where(frow < HEAD_DIM, q_all, 0), jnp.where(frow >= HEAD_DIM, q_all, 0)]
    t_q = t0 + lax.broadcasted_iota(jnp.int32, (1, TQ), 1)

    def fill_where(a, mask, fill):
        return jnp.concatenate([jnp.where(mask, a[:, r * TQ:(r + 1) * TQ], fill)
                                for r in range(GROUP_HEADS)], axis=1)

    vrows = lambda v, g: v[g * HEAD_DIM:(g + 1) * HEAD_DIM, :]

    s_cmp = [_dot(kc_ref[0], qg[g]) for g in groups]

    c_idx = lax.broadcasted_iota(jnp.int32, (ncp, 1), 0)
    mask_c = ((c_idx * CMP_STRIDE + (CMP_LEN - 1)) <= t_q) & (c_idx < ncp - 1)
    p_c, o_c = [], []
    for g in groups:
        if bounded:
            e_c = fill_where(jnp.exp2(s_cmp[g]), mask_c, 0.0)
        else:
            sm = fill_where(s_cmp[g], mask_c, NEG_INF)
            e_c = fill_where(jnp.exp2(sm - jnp.max(sm, axis=0, keepdims=True)), mask_c, 0.0)
        l_c = jnp.sum(e_c, axis=0, keepdims=True)
        p = e_c / jnp.where(l_c > 0.0, l_c, 1.0)
        p_c.append(p)
        o_c.append(_dot(vrows(vct_ref[0], g), p.astype(BF16)))

    n_idx = lax.broadcasted_iota(jnp.int32, (nblk, TQ), 0)
    tq = t0 + lax.broadcasted_iota(jnp.int32, (nblk, TQ), 1)
    cur = tq >> 6
    forced = (n_idx == 0) | (n_idx == cur) | (n_idx == cur - 1)
    visible = (n_idx << 6) <= tq
    ovt = ovt_ref[...]
    left, pen_t = [], []
    for g in groups:
        psum = p_c[g][:, 0:TQ]
        for r in range(1, GROUP_HEADS):
            psum = psum + p_c[g][:, r * TQ:(r + 1) * TQ]
        hi, mid, lo = _split3(psum)
        imp = _dot(ovt, hi) + _dot(ovt, mid) + _dot(ovt, lo)
        left.append(jnp.where(forced, TAKEN, jnp.where(visible, imp, -1.0)))
        pen_t.append(jnp.where(forced, 0.0, NEG_INF))
    for _ in range(SEL_TOPK - N_FORCED):
        for g in groups:
            top = jnp.max(left[g], axis=0, keepdims=True)
            first = jnp.min(jnp.where(left[g] == top, n_idx, nblk), axis=0, keepdims=True)
            pick = n_idx == first
            left[g] = jnp.where(pick, TAKEN, left[g])
            pen_t[g] = jnp.where(pick, 0.0, pen_t[g])
    for g in groups:
        pen_sc[g, 0:nblk, :] = pen_t[g].astype(BF16)
        pen_sc[g, nblk:LANES, :] = jnp.zeros((LANES - nblk, TQ), BF16)

    nwc = WINDOW // VCHUNK + TQ // VCHUNK
    cw = jnp.maximum(i * (TQ // VCHUNK) - WINDOW // VCHUNK, 0)
    w0 = pl.multiple_of(cw * VCHUNK, VCHUNK)
    kwin = kw_ref[0, pl.ds(w0, nwc * VCHUNK), :]
    s_win = [_dot(kwin, qg[g]) for g in groups]

    q_aug = [jnp.concatenate([qg[g], jnp.concatenate([pen_sc[g]] * GROUP_HEADS, axis=1)], axis=0)
             for g in groups]

    def window_branch():
        vwin = jnp.concatenate([vwt_ref[0, cw + j] for j in range(nwc)], axis=1)
        kpos = w0 + lax.broadcasted_iota(jnp.int32, (nwc * VCHUNK, 1), 0)
        mask_w = (kpos <= t_q) & (kpos > t_q - WINDOW)
        o_w, l_w = [], []
        for g in groups:
            if bounded:
                p_w = fill_where(jnp.exp2(s_win[g]), mask_w, 0.0)
            else:
                sw = fill_where(s_win[g], mask_w, NEG_INF)
                p_w = jnp.exp2(sw - jnp.max(sw, axis=0, keepdims=True))
            l_w.append(jnp.sum(p_w, axis=0, keepdims=True))
            o_w.append(_dot(vrows(vwin, g), p_w.astype(BF16)))
        return o_w, l_w

    def v_chunk(c, g, nk=TK):
        cv = c * (TK // VCHUNK)
        return jnp.concatenate([vrows(vst_ref[0, cv + j], g) for j in range(nk // VCHUNK)], axis=1)

    def qk(c, slot, g):
        k0 = pl.multiple_of(c * TK, TK)
        s = _dot(ksf_ref[0, pl.ds(k0, TK), :], q_aug[g])
        s_sc[slot, g] = s
        if not bounded:
            cm_sc[slot, g] = jnp.max(s, axis=0, keepdims=True)

    def sm_pv(c, slot, g, state, diag_keys=None):
        m_p, l_p, acc = state
        nk = TK if diag_keys is None else diag_keys
        s = s_sc[slot, g, 0:nk, :]
        if diag_keys is not None:
            kpos = c * TK + lax.broadcasted_iota(jnp.int32, (nk, 1), 0)
            s = fill_where(s, kpos <= t_q, NEG_INF)
        if bounded:
            p = jnp.exp2(s)
            return (m_p, l_p + jnp.sum(p, axis=0, keepdims=True),
                    acc + _dot(v_chunk(c, g, nk), p.astype(BF16)))
        cm = cm_sc[slot, g] if diag_keys is None else jnp.max(s, axis=0, keepdims=True)
        m_n = jnp.maximum(m_p, cm)
        alpha = jnp.exp2(m_p - m_n)
        p = jnp.exp2(s - m_n)
        l_n = alpha * l_p + jnp.sum(p, axis=0, keepdims=True)
        acc = alpha * acc + _dot(v_chunk(c, g, nk), p.astype(BF16))
        return m_n, l_n, acc

    def step(c, slot, carry):
        out = []
        for g in groups:
            qk(c + 1, 1 - slot, g)
            out.append(sm_pv(c, slot, g, carry[g]))
        return tuple(out)

    def last(slot, diag_keys, carry):
        return tuple(sm_pv(n_full, slot, g, carry[g], diag_keys) for g in groups)

    n_full = t0 // TK
    for g in groups:
        qk(0, 0, g)
    o_w, l_w = window_branch()
    init = tuple((jnp.full((1, rows), 0.0 if bounded else NEG_INF, F32), jnp.zeros((1, rows), F32),
                  jnp.zeros((HEAD_DIM, rows), F32)) for _ in groups)
    carry = lax.fori_loop(0, n_full // 2, lambda h, cr: step(2 * h + 1, 1, step(2 * h, 0, cr)), init)
    carry = lax.switch(i % 4, [lambda cr: last(0, TQ, cr),
                               lambda cr: last(0, TK, cr),
                               lambda cr: last(1, TQ, step(n_full - 1, 0, cr)),
                               lambda cr: last(1, TK, step(n_full - 1, 0, cr))], carry)
    sel_out = [(carry[g][1], carry[g][2]) for g in groups]

    g_t = gate_ref[0].T
    slabs = []
    for g in groups:
        def gate_row(br):
            base = g * (N_BRANCH * GROUP_HEADS) + br * GROUP_HEADS
            return jnp.concatenate([g_t[base + r:base + r + 1, :] for r in range(GROUP_HEADS)], axis=1)
        l_s, o_s = sel_out[g]
        og = o_c[g] * gate_row(0) + o_s * (gate_row(1) / l_s) + o_w[g] * (gate_row(2) / l_w[g])
        slabs += [og[:, r * TQ:(r + 1) * TQ] for r in range(GROUP_HEADS)]
    out_t = jnp.concatenate(slabs, axis=0)
    o_ref[0] = (out_t.T * zs_ref[0].astype(F32)).astype(BF16)


def _out_kernel(x_ref, on_ref, oc_ref, w_ref, o_ref):
    acc = _dot(on_ref[0], w_ref[0:D_NSA, :]) + _dot(oc_ref[0], w_ref[D_NSA:D_NSA + D_CONV, :])
    o_ref[0] = x_ref[0] + acc


def _rope_tables(pos):
    inv_freq = (np.float32(ROPE_THETA) ** (-np.arange(0, ROT_DIM, 2, dtype=np.float32) / ROT_DIM)).astype(np.float32)
    ang = pos.astype(np.float32)[:, None] * inv_freq[None, :]
    cos, sin = np.cos(ang), np.sin(ang)
    n = pos.shape[0]
    ones = np.ones((n, HEAD_DIM - ROT_DIM), np.float32)
    zeros = np.zeros((n, HEAD_DIM - ROT_DIM), np.float32)
    z8 = np.zeros((n, ROT_DIM // 2), np.float32)
    c = np.concatenate([cos, cos, ones], axis=1)
    sa = np.concatenate([z8, sin, zeros], axis=1)
    sb = np.concatenate([-sin, z8, zeros], axis=1)
    return tuple(jnp.asarray(np.concatenate([t, t], axis=1), F32) for t in (c, sa, sb))


def _overlap_t(ncp):
    cs = np.arange(ncp) * CMP_STRIDE
    ce = cs + CMP_LEN
    ss = np.arange(SEL_BLOCK) * SEL_BLOCK
    se = ss + SEL_BLOCK
    ov = np.clip(np.minimum(ce[None, :], se[:, None]) - np.maximum(cs[None, :], ss[:, None]), 0, None)
    ov = ov.astype(np.float32) / CMP_LEN
    ov[:, ncp - 1] = 0.0
    return jnp.asarray(ov, BF16)


def _cmp_weights(pos, w1, b1, w2):
    half = CMP_STRIDE
    def big_w1(w):
        w = w.reshape(half, 1, HEAD_DIM, 1, CMP_HIDDEN)
        eye = jnp.eye(KV_GROUPS, dtype=F32).reshape(1, KV_GROUPS, 1, KV_GROUPS, 1)
        return (w * eye).reshape(half * KV_GROUPS * HEAD_DIM, KV_GROUPS * CMP_HIDDEN).astype(BF16)
    def big_pos(p):
        return jnp.broadcast_to(p[:, None, :], (half, KV_GROUPS, HEAD_DIM)).reshape(1, -1)
    w1a, w1b = w1[:half * HEAD_DIM], w1[half * HEAD_DIM:]
    eye2 = jnp.eye(KV_GROUPS, dtype=F32)
    w2b = (w2[None, :, None, :] * eye2[:, None, :, None]).reshape(KV_GROUPS * CMP_HIDDEN,
                                                                 KV_GROUPS * HEAD_DIM).astype(BF16)
    b1b = jnp.tile(b1, KV_GROUPS)[None, :]
    return big_pos(pos[:half]), big_pos(pos[half:]), big_w1(w1a), big_w1(w1b), b1b, w2b


def _full(shape):
    nd = len(shape)
    return pl.BlockSpec(shape, lambda *_: (0,) * nd)


def kernel(x, norm_w, w_in, q_norm_w, k_norm_w, cmp_k_pos, cmp_k_w1, cmp_k_b1, cmp_k_w2,
           cmp_v_pos, cmp_v_w1, cmp_v_b1, cmp_v_w2, conv_w, conv_b, w_out):
    B, S, D = x.shape
    assert norm_w.shape[0] == 1, "single layer"
    assert TK == 2 * TQ and S % TM == 0 and S % TM_OUT == 0 and S // SEL_BLOCK <= SEL_BLOCK and S >= WINDOW + TQ
    ncp = S // CMP_STRIDE
    nst = S // TM
    f32 = lambda a: a.astype(F32)

    w = f32(w_in[0])
    head_order = []
    for j in range(GROUP_HEADS):
        head_order += [j, GROUP_HEADS + j]
    wq = w[:, :D_NSA].reshape(D, NSA_HEADS, HEAD_DIM)[:, jnp.array(head_order), :].reshape(D, D_NSA)
    gate_cols = []
    for g in range(KV_GROUPS):
        for br in range(N_BRANCH):
            for r in range(GROUP_HEADS):
                gate_cols.append((g * GROUP_HEADS + r) * N_BRANCH + br)
    g0 = D_NSA + 6 * LANES
    wg = w[:, g0:g0 + NSA_HEADS * N_BRANCH][:, jnp.array(gate_cols)]
    wg = jnp.pad(wg, ((0, 0), (0, LANES - NSA_HEADS * N_BRANCH)))
    rest0 = g0 + NSA_HEADS * N_BRANCH
    wq, wkv, wg, wr = (t.astype(BF16) for t in (wq, w[:, D_NSA:g0], wg, w[:, rest0:]))
    n_rest = wr.shape[1]

    rc, rsa, rsb = _rope_tables(np.arange(S))
    mbd = jnp.asarray(np.kron(np.eye(4), np.full((HEAD_DIM, HEAD_DIM), 1.0 / HEAD_DIM)), BF16)
    qnw = jnp.tile(f32(q_norm_w[0]), 4)[None, :]
    knw_sw = jnp.concatenate([jnp.tile(f32(k_norm_w[0, 1]), 2), jnp.tile(f32(k_norm_w[0, 2]), 2)])[None, :]
    knw_c = jnp.tile(f32(k_norm_w[0, 0]), 2)[None, :]

    cp = pltpu.CompilerParams(dimension_semantics=("arbitrary", "arbitrary"),
                              vmem_limit_bytes=VMEM_LIMIT)
    row_blk = lambda n: pl.BlockSpec((1, TM, n), lambda b, s: (b, s, 0))
    tab_blk = pl.BlockSpec((TM, LANES), lambda b, s: (s, 0))
    vt_blk = pl.BlockSpec((1, TM // VCHUNK, LANES, VCHUNK), lambda b, s: (b, s, 0, 0))
    sds = jax.ShapeDtypeStruct

    (qt4, kc_raw, vc_raw, ksf, vst, kw, vwt, gates, zs, oconv) = pl.pallas_call(
        _proj_kernel,
        grid=(B, nst),
        in_specs=[row_blk(D), _full((1, D)), _full((D, D_NSA)), _full((D, 6 * LANES)), _full((D, LANES)),
                  _full((D, n_rest)), _full((1, 256)), _full((1, 256)),
                  tab_blk, tab_blk, tab_blk, _full((256, 256)), _full((3, D_CONV)), _full((1, D_CONV))],
        out_specs=[pl.BlockSpec((1, 4, LANES, TM), lambda b, s: (b, 0, 0, s)),
                   row_blk(LANES), row_blk(LANES), row_blk(2 * LANES), vt_blk, row_blk(LANES), vt_blk,
                   row_blk(LANES), row_blk(D_NSA), row_blk(D_CONV)],
        out_shape=[sds((B, 4, LANES, S), BF16), sds((B, S, LANES), F32), sds((B, S, LANES), F32),
                   sds((B, S, 2 * LANES), BF16), sds((B, S // VCHUNK, LANES, VCHUNK), BF16),
                   sds((B, S, LANES), BF16), sds((B, S // VCHUNK, LANES, VCHUNK), BF16),
                   sds((B, S, LANES), F32), sds((B, S, D_NSA), BF16), sds((B, S, D_CONV), BF16)],
        scratch_shapes=[pltpu.VMEM((TM + 16, D_CONV), F32)],
        compiler_params=cp,
        name="nsa_proj",
    )(x, f32(norm_w), wq, wkv, wg, wr, qnw, knw_sw, rc, rsa, rsb, mbd, f32(conv_w[0]), f32(conv_b))

    cmp_pos = np.arange(ncp) * CMP_STRIDE + (CMP_LEN - 1)
    crc, crsa, crsb = _rope_tables(cmp_pos)
    kpa, kpb, kw1a, kw1b, kb1, kw2 = _cmp_weights(f32(cmp_k_pos[0]), f32(cmp_k_w1[0]),
                                                  f32(cmp_k_b1[0]), f32(cmp_k_w2[0]))
    vpa, vpb, vw1a, vw1b, vb1, vw2 = _cmp_weights(f32(cmp_v_pos[0]), f32(cmp_v_w1[0]),
                                                  f32(cmp_v_b1[0]), f32(cmp_v_w2[0]))
    flat = CMP_STRIDE * LANES
    h_blk = pl.BlockSpec((1, S, LANES), lambda b: (b, 0, 0))
    kc, vct = pl.pallas_call(
        functools.partial(_cmp_kernel, ncp=ncp),
        grid=(B,),
        in_specs=[h_blk, h_blk] + [_full((1, flat))] * 4
                 + [_full((flat, 256)), _full((flat, 256)), _full((1, 256)), _full((256, LANES))] * 2
                 + [_full((1, LANES)), _full((ncp, LANES)), _full((ncp, LANES)), _full((ncp, LANES)),
                    _full((256, 256))],
        out_specs=[pl.BlockSpec((1, ncp, LANES), lambda b: (b, 0, 0)),
                   pl.BlockSpec((1, LANES, ncp), lambda b: (b, 0, 0))],
        out_shape=[sds((B, ncp, LANES), BF16), sds((B, LANES, ncp), BF16)],
        compiler_params=pltpu.CompilerParams(dimension_semantics=("arbitrary",),
                                             vmem_limit_bytes=VMEM_LIMIT),
        name="nsa_compress",
    )(kc_raw, vc_raw, kpa, kpb, vpa, vpb,
      kw1a, kw1b, kb1, kw2, vw1a, vw1b, vb1, vw2, knw_c, crc, crsa, crsb, mbd)

    score_bound = (HEAD_DIM * Q_SCALE) * jnp.max(jnp.abs(f32(q_norm_w))) * jnp.max(jnp.abs(f32(k_norm_w)))
    nqt = S // TQ

    def attention(bounded):
        return pl.pallas_call(
            functools.partial(_attn_kernel, ncp=ncp, bounded=bounded),
            grid=(B, nqt),
            in_specs=[pl.BlockSpec((1, 4, LANES, TQ), lambda b, i: (b, 0, 0, i)),
                      pl.BlockSpec((1, ncp, LANES), lambda b, i: (b, 0, 0)),
                      pl.BlockSpec((1, LANES, ncp), lambda b, i: (b, 0, 0)),
                      pl.BlockSpec((1, S, 2 * LANES), lambda b, i: (b, 0, 0)),
                      pl.BlockSpec((1, S // VCHUNK, LANES, VCHUNK), lambda b, i: (b, 0, 0, 0)),
                      pl.BlockSpec((1, S, LANES), lambda b, i: (b, 0, 0)),
                      pl.BlockSpec((1, S // VCHUNK, LANES, VCHUNK), lambda b, i: (b, 0, 0, 0)),
                      pl.BlockSpec((1, TQ, LANES), lambda b, i: (b, i, 0)),
                      pl.BlockSpec((1, TQ, D_NSA), lambda b, i: (b, i, 0)),
                      _full((SEL_BLOCK, ncp))],
            out_specs=pl.BlockSpec((1, TQ, D_NSA), lambda b, i: (b, i, 0)),
            out_shape=sds((B, S, D_NSA), BF16),
            scratch_shapes=[pltpu.VMEM((KV_GROUPS, LANES, TQ), BF16),
                            pltpu.VMEM((2, KV_GROUPS, TK, GROUP_HEADS * TQ), F32),
                            pltpu.VMEM((2, KV_GROUPS, 1, GROUP_HEADS * TQ), F32)],
            compiler_params=cp,
            name="nsa_attn_bounded" if bounded else "nsa_attn_online",
        )

    attn_args = (qt4, kc, vct, ksf, vst, kw, vwt, gates, zs, _overlap_t(ncp))
    o_nsa = lax.cond(score_bound <= MAX_SAFE_SCORE,
                     lambda *a: attention(True)(*a), lambda *a: attention(False)(*a), *attn_args)

    out_blk = lambda n: pl.BlockSpec((1, TM_OUT, n), lambda b, s: (b, s, 0))
    out = pl.pallas_call(
        _out_kernel,
        grid=(B, S // TM_OUT),
        in_specs=[out_blk(D), out_blk(D_NSA), out_blk(D_CONV), _full((D_NSA + D_CONV, D))],
        out_specs=out_blk(D),
        out_shape=sds((B, S, D), x.dtype),
        compiler_params=cp,
        name="nsa_out",
    )(x, o_nsa, oconv, f32(w_out[0]).astype(BF16))
    return out
```

```python
import functools

import numpy as np
import jax
import jax.numpy as jnp
from jax import lax
from jax.experimental import pallas as pl
from jax.experimental.pallas import tpu as pltpu

LANES = 128
SUBLANES = 8
HEAD_DIM = 64
NSA_HEADS = 8
KV_GROUPS = 2
GROUP_HEADS = NSA_HEADS // KV_GROUPS
D_NSA = NSA_HEADS * HEAD_DIM
D_CONV = 512
N_BRANCH = 3
ROT_DIM = HEAD_DIM // 4
ROPE_THETA = 500000.0
CMP_LEN = 32
CMP_STRIDE = 16
CMP_HIDDEN = 2 * HEAD_DIM
SEL_BLOCK = 64
SEL_TOPK = 16
WINDOW = 512
EPS = 1e-6
NEG_INF = -1e30
N_FORCED = 3
TAKEN = -2.0
SCALE = HEAD_DIM ** -0.5
Q_SCALE = SCALE * float(np.log2(np.e))

TM = 512
TM_OUT = 2048
TQ = 256
TK = 512
VCHUNK = 128
VMEM_LIMIT = 56 * 1024 * 1024
MAX_SAFE_SCORE = 50.0

BF16 = jnp.bfloat16
F32 = jnp.float32


def _dot(a, b):
    return jnp.dot(a, b, preferred_element_type=F32)


def _dot_nt(a, b):
    return lax.dot_general(a, b, (((1,), (1,)), ((), ())), preferred_element_type=F32)


def _split3(a):
    hi = a.astype(BF16)
    r1 = a - hi.astype(F32)
    mid = r1.astype(BF16)
    lo = (r1 - mid.astype(F32)).astype(BF16)
    return hi, mid, lo


def _group_mean(sq, mbd):
    return _dot(sq.astype(BF16), mbd)


def _rope(xn, c, sa, sb):
    return xn * c + pltpu.roll(xn, 8, 1) * sa + pltpu.roll(xn, LANES - 8, 1) * sb


def _silu(z):
    return z * (1.0 / (1.0 + jnp.exp(-z)))


def _proj_kernel(x_ref, nw_ref, wq_ref, wkv_ref, wg_ref, wr_ref, qnw_ref, knw_ref,
                 rc_ref, rsa_ref, rsb_ref, mbd_ref, cw_ref, cb_ref,
                 qt_ref, kc_ref, vc_ref, ksf_ref, vst_ref, kw_ref, vwt_ref,
                 gate_ref, zs_ref, oconv_ref, ubuf):
    si = pl.program_id(1)

    @pl.when(si == 0)
    def _():
        ubuf[0:8, :] = jnp.zeros((8, D_CONV), F32)

    x = x_ref[0]
    ms = jnp.mean(x * x, axis=-1, keepdims=True)
    h = (x * lax.rsqrt(ms + EPS) * nw_ref[...]).astype(BF16)

    rc, rsa, rsb = rc_ref[...], rsa_ref[...], rsb_ref[...]
    mbd = mbd_ref[...]

    pq = _dot(h, wq_ref[...])
    for pair in range(2):
        blk = pq[:, pair * 256:(pair + 1) * 256]
        msq = _group_mean(blk * blk, mbd)
        qn = blk * lax.rsqrt(msq + EPS) * qnw_ref[...]
        for half in range(2):
            t = qn[:, half * LANES:(half + 1) * LANES]
            qt_ref[0, pair * 2 + half] = (_rope(t, rc, rsa, rsb) * Q_SCALE).T.astype(BF16)

    pkv = _dot(h, wkv_ref[...])
    kc_ref[0] = pkv[:, 0:128]
    vc_ref[0] = pkv[:, 128:256]
    ksw = jnp.concatenate([pkv[:, 256:384], pkv[:, 512:640]], axis=1)
    msk = _group_mean(ksw * ksw, mbd)
    kn = ksw * lax.rsqrt(msk + EPS) * knw_ref[...]
    ks = _rope(kn[:, 0:128], rc, rsa, rsb)
    kw = _rope(kn[:, 128:256], rc, rsa, rsb)
    row = lax.broadcasted_iota(jnp.int32, (TM, LANES), 0) + si * TM
    lane = lax.broadcasted_iota(jnp.int32, (TM, LANES), 1)
    onehot = jnp.where(lane == (row >> 6), 1.0, 0.0)
    ksf_ref[0, :, 0:128] = ks.astype(BF16)
    ksf_ref[0, :, 128:256] = onehot.astype(BF16)
    kw_ref[0] = kw.astype(BF16)
    vs = pkv[:, 384:512]
    vw = pkv[:, 640:768]
    for j in range(TM // VCHUNK):
        vst_ref[0, j] = vs[j * VCHUNK:(j + 1) * VCHUNK, :].T.astype(BF16)
        vwt_ref[0, j] = vw[j * VCHUNK:(j + 1) * VCHUNK, :].T.astype(BF16)

    gate_ref[0] = 1.0 / (1.0 + jnp.exp(-_dot(h, wg_ref[...])))
    zs_ref[0] = _silu(_dot(h, wr_ref[:, 0:D_NSA])).astype(BF16)

    pcv = _dot(h, wr_ref[:, D_NSA:D_NSA + 4 * D_CONV])
    u = pcv[:, 1024:1536] * pcv[:, 0:512]
    ubuf[8:8 + TM, :] = u
    u1 = ubuf[7:7 + TM, :]
    u2 = ubuf[6:6 + TM, :]
    conv = cw_ref[0:1, :] * u2 + cw_ref[1:2, :] * u1 + cw_ref[2:3, :] * u + cb_ref[...]
    oconv_ref[0] = (pcv[:, 512:1024] * conv * _silu(pcv[:, 1536:2048])).astype(BF16)
    ubuf[0:8, :] = ubuf[TM:TM + 8, :]


def _cmp_kernel(hk_ref, hv_ref, pka_ref, pkb_ref, pva_ref, pvb_ref,
                wk1a_ref, wk1b_ref, bk1_ref, wk2_ref,
                wv1a_ref, wv1b_ref, bv1_ref, wv2_ref,
                knw_ref, rc_ref, rsa_ref, rsb_ref, mbd_ref,
                kc_ref, vct_ref, *, ncp):
    def first_layer(h_ref, p_ref, w_ref):
        acc = None
        for l in range(0, CMP_STRIDE, 2):
            lhs = jnp.concatenate([h_ref[0, pl.ds(l, ncp, stride=CMP_STRIDE), :],
                                   h_ref[0, pl.ds(l + 1, ncp, stride=CMP_STRIDE), :]], axis=1)
            lhs = (lhs + p_ref[:, l * LANES:(l + 2) * LANES]).astype(BF16)
            part = _dot(lhs, w_ref[l * LANES:(l + 2) * LANES, :])
            acc = part if acc is None else acc + part
        return acc

    def mlp(h_ref, pa_ref, pb_ref, w1a_ref, w1b_ref, b1_ref, w2_ref):
        p = first_layer(h_ref, pa_ref, w1a_ref)
        q = first_layer(h_ref, pb_ref, w1b_ref)
        pre = p + pltpu.roll(q, ncp - 1, 0) + b1_ref[...]
        return _dot(_silu(pre).astype(BF16), w2_ref[...])

    kc = mlp(hk_ref, pka_ref, pkb_ref, wk1a_ref, wk1b_ref, bk1_ref, wk2_ref)
    msk = _group_mean(kc * kc, mbd_ref[0:LANES, 0:LANES])
    kn = kc * lax.rsqrt(msk + EPS) * knw_ref[...]
    kc_ref[0] = _rope(kn, rc_ref[...], rsa_ref[...], rsb_ref[...]).astype(BF16)
    vc = mlp(hv_ref, pva_ref, pvb_ref, wv1a_ref, wv1b_ref, bv1_ref, wv2_ref)
    vct_ref[0] = vc.T.astype(BF16)


def _attn_kernel(qt_ref, kc_ref, vct_ref, ksf_ref, vst_ref, kw_ref, vwt_ref,
                 gate_ref, zs_ref, ovt_ref, o_ref, pen_sc, s_sc, cm_sc, *, ncp, bounded):
    i = pl.program_id(1)
    t0 = i * TQ
    rows = GROUP_HEADS * TQ
    nblk = SEL_BLOCK
    groups = range(KV_GROUPS)

    q_all = jnp.concatenate([qt_ref[0, j] for j in range(GROUP_HEADS)], axis=1)
    frow = lax.broadcasted_iota(jnp.int32, (LANES, rows), 0)
    qg = [jnp.where(frow < HEAD_DIM, q_all, 0), jnp.where(frow >= HEAD_DIM, q_all, 0)]
    t_q = t0 + lax.broadcasted_iota(jnp.int32, (1, TQ), 1)

    def fill_where(a, mask, fill):
        return jnp.concatenate([jnp.where(mask, a[:, r * TQ:(r + 1) * TQ], fill)
                                for r in range(GROUP_HEADS)], axis=1)

    vrows = lambda v, g: v[g * HEAD_DIM:(g + 1) * HEAD_DIM, :]

    s_cmp = [_dot(kc_ref[0], qg[g]) for g in groups]

    c_idx = lax.broadcasted_iota(jnp.int32, (ncp, 1), 0)
    mask_c = ((c_idx * CMP_STRIDE + (CMP_LEN - 1)) <= t_q) & (c_idx < ncp - 1)
    p_c, o_c = [], []
    for g in groups:
        if bounded:
            e_c = fill_where(jnp.exp2(s_cmp[g]), mask_c, 0.0)
        else:
            sm = fill_where(s_cmp[g], mask_c, NEG_INF)
            e_c = fill_where(jnp.exp2(sm - jnp.max(sm, axis=0, keepdims=True)), mask_c, 0.0)
        l_c = jnp.sum(e_c, axis=0, keepdims=True)
        p = e_c / jnp.where(l_c > 0.0, l_c, 1.0)
        p_c.append(p)
        o_c.append(_dot(vrows(vct_ref[0], g), p.astype(BF16)))

    n_idx = lax.broadcasted_iota(jnp.int32, (nblk, TQ), 0)
    tq = t0 + lax.broadcasted_iota(jnp.int32, (nblk, TQ), 1)
    cur = tq >> 6
    forced = (n_idx == 0) | (n_idx == cur) | (n_idx == cur - 1)
    visible = (n_idx << 6) <= tq
    ovt = ovt_ref[...]
    left, pen_t = [], []
    for g in groups:
        psum = p_c[g][:, 0:TQ]
        for r in range(1, GROUP_HEADS):
            psum = psum + p_c[g][:, r * TQ:(r + 1) * TQ]
        hi, mid, lo = _split3(psum)
        imp = _dot(ovt, hi) + _dot(ovt, mid) + _dot(ovt, lo)
        left.append(jnp.where(forced, TAKEN, jnp.where(visible, imp, -1.0)))
        pen_t.append(jnp.where(forced, 0.0, NEG_INF))
    for _ in range(SEL_TOPK - N_FORCED):
        for g in groups:
            top = jnp.max(left[g], axis=0, keepdims=True)
            first = jnp.min(jnp.where(left[g] == top, n_idx, nblk), axis=0, keepdims=True)
            pick = n_idx == first
            left[g] = jnp.where(pick, TAKEN, left[g])
            pen_t[g] = jnp.where(pick, 0.0, pen_t[g])
    for g in groups:
        pen_sc[g, 0:nblk, :] = pen_t[g].astype(BF16)
        pen_sc[g, nblk:LANES, :] = jnp.zeros((LANES - nblk, TQ), BF16)

    nwc = WINDOW // VCHUNK + TQ // VCHUNK
    cw = jnp.maximum(i * (TQ // VCHUNK) - WINDOW // VCHUNK, 0)
    w0 = pl.multiple_of(cw * VCHUNK, VCHUNK)
    kwin = kw_ref[0, pl.ds(w0, nwc * VCHUNK), :]
    s_win = [_dot(kwin, qg[g]) for g in groups]

    q_aug = [jnp.concatenate([qg[g], jnp.concatenate([pen_sc[g]] * GROUP_HEADS, axis=1)], axis=0)
             for g in groups]

    def window_branch():
        vwin = jnp.concatenate([vwt_ref[0, cw + j] for j in range(nwc)], axis=1)
        kpos = w0 + lax.broadcasted_iota(jnp.int32, (nwc * VCHUNK, 1), 0)
        mask_w = (kpos <= t_q) & (kpos > t_q - WINDOW)
        o_w, l_w = [], []
        for g in groups:
            if bounded:
                p_w = fill_where(jnp.exp2(s_win[g]), mask_w, 0.0)
            else:
                sw = fill_where(s_win[g], mask_w, NEG_INF)
                p_w = jnp.exp2(sw - jnp.max(sw, axis=0, keepdims=True))
            l_w.append(jnp.sum(p_w, axis=0, keepdims=True))
            o_w.append(_dot(vrows(vwin, g), p_w.astype(BF16)))
        return o_w, l_w

    def v_chunk(c, g):
        cv = c * (TK // VCHUNK)
        return jnp.concatenate([vrows(vst_ref[0, cv + j], g) for j in range(TK // VCHUNK)], axis=1)

    def qk(c, slot, g):
        k0 = pl.multiple_of(c * TK, TK)
        s = _dot(ksf_ref[0, pl.ds(k0, TK), :], q_aug[g])
        s_sc[slot, g] = s
        if not bounded:
            cm_sc[slot, g] = jnp.max(s, axis=0, keepdims=True)

    def sm_pv(c, slot, g, state, masked):
        m_p, l_p, acc = state
        s = s_sc[slot, g]
        if masked:
            kpos = c * TK + lax.broadcasted_iota(jnp.int32, (TK, 1), 0)
            s = fill_where(s, kpos <= t_q, NEG_INF)
        if bounded:
            p = jnp.exp2(s)
            return m_p, l_p + jnp.sum(p, axis=0, keepdims=True), acc + _dot(v_chunk(c, g), p.astype(BF16))
        cm = jnp.max(s, axis=0, keepdims=True) if masked else cm_sc[slot, g]
        m_n = jnp.maximum(m_p, cm)
        alpha = jnp.exp2(m_p - m_n)
        p = jnp.exp2(s - m_n)
        l_n = alpha * l_p + jnp.sum(p, axis=0, keepdims=True)
        acc = alpha * acc + _dot(v_chunk(c, g), p.astype(BF16))
        return m_n, l_n, acc

    def step(c, slot, carry):
        out = []
        for g in groups:
            qk(c + 1, 1 - slot, g)
            out.append(sm_pv(c, slot, g, carry[g], False))
        return tuple(out)

    def last(slot, carry):
        return tuple(sm_pv(n_full, slot, g, carry[g], True) for g in groups)

    n_full = t0 // TK
    for g in groups:
        qk(0, 0, g)
    o_w, l_w = window_branch()
    init = tuple((jnp.full((1, rows), 0.0 if bounded else NEG_INF, F32), jnp.zeros((1, rows), F32),
                  jnp.zeros((HEAD_DIM, rows), F32)) for _ in groups)
    carry = lax.fori_loop(0, n_full // 2, lambda h, cr: step(2 * h + 1, 1, step(2 * h, 0, cr)), init)
    carry = lax.cond((n_full & 1) == 1,
                     lambda cr: last(1, step(n_full - 1, 0, cr)),
                     lambda cr: last(0, cr), carry)
    sel_out = [(carry[g][1], carry[g][2]) for g in groups]

    g_t = gate_ref[0].T
    slabs = []
    for g in groups:
        def gate_row(br):
            base = g * (N_BRANCH * GROUP_HEADS) + br * GROUP_HEADS
            return jnp.concatenate([g_t[base + r:base + r + 1, :] for r in range(GROUP_HEADS)], axis=1)
        l_s, o_s = sel_out[g]
        og = o_c[g] * gate_row(0) + o_s * (gate_row(1) / l_s) + o_w[g] * (gate_row(2) / l_w[g])
        slabs += [og[:, r * TQ:(r + 1) * TQ] for r in range(GROUP_HEADS)]
    out_t = jnp.concatenate(slabs, axis=0)
    o_ref[0] = (out_t.T * zs_ref[0].astype(F32)).astype(BF16)


def _out_kernel(x_ref, on_ref, oc_ref, w_ref, o_ref):
    acc = _dot(on_ref[0], w_ref[0:D_NSA, :]) + _dot(oc_ref[0], w_ref[D_NSA:D_NSA + D_CONV, :])
    o_ref[0] = x_ref[0] + acc


def _rope_tables(pos):
    inv_freq = (np.float32(ROPE_THETA) ** (-np.arange(0, ROT_DIM, 2, dtype=np.float32) / ROT_DIM)).astype(np.float32)
    ang = pos.astype(np.float32)[:, None] * inv_freq[None, :]
    cos, sin = np.cos(ang), np.sin(ang)
    n = pos.shape[0]
    ones = np.ones((n, HEAD_DIM - ROT_DIM), np.float32)
    zeros = np.zeros((n, HEAD_DIM - ROT_DIM), np.float32)
    z8 = np.zeros((n, ROT_DIM // 2), np.float32)
    c = np.concatenate([cos, cos, ones], axis=1)
    sa = np.concatenate([z8, sin, zeros], axis=1)
    sb = np.concatenate([-sin, z8, zeros], axis=1)
    return tuple(jnp.asarray(np.concatenate([t, t], axis=1), F32) for t in (c, sa, sb))


def _overlap_t(ncp):
    cs = np.arange(ncp) * CMP_STRIDE
    ce = cs + CMP_LEN
    ss = np.arange(SEL_BLOCK) * SEL_BLOCK
    se = ss + SEL_BLOCK
    ov = np.clip(np.minimum(ce[None, :], se[:, None]) - np.maximum(cs[None, :], ss[:, None]), 0, None)
    ov = ov.astype(np.float32) / CMP_LEN
    ov[:, ncp - 1] = 0.0
    return jnp.asarray(ov, BF16)


def _cmp_weights(pos, w1, b1, w2):
    half = CMP_STRIDE
    def big_w1(w):
        w = w.reshape(half, 1, HEAD_DIM, 1, CMP_HIDDEN)
        eye = jnp.eye(KV_GROUPS, dtype=F32).reshape(1, KV_GROUPS, 1, KV_GROUPS, 1)
        return (w * eye).reshape(half * KV_GROUPS * HEAD_DIM, KV_GROUPS * CMP_HIDDEN).astype(BF16)
    def big_pos(p):
        return jnp.broadcast_to(p[:, None, :], (half, KV_GROUPS, HEAD_DIM)).reshape(1, -1)
    w1a, w1b = w1[:half * HEAD_DIM], w1[half * HEAD_DIM:]
    eye2 = jnp.eye(KV_GROUPS, dtype=F32)
    w2b = (w2[None, :, None, :] * eye2[:, None, :, None]).reshape(KV_GROUPS * CMP_HIDDEN,
                                                                 KV_GROUPS * HEAD_DIM).astype(BF16)
    b1b = jnp.tile(b1, KV_GROUPS)[None, :]
    return big_pos(pos[:half]), big_pos(pos[half:]), big_w1(w1a), big_w1(w1b), b1b, w2b


def _full(shape):
    nd = len(shape)
    return pl.BlockSpec(shape, lambda *_: (0,) * nd)


def kernel(x, norm_w, w_in, q_norm_w, k_norm_w, cmp_k_pos, cmp_k_w1, cmp_k_b1, cmp_k_w2,
           cmp_v_pos, cmp_v_w1, cmp_v_b1, cmp_v_w2, conv_w, conv_b, w_out):
    B, S, D = x.shape
    assert norm_w.shape[0] == 1, "single layer"
    assert S % TM == 0 and S % TM_OUT == 0 and S // SEL_BLOCK <= SEL_BLOCK and S >= WINDOW + TQ
    ncp = S // CMP_STRIDE
    nst = S // TM
    f32 = lambda a: a.astype(F32)

    w = f32(w_in[0])
    head_order = []
    for j in range(GROUP_HEADS):
        head_order += [j, GROUP_HEADS + j]
    wq = w[:, :D_NSA].reshape(D, NSA_HEADS, HEAD_DIM)[:, jnp.array(head_order), :].reshape(D, D_NSA)
    gate_cols = []
    for g in range(KV_GROUPS):
        for br in range(N_BRANCH):
            for r in range(GROUP_HEADS):
                gate_cols.append((g * GROUP_HEADS + r) * N_BRANCH + br)
    g0 = D_NSA + 6 * LANES
    wg = w[:, g0:g0 + NSA_HEADS * N_BRANCH][:, jnp.array(gate_cols)]
    wg = jnp.pad(wg, ((0, 0), (0, LANES - NSA_HEADS * N_BRANCH)))
    rest0 = g0 + NSA_HEADS * N_BRANCH
    wq, wkv, wg, wr = (t.astype(BF16) for t in (wq, w[:, D_NSA:g0], wg, w[:, rest0:]))
    n_rest = wr.shape[1]

    rc, rsa, rsb = _rope_tables(np.arange(S))
    mbd = jnp.asarray(np.kron(np.eye(4), np.full((HEAD_DIM, HEAD_DIM), 1.0 / HEAD_DIM)), BF16)
    qnw = jnp.tile(f32(q_norm_w[0]), 4)[None, :]
    knw_sw = jnp.concatenate([jnp.tile(f32(k_norm_w[0, 1]), 2), jnp.tile(f32(k_norm_w[0, 2]), 2)])[None, :]
    knw_c = jnp.tile(f32(k_norm_w[0, 0]), 2)[None, :]

    cp = pltpu.CompilerParams(dimension_semantics=("arbitrary", "arbitrary"),
                              vmem_limit_bytes=VMEM_LIMIT)
    row_blk = lambda n: pl.BlockSpec((1, TM, n), lambda b, s: (b, s, 0))
    tab_blk = pl.BlockSpec((TM, LANES), lambda b, s: (s, 0))
    vt_blk = pl.BlockSpec((1, TM // VCHUNK, LANES, VCHUNK), lambda b, s: (b, s, 0, 0))
    sds = jax.ShapeDtypeStruct

    (qt4, kc_raw, vc_raw, ksf, vst, kw, vwt, gates, zs, oconv) = pl.pallas_call(
        _proj_kernel,
        grid=(B, nst),
        in_specs=[row_blk(D), _full((1, D)), _full((D, D_NSA)), _full((D, 6 * LANES)), _full((D, LANES)),
                  _full((D, n_rest)), _full((1, 256)), _full((1, 256)),
                  tab_blk, tab_blk, tab_blk, _full((256, 256)), _full((3, D_CONV)), _full((1, D_CONV))],
        out_specs=[pl.BlockSpec((1, 4, LANES, TM), lambda b, s: (b, 0, 0, s)),
                   row_blk(LANES), row_blk(LANES), row_blk(2 * LANES), vt_blk, row_blk(LANES), vt_blk,
                   row_blk(LANES), row_blk(D_NSA), row_blk(D_CONV)],
        out_shape=[sds((B, 4, LANES, S), BF16), sds((B, S, LANES), F32), sds((B, S, LANES), F32),
                   sds((B, S, 2 * LANES), BF16), sds((B, S // VCHUNK, LANES, VCHUNK), BF16),
                   sds((B, S, LANES), BF16), sds((B, S // VCHUNK, LANES, VCHUNK), BF16),
                   sds((B, S, LANES), F32), sds((B, S, D_NSA), BF16), sds((B, S, D_CONV), BF16)],
        scratch_shapes=[pltpu.VMEM((TM + 16, D_CONV), F32)],
        compiler_params=cp,
        name="nsa_proj",
    )(x, f32(norm_w), wq, wkv, wg, wr, qnw, knw_sw, rc, rsa, rsb, mbd, f32(conv_w[0]), f32(conv_b))

    cmp_pos = np.arange(ncp) * CMP_STRIDE + (CMP_LEN - 1)
    crc, crsa, crsb = _rope_tables(cmp_pos)
    kpa, kpb, kw1a, kw1b, kb1, kw2 = _cmp_weights(f32(cmp_k_pos[0]), f32(cmp_k_w1[0]),
                                                  f32(cmp_k_b1[0]), f32(cmp_k_w2[0]))
    vpa, vpb, vw1a, vw1b, vb1, vw2 = _cmp_weights(f32(cmp_v_pos[0]), f32(cmp_v_w1[0]),
                                                  f32(cmp_v_b1[0]), f32(cmp_v_w2[0]))
    flat = CMP_STRIDE * LANES
    h_blk = pl.BlockSpec((1, S, LANES), lambda b: (b, 0, 0))
    kc, vct = pl.pallas_call(
        functools.partial(_cmp_kernel, ncp=ncp),
        grid=(B,),
        in_specs=[h_blk, h_blk] + [_full((1, flat))] * 4
                 + [_full((flat, 256)), _full((flat, 256)), _full((1, 256)), _full((256, LANES))] * 2
                 + [_full((1, LANES)), _full((ncp, LANES)), _full((ncp, LANES)), _full((ncp, LANES)),
                    _full((256, 256))],
        out_specs=[pl.BlockSpec((1, ncp, LANES), lambda b: (b, 0, 0)),
                   pl.BlockSpec((1, LANES, ncp), lambda b: (b, 0, 0))],
        out_shape=[sds((B, ncp, LANES), BF16), sds((B, LANES, ncp), BF16)],
        compiler_params=pltpu.CompilerParams(dimension_semantics=("arbitrary",),
                                             vmem_limit_bytes=VMEM_LIMIT),
        name="nsa_compress",
    )(kc_raw, vc_raw, kpa, kpb, vpa, vpb,
      kw1a, kw1b, kb1, kw2, vw1a, vw1b, vb1, vw2, knw_c, crc, crsa, crsb, mbd)

    score_bound = (HEAD_DIM * Q_SCALE) * jnp.max(jnp.abs(f32(q_norm_w))) * jnp.max(jnp.abs(f32(k_norm_w)))
    nqt = S // TQ

    def attention(bounded):
        return pl.pallas_call(
            functools.partial(_attn_kernel, ncp=ncp, bounded=bounded),
            grid=(B, nqt),
            in_specs=[pl.BlockSpec((1, 4, LANES, TQ), lambda b, i: (b, 0, 0, i)),
                      pl.BlockSpec((1, ncp, LANES), lambda b, i: (b, 0, 0)),
                      pl.BlockSpec((1, LANES, ncp), lambda b, i: (b, 0, 0)),
                      pl.BlockSpec((1, S, 2 * LANES), lambda b, i: (b, 0, 0)),
                      pl.BlockSpec((1, S // VCHUNK, LANES, VCHUNK), lambda b, i: (b, 0, 0, 0)),
                      pl.BlockSpec((1, S, LANES), lambda b, i: (b, 0, 0)),
                      pl.BlockSpec((1, S // VCHUNK, LANES, VCHUNK), lambda b, i: (b, 0, 0, 0)),
                      pl.BlockSpec((1, TQ, LANES), lambda b, i: (b, i, 0)),
                      pl.BlockSpec((1, TQ, D_NSA), lambda b, i: (b, i, 0)),
                      _full((SEL_BLOCK, ncp))],
            out_specs=pl.BlockSpec((1, TQ, D_NSA), lambda b, i: (b, i, 0)),
            out_shape=sds((B, S, D_NSA), BF16),
            scratch_shapes=[pltpu.VMEM((KV_GROUPS, LANES, TQ), BF16),
                            pltpu.VMEM((2, KV_GROUPS, TK, GROUP_HEADS * TQ), F32),
                            pltpu.VMEM((2, KV_GROUPS, 1, GROUP_HEADS * TQ), F32)],
            compiler_params=cp,
            name="nsa_attn_bounded" if bounded else "nsa_attn_online",
        )

    attn_args = (qt4, kc, vct, ksf, vst, kw, vwt, gates, zs, _overlap_t(ncp))
    o_nsa = lax.cond(score_bound <= MAX_SAFE_SCORE,
                     lambda *a: attention(True)(*a), lambda *a: attention(False)(*a), *attn_args)

    out_blk = lambda n: pl.BlockSpec((1, TM_OUT, n), lambda b, s: (b, s, 0))
    out = pl.pallas_call(
        _out_kernel,
        grid=(B, S // TM_OUT),
        in_specs=[out_blk(D), out_blk(D_NSA), out_blk(D_CONV), _full((D_NSA + D_CONV, D))],
        out_specs=out_blk(D),
        out_shape=sds((B, S, D), x.dtype),
        compiler_params=cp,
        name="nsa_out",
    )(x, o_nsa, oconv, f32(w_out[0]).astype(BF16))
    return out
```

```python
import functools

import numpy as np
import jax
import jax.numpy as jnp
from jax import lax
from jax.experimental import pallas as pl
from jax.experimental.pallas import tpu as pltpu

LANES = 128
SUBLANES = 8
HEAD_DIM = 64
NSA_HEADS = 8
KV_GROUPS = 2
GROUP_HEADS = NSA_HEADS // KV_GROUPS
D_NSA = NSA_HEADS * HEAD_DIM
D_CONV = 512
N_BRANCH = 3
ROT_DIM = HEAD_DIM // 4
ROPE_THETA = 500000.0
CMP_LEN = 32
CMP_STRIDE = 16
CMP_HIDDEN = 2 * HEAD_DIM
SEL_BLOCK = 64
SEL_TOPK = 16
WINDOW = 512
EPS = 1e-6
NEG_INF = -1e30
N_FORCED = 3
TAKEN = -2.0
SCALE = HEAD_DIM ** -0.5
Q_SCALE = SCALE * float(np.log2(np.e))

TM = 512
TM_OUT = 1024
TQ = 256
TK = 512
VCHUNK = 128
VMEM_LIMIT = 56 * 1024 * 1024
MAX_SAFE_SCORE = -1.0

BF16 = jnp.bfloat16
F32 = jnp.float32


def _dot(a, b):
    return jnp.dot(a, b, preferred_element_type=F32)


def _dot_nt(a, b):
    return lax.dot_general(a, b, (((1,), (1,)), ((), ())), preferred_element_type=F32)


def _split3(a):
    hi = a.astype(BF16)
    r1 = a - hi.astype(F32)
    mid = r1.astype(BF16)
    lo = (r1 - mid.astype(F32)).astype(BF16)
    return hi, mid, lo


def _group_mean(sq, mbd):
    return _dot(sq.astype(BF16), mbd)


def _rope(xn, c, sa, sb):
    return xn * c + pltpu.roll(xn, 8, 1) * sa + pltpu.roll(xn, LANES - 8, 1) * sb


def _silu(z):
    return z * (1.0 / (1.0 + jnp.exp(-z)))


def _proj_kernel(x_ref, nw_ref, wq_ref, wkv_ref, wg_ref, wr_ref, qnw_ref, knw_ref,
                 rc_ref, rsa_ref, rsb_ref, mbd_ref, cw_ref, cb_ref,
                 qt_ref, kc_ref, vc_ref, ksf_ref, vst_ref, kw_ref, vwt_ref,
                 gate_ref, zs_ref, oconv_ref, ubuf):
    si = pl.program_id(1)

    @pl.when(si == 0)
    def _():
        ubuf[0:8, :] = jnp.zeros((8, D_CONV), F32)

    x = x_ref[0]
    ms = jnp.mean(x * x, axis=-1, keepdims=True)
    h = (x * lax.rsqrt(ms + EPS) * nw_ref[...]).astype(BF16)

    rc, rsa, rsb = rc_ref[...], rsa_ref[...], rsb_ref[...]
    mbd = mbd_ref[...]

    pq = _dot(h, wq_ref[...])
    for pair in range(2):
        blk = pq[:, pair * 256:(pair + 1) * 256]
        msq = _group_mean(blk * blk, mbd)
        qn = blk * lax.rsqrt(msq + EPS) * qnw_ref[...]
        for half in range(2):
            t = qn[:, half * LANES:(half + 1) * LANES]
            qt_ref[0, pair * 2 + half] = (_rope(t, rc, rsa, rsb) * Q_SCALE).T.astype(BF16)

    pkv = _dot(h, wkv_ref[...])
    kc_ref[0] = pkv[:, 0:128]
    vc_ref[0] = pkv[:, 128:256]
    ksw = jnp.concatenate([pkv[:, 256:384], pkv[:, 512:640]], axis=1)
    msk = _group_mean(ksw * ksw, mbd)
    kn = ksw * lax.rsqrt(msk + EPS) * knw_ref[...]
    ks = _rope(kn[:, 0:128], rc, rsa, rsb)
    kw = _rope(kn[:, 128:256], rc, rsa, rsb)
    row = lax.broadcasted_iota(jnp.int32, (TM, LANES), 0) + si * TM
    lane = lax.broadcasted_iota(jnp.int32, (TM, LANES), 1)
    onehot = jnp.where(lane == (row >> 6), 1.0, 0.0)
    ksf_ref[0, :, 0:128] = ks.astype(BF16)
    ksf_ref[0, :, 128:256] = onehot.astype(BF16)
    kw_ref[0] = kw.astype(BF16)
    vs = pkv[:, 384:512]
    vw = pkv[:, 640:768]
    for j in range(TM // VCHUNK):
        vst_ref[0, j] = vs[j * VCHUNK:(j + 1) * VCHUNK, :].T.astype(BF16)
        vwt_ref[0, j] = vw[j * VCHUNK:(j + 1) * VCHUNK, :].T.astype(BF16)

    gate_ref[0] = 1.0 / (1.0 + jnp.exp(-_dot(h, wg_ref[...])))
    zs_ref[0] = _silu(_dot(h, wr_ref[:, 0:D_NSA])).astype(BF16)

    pcv = _dot(h, wr_ref[:, D_NSA:D_NSA + 4 * D_CONV])
    u = pcv[:, 1024:1536] * pcv[:, 0:512]
    ubuf[8:8 + TM, :] = u
    u1 = ubuf[7:7 + TM, :]
    u2 = ubuf[6:6 + TM, :]
    conv = cw_ref[0:1, :] * u2 + cw_ref[1:2, :] * u1 + cw_ref[2:3, :] * u + cb_ref[...]
    oconv_ref[0] = (pcv[:, 512:1024] * conv * _silu(pcv[:, 1536:2048])).astype(BF16)
    ubuf[0:8, :] = ubuf[TM:TM + 8, :]


def _cmp_kernel(hk_ref, hv_ref, pka_ref, pkb_ref, pva_ref, pvb_ref,
                wk1a_ref, wk1b_ref, bk1_ref, wk2_ref,
                wv1a_ref, wv1b_ref, bv1_ref, wv2_ref,
                knw_ref, rc_ref, rsa_ref, rsb_ref, mbd_ref,
                kc_ref, vct_ref, *, ncp):
    def first_layer(h_ref, p_ref, w_ref):
        acc = None
        for l in range(0, CMP_STRIDE, 2):
            lhs = jnp.concatenate([h_ref[0, pl.ds(l, ncp, stride=CMP_STRIDE), :],
                                   h_ref[0, pl.ds(l + 1, ncp, stride=CMP_STRIDE), :]], axis=1)
            lhs = (lhs + p_ref[:, l * LANES:(l + 2) * LANES]).astype(BF16)
            part = _dot(lhs, w_ref[l * LANES:(l + 2) * LANES, :])
            acc = part if acc is None else acc + part
        return acc

    def mlp(h_ref, pa_ref, pb_ref, w1a_ref, w1b_ref, b1_ref, w2_ref):
        p = first_layer(h_ref, pa_ref, w1a_ref)
        q = first_layer(h_ref, pb_ref, w1b_ref)
        pre = p + pltpu.roll(q, ncp - 1, 0) + b1_ref[...]
        return _dot(_silu(pre).astype(BF16), w2_ref[...])

    kc = mlp(hk_ref, pka_ref, pkb_ref, wk1a_ref, wk1b_ref, bk1_ref, wk2_ref)
    msk = _group_mean(kc * kc, mbd_ref[0:LANES, 0:LANES])
    kn = kc * lax.rsqrt(msk + EPS) * knw_ref[...]
    kc_ref[0] = _rope(kn, rc_ref[...], rsa_ref[...], rsb_ref[...]).astype(BF16)
    vc = mlp(hv_ref, pva_ref, pvb_ref, wv1a_ref, wv1b_ref, bv1_ref, wv2_ref)
    vct_ref[0] = vc.T.astype(BF16)


def _attn_kernel(qt_ref, kc_ref, vct_ref, ksf_ref, vst_ref, kw_ref, vwt_ref,
                 gate_ref, zs_ref, ovt_ref, o_ref, pen_sc, s_sc, cm_sc, *, ncp, bounded):
    i = pl.program_id(1)
    t0 = i * TQ
    rows = GROUP_HEADS * TQ
    nblk = SEL_BLOCK
    groups = range(KV_GROUPS)

    q_all = jnp.concatenate([qt_ref[0, j] for j in range(GROUP_HEADS)], axis=1)
    frow = lax.broadcasted_iota(jnp.int32, (LANES, rows), 0)
    qg = [jnp.where(frow < HEAD_DIM, q_all, 0), jnp.where(frow >= HEAD_DIM, q_all, 0)]
    t_q = t0 + lax.broadcasted_iota(jnp.int32, (1, TQ), 1)

    def fill_where(a, mask, fill):
        return jnp.concatenate([jnp.where(mask, a[:, r * TQ:(r + 1) * TQ], fill)
                                for r in range(GROUP_HEADS)], axis=1)

    vrows = lambda v, g: v[g * HEAD_DIM:(g + 1) * HEAD_DIM, :]

    s_cmp = [_dot(kc_ref[0], qg[g]) for g in groups]

    c_idx = lax.broadcasted_iota(jnp.int32, (ncp, 1), 0)
    mask_c = ((c_idx * CMP_STRIDE + (CMP_LEN - 1)) <= t_q) & (c_idx < ncp - 1)
    p_c, o_c = [], []
    for g in groups:
        if bounded:
            e_c = fill_where(jnp.exp2(s_cmp[g]), mask_c, 0.0)
        else:
            sm = fill_where(s_cmp[g], mask_c, NEG_INF)
            e_c = fill_where(jnp.exp2(sm - jnp.max(sm, axis=0, keepdims=True)), mask_c, 0.0)
        l_c = jnp.sum(e_c, axis=0, keepdims=True)
        p = e_c / jnp.where(l_c > 0.0, l_c, 1.0)
        p_c.append(p)
        o_c.append(_dot(vrows(vct_ref[0], g), p.astype(BF16)))

    n_idx = lax.broadcasted_iota(jnp.int32, (nblk, TQ), 0)
    tq = t0 + lax.broadcasted_iota(jnp.int32, (nblk, TQ), 1)
    cur = tq >> 6
    forced = (n_idx == 0) | (n_idx == cur) | (n_idx == cur - 1)
    visible = (n_idx << 6) <= tq
    ovt = ovt_ref[...]
    left, pen_t = [], []
    for g in groups:
        psum = p_c[g][:, 0:TQ]
        for r in range(1, GROUP_HEADS):
            psum = psum + p_c[g][:, r * TQ:(r + 1) * TQ]
        hi, mid, lo = _split3(psum)
        imp = _dot(ovt, hi) + _dot(ovt, mid) + _dot(ovt, lo)
        left.append(jnp.where(forced, TAKEN, jnp.where(visible, imp, -1.0)))
        pen_t.append(jnp.where(forced, 0.0, NEG_INF))
    for _ in range(SEL_TOPK - N_FORCED):
        for g in groups:
            top = jnp.max(left[g], axis=0, keepdims=True)
            first = jnp.min(jnp.where(left[g] == top, n_idx, nblk), axis=0, keepdims=True)
            pick = n_idx == first
            left[g] = jnp.where(pick, TAKEN, left[g])
            pen_t[g] = jnp.where(pick, 0.0, pen_t[g])
    for g in groups:
        pen_sc[g, 0:nblk, :] = pen_t[g].astype(BF16)
        pen_sc[g, nblk:LANES, :] = jnp.zeros((LANES - nblk, TQ), BF16)

    nwc = WINDOW // VCHUNK + TQ // VCHUNK
    cw = jnp.maximum(i * (TQ // VCHUNK) - WINDOW // VCHUNK, 0)
    w0 = pl.multiple_of(cw * VCHUNK, VCHUNK)
    kwin = kw_ref[0, pl.ds(w0, nwc * VCHUNK), :]
    s_win = [_dot(kwin, qg[g]) for g in groups]

    q_aug = [jnp.concatenate([qg[g], jnp.concatenate([pen_sc[g]] * GROUP_HEADS, axis=1)], axis=0)
             for g in groups]

    def window_branch():
        vwin = jnp.concatenate([vwt_ref[0, cw + j] for j in range(nwc)], axis=1)
        kpos = w0 + lax.broadcasted_iota(jnp.int32, (nwc * VCHUNK, 1), 0)
        mask_w = (kpos <= t_q) & (kpos > t_q - WINDOW)
        o_w, l_w = [], []
        for g in groups:
            if bounded:
                p_w = fill_where(jnp.exp2(s_win[g]), mask_w, 0.0)
            else:
                sw = fill_where(s_win[g], mask_w, NEG_INF)
                p_w = jnp.exp2(sw - jnp.max(sw, axis=0, keepdims=True))
            l_w.append(jnp.sum(p_w, axis=0, keepdims=True))
            o_w.append(_dot(vrows(vwin, g), p_w.astype(BF16)))
        return o_w, l_w

    def v_chunk(c, g):
        cv = c * (TK // VCHUNK)
        return jnp.concatenate([vrows(vst_ref[0, cv + j], g) for j in range(TK // VCHUNK)], axis=1)

    def qk(c, slot, g):
        k0 = pl.multiple_of(c * TK, TK)
        s = _dot(ksf_ref[0, pl.ds(k0, TK), :], q_aug[g])
        s_sc[slot, g] = s
        if not bounded:
            cm_sc[slot, g] = jnp.max(s, axis=0, keepdims=True)

    def sm_pv(c, slot, g, state, masked):
        m_p, l_p, acc = state
        s = s_sc[slot, g]
        if masked:
            kpos = c * TK + lax.broadcasted_iota(jnp.int32, (TK, 1), 0)
            s = fill_where(s, kpos <= t_q, NEG_INF)
        if bounded:
            p = jnp.exp2(s)
            return m_p, l_p + jnp.sum(p, axis=0, keepdims=True), acc + _dot(v_chunk(c, g), p.astype(BF16))
        cm = jnp.max(s, axis=0, keepdims=True) if masked else cm_sc[slot, g]
        m_n = jnp.maximum(m_p, cm)
        alpha = jnp.exp2(m_p - m_n)
        p = jnp.exp2(s - m_n)
        l_n = alpha * l_p + jnp.sum(p, axis=0, keepdims=True)
        acc = alpha * acc + _dot(v_chunk(c, g), p.astype(BF16))
        return m_n, l_n, acc

    def step(c, slot, carry):
        out = []
        for g in groups:
            qk(c + 1, 1 - slot, g)
            out.append(sm_pv(c, slot, g, carry[g], False))
        return tuple(out)

    def last(slot, carry):
        return tuple(sm_pv(n_full, slot, g, carry[g], True) for g in groups)

    n_full = t0 // TK
    for g in groups:
        qk(0, 0, g)
    o_w, l_w = window_branch()
    init = tuple((jnp.full((1, rows), 0.0 if bounded else NEG_INF, F32), jnp.zeros((1, rows), F32),
                  jnp.zeros((HEAD_DIM, rows), F32)) for _ in groups)
    carry = lax.fori_loop(0, n_full // 2, lambda h, cr: step(2 * h + 1, 1, step(2 * h, 0, cr)), init)
    carry = lax.cond((n_full & 1) == 1,
                     lambda cr: last(1, step(n_full - 1, 0, cr)),
                     lambda cr: last(0, cr), carry)
    sel_out = [(carry[g][1], carry[g][2]) for g in groups]

    g_t = gate_ref[0].T
    slabs = []
    for g in groups:
        def gate_row(br):
            base = g * (N_BRANCH * GROUP_HEADS) + br * GROUP_HEADS
            return jnp.concatenate([g_t[base + r:base + r + 1, :] for r in range(GROUP_HEADS)], axis=1)
        l_s, o_s = sel_out[g]
        og = o_c[g] * gate_row(0) + o_s * (gate_row(1) / l_s) + o_w[g] * (gate_row(2) / l_w[g])
        slabs += [og[:, r * TQ:(r + 1) * TQ] for r in range(GROUP_HEADS)]
    out_t = jnp.concatenate(slabs, axis=0)
    o_ref[0] = (out_t.T * zs_ref[0].astype(F32)).astype(BF16)


def _out_kernel(x_ref, on_ref, oc_ref, w_ref, o_ref):
    acc = _dot(on_ref[0], w_ref[0:D_NSA, :]) + _dot(oc_ref[0], w_ref[D_NSA:D_NSA + D_CONV, :])
    o_ref[0] = x_ref[0] + acc


def _rope_tables(pos):
    inv_freq = (np.float32(ROPE_THETA) ** (-np.arange(0, ROT_DIM, 2, dtype=np.float32) / ROT_DIM)).astype(np.float32)
    ang = pos.astype(np.float32)[:, None] * inv_freq[None, :]
    cos, sin = np.cos(ang), np.sin(ang)
    n = pos.shape[0]
    ones = np.ones((n, HEAD_DIM - ROT_DIM), np.float32)
    zeros = np.zeros((n, HEAD_DIM - ROT_DIM), np.float32)
    z8 = np.zeros((n, ROT_DIM // 2), np.float32)
    c = np.concatenate([cos, cos, ones], axis=1)
    sa = np.concatenate([z8, sin, zeros], axis=1)
    sb = np.concatenate([-sin, z8, zeros], axis=1)
    return tuple(jnp.asarray(np.concatenate([t, t], axis=1), F32) for t in (c, sa, sb))


def _overlap_t(ncp):
    cs = np.arange(ncp) * CMP_STRIDE
    ce = cs + CMP_LEN
    ss = np.arange(SEL_BLOCK) * SEL_BLOCK
    se = ss + SEL_BLOCK
    ov = np.clip(np.minimum(ce[None, :], se[:, None]) - np.maximum(cs[None, :], ss[:, None]), 0, None)
    ov = ov.astype(np.float32) / CMP_LEN
    ov[:, ncp - 1] = 0.0
    return jnp.asarray(ov, BF16)


def _cmp_weights(pos, w1, b1, w2):
    half = CMP_STRIDE
    def big_w1(w):
        w = w.reshape(half, 1, HEAD_DIM, 1, CMP_HIDDEN)
        eye = jnp.eye(KV_GROUPS, dtype=F32).reshape(1, KV_GROUPS, 1, KV_GROUPS, 1)
        return (w * eye).reshape(half * KV_GROUPS * HEAD_DIM, KV_GROUPS * CMP_HIDDEN).astype(BF16)
    def big_pos(p):
        return jnp.broadcast_to(p[:, None, :], (half, KV_GROUPS, HEAD_DIM)).reshape(1, -1)
    w1a, w1b = w1[:half * HEAD_DIM], w1[half * HEAD_DIM:]
    eye2 = jnp.eye(KV_GROUPS, dtype=F32)
    w2b = (w2[None, :, None, :] * eye2[:, None, :, None]).reshape(KV_GROUPS * CMP_HIDDEN,
                                                                 KV_GROUPS * HEAD_DIM).astype(BF16)
    b1b = jnp.tile(b1, KV_GROUPS)[None, :]
    return big_pos(pos[:half]), big_pos(pos[half:]), big_w1(w1a), big_w1(w1b), b1b, w2b


def _full(shape):
    nd = len(shape)
    return pl.BlockSpec(shape, lambda *_: (0,) * nd)


def kernel(x, norm_w, w_in, q_norm_w, k_norm_w, cmp_k_pos, cmp_k_w1, cmp_k_b1, cmp_k_w2,
           cmp_v_pos, cmp_v_w1, cmp_v_b1, cmp_v_w2, conv_w, conv_b, w_out):
    B, S, D = x.shape
    assert norm_w.shape[0] == 1, "single layer"
    assert S % TM == 0 and S % TM_OUT == 0 and S // SEL_BLOCK <= SEL_BLOCK and S >= WINDOW + TQ
    ncp = S // CMP_STRIDE
    nst = S // TM
    f32 = lambda a: a.astype(F32)

    w = f32(w_in[0])
    head_order = []
    for j in range(GROUP_HEADS):
        head_order += [j, GROUP_HEADS + j]
    wq = w[:, :D_NSA].reshape(D, NSA_HEADS, HEAD_DIM)[:, jnp.array(head_order), :].reshape(D, D_NSA)
    gate_cols = []
    for g in range(KV_GROUPS):
        for br in range(N_BRANCH):
            for r in range(GROUP_HEADS):
                gate_cols.append((g * GROUP_HEADS + r) * N_BRANCH + br)
    g0 = D_NSA + 6 * LANES
    wg = w[:, g0:g0 + NSA_HEADS * N_BRANCH][:, jnp.array(gate_cols)]
    wg = jnp.pad(wg, ((0, 0), (0, LANES - NSA_HEADS * N_BRANCH)))
    rest0 = g0 + NSA_HEADS * N_BRANCH
    wq, wkv, wg, wr = (t.astype(BF16) for t in (wq, w[:, D_NSA:g0], wg, w[:, rest0:]))
    n_rest = wr.shape[1]

    rc, rsa, rsb = _rope_tables(np.arange(S))
    mbd = jnp.asarray(np.kron(np.eye(4), np.full((HEAD_DIM, HEAD_DIM), 1.0 / HEAD_DIM)), BF16)
    qnw = jnp.tile(f32(q_norm_w[0]), 4)[None, :]
    knw_sw = jnp.concatenate([jnp.tile(f32(k_norm_w[0, 1]), 2), jnp.tile(f32(k_norm_w[0, 2]), 2)])[None, :]
    knw_c = jnp.tile(f32(k_norm_w[0, 0]), 2)[None, :]

    cp = pltpu.CompilerParams(dimension_semantics=("arbitrary", "arbitrary"),
                              vmem_limit_bytes=VMEM_LIMIT)
    row_blk = lambda n: pl.BlockSpec((1, TM, n), lambda b, s: (b, s, 0))
    tab_blk = pl.BlockSpec((TM, LANES), lambda b, s: (s, 0))
    vt_blk = pl.BlockSpec((1, TM // VCHUNK, LANES, VCHUNK), lambda b, s: (b, s, 0, 0))
    sds = jax.ShapeDtypeStruct

    (qt4, kc_raw, vc_raw, ksf, vst, kw, vwt, gates, zs, oconv) = pl.pallas_call(
        _proj_kernel,
        grid=(B, nst),
        in_specs=[row_blk(D), _full((1, D)), _full((D, D_NSA)), _full((D, 6 * LANES)), _full((D, LANES)),
                  _full((D, n_rest)), _full((1, 256)), _full((1, 256)),
                  tab_blk, tab_blk, tab_blk, _full((256, 256)), _full((3, D_CONV)), _full((1, D_CONV))],
        out_specs=[pl.BlockSpec((1, 4, LANES, TM), lambda b, s: (b, 0, 0, s)),
                   row_blk(LANES), row_blk(LANES), row_blk(2 * LANES), vt_blk, row_blk(LANES), vt_blk,
                   row_blk(LANES), row_blk(D_NSA), row_blk(D_CONV)],
        out_shape=[sds((B, 4, LANES, S), BF16), sds((B, S, LANES), F32), sds((B, S, LANES), F32),
                   sds((B, S, 2 * LANES), BF16), sds((B, S // VCHUNK, LANES, VCHUNK), BF16),
                   sds((B, S, LANES), BF16), sds((B, S // VCHUNK, LANES, VCHUNK), BF16),
                   sds((B, S, LANES), F32), sds((B, S, D_NSA), BF16), sds((B, S, D_CONV), BF16)],
        scratch_shapes=[pltpu.VMEM((TM + 16, D_CONV), F32)],
        compiler_params=cp,
        name="nsa_proj",
    )(x, f32(norm_w), wq, wkv, wg, wr, qnw, knw_sw, rc, rsa, rsb, mbd, f32(conv_w[0]), f32(conv_b))

    cmp_pos = np.arange(ncp) * CMP_STRIDE + (CMP_LEN - 1)
    crc, crsa, crsb = _rope_tables(cmp_pos)
    kpa, kpb, kw1a, kw1b, kb1, kw2 = _cmp_weights(f32(cmp_k_pos[0]), f32(cmp_k_w1[0]),
                                                  f32(cmp_k_b1[0]), f32(cmp_k_w2[0]))
    vpa, vpb, vw1a, vw1b, vb1, vw2 = _cmp_weights(f32(cmp_v_pos[0]), f32(cmp_v_w1[0]),
                                                  f32(cmp_v_b1[0]), f32(cmp_v_w2[0]))
    flat = CMP_STRIDE * LANES
    h_blk = pl.BlockSpec((1, S, LANES), lambda b: (b, 0, 0))
    kc, vct = pl.pallas_call(
        functools.partial(_cmp_kernel, ncp=ncp),
        grid=(B,),
        in_specs=[h_blk, h_blk] + [_full((1, flat))] * 4
                 + [_full((flat, 256)), _full((flat, 256)), _full((1, 256)), _full((256, LANES))] * 2
                 + [_full((1, LANES)), _full((ncp, LANES)), _full((ncp, LANES)), _full((ncp, LANES)),
                    _full((256, 256))],
        out_specs=[pl.BlockSpec((1, ncp, LANES), lambda b: (b, 0, 0)),
                   pl.BlockSpec((1, LANES, ncp), lambda b: (b, 0, 0))],
        out_shape=[sds((B, ncp, LANES), BF16), sds((B, LANES, ncp), BF16)],
        compiler_params=pltpu.CompilerParams(dimension_semantics=("arbitrary",),
                                             vmem_limit_bytes=VMEM_LIMIT),
        name="nsa_compress",
    )(kc_raw, vc_raw, kpa, kpb, vpa, vpb,
      kw1a, kw1b, kb1, kw2, vw1a, vw1b, vb1, vw2, knw_c, crc, crsa, crsb, mbd)

    score_bound = (HEAD_DIM * Q_SCALE) * jnp.max(jnp.abs(f32(q_norm_w))) * jnp.max(jnp.abs(f32(k_norm_w)))
    nqt = S // TQ

    def attention(bounded):
        return pl.pallas_call(
            functools.partial(_attn_kernel, ncp=ncp, bounded=bounded),
            grid=(B, nqt),
            in_specs=[pl.BlockSpec((1, 4, LANES, TQ), lambda b, i: (b, 0, 0, i)),
                      pl.BlockSpec((1, ncp, LANES), lambda b, i: (b, 0, 0)),
                      pl.BlockSpec((1, LANES, ncp), lambda b, i: (b, 0, 0)),
                      pl.BlockSpec((1, S, 2 * LANES), lambda b, i: (b, 0, 0)),
                      pl.BlockSpec((1, S // VCHUNK, LANES, VCHUNK), lambda b, i: (b, 0, 0, 0)),
                      pl.BlockSpec((1, S, LANES), lambda b, i: (b, 0, 0)),
                      pl.BlockSpec((1, S // VCHUNK, LANES, VCHUNK), lambda b, i: (b, 0, 0, 0)),
                      pl.BlockSpec((1, TQ, LANES), lambda b, i: (b, i, 0)),
                      pl.BlockSpec((1, TQ, D_NSA), lambda b, i: (b, i, 0)),
                      _full((SEL_BLOCK, ncp))],
            out_specs=pl.BlockSpec((1, TQ, D_NSA), lambda b, i: (b, i, 0)),
            out_shape=sds((B, S, D_NSA), BF16),
            scratch_shapes=[pltpu.VMEM((KV_GROUPS, LANES, TQ), BF16),
                            pltpu.VMEM((2, KV_GROUPS, TK, GROUP_HEADS * TQ), F32),
                            pltpu.VMEM((2, KV_GROUPS, 1, GROUP_HEADS * TQ), F32)],
            compiler_params=cp,
            name="nsa_attn_bounded" if bounded else "nsa_attn_online",
        )

    attn_args = (qt4, kc, vct, ksf, vst, kw, vwt, gates, zs, _overlap_t(ncp))
    o_nsa = lax.cond(score_bound <= MAX_SAFE_SCORE,
                     lambda *a: attention(True)(*a), lambda *a: attention(False)(*a), *attn_args)

    out_blk = lambda n: pl.BlockSpec((1, TM_OUT, n), lambda b, s: (b, s, 0))
    out = pl.pallas_call(
        _out_kernel,
        grid=(B, S // TM_OUT),
        in_specs=[out_blk(D), out_blk(D_NSA), out_blk(D_CONV), _full((D_NSA + D_CONV, D))],
        out_specs=out_blk(D),
        out_shape=sds((B, S, D), x.dtype),
        compiler_params=cp,
        name="nsa_out",
    )(x, o_nsa, oconv, f32(w_out[0]).astype(BF16))
    return out
```

```python
import functools

import numpy as np
import jax
import jax.numpy as jnp
from jax import lax
from jax.experimental import pallas as pl
from jax.experimental.pallas import tpu as pltpu

LANES = 128
HEAD_DIM = 64
NSA_HEADS = 8
KV_GROUPS = 2
GROUP_HEADS = NSA_HEADS // KV_GROUPS
D_NSA = NSA_HEADS * HEAD_DIM
D_CONV = 512
N_BRANCH = 3
ROT_DIM = HEAD_DIM // 4
ROPE_THETA = 500000.0
CMP_LEN = 32
CMP_STRIDE = 16
CMP_HIDDEN = 2 * HEAD_DIM
SEL_BLOCK = 64
SEL_TOPK = 16
WINDOW = 512
EPS = 1e-6
NEG_INF = -1e30
N_FORCED = 3
TAKEN = -2.0
SCALE = HEAD_DIM ** -0.5
Q_SCALE = SCALE * float(np.log2(np.e))

TM = 512
TM_OUT = 1024
TQ = 256
TK = 512
VCHUNK = 128
VMEM_LIMIT = 56 * 1024 * 1024
MAX_SAFE_SCORE = 50.0

BF16 = jnp.bfloat16
F32 = jnp.float32


def _dot(a, b):
    return jnp.dot(a, b, preferred_element_type=F32)


def _split3(a):
    hi = a.astype(BF16)
    r1 = a - hi.astype(F32)
    mid = r1.astype(BF16)
    lo = (r1 - mid.astype(F32)).astype(BF16)
    return hi, mid, lo


def _group_mean(sq, mbd):
    return _dot(sq.astype(BF16), mbd)


def _rope(xn, c, sa, sb):
    return xn * c + pltpu.roll(xn, 8, 1) * sa + pltpu.roll(xn, LANES - 8, 1) * sb


def _silu(z):
    return z * (1.0 / (1.0 + jnp.exp(-z)))


def _proj_kernel(x_ref, nw_ref, wq_ref, wkv_ref, wg_ref, wr_ref, qnw_ref, knw_ref,
                 rc_ref, rsa_ref, rsb_ref, mbd_ref, cw_ref, cb_ref,
                 qt_ref, kc_ref, vc_ref, ksf_ref, vst_ref, kw_ref, vwt_ref,
                 gate_ref, zs_ref, oconv_ref, ubuf):
    si = pl.program_id(1)

    @pl.when(si == 0)
    def _():
        ubuf[0:8, :] = jnp.zeros((8, D_CONV), F32)

    x = x_ref[0]
    ms = jnp.mean(x * x, axis=-1, keepdims=True)
    h = (x * lax.rsqrt(ms + EPS) * nw_ref[...]).astype(BF16)

    rc, rsa, rsb = rc_ref[...], rsa_ref[...], rsb_ref[...]
    mbd = mbd_ref[...]

    pq = _dot(h, wq_ref[...])
    for pair in range(2):
        blk = pq[:, pair * 256:(pair + 1) * 256]
        msq = _group_mean(blk * blk, mbd)
        qn = blk * lax.rsqrt(msq + EPS) * qnw_ref[...]
        for half in range(2):
            t = qn[:, half * LANES:(half + 1) * LANES]
            qt_ref[0, pair * 2 + half] = (_rope(t, rc, rsa, rsb) * Q_SCALE).T.astype(BF16)

    pkv = _dot(h, wkv_ref[...])
    kc_ref[0] = pkv[:, 0:128]
    vc_ref[0] = pkv[:, 128:256]
    ksw = jnp.concatenate([pkv[:, 256:384], pkv[:, 512:640]], axis=1)
    msk = _group_mean(ksw * ksw, mbd)
    kn = ksw * lax.rsqrt(msk + EPS) * knw_ref[...]
    ks = _rope(kn[:, 0:128], rc, rsa, rsb)
    kw = _rope(kn[:, 128:256], rc, rsa, rsb)
    row = lax.broadcasted_iota(jnp.int32, (TM, LANES), 0) + si * TM
    lane = lax.broadcasted_iota(jnp.int32, (TM, LANES), 1)
    onehot = jnp.where(lane == (row >> 6), 1.0, 0.0)
    ksf_ref[0, :, 0:128] = ks.astype(BF16)
    ksf_ref[0, :, 128:256] = onehot.astype(BF16)
    kw_ref[0] = kw.astype(BF16)
    vs = pkv[:, 384:512]
    vw = pkv[:, 640:768]
    for j in range(TM // VCHUNK):
        vst_ref[0, j] = vs[j * VCHUNK:(j + 1) * VCHUNK, :].T.astype(BF16)
        vwt_ref[0, j] = vw[j * VCHUNK:(j + 1) * VCHUNK, :].T.astype(BF16)

    gate_ref[0] = 1.0 / (1.0 + jnp.exp(-_dot(h, wg_ref[...])))
    zs_ref[0] = _silu(_dot(h, wr_ref[:, 0:D_NSA])).astype(BF16)

    pcv = _dot(h, wr_ref[:, D_NSA:D_NSA + 4 * D_CONV])
    u = pcv[:, 1024:1536] * pcv[:, 0:512]
    ubuf[8:8 + TM, :] = u
    u1 = ubuf[7:7 + TM, :]
    u2 = ubuf[6:6 + TM, :]
    conv = cw_ref[0:1, :] * u2 + cw_ref[1:2, :] * u1 + cw_ref[2:3, :] * u + cb_ref[...]
    oconv_ref[0] = (pcv[:, 512:1024] * conv * _silu(pcv[:, 1536:2048])).astype(BF16)
    ubuf[0:8, :] = ubuf[TM:TM + 8, :]


def _cmp_kernel(hk_ref, hv_ref, pka_ref, pkb_ref, pva_ref, pvb_ref,
                wk1a_ref, wk1b_ref, bk1_ref, wk2_ref,
                wv1a_ref, wv1b_ref, bv1_ref, wv2_ref,
                knw_ref, rc_ref, rsa_ref, rsb_ref, mbd_ref,
                kc_ref, vct_ref, *, ncp):
    def first_layer(h_ref, p_ref, w_ref):
        acc = None
        for l in range(0, CMP_STRIDE, 2):
            lhs = jnp.concatenate([h_ref[0, pl.ds(l, ncp, stride=CMP_STRIDE), :],
                                   h_ref[0, pl.ds(l + 1, ncp, stride=CMP_STRIDE), :]], axis=1)
            lhs = (lhs + p_ref[:, l * LANES:(l + 2) * LANES]).astype(BF16)
            part = _dot(lhs, w_ref[l * LANES:(l + 2) * LANES, :])
            acc = part if acc is None else acc + part
        return acc

    def mlp(h_ref, pa_ref, pb_ref, w1a_ref, w1b_ref, b1_ref, w2_ref):
        p = first_layer(h_ref, pa_ref, w1a_ref)
        q = first_layer(h_ref, pb_ref, w1b_ref)
        pre = p + pltpu.roll(q, ncp - 1, 0) + b1_ref[...]
        return _dot(_silu(pre).astype(BF16), w2_ref[...])

    kc = mlp(hk_ref, pka_ref, pkb_ref, wk1a_ref, wk1b_ref, bk1_ref, wk2_ref)
    msk = _group_mean(kc * kc, mbd_ref[0:LANES, 0:LANES])
    kn = kc * lax.rsqrt(msk + EPS) * knw_ref[...]
    kc_ref[0] = _rope(kn, rc_ref[...], rsa_ref[...], rsb_ref[...]).astype(BF16)
    vc = mlp(hv_ref, pva_ref, pvb_ref, wv1a_ref, wv1b_ref, bv1_ref, wv2_ref)
    vct_ref[0] = vc.T.astype(BF16)


def _attn_kernel(qt_ref, kc_ref, vct_ref, ksf_ref, vst_ref, kw_ref, vwt_ref,
                 gate_ref, zs_ref, ovt_ref, o_ref, pen_sc, s_sc, cm_sc, *, ncp, bounded):
    i = pl.program_id(1)
    t0 = i * TQ
    rows = GROUP_HEADS * TQ
    nblk = SEL_BLOCK
    groups = range(KV_GROUPS)

    q_all = jnp.concatenate([qt_ref[0, j] for j in range(GROUP_HEADS)], axis=1)
    frow = lax.broadcasted_iota(jnp.int32, (LANES, rows), 0)
    qg = [jnp.where(frow < HEAD_DIM, q_all, 0), jnp.where(frow >= HEAD_DIM, q_all, 0)]
    t_q = t0 + lax.broadcasted_iota(jnp.int32, (1, TQ), 1)

    def fill_where(a, mask, fill):
        return jnp.concatenate([jnp.where(mask, a[:, r * TQ:(r + 1) * TQ], fill)
                                for r in range(GROUP_HEADS)], axis=1)

    vrows = lambda v, g: v[g * HEAD_DIM:(g + 1) * HEAD_DIM, :]

    s_cmp = [_dot(kc_ref[0], qg[g]) for g in groups]

    c_idx = lax.broadcasted_iota(jnp.int32, (ncp, 1), 0)
    mask_c = ((c_idx * CMP_STRIDE + (CMP_LEN - 1)) <= t_q) & (c_idx < ncp - 1)
    p_c, o_c = [], []
    for g in groups:
        if bounded:
            e_c = fill_where(jnp.exp2(s_cmp[g]), mask_c, 0.0)
        else:
            sm = fill_where(s_cmp[g], mask_c, NEG_INF)
            e_c = fill_where(jnp.exp2(sm - jnp.max(sm, axis=0, keepdims=True)), mask_c, 0.0)
        l_c = jnp.sum(e_c, axis=0, keepdims=True)
        p = e_c * (1.0 / jnp.where(l_c > 0.0, l_c, 1.0))
        p_c.append(p)
        o_c.append(_dot(vrows(vct_ref[0], g), p.astype(BF16)))

    n_idx = lax.broadcasted_iota(jnp.int32, (nblk, TQ), 0)
    tq = t0 + lax.broadcasted_iota(jnp.int32, (nblk, TQ), 1)
    cur = tq >> 6
    forced = (n_idx == 0) | (n_idx == cur) | (n_idx == cur - 1)
    visible = (n_idx << 6) <= tq
    ovt = ovt_ref[...]
    left = []
    for g in groups:
        psum = p_c[g][:, 0:TQ]
        for r in range(1, GROUP_HEADS):
            psum = psum + p_c[g][:, r * TQ:(r + 1) * TQ]
        hi, mid, lo = _split3(psum)
        imp = _dot(ovt, hi) + _dot(ovt, mid) + _dot(ovt, lo)
        left.append(jnp.where(forced, TAKEN, jnp.where(visible, imp, -1.0)))
    for _ in range(SEL_TOPK - N_FORCED):
        for g in groups:
            top = jnp.max(left[g], axis=0, keepdims=True)
            first = jnp.min(jnp.where(left[g] == top, n_idx, nblk), axis=0, keepdims=True)
            left[g] = jnp.where(n_idx == first, TAKEN, left[g])
    for g in groups:
        pen_sc[g, 0:nblk, :] = jnp.where(left[g] == TAKEN, 0.0, NEG_INF).astype(BF16)
        pen_sc[g, nblk:LANES, :] = jnp.zeros((LANES - nblk, TQ), BF16)

    nwc = WINDOW // VCHUNK + TQ // VCHUNK
    cw = jnp.maximum(i * (TQ // VCHUNK) - WINDOW // VCHUNK, 0)
    w0 = pl.multiple_of(cw * VCHUNK, VCHUNK)
    kwin = kw_ref[0, pl.ds(w0, nwc * VCHUNK), :]
    s_win = [_dot(kwin, qg[g]) for g in groups]

    q_aug = [jnp.concatenate([qg[g], jnp.concatenate([pen_sc[g]] * GROUP_HEADS, axis=1)], axis=0)
             for g in groups]

    def window_branch():
        vwin = jnp.concatenate([vwt_ref[0, cw + j] for j in range(nwc)], axis=1)
        kpos = w0 + lax.broadcasted_iota(jnp.int32, (nwc * VCHUNK, 1), 0)
        mask_w = (kpos <= t_q) & (kpos > t_q - WINDOW)
        o_w, l_w = [], []
        for g in groups:
            if bounded:
                p_w = fill_where(jnp.exp2(s_win[g]), mask_w, 0.0)
            else:
                sw = fill_where(s_win[g], mask_w, NEG_INF)
                p_w = jnp.exp2(sw - jnp.max(sw, axis=0, keepdims=True))
            v_ext = jnp.concatenate([vrows(vwin, g), jnp.ones((16, nwc * VCHUNK), BF16)], axis=0)
            ol = _dot(v_ext, p_w.astype(BF16))
            o_w.append(ol[0:HEAD_DIM, :])
            l_w.append(ol[HEAD_DIM:HEAD_DIM + 1, :])
        return o_w, l_w

    def v_chunk(c, g):
        cv = c * (TK // VCHUNK)
        return jnp.concatenate([vrows(vst_ref[0, cv + j], g) for j in range(TK // VCHUNK)], axis=1)

    def qk(c, slot, g):
        k0 = pl.multiple_of(c * TK, TK)
        s = _dot(ksf_ref[0, pl.ds(k0, TK), :], q_aug[g])
        s_sc[slot, g] = s
        if not bounded:
            cm_sc[slot, g] = jnp.max(s, axis=0, keepdims=True)

    def sm_pv(c, slot, g, state, masked):
        m_p, l_p, acc = state
        s = s_sc[slot, g]
        if masked:
            kpos = c * TK + lax.broadcasted_iota(jnp.int32, (TK, 1), 0)
            s = fill_where(s, kpos <= t_q, NEG_INF)
        if bounded:
            p = jnp.exp2(s)
            return m_p, l_p + jnp.sum(p, axis=0, keepdims=True), acc + _dot(v_chunk(c, g), p.astype(BF16))
        cm = jnp.max(s, axis=0, keepdims=True) if masked else cm_sc[slot, g]
        m_n = jnp.maximum(m_p, cm)
        alpha = jnp.exp2(m_p - m_n)
        p = jnp.exp2(s - m_n)
        l_n = alpha * l_p + jnp.sum(p, axis=0, keepdims=True)
        acc = alpha * acc + _dot(v_chunk(c, g), p.astype(BF16))
        return m_n, l_n, acc

    def step(c, slot, carry):
        out = []
        for g in groups:
            qk(c + 1, 1 - slot, g)
            out.append(sm_pv(c, slot, g, carry[g], False))
        return tuple(out)

    def last(slot, carry):
        return tuple(sm_pv(n_full, slot, g, carry[g], True) for g in groups)

    n_full = t0 // TK
    for g in groups:
        qk(0, 0, g)
    o_w, l_w = window_branch()
    init = tuple((jnp.full((1, rows), 0.0 if bounded else NEG_INF, F32), jnp.zeros((1, rows), F32),
                  jnp.zeros((HEAD_DIM, rows), F32)) for _ in groups)
    carry = lax.fori_loop(0, n_full // 2, lambda h, cr: step(2 * h + 1, 1, step(2 * h, 0, cr)), init)
    carry = lax.cond((n_full & 1) == 1,
                     lambda cr: last(1, step(n_full - 1, 0, cr)),
                     lambda cr: last(0, cr), carry)
    sel_out = [(carry[g][1], carry[g][2]) for g in groups]

    g_t = gate_ref[0].T
    slabs = []
    for g in groups:
        def gate_row(br):
            base = g * (N_BRANCH * GROUP_HEADS) + br * GROUP_HEADS
            return jnp.concatenate([g_t[base + r:base + r + 1, :] for r in range(GROUP_HEADS)], axis=1)
        l_s, o_s = sel_out[g]
        og = o_c[g] * gate_row(0) + o_s * (gate_row(1) / l_s) + o_w[g] * (gate_row(2) / l_w[g])
        slabs += [og[:, r * TQ:(r + 1) * TQ] for r in range(GROUP_HEADS)]
    out_t = jnp.concatenate(slabs, axis=0)
    o_ref[0] = (out_t.T * zs_ref[0].astype(F32)).astype(BF16)


def _out_kernel(x_ref, on_ref, oc_ref, w_ref, o_ref):
    acc = _dot(on_ref[0], w_ref[0:D_NSA, :]) + _dot(oc_ref[0], w_ref[D_NSA:D_NSA + D_CONV, :])
    o_ref[0] = x_ref[0] + acc


def _rope_tables(pos):
    inv_freq = (np.float32(ROPE_THETA) ** (-np.arange(0, ROT_DIM, 2, dtype=np.float32) / ROT_DIM)).astype(np.float32)
    ang = pos.astype(np.float32)[:, None] * inv_freq[None, :]
    cos, sin = np.cos(ang), np.sin(ang)
    n = pos.shape[0]
    ones = np.ones((n, HEAD_DIM - ROT_DIM), np.float32)
    zeros = np.zeros((n, HEAD_DIM - ROT_DIM), np.float32)
    z8 = np.zeros((n, ROT_DIM // 2), np.float32)
    c = np.concatenate([cos, cos, ones], axis=1)
    sa = np.concatenate([z8, sin, zeros], axis=1)
    sb = np.concatenate([-sin, z8, zeros], axis=1)
    return tuple(jnp.asarray(np.concatenate([t, t], axis=1), F32) for t in (c, sa, sb))


def _overlap_t(ncp):
    cs = np.arange(ncp) * CMP_STRIDE
    ce = cs + CMP_LEN
    ss = np.arange(SEL_BLOCK) * SEL_BLOCK
    se = ss + SEL_BLOCK
    ov = np.clip(np.minimum(ce[None, :], se[:, None]) - np.maximum(cs[None, :], ss[:, None]), 0, None)
    ov = ov.astype(np.float32) / CMP_LEN
    ov[:, ncp - 1] = 0.0
    return jnp.asarray(ov, BF16)


def _cmp_weights(pos, w1, b1, w2):
    half = CMP_STRIDE
    def big_w1(w):
        w = w.reshape(half, 1, HEAD_DIM, 1, CMP_HIDDEN)
        eye = jnp.eye(KV_GROUPS, dtype=F32).reshape(1, KV_GROUPS, 1, KV_GROUPS, 1)
        return (w * eye).reshape(half * KV_GROUPS * HEAD_DIM, KV_GROUPS * CMP_HIDDEN).astype(BF16)
    def big_pos(p):
        return jnp.broadcast_to(p[:, None, :], (half, KV_GROUPS, HEAD_DIM)).reshape(1, -1)
    w1a, w1b = w1[:half * HEAD_DIM], w1[half * HEAD_DIM:]
    eye2 = jnp.eye(KV_GROUPS, dtype=F32)
    w2b = (w2[None, :, None, :] * eye2[:, None, :, None]).reshape(KV_GROUPS * CMP_HIDDEN,
                                                                 KV_GROUPS * HEAD_DIM).astype(BF16)
    b1b = jnp.tile(b1, KV_GROUPS)[None, :]
    return big_pos(pos[:half]), big_pos(pos[half:]), big_w1(w1a), big_w1(w1b), b1b, w2b


def _full(shape):
    nd = len(shape)
    return pl.BlockSpec(shape, lambda *_: (0,) * nd)


def kernel(x, norm_w, w_in, q_norm_w, k_norm_w, cmp_k_pos, cmp_k_w1, cmp_k_b1, cmp_k_w2,
           cmp_v_pos, cmp_v_w1, cmp_v_b1, cmp_v_w2, conv_w, conv_b, w_out):
    B, S, D = x.shape
    assert norm_w.shape[0] == 1, "single layer"
    assert S % TM == 0 and S % TM_OUT == 0 and S // SEL_BLOCK <= SEL_BLOCK and S >= WINDOW + TQ
    ncp = S // CMP_STRIDE
    nst = S // TM
    f32 = lambda a: a.astype(F32)

    w = f32(w_in[0])
    head_order = []
    for j in range(GROUP_HEADS):
        head_order += [j, GROUP_HEADS + j]
    wq = w[:, :D_NSA].reshape(D, NSA_HEADS, HEAD_DIM)[:, jnp.array(head_order), :].reshape(D, D_NSA)
    gate_cols = []
    for g in range(KV_GROUPS):
        for br in range(N_BRANCH):
            for r in range(GROUP_HEADS):
                gate_cols.append((g * GROUP_HEADS + r) * N_BRANCH + br)
    g0 = D_NSA + 6 * LANES
    wg = w[:, g0:g0 + NSA_HEADS * N_BRANCH][:, jnp.array(gate_cols)]
    wg = jnp.pad(wg, ((0, 0), (0, LANES - NSA_HEADS * N_BRANCH)))
    rest0 = g0 + NSA_HEADS * N_BRANCH
    wq, wkv, wg, wr = (t.astype(BF16) for t in (wq, w[:, D_NSA:g0], wg, w[:, rest0:]))
    n_rest = wr.shape[1]

    rc, rsa, rsb = _rope_tables(np.arange(S))
    mbd = jnp.asarray(np.kron(np.eye(4), np.full((HEAD_DIM, HEAD_DIM), 1.0 / HEAD_DIM)), BF16)
    qnw = jnp.tile(f32(q_norm_w[0]), 4)[None, :]
    knw_sw = jnp.concatenate([jnp.tile(f32(k_norm_w[0, 1]), 2), jnp.tile(f32(k_norm_w[0, 2]), 2)])[None, :]
    knw_c = jnp.tile(f32(k_norm_w[0, 0]), 2)[None, :]

    cp = pltpu.CompilerParams(dimension_semantics=("arbitrary", "arbitrary"),
                              vmem_limit_bytes=VMEM_LIMIT)
    row_blk = lambda n: pl.BlockSpec((1, TM, n), lambda b, s: (b, s, 0))
    tab_blk = pl.BlockSpec((TM, LANES), lambda b, s: (s, 0))
    vt_blk = pl.BlockSpec((1, TM // VCHUNK, LANES, VCHUNK), lambda b, s: (b, s, 0, 0))
    sds = jax.ShapeDtypeStruct

    (qt4, kc_raw, vc_raw, ksf, vst, kw, vwt, gates, zs, oconv) = pl.pallas_call(
        _proj_kernel,
        grid=(B, nst),
        in_specs=[row_blk(D), _full((1, D)), _full((D, D_NSA)), _full((D, 6 * LANES)), _full((D, LANES)),
                  _full((D, n_rest)), _full((1, 256)), _full((1, 256)),
                  tab_blk, tab_blk, tab_blk, _full((256, 256)), _full((3, D_CONV)), _full((1, D_CONV))],
        out_specs=[pl.BlockSpec((1, 4, LANES, TM), lambda b, s: (b, 0, 0, s)),
                   row_blk(LANES), row_blk(LANES), row_blk(2 * LANES), vt_blk, row_blk(LANES), vt_blk,
                   row_blk(LANES), row_blk(D_NSA), row_blk(D_CONV)],
        out_shape=[sds((B, 4, LANES, S), BF16), sds((B, S, LANES), F32), sds((B, S, LANES), F32),
                   sds((B, S, 2 * LANES), BF16), sds((B, S // VCHUNK, LANES, VCHUNK), BF16),
                   sds((B, S, LANES), BF16), sds((B, S // VCHUNK, LANES, VCHUNK), BF16),
                   sds((B, S, LANES), F32), sds((B, S, D_NSA), BF16), sds((B, S, D_CONV), BF16)],
        scratch_shapes=[pltpu.VMEM((TM + 16, D_CONV), F32)],
        compiler_params=cp,
        name="nsa_proj",
    )(x, f32(norm_w), wq, wkv, wg, wr, qnw, knw_sw, rc, rsa, rsb, mbd, f32(conv_w[0]), f32(conv_b))

    cmp_pos = np.arange(ncp) * CMP_STRIDE + (CMP_LEN - 1)
    crc, crsa, crsb = _rope_tables(cmp_pos)
    kpa, kpb, kw1a, kw1b, kb1, kw2 = _cmp_weights(f32(cmp_k_pos[0]), f32(cmp_k_w1[0]),
                                                  f32(cmp_k_b1[0]), f32(cmp_k_w2[0]))
    vpa, vpb, vw1a, vw1b, vb1, vw2 = _cmp_weights(f32(cmp_v_pos[0]), f32(cmp_v_w1[0]),
                                                  f32(cmp_v_b1[0]), f32(cmp_v_w2[0]))
    flat = CMP_STRIDE * LANES
    h_blk = pl.BlockSpec((1, S, LANES), lambda b: (b, 0, 0))
    kc, vct = pl.pallas_call(
        functools.partial(_cmp_kernel, ncp=ncp),
        grid=(B,),
        in_specs=[h_blk, h_blk] + [_full((1, flat))] * 4
                 + [_full((flat, 256)), _full((flat, 256)), _full((1, 256)), _full((256, LANES))] * 2
                 + [_full((1, LANES)), _full((ncp, LANES)), _full((ncp, LANES)), _full((ncp, LANES)),
                    _full((256, 256))],
        out_specs=[pl.BlockSpec((1, ncp, LANES), lambda b: (b, 0, 0)),
                   pl.BlockSpec((1, LANES, ncp), lambda b: (b, 0, 0))],
        out_shape=[sds((B, ncp, LANES), BF16), sds((B, LANES, ncp), BF16)],
        compiler_params=pltpu.CompilerParams(dimension_semantics=("arbitrary",),
                                             vmem_limit_bytes=VMEM_LIMIT),
        name="nsa_compress",
    )(kc_raw, vc_raw, kpa, kpb, vpa, vpb,
      kw1a, kw1b, kb1, kw2, vw1a, vw1b, vb1, vw2, knw_c, crc, crsa, crsb, mbd)

    score_bound = (HEAD_DIM * Q_SCALE) * jnp.max(jnp.abs(f32(q_norm_w))) * jnp.max(jnp.abs(f32(k_norm_w)))
    nqt = S // TQ

    def attention(bounded):
        return pl.pallas_call(
            functools.partial(_attn_kernel, ncp=ncp, bounded=bounded),
            grid=(B, nqt),
            in_specs=[pl.BlockSpec((1, 4, LANES, TQ), lambda b, i: (b, 0, 0, i)),
                      pl.BlockSpec((1, ncp, LANES), lambda b, i: (b, 0, 0)),
                      pl.BlockSpec((1, LANES, ncp), lambda b, i: (b, 0, 0)),
                      pl.BlockSpec((1, S, 2 * LANES), lambda b, i: (b, 0, 0)),
                      pl.BlockSpec((1, S // VCHUNK, LANES, VCHUNK), lambda b, i: (b, 0, 0, 0)),
                      pl.BlockSpec((1, S, LANES), lambda b, i: (b, 0, 0)),
                      pl.BlockSpec((1, S // VCHUNK, LANES, VCHUNK), lambda b, i: (b, 0, 0, 0)),
                      pl.BlockSpec((1, TQ, LANES), lambda b, i: (b, i, 0)),
                      pl.BlockSpec((1, TQ, D_NSA), lambda b, i: (b, i, 0)),
                      _full((SEL_BLOCK, ncp))],
            out_specs=pl.BlockSpec((1, TQ, D_NSA), lambda b, i: (b, i, 0)),
            out_shape=sds((B, S, D_NSA), BF16),
            scratch_shapes=[pltpu.VMEM((KV_GROUPS, LANES, TQ), BF16),
                            pltpu.VMEM((2, KV_GROUPS, TK, GROUP_HEADS * TQ), F32),
                            pltpu.VMEM((2, KV_GROUPS, 1, GROUP_HEADS * TQ), F32)],
            compiler_params=cp,
            name="nsa_attn_bounded" if bounded else "nsa_attn_online",
        )

    attn_args = (qt4, kc, vct, ksf, vst, kw, vwt, gates, zs, _overlap_t(ncp))
    o_nsa = lax.cond(score_bound <= MAX_SAFE_SCORE,
                     lambda *a: attention(True)(*a), lambda *a: attention(False)(*a), *attn_args)

    out_blk = lambda n: pl.BlockSpec((1, TM_OUT, n), lambda b, s: (b, s, 0))
    out = pl.pallas_call(
        _out_kernel,
        grid=(B, S // TM_OUT),
        in_specs=[out_blk(D), out_blk(D_NSA), out_blk(D_CONV), _full((D_NSA + D_CONV, D))],
        out_specs=out_blk(D),
        out_shape=sds((B, S, D), x.dtype),
        compiler_params=cp,
        name="nsa_out",
    )(x, o_nsa, oconv, f32(w_out[0]).astype(BF16))
    return out
```

```python
import functools

import numpy as np
import jax
import jax.numpy as jnp
from jax import lax
from jax.experimental import pallas as pl
from jax.experimental.pallas import tpu as pltpu

LANES = 128
HEAD_DIM = 64
NSA_HEADS = 8
KV_GROUPS = 2
GROUP_HEADS = NSA_HEADS // KV_GROUPS
D_NSA = NSA_HEADS * HEAD_DIM
D_CONV = 512
N_BRANCH = 3
ROT_DIM = HEAD_DIM // 4
ROPE_THETA = 500000.0
CMP_LEN = 32
CMP_STRIDE = 16
CMP_HIDDEN = 2 * HEAD_DIM
SEL_BLOCK = 64
SEL_TOPK = 16
WINDOW = 512
EPS = 1e-6
NEG_INF = -1e30
N_FORCED = 3
TAKEN = -2.0
SCALE = HEAD_DIM ** -0.5
Q_SCALE = SCALE * float(np.log2(np.e))

TM = 512
TM_OUT = 1024
TQ = 256
TK = 512
VCHUNK = 128
VMEM_LIMIT = 56 * 1024 * 1024
MAX_SAFE_SCORE = 50.0

BF16 = jnp.bfloat16
F32 = jnp.float32


def _dot(a, b):
    return jnp.dot(a, b, preferred_element_type=F32)


def _split3(a):
    hi = a.astype(BF16)
    r1 = a - hi.astype(F32)
    mid = r1.astype(BF16)
    lo = (r1 - mid.astype(F32)).astype(BF16)
    return hi, mid, lo


def _group_mean(sq, mbd):
    return _dot(sq.astype(BF16), mbd)


def _rope(xn, c, sa, sb):
    return xn * c + pltpu.roll(xn, 8, 1) * sa + pltpu.roll(xn, LANES - 8, 1) * sb


def _silu(z):
    return z * (1.0 / (1.0 + jnp.exp(-z)))


def _proj_kernel(x_ref, nw_ref, wq_ref, wkv_ref, wg_ref, wr_ref, qnw_ref, knw_ref,
                 rc_ref, rsa_ref, rsb_ref, mbd_ref, cw_ref, cb_ref,
                 qt_ref, kc_ref, vc_ref, ksf_ref, vst_ref, kw_ref, vwt_ref,
                 gate_ref, zs_ref, oconv_ref, ubuf):
    si = pl.program_id(1)

    @pl.when(si == 0)
    def _():
        ubuf[0:8, :] = jnp.zeros((8, D_CONV), F32)

    x = x_ref[0]
    ms = jnp.mean(x * x, axis=-1, keepdims=True)
    h = (x * lax.rsqrt(ms + EPS) * nw_ref[...]).astype(BF16)

    rc, rsa, rsb = rc_ref[...], rsa_ref[...], rsb_ref[...]
    mbd = mbd_ref[...]

    pq = _dot(h, wq_ref[...])
    for pair in range(2):
        blk = pq[:, pair * 256:(pair + 1) * 256]
        msq = _group_mean(blk * blk, mbd)
        qn = blk * lax.rsqrt(msq + EPS) * qnw_ref[...]
        for half in range(2):
            t = qn[:, half * LANES:(half + 1) * LANES]
            qt_ref[0, pair * 2 + half] = (_rope(t, rc, rsa, rsb) * Q_SCALE).T.astype(BF16)

    pkv = _dot(h, wkv_ref[...])
    kc_ref[0] = pkv[:, 0:128]
    vc_ref[0] = pkv[:, 128:256]
    ksw = jnp.concatenate([pkv[:, 256:384], pkv[:, 512:640]], axis=1)
    msk = _group_mean(ksw * ksw, mbd)
    kn = ksw * lax.rsqrt(msk + EPS) * knw_ref[...]
    ks = _rope(kn[:, 0:128], rc, rsa, rsb)
    kw = _rope(kn[:, 128:256], rc, rsa, rsb)
    row = lax.broadcasted_iota(jnp.int32, (TM, LANES), 0) + si * TM
    lane = lax.broadcasted_iota(jnp.int32, (TM, LANES), 1)
    onehot = jnp.where(lane == (row >> 6), 1.0, 0.0)
    ksf_ref[0, :, 0:128] = ks.astype(BF16)
    ksf_ref[0, :, 128:256] = onehot.astype(BF16)
    kw_ref[0] = kw.astype(BF16)
    vs = pkv[:, 384:512]
    vw = pkv[:, 640:768]
    for j in range(TM // VCHUNK):
        vst_ref[0, j] = vs[j * VCHUNK:(j + 1) * VCHUNK, :].T.astype(BF16)
        vwt_ref[0, j] = vw[j * VCHUNK:(j + 1) * VCHUNK, :].T.astype(BF16)

    gate_ref[0] = 1.0 / (1.0 + jnp.exp(-_dot(h, wg_ref[...])))
    zs_ref[0] = _silu(_dot(h, wr_ref[:, 0:D_NSA])).astype(BF16)

    pcv = _dot(h, wr_ref[:, D_NSA:D_NSA + 4 * D_CONV])
    u = pcv[:, 1024:1536] * pcv[:, 0:512]
    ubuf[8:8 + TM, :] = u
    u1 = ubuf[7:7 + TM, :]
    u2 = ubuf[6:6 + TM, :]
    conv = cw_ref[0:1, :] * u2 + cw_ref[1:2, :] * u1 + cw_ref[2:3, :] * u + cb_ref[...]
    oconv_ref[0] = (pcv[:, 512:1024] * conv * _silu(pcv[:, 1536:2048])).astype(BF16)
    ubuf[0:8, :] = ubuf[TM:TM + 8, :]


def _cmp_kernel(hk_ref, hv_ref, pka_ref, pkb_ref, pva_ref, pvb_ref,
                wk1a_ref, wk1b_ref, bk1_ref, wk2_ref,
                wv1a_ref, wv1b_ref, bv1_ref, wv2_ref,
                knw_ref, rc_ref, rsa_ref, rsb_ref, mbd_ref,
                kc_ref, vct_ref, *, ncp):
    def first_layer(h_ref, p_ref, w_ref):
        acc = None
        for l in range(0, CMP_STRIDE, 2):
            lhs = jnp.concatenate([h_ref[0, pl.ds(l, ncp, stride=CMP_STRIDE), :],
                                   h_ref[0, pl.ds(l + 1, ncp, stride=CMP_STRIDE), :]], axis=1)
            lhs = (lhs + p_ref[:, l * LANES:(l + 2) * LANES]).astype(BF16)
            part = _dot(lhs, w_ref[l * LANES:(l + 2) * LANES, :])
            acc = part if acc is None else acc + part
        return acc

    def mlp(h_ref, pa_ref, pb_ref, w1a_ref, w1b_ref, b1_ref, w2_ref):
        p = first_layer(h_ref, pa_ref, w1a_ref)
        q = first_layer(h_ref, pb_ref, w1b_ref)
        pre = p + pltpu.roll(q, ncp - 1, 0) + b1_ref[...]
        return _dot(_silu(pre).astype(BF16), w2_ref[...])

    kc = mlp(hk_ref, pka_ref, pkb_ref, wk1a_ref, wk1b_ref, bk1_ref, wk2_ref)
    msk = _group_mean(kc * kc, mbd_ref[0:LANES, 0:LANES])
    kn = kc * lax.rsqrt(msk + EPS) * knw_ref[...]
    kc_ref[0] = _rope(kn, rc_ref[...], rsa_ref[...], rsb_ref[...]).astype(BF16)
    vc = mlp(hv_ref, pva_ref, pvb_ref, wv1a_ref, wv1b_ref, bv1_ref, wv2_ref)
    vct_ref[0] = vc.T.astype(BF16)


def _attn_kernel(qt_ref, kc_ref, vct_ref, ksf_ref, vst_ref, kw_ref, vwt_ref,
                 gate_ref, zs_ref, ovt_ref, o_ref, pen_sc, s_sc, cm_sc, *, ncp, bounded, tile0, ncv, nblk):
    i = pl.program_id(1) + tile0
    t0 = i * TQ
    rows = GROUP_HEADS * TQ
    groups = range(KV_GROUPS)

    q_all = jnp.concatenate([qt_ref[0, j] for j in range(GROUP_HEADS)], axis=1)
    frow = lax.broadcasted_iota(jnp.int32, (LANES, rows), 0)
    qg = [jnp.where(frow < HEAD_DIM, q_all, 0), jnp.where(frow >= HEAD_DIM, q_all, 0)]
    t_q = t0 + lax.broadcasted_iota(jnp.int32, (1, TQ), 1)

    def fill_where(a, mask, fill):
        return jnp.concatenate([jnp.where(mask, a[:, r * TQ:(r + 1) * TQ], fill)
                                for r in range(GROUP_HEADS)], axis=1)

    vrows = lambda v, g: v[g * HEAD_DIM:(g + 1) * HEAD_DIM, :]

    s_cmp = [_dot(kc_ref[0, 0:ncv, :], qg[g]) for g in groups]

    c_idx = lax.broadcasted_iota(jnp.int32, (ncv, 1), 0)
    mask_c = ((c_idx * CMP_STRIDE + (CMP_LEN - 1)) <= t_q) & (c_idx < ncp - 1)
    p_c, o_c = [], []
    for g in groups:
        if bounded:
            e_c = fill_where(jnp.exp2(s_cmp[g]), mask_c, 0.0)
        else:
            sm = fill_where(s_cmp[g], mask_c, NEG_INF)
            e_c = fill_where(jnp.exp2(sm - jnp.max(sm, axis=0, keepdims=True)), mask_c, 0.0)
        l_c = jnp.sum(e_c, axis=0, keepdims=True)
        p = e_c * (1.0 / jnp.where(l_c > 0.0, l_c, 1.0))
        p_c.append(p)
        o_c.append(_dot(vrows(vct_ref[0], g)[:, 0:ncv], p.astype(BF16)))

    if nblk <= SEL_TOPK:
        pen_sc[...] = jnp.zeros(pen_sc.shape, BF16)
    else:
        n_idx = lax.broadcasted_iota(jnp.int32, (nblk, TQ), 0)
        tq = t0 + lax.broadcasted_iota(jnp.int32, (nblk, TQ), 1)
        cur = tq >> 6
        forced = (n_idx == 0) | (n_idx == cur) | (n_idx == cur - 1)
        visible = (n_idx << 6) <= tq
        ovt = ovt_ref[0:nblk, 0:ncv]
        left = []
        for g in groups:
            psum = p_c[g][:, 0:TQ]
            for r in range(1, GROUP_HEADS):
                psum = psum + p_c[g][:, r * TQ:(r + 1) * TQ]
            hi, mid, lo = _split3(psum)
            imp = _dot(ovt, hi) + _dot(ovt, mid) + _dot(ovt, lo)
            left.append(jnp.where(forced, TAKEN, jnp.where(visible, imp, -1.0)))
        for _ in range(SEL_TOPK - N_FORCED):
            for g in groups:
                top = jnp.max(left[g], axis=0, keepdims=True)
                first = jnp.min(jnp.where(left[g] == top, n_idx, nblk), axis=0, keepdims=True)
                left[g] = jnp.where(n_idx == first, TAKEN, left[g])
        for g in groups:
            pen_sc[g, 0:nblk, :] = jnp.where(left[g] == TAKEN, 0.0, NEG_INF).astype(BF16)
            pen_sc[g, nblk:LANES, :] = jnp.zeros((LANES - nblk, TQ), BF16)

    nwc = WINDOW // VCHUNK + TQ // VCHUNK
    cw = jnp.maximum(i * (TQ // VCHUNK) - WINDOW // VCHUNK, 0)
    w0 = pl.multiple_of(cw * VCHUNK, VCHUNK)
    kwin = kw_ref[0, pl.ds(w0, nwc * VCHUNK), :]
    s_win = [_dot(kwin, qg[g]) for g in groups]

    q_aug = [jnp.concatenate([qg[g], jnp.concatenate([pen_sc[g]] * GROUP_HEADS, axis=1)], axis=0)
             for g in groups]

    def window_branch():
        vwin = jnp.concatenate([vwt_ref[0, cw + j] for j in range(nwc)], axis=1)
        kpos = w0 + lax.broadcasted_iota(jnp.int32, (nwc * VCHUNK, 1), 0)
        mask_w = (kpos <= t_q) & (kpos > t_q - WINDOW)
        o_w, l_w = [], []
        for g in groups:
            if bounded:
                p_w = fill_where(jnp.exp2(s_win[g]), mask_w, 0.0)
            else:
                sw = fill_where(s_win[g], mask_w, NEG_INF)
                p_w = jnp.exp2(sw - jnp.max(sw, axis=0, keepdims=True))
            v_ext = jnp.concatenate([vrows(vwin, g), jnp.ones((16, nwc * VCHUNK), BF16)], axis=0)
            ol = _dot(v_ext, p_w.astype(BF16))
            o_w.append(ol[0:HEAD_DIM, :])
            l_w.append(ol[HEAD_DIM:HEAD_DIM + 1, :])
        return o_w, l_w

    def v_chunk(c, g):
        cv = c * (TK // VCHUNK)
        return jnp.concatenate([vrows(vst_ref[0, cv + j], g) for j in range(TK // VCHUNK)], axis=1)

    def qk(c, slot, g):
        k0 = pl.multiple_of(c * TK, TK)
        s = _dot(ksf_ref[0, pl.ds(k0, TK), :], q_aug[g])
        s_sc[slot, g] = s
        if not bounded:
            cm_sc[slot, g] = jnp.max(s, axis=0, keepdims=True)

    def sm_pv(c, slot, g, state, masked):
        m_p, l_p, acc = state
        s = s_sc[slot, g]
        if masked:
            kpos = c * TK + lax.broadcasted_iota(jnp.int32, (TK, 1), 0)
            s = fill_where(s, kpos <= t_q, NEG_INF)
        if bounded:
            p = jnp.exp2(s)
            return m_p, l_p + jnp.sum(p, axis=0, keepdims=True), acc + _dot(v_chunk(c, g), p.astype(BF16))
        cm = jnp.max(s, axis=0, keepdims=True) if masked else cm_sc[slot, g]
        m_n = jnp.maximum(m_p, cm)
        alpha = jnp.exp2(m_p - m_n)
        p = jnp.exp2(s - m_n)
        l_n = alpha * l_p + jnp.sum(p, axis=0, keepdims=True)
        acc = alpha * acc + _dot(v_chunk(c, g), p.astype(BF16))
        return m_n, l_n, acc

    def step(c, slot, carry):
        out = []
        for g in groups:
            qk(c + 1, 1 - slot, g)
            out.append(sm_pv(c, slot, g, carry[g], False))
        return tuple(out)

    def last(slot, carry):
        return tuple(sm_pv(n_full, slot, g, carry[g], True) for g in groups)

    n_full = t0 // TK
    for g in groups:
        qk(0, 0, g)
    o_w, l_w = window_branch()
    init = tuple((jnp.full((1, rows), 0.0 if bounded else NEG_INF, F32), jnp.zeros((1, rows), F32),
                  jnp.zeros((HEAD_DIM, rows), F32)) for _ in groups)
    carry = lax.fori_loop(0, n_full // 2, lambda h, cr: step(2 * h + 1, 1, step(2 * h, 0, cr)), init)
    carry = lax.cond((n_full & 1) == 1,
                     lambda cr: last(1, step(n_full - 1, 0, cr)),
                     lambda cr: last(0, cr), carry)
    sel_out = [(carry[g][1], carry[g][2]) for g in groups]

    g_t = gate_ref[0].T
    slabs = []
    for g in groups:
        def gate_row(br):
            base = g * (N_BRANCH * GROUP_HEADS) + br * GROUP_HEADS
            return jnp.concatenate([g_t[base + r:base + r + 1, :] for r in range(GROUP_HEADS)], axis=1)
        l_s, o_s = sel_out[g]
        og = o_c[g] * gate_row(0) + o_s * (gate_row(1) / l_s) + o_w[g] * (gate_row(2) / l_w[g])
        slabs += [og[:, r * TQ:(r + 1) * TQ] for r in range(GROUP_HEADS)]
    out_t = jnp.concatenate(slabs, axis=0)
    o_ref[0] = (out_t.T * zs_ref[0].astype(F32)).astype(BF16)


def _out_kernel(x_ref, *refs):
    on_refs, (oc_ref, w_ref, o_ref) = refs[:-3], refs[-3:]
    for k, on_ref in enumerate(on_refs):
        @pl.when(pl.program_id(1) == k)
        def _(on_ref=on_ref):
            acc = _dot(on_ref[0], w_ref[0:D_NSA, :]) + _dot(oc_ref[0], w_ref[D_NSA:D_NSA + D_CONV, :])
            o_ref[0] = x_ref[0] + acc


def _rope_tables(pos):
    inv_freq = (np.float32(ROPE_THETA) ** (-np.arange(0, ROT_DIM, 2, dtype=np.float32) / ROT_DIM)).astype(np.float32)
    ang = pos.astype(np.float32)[:, None] * inv_freq[None, :]
    cos, sin = np.cos(ang), np.sin(ang)
    n = pos.shape[0]
    ones = np.ones((n, HEAD_DIM - ROT_DIM), np.float32)
    zeros = np.zeros((n, HEAD_DIM - ROT_DIM), np.float32)
    z8 = np.zeros((n, ROT_DIM // 2), np.float32)
    c = np.concatenate([cos, cos, ones], axis=1)
    sa = np.concatenate([z8, sin, zeros], axis=1)
    sb = np.concatenate([-sin, z8, zeros], axis=1)
    return tuple(jnp.asarray(np.concatenate([t, t], axis=1), F32) for t in (c, sa, sb))


def _overlap_t(ncp):
    cs = np.arange(ncp) * CMP_STRIDE
    ce = cs + CMP_LEN
    ss = np.arange(SEL_BLOCK) * SEL_BLOCK
    se = ss + SEL_BLOCK
    ov = np.clip(np.minimum(ce[None, :], se[:, None]) - np.maximum(cs[None, :], ss[:, None]), 0, None)
    ov = ov.astype(np.float32) / CMP_LEN
    ov[:, ncp - 1] = 0.0
    return jnp.asarray(ov, BF16)


def _cmp_weights(pos, w1, b1, w2):
    half = CMP_STRIDE
    def big_w1(w):
        w = w.reshape(half, 1, HEAD_DIM, 1, CMP_HIDDEN)
        eye = jnp.eye(KV_GROUPS, dtype=F32).reshape(1, KV_GROUPS, 1, KV_GROUPS, 1)
        return (w * eye).reshape(half * KV_GROUPS * HEAD_DIM, KV_GROUPS * CMP_HIDDEN).astype(BF16)
    def big_pos(p):
        return jnp.broadcast_to(p[:, None, :], (half, KV_GROUPS, HEAD_DIM)).reshape(1, -1)
    w1a, w1b = w1[:half * HEAD_DIM], w1[half * HEAD_DIM:]
    eye2 = jnp.eye(KV_GROUPS, dtype=F32)
    w2b = (w2[None, :, None, :] * eye2[:, None, :, None]).reshape(KV_GROUPS * CMP_HIDDEN,
                                                                 KV_GROUPS * HEAD_DIM).astype(BF16)
    b1b = jnp.tile(b1, KV_GROUPS)[None, :]
    return big_pos(pos[:half]), big_pos(pos[half:]), big_w1(w1a), big_w1(w1b), b1b, w2b


def _full(shape):
    nd = len(shape)
    return pl.BlockSpec(shape, lambda *_: (0,) * nd)


def kernel(x, norm_w, w_in, q_norm_w, k_norm_w, cmp_k_pos, cmp_k_w1, cmp_k_b1, cmp_k_w2,
           cmp_v_pos, cmp_v_w1, cmp_v_b1, cmp_v_w2, conv_w, conv_b, w_out):
    B, S, D = x.shape
    assert norm_w.shape[0] == 1, "single layer"
    assert S % TM == 0 and S % TM_OUT == 0 and TM_OUT % TQ == 0 and S // SEL_BLOCK <= SEL_BLOCK and S >= WINDOW + TQ
    ncp = S // CMP_STRIDE
    nst = S // TM
    f32 = lambda a: a.astype(F32)

    w = f32(w_in[0])
    head_order = []
    for j in range(GROUP_HEADS):
        head_order += [j, GROUP_HEADS + j]
    wq = w[:, :D_NSA].reshape(D, NSA_HEADS, HEAD_DIM)[:, jnp.array(head_order), :].reshape(D, D_NSA)
    gate_cols = []
    for g in range(KV_GROUPS):
        for br in range(N_BRANCH):
            for r in range(GROUP_HEADS):
                gate_cols.append((g * GROUP_HEADS + r) * N_BRANCH + br)
    g0 = D_NSA + 6 * LANES
    wg = w[:, g0:g0 + NSA_HEADS * N_BRANCH][:, jnp.array(gate_cols)]
    wg = jnp.pad(wg, ((0, 0), (0, LANES - NSA_HEADS * N_BRANCH)))
    rest0 = g0 + NSA_HEADS * N_BRANCH
    wq, wkv, wg, wr = (t.astype(BF16) for t in (wq, w[:, D_NSA:g0], wg, w[:, rest0:]))
    n_rest = wr.shape[1]

    rc, rsa, rsb = _rope_tables(np.arange(S))
    mbd = jnp.asarray(np.kron(np.eye(4), np.full((HEAD_DIM, HEAD_DIM), 1.0 / HEAD_DIM)), BF16)
    qnw = jnp.tile(f32(q_norm_w[0]), 4)[None, :]
    knw_sw = jnp.concatenate([jnp.tile(f32(k_norm_w[0, 1]), 2), jnp.tile(f32(k_norm_w[0, 2]), 2)])[None, :]
    knw_c = jnp.tile(f32(k_norm_w[0, 0]), 2)[None, :]

    cp = pltpu.CompilerParams(dimension_semantics=("arbitrary", "arbitrary"),
                              vmem_limit_bytes=VMEM_LIMIT)
    row_blk = lambda n: pl.BlockSpec((1, TM, n), lambda b, s: (b, s, 0))
    tab_blk = pl.BlockSpec((TM, LANES), lambda b, s: (s, 0))
    vt_blk = pl.BlockSpec((1, TM // VCHUNK, LANES, VCHUNK), lambda b, s: (b, s, 0, 0))
    sds = jax.ShapeDtypeStruct

    (qt4, kc_raw, vc_raw, ksf, vst, kw, vwt, gates, zs, oconv) = pl.pallas_call(
        _proj_kernel,
        grid=(B, nst),
        in_specs=[row_blk(D), _full((1, D)), _full((D, D_NSA)), _full((D, 6 * LANES)), _full((D, LANES)),
                  _full((D, n_rest)), _full((1, 256)), _full((1, 256)),
                  tab_blk, tab_blk, tab_blk, _full((256, 256)), _full((3, D_CONV)), _full((1, D_CONV))],
        out_specs=[pl.BlockSpec((1, 4, LANES, TM), lambda b, s: (b, 0, 0, s)),
                   row_blk(LANES), row_blk(LANES), row_blk(2 * LANES), vt_blk, row_blk(LANES), vt_blk,
                   row_blk(LANES), row_blk(D_NSA), row_blk(D_CONV)],
        out_shape=[sds((B, 4, LANES, S), BF16), sds((B, S, LANES), F32), sds((B, S, LANES), F32),
                   sds((B, S, 2 * LANES), BF16), sds((B, S // VCHUNK, LANES, VCHUNK), BF16),
                   sds((B, S, LANES), BF16), sds((B, S // VCHUNK, LANES, VCHUNK), BF16),
                   sds((B, S, LANES), F32), sds((B, S, D_NSA), BF16), sds((B, S, D_CONV), BF16)],
        scratch_shapes=[pltpu.VMEM((TM + 16, D_CONV), F32)],
        compiler_params=cp,
        name="nsa_proj",
    )(x, f32(norm_w), wq, wkv, wg, wr, qnw, knw_sw, rc, rsa, rsb, mbd, f32(conv_w[0]), f32(conv_b))

    cmp_pos = np.arange(ncp) * CMP_STRIDE + (CMP_LEN - 1)
    crc, crsa, crsb = _rope_tables(cmp_pos)
    kpa, kpb, kw1a, kw1b, kb1, kw2 = _cmp_weights(f32(cmp_k_pos[0]), f32(cmp_k_w1[0]),
                                                  f32(cmp_k_b1[0]), f32(cmp_k_w2[0]))
    vpa, vpb, vw1a, vw1b, vb1, vw2 = _cmp_weights(f32(cmp_v_pos[0]), f32(cmp_v_w1[0]),
                                                  f32(cmp_v_b1[0]), f32(cmp_v_w2[0]))
    flat = CMP_STRIDE * LANES
    h_blk = pl.BlockSpec((1, S, LANES), lambda b: (b, 0, 0))
    kc, vct = pl.pallas_call(
        functools.partial(_cmp_kernel, ncp=ncp),
        grid=(B,),
        in_specs=[h_blk, h_blk] + [_full((1, flat))] * 4
                 + [_full((flat, 256)), _full((flat, 256)), _full((1, 256)), _full((256, LANES))] * 2
                 + [_full((1, LANES)), _full((ncp, LANES)), _full((ncp, LANES)), _full((ncp, LANES)),
                    _full((256, 256))],
        out_specs=[pl.BlockSpec((1, ncp, LANES), lambda b: (b, 0, 0)),
                   pl.BlockSpec((1, LANES, ncp), lambda b: (b, 0, 0))],
        out_shape=[sds((B, ncp, LANES), BF16), sds((B, LANES, ncp), BF16)],
        compiler_params=pltpu.CompilerParams(dimension_semantics=("arbitrary",),
                                             vmem_limit_bytes=VMEM_LIMIT),
        name="nsa_compress",
    )(kc_raw, vc_raw, kpa, kpb, vpa, vpb,
      kw1a, kw1b, kb1, kw2, vw1a, vw1b, vb1, vw2, knw_c, crc, crsa, crsb, mbd)

    score_bound = (HEAD_DIM * Q_SCALE) * jnp.max(jnp.abs(f32(q_norm_w))) * jnp.max(jnp.abs(f32(k_norm_w)))
    tiles_per_range = TM_OUT // TQ
    n_ranges = S // TM_OUT

    def attention(bounded, r):
        last_pos = (r + 1) * TM_OUT
        tile0 = r * tiles_per_range
        return pl.pallas_call(
            functools.partial(_attn_kernel, ncp=ncp, bounded=bounded, tile0=tile0,
                              ncv=min(ncp, last_pos // CMP_STRIDE), nblk=min(SEL_BLOCK, last_pos // SEL_BLOCK)),
            grid=(B, tiles_per_range),
            in_specs=[pl.BlockSpec((1, 4, LANES, TQ), lambda b, i: (b, 0, 0, i + tile0)),
                      pl.BlockSpec((1, ncp, LANES), lambda b, i: (b, 0, 0)),
                      pl.BlockSpec((1, LANES, ncp), lambda b, i: (b, 0, 0)),
                      pl.BlockSpec((1, S, 2 * LANES), lambda b, i: (b, 0, 0)),
                      pl.BlockSpec((1, S // VCHUNK, LANES, VCHUNK), lambda b, i: (b, 0, 0, 0)),
                      pl.BlockSpec((1, S, LANES), lambda b, i: (b, 0, 0)),
                      pl.BlockSpec((1, S // VCHUNK, LANES, VCHUNK), lambda b, i: (b, 0, 0, 0)),
                      pl.BlockSpec((1, TQ, LANES), lambda b, i: (b, i + tile0, 0)),
                      pl.BlockSpec((1, TQ, D_NSA), lambda b, i: (b, i + tile0, 0)),
                      _full((SEL_BLOCK, ncp))],
            out_specs=pl.BlockSpec((1, TQ, D_NSA), lambda b, i: (b, i, 0)),
            out_shape=sds((B, TM_OUT, D_NSA), BF16),
            scratch_shapes=[pltpu.VMEM((KV_GROUPS, LANES, TQ), BF16),
                            pltpu.VMEM((2, KV_GROUPS, TK, GROUP_HEADS * TQ), F32),
                            pltpu.VMEM((2, KV_GROUPS, 1, GROUP_HEADS * TQ), F32)],
            compiler_params=cp,
            name=("nsa_attn_bounded_r%d" if bounded else "nsa_attn_online_r%d") % r,
        )

    attn_args = (qt4, kc, vct, ksf, vst, kw, vwt, gates, zs, _overlap_t(ncp))
    all_ranges = lambda bounded: (lambda *a: tuple(attention(bounded, r)(*a) for r in range(n_ranges)))
    o_nsa = lax.cond(score_bound <= MAX_SAFE_SCORE, all_ranges(True), all_ranges(False), *attn_args)

    out_blk = lambda n: pl.BlockSpec((1, TM_OUT, n), lambda b, s: (b, s, 0))
    on_blk = lambda k: pl.BlockSpec((1, TM_OUT, D_NSA),
                                    lambda b, s: (jnp.where(s >= k, b, jnp.maximum(b - 1, 0)), 0, 0))
    out = pl.pallas_call(
        _out_kernel,
        grid=(B, S // TM_OUT),
        in_specs=[out_blk(D)] + [on_blk(k) for k in range(n_ranges)]
                 + [out_blk(D_CONV), _full((D_NSA + D_CONV, D))],
        out_specs=out_blk(D),
        out_shape=sds((B, S, D), x.dtype),
        compiler_params=cp,
        name="nsa_out",
    )(x, *o_nsa, oconv, f32(w_out[0]).astype(BF16))
    return out
```

```python
import functools

import numpy as np
import jax
import jax.numpy as jnp
from jax import lax
from jax.experimental import pallas as pl
from jax.experimental.pallas import tpu as pltpu

LANES = 128
HEAD_DIM = 64
NSA_HEADS = 8
KV_GROUPS = 2
GROUP_HEADS = NSA_HEADS // KV_GROUPS
D_NSA = NSA_HEADS * HEAD_DIM
D_CONV = 512
N_BRANCH = 3
ROT_DIM = HEAD_DIM // 4
ROPE_THETA = 500000.0
CMP_LEN = 32
CMP_STRIDE = 16
CMP_HIDDEN = 2 * HEAD_DIM
SEL_BLOCK = 64
SEL_TOPK = 16
WINDOW = 512
EPS = 1e-6
NEG_INF = -1e30
N_FORCED = 3
TAKEN = -2.0
SCALE = HEAD_DIM ** -0.5
Q_SCALE = SCALE * float(np.log2(np.e))

TM = 512
TM_OUT = 1024
TQ = 256
TK = 512
VCHUNK = 128
VMEM_LIMIT = 56 * 1024 * 1024
MAX_SAFE_SCORE = 50.0

BF16 = jnp.bfloat16
F32 = jnp.float32


def _dot(a, b):
    return jnp.dot(a, b, preferred_element_type=F32)


def _split3(a):
    hi = a.astype(BF16)
    r1 = a - hi.astype(F32)
    mid = r1.astype(BF16)
    lo = (r1 - mid.astype(F32)).astype(BF16)
    return hi, mid, lo


def _group_mean(sq, mbd):
    return _dot(sq.astype(BF16), mbd)


def _rope(xn, c, sa, sb):
    return xn * c + pltpu.roll(xn, 8, 1) * sa + pltpu.roll(xn, LANES - 8, 1) * sb


def _silu(z):
    return z * (1.0 / (1.0 + jnp.exp(-z)))


def _proj_kernel(x_ref, nw_ref, wq_ref, wkv_ref, wr_ref, qnw_ref, knw_ref,
                 rc_ref, rsa_ref, rsb_ref, mbd_ref, cw_ref, cb_ref,
                 qt_ref, kc_ref, vc_ref, ksf_ref, vst_ref, kw_ref, vwt_ref,
                 gate_ref, zs_ref, oconv_ref, ubuf):
    si = pl.program_id(1)

    @pl.when(si == 0)
    def _():
        ubuf[0:8, :] = jnp.zeros((8, D_CONV), F32)

    x = x_ref[0]
    ms = jnp.mean(x * x, axis=-1, keepdims=True)
    h = (x * lax.rsqrt(ms + EPS) * nw_ref[...]).astype(BF16)

    rc, rsa, rsb = rc_ref[...], rsa_ref[...], rsb_ref[...]
    mbd = mbd_ref[...]

    pq = _dot(h, wq_ref[...])
    for pair in range(2):
        blk = pq[:, pair * 256:(pair + 1) * 256]
        msq = _group_mean(blk * blk, mbd)
        qn = blk * lax.rsqrt(msq + EPS) * qnw_ref[...]
        for half in range(2):
            t = qn[:, half * LANES:(half + 1) * LANES]
            qt_ref[0, pair * 2 + half] = (_rope(t, rc, rsa, rsb) * Q_SCALE).T.astype(BF16)

    pkv = _dot(h, wkv_ref[...])
    kc_ref[0] = pkv[:, 0:128]
    vc_ref[0] = pkv[:, 128:256]
    ksw = jnp.concatenate([pkv[:, 256:384], pkv[:, 512:640]], axis=1)
    msk = _group_mean(ksw * ksw, mbd)
    kn = ksw * lax.rsqrt(msk + EPS) * knw_ref[...]
    ks = _rope(kn[:, 0:128], rc, rsa, rsb)
    kw = _rope(kn[:, 128:256], rc, rsa, rsb)
    row = lax.broadcasted_iota(jnp.int32, (TM, LANES), 0) + si * TM
    lane = lax.broadcasted_iota(jnp.int32, (TM, LANES), 1)
    onehot = jnp.where(lane == (row >> 6), 1.0, 0.0)
    ksf_ref[0, :, 0:128] = ks.astype(BF16)
    ksf_ref[0, :, 128:256] = onehot.astype(BF16)
    kw_ref[0] = kw.astype(BF16)
    vs = pkv[:, 384:512]
    vw = pkv[:, 640:768]
    for j in range(TM // VCHUNK):
        vst_ref[0, j] = vs[j * VCHUNK:(j + 1) * VCHUNK, :].T.astype(BF16)
        vwt_ref[0, j] = vw[j * VCHUNK:(j + 1) * VCHUNK, :].T.astype(BF16)

    gate_ref[0] = 1.0 / (1.0 + jnp.exp(-pkv[:, 768:896]))
    zs_ref[0] = _silu(_dot(h, wr_ref[:, 0:D_NSA])).astype(BF16)

    pcv = _dot(h, wr_ref[:, D_NSA:D_NSA + 4 * D_CONV])
    u = pcv[:, 1024:1536] * pcv[:, 0:512]
    ubuf[8:8 + TM, :] = u
    u1 = ubuf[7:7 + TM, :]
    u2 = ubuf[6:6 + TM, :]
    conv = cw_ref[0:1, :] * u2 + cw_ref[1:2, :] * u1 + cw_ref[2:3, :] * u + cb_ref[...]
    oconv_ref[0] = (pcv[:, 512:1024] * conv * _silu(pcv[:, 1536:2048])).astype(BF16)
    ubuf[0:8, :] = ubuf[TM:TM + 8, :]


def _cmp_kernel(hk_ref, hv_ref, pka_ref, pkb_ref, pva_ref, pvb_ref,
                wk1a_ref, wk1b_ref, bk1_ref, wk2_ref,
                wv1a_ref, wv1b_ref, bv1_ref, wv2_ref,
                knw_ref, rc_ref, rsa_ref, rsb_ref, mbd_ref,
                kc_ref, vct_ref, *, ncp):
    def first_layer(h_ref, p_ref, w_ref):
        acc = None
        for l in range(0, CMP_STRIDE, 2):
            lhs = jnp.concatenate([h_ref[0, pl.ds(l, ncp, stride=CMP_STRIDE), :],
                                   h_ref[0, pl.ds(l + 1, ncp, stride=CMP_STRIDE), :]], axis=1)
            lhs = (lhs + p_ref[:, l * LANES:(l + 2) * LANES]).astype(BF16)
            part = _dot(lhs, w_ref[l * LANES:(l + 2) * LANES, :])
            acc = part if acc is None else acc + part
        return acc

    def mlp(h_ref, pa_ref, pb_ref, w1a_ref, w1b_ref, b1_ref, w2_ref):
        p = first_layer(h_ref, pa_ref, w1a_ref)
        q = first_layer(h_ref, pb_ref, w1b_ref)
        pre = p + pltpu.roll(q, ncp - 1, 0) + b1_ref[...]
        return _dot(_silu(pre).astype(BF16), w2_ref[...])

    kc = mlp(hk_ref, pka_ref, pkb_ref, wk1a_ref, wk1b_ref, bk1_ref, wk2_ref)
    msk = _group_mean(kc * kc, mbd_ref[0:LANES, 0:LANES])
    kn = kc * lax.rsqrt(msk + EPS) * knw_ref[...]
    kc_ref[0] = _rope(kn, rc_ref[...], rsa_ref[...], rsb_ref[...]).astype(BF16)
    vc = mlp(hv_ref, pva_ref, pvb_ref, wv1a_ref, wv1b_ref, bv1_ref, wv2_ref)
    vct_ref[0] = vc.T.astype(BF16)


def _attn_kernel(qt_ref, kc_ref, vct_ref, ksf_ref, vst_ref, kw_ref, vwt_ref,
                 gate_ref, zs_ref, ovt_ref, o_ref, pen_sc, s_sc, cm_sc, *, ncp, bounded, tile0, ncv, nblk):
    i = pl.program_id(1) + tile0
    t0 = i * TQ
    rows = GROUP_HEADS * TQ
    groups = range(KV_GROUPS)

    q_all = jnp.concatenate([qt_ref[0, j] for j in range(GROUP_HEADS)], axis=1)
    frow = lax.broadcasted_iota(jnp.int32, (LANES, rows), 0)
    qg = [jnp.where(frow < HEAD_DIM, q_all, 0), jnp.where(frow >= HEAD_DIM, q_all, 0)]
    t_q = t0 + lax.broadcasted_iota(jnp.int32, (1, TQ), 1)

    def fill_where(a, mask, fill):
        return jnp.concatenate([jnp.where(mask, a[:, r * TQ:(r + 1) * TQ], fill)
                                for r in range(GROUP_HEADS)], axis=1)

    vrows = lambda v, g: v[g * HEAD_DIM:(g + 1) * HEAD_DIM, :]

    s_cmp = [_dot(kc_ref[0, 0:ncv, :], qg[g]) for g in groups]

    c_idx = lax.broadcasted_iota(jnp.int32, (ncv, 1), 0)
    mask_c = ((c_idx * CMP_STRIDE + (CMP_LEN - 1)) <= t_q) & (c_idx < ncp - 1)
    p_c, o_c = [], []
    for g in groups:
        if bounded:
            e_c = fill_where(jnp.exp2(s_cmp[g]), mask_c, 0.0)
        else:
            sm = fill_where(s_cmp[g], mask_c, NEG_INF)
            e_c = fill_where(jnp.exp2(sm - jnp.max(sm, axis=0, keepdims=True)), mask_c, 0.0)
        l_c = jnp.sum(e_c, axis=0, keepdims=True)
        p = e_c * (1.0 / jnp.where(l_c > 0.0, l_c, 1.0))
        p_c.append(p)
        o_c.append(_dot(vrows(vct_ref[0], g)[:, 0:ncv], p.astype(BF16)))

    if nblk <= SEL_TOPK:
        pen_sc[...] = jnp.zeros(pen_sc.shape, BF16)
    else:
        n_idx = lax.broadcasted_iota(jnp.int32, (nblk, TQ), 0)
        tq = t0 + lax.broadcasted_iota(jnp.int32, (nblk, TQ), 1)
        cur = tq >> 6
        forced = (n_idx == 0) | (n_idx == cur) | (n_idx == cur - 1)
        visible = (n_idx << 6) <= tq
        ovt = ovt_ref[0:nblk, 0:ncv]
        left = []
        for g in groups:
            psum = p_c[g][:, 0:TQ]
            for r in range(1, GROUP_HEADS):
                psum = psum + p_c[g][:, r * TQ:(r + 1) * TQ]
            hi, mid, lo = _split3(psum)
            imp = _dot(ovt, hi) + _dot(ovt, mid) + _dot(ovt, lo)
            left.append(jnp.where(forced, TAKEN, jnp.where(visible, imp, -1.0)))
        for _ in range(SEL_TOPK - N_FORCED):
            for g in groups:
                top = jnp.max(left[g], axis=0, keepdims=True)
                first = jnp.min(jnp.where(left[g] == top, n_idx, nblk), axis=0, keepdims=True)
                left[g] = jnp.where(n_idx == first, TAKEN, left[g])
        for g in groups:
            pen_sc[g, 0:nblk, :] = jnp.where(left[g] == TAKEN, 0.0, NEG_INF).astype(BF16)
            pen_sc[g, nblk:LANES, :] = jnp.zeros((LANES - nblk, TQ), BF16)

    nwc = WINDOW // VCHUNK + TQ // VCHUNK
    cw = jnp.maximum(i * (TQ // VCHUNK) - WINDOW // VCHUNK, 0)
    w0 = pl.multiple_of(cw * VCHUNK, VCHUNK)
    kwin = kw_ref[0, pl.ds(w0, nwc * VCHUNK), :]
    s_win = [_dot(kwin, qg[g]) for g in groups]

    q_aug = [jnp.concatenate([qg[g], jnp.concatenate([pen_sc[g]] * GROUP_HEADS, axis=1)], axis=0)
             for g in groups]

    def window_branch():
        vwin = jnp.concatenate([vwt_ref[0, cw + j] for j in range(nwc)], axis=1)
        kpos = w0 + lax.broadcasted_iota(jnp.int32, (nwc * VCHUNK, 1), 0)
        mask_w = (kpos <= t_q) & (kpos > t_q - WINDOW)
        o_w, l_w = [], []
        for g in groups:
            if bounded:
                p_w = fill_where(jnp.exp2(s_win[g]), mask_w, 0.0)
            else:
                sw = fill_where(s_win[g], mask_w, NEG_INF)
                p_w = jnp.exp2(sw - jnp.max(sw, axis=0, keepdims=True))
            v_ext = jnp.concatenate([vrows(vwin, g), jnp.ones((16, nwc * VCHUNK), BF16)], axis=0)
            ol = _dot(v_ext, p_w.astype(BF16))
            o_w.append(ol[0:HEAD_DIM, :])
            l_w.append(ol[HEAD_DIM:HEAD_DIM + 1, :])
        return o_w, l_w

    def v_chunk(c, g):
        cv = c * (TK // VCHUNK)
        return jnp.concatenate([vrows(vst_ref[0, cv + j], g) for j in range(TK // VCHUNK)], axis=1)

    def qk(c, slot, g):
        k0 = pl.multiple_of(c * TK, TK)
        s = _dot(ksf_ref[0, pl.ds(k0, TK), :], q_aug[g])
        s_sc[slot, g] = s
        if not bounded:
            cm_sc[slot, g] = jnp.max(s, axis=0, keepdims=True)

    def sm_pv(c, slot, g, state, masked):
        m_p, l_p, acc = state
        s = s_sc[slot, g]
        if masked:
            kpos = c * TK + lax.broadcasted_iota(jnp.int32, (TK, 1), 0)
            s = fill_where(s, kpos <= t_q, NEG_INF)
        if bounded:
            p = jnp.exp2(s)
            return m_p, l_p + jnp.sum(p, axis=0, keepdims=True), acc + _dot(v_chunk(c, g), p.astype(BF16))
        cm = jnp.max(s, axis=0, keepdims=True) if masked else cm_sc[slot, g]
        m_n = jnp.maximum(m_p, cm)
        alpha = jnp.exp2(m_p - m_n)
        p = jnp.exp2(s - m_n)
        l_n = alpha * l_p + jnp.sum(p, axis=0, keepdims=True)
        acc = alpha * acc + _dot(v_chunk(c, g), p.astype(BF16))
        return m_n, l_n, acc

    def step(c, slot, carry):
        out = []
        for g in groups:
            qk(c + 1, 1 - slot, g)
            out.append(sm_pv(c, slot, g, carry[g], False))
        return tuple(out)

    def last(slot, carry):
        return tuple(sm_pv(n_full, slot, g, carry[g], True) for g in groups)

    n_full = t0 // TK
    for g in groups:
        qk(0, 0, g)
    o_w, l_w = window_branch()
    init = tuple((jnp.full((1, rows), 0.0 if bounded else NEG_INF, F32), jnp.zeros((1, rows), F32),
                  jnp.zeros((HEAD_DIM, rows), F32)) for _ in groups)
    carry = init
    for h in range(tile0 * TQ // (2 * TK)):
        carry = step(2 * h + 1, 1, step(2 * h, 0, carry))
    carry = lax.cond((n_full & 1) == 1,
                     lambda cr: last(1, step(n_full - 1, 0, cr)),
                     lambda cr: last(0, cr), carry)
    sel_out = [(carry[g][1], carry[g][2]) for g in groups]

    g_t = gate_ref[0].T
    slabs = []
    for g in groups:
        def gate_row(br):
            base = g * (N_BRANCH * GROUP_HEADS) + br * GROUP_HEADS
            return jnp.concatenate([g_t[base + r:base + r + 1, :] for r in range(GROUP_HEADS)], axis=1)
        l_s, o_s = sel_out[g]
        og = o_c[g] * gate_row(0) + o_s * (gate_row(1) / l_s) + o_w[g] * (gate_row(2) / l_w[g])
        slabs += [og[:, r * TQ:(r + 1) * TQ] for r in range(GROUP_HEADS)]
    out_t = jnp.concatenate(slabs, axis=0)
    o_ref[0] = (out_t.T * zs_ref[0].astype(F32)).astype(BF16)


def _out_kernel(x_ref, *refs):
    on_refs, (oc_ref, w_ref, o_ref) = refs[:-3], refs[-3:]
    for k, on_ref in enumerate(on_refs):
        @pl.when(pl.program_id(1) == k)
        def _(on_ref=on_ref):
            acc = _dot(on_ref[0], w_ref[0:D_NSA, :]) + _dot(oc_ref[0], w_ref[D_NSA:D_NSA + D_CONV, :])
            o_ref[0] = x_ref[0] + acc


def _rope_tables(pos):
    inv_freq = (np.float32(ROPE_THETA) ** (-np.arange(0, ROT_DIM, 2, dtype=np.float32) / ROT_DIM)).astype(np.float32)
    ang = pos.astype(np.float32)[:, None] * inv_freq[None, :]
    cos, sin = np.cos(ang), np.sin(ang)
    n = pos.shape[0]
    ones = np.ones((n, HEAD_DIM - ROT_DIM), np.float32)
    zeros = np.zeros((n, HEAD_DIM - ROT_DIM), np.float32)
    z8 = np.zeros((n, ROT_DIM // 2), np.float32)
    c = np.concatenate([cos, cos, ones], axis=1)
    sa = np.concatenate([z8, sin, zeros], axis=1)
    sb = np.concatenate([-sin, z8, zeros], axis=1)
    return tuple(jnp.asarray(np.concatenate([t, t], axis=1), F32) for t in (c, sa, sb))


def _overlap_t(ncp):
    cs = np.arange(ncp) * CMP_STRIDE
    ce = cs + CMP_LEN
    ss = np.arange(SEL_BLOCK) * SEL_BLOCK
    se = ss + SEL_BLOCK
    ov = np.clip(np.minimum(ce[None, :], se[:, None]) - np.maximum(cs[None, :], ss[:, None]), 0, None)
    ov = ov.astype(np.float32) / CMP_LEN
    ov[:, ncp - 1] = 0.0
    return jnp.asarray(ov, BF16)


def _cmp_weights(pos, w1, b1, w2):
    half = CMP_STRIDE
    def big_w1(w):
        w = w.reshape(half, 1, HEAD_DIM, 1, CMP_HIDDEN)
        eye = jnp.eye(KV_GROUPS, dtype=F32).reshape(1, KV_GROUPS, 1, KV_GROUPS, 1)
        return (w * eye).reshape(half * KV_GROUPS * HEAD_DIM, KV_GROUPS * CMP_HIDDEN).astype(BF16)
    def big_pos(p):
        return jnp.broadcast_to(p[:, None, :], (half, KV_GROUPS, HEAD_DIM)).reshape(1, -1)
    w1a, w1b = w1[:half * HEAD_DIM], w1[half * HEAD_DIM:]
    eye2 = jnp.eye(KV_GROUPS, dtype=F32)
    w2b = (w2[None, :, None, :] * eye2[:, None, :, None]).reshape(KV_GROUPS * CMP_HIDDEN,
                                                                 KV_GROUPS * HEAD_DIM).astype(BF16)
    b1b = jnp.tile(b1, KV_GROUPS)[None, :]
    return big_pos(pos[:half]), big_pos(pos[half:]), big_w1(w1a), big_w1(w1b), b1b, w2b


def _full(shape):
    nd = len(shape)
    return pl.BlockSpec(shape, lambda *_: (0,) * nd)


def kernel(x, norm_w, w_in, q_norm_w, k_norm_w, cmp_k_pos, cmp_k_w1, cmp_k_b1, cmp_k_w2,
           cmp_v_pos, cmp_v_w1, cmp_v_b1, cmp_v_w2, conv_w, conv_b, w_out):
    B, S, D = x.shape
    assert norm_w.shape[0] == 1, "single layer"
    assert S % TM == 0 and S % TM_OUT == 0 and TM_OUT % TQ == 0 and TM_OUT == 2 * TK and S // SEL_BLOCK <= SEL_BLOCK and S >= WINDOW + TQ
    ncp = S // CMP_STRIDE
    nst = S // TM
    f32 = lambda a: a.astype(F32)

    w = f32(w_in[0])
    head_order = []
    for j in range(GROUP_HEADS):
        head_order += [j, GROUP_HEADS + j]
    wq = w[:, :D_NSA].reshape(D, NSA_HEADS, HEAD_DIM)[:, jnp.array(head_order), :].reshape(D, D_NSA)
    gate_cols = []
    for g in range(KV_GROUPS):
        for br in range(N_BRANCH):
            for r in range(GROUP_HEADS):
                gate_cols.append((g * GROUP_HEADS + r) * N_BRANCH + br)
    g0 = D_NSA + 6 * LANES
    wg = w[:, g0:g0 + NSA_HEADS * N_BRANCH][:, jnp.array(gate_cols)]
    wg = jnp.pad(wg, ((0, 0), (0, LANES - NSA_HEADS * N_BRANCH)))
    rest0 = g0 + NSA_HEADS * N_BRANCH
    wq, wkv, wr = (t.astype(BF16) for t in (wq, jnp.concatenate([w[:, D_NSA:g0], wg], axis=1), w[:, rest0:]))
    n_rest = wr.shape[1]

    rc, rsa, rsb = _rope_tables(np.arange(S))
    mbd = jnp.asarray(np.kron(np.eye(4), np.full((HEAD_DIM, HEAD_DIM), 1.0 / HEAD_DIM)), BF16)
    qnw = jnp.tile(f32(q_norm_w[0]), 4)[None, :]
    knw_sw = jnp.concatenate([jnp.tile(f32(k_norm_w[0, 1]), 2), jnp.tile(f32(k_norm_w[0, 2]), 2)])[None, :]
    knw_c = jnp.tile(f32(k_norm_w[0, 0]), 2)[None, :]

    cp = pltpu.CompilerParams(dimension_semantics=("arbitrary", "arbitrary"),
                              vmem_limit_bytes=VMEM_LIMIT)
    row_blk = lambda n: pl.BlockSpec((1, TM, n), lambda b, s: (b, s, 0))
    tab_blk = pl.BlockSpec((TM, LANES), lambda b, s: (s, 0))
    vt_blk = pl.BlockSpec((1, TM // VCHUNK, LANES, VCHUNK), lambda b, s: (b, s, 0, 0))
    sds = jax.ShapeDtypeStruct

    (qt4, kc_raw, vc_raw, ksf, vst, kw, vwt, gates, zs, oconv) = pl.pallas_call(
        _proj_kernel,
        grid=(B, nst),
        in_specs=[row_blk(D), _full((1, D)), _full((D, D_NSA)), _full((D, 7 * LANES)),
                  _full((D, n_rest)), _full((1, 256)), _full((1, 256)),
                  tab_blk, tab_blk, tab_blk, _full((256, 256)), _full((3, D_CONV)), _full((1, D_CONV))],
        out_specs=[pl.BlockSpec((1, 4, LANES, TM), lambda b, s: (b, 0, 0, s)),
                   row_blk(LANES), row_blk(LANES), row_blk(2 * LANES), vt_blk, row_blk(LANES), vt_blk,
                   row_blk(LANES), row_blk(D_NSA), row_blk(D_CONV)],
        out_shape=[sds((B, 4, LANES, S), BF16), sds((B, S, LANES), F32), sds((B, S, LANES), F32),
                   sds((B, S, 2 * LANES), BF16), sds((B, S // VCHUNK, LANES, VCHUNK), BF16),
                   sds((B, S, LANES), BF16), sds((B, S // VCHUNK, LANES, VCHUNK), BF16),
                   sds((B, S, LANES), F32), sds((B, S, D_NSA), BF16), sds((B, S, D_CONV), BF16)],
        scratch_shapes=[pltpu.VMEM((TM + 16, D_CONV), F32)],
        compiler_params=cp,
        name="nsa_proj",
    )(x, f32(norm_w), wq, wkv, wr, qnw, knw_sw, rc, rsa, rsb, mbd, f32(conv_w[0]), f32(conv_b))

    cmp_pos = np.arange(ncp) * CMP_STRIDE + (CMP_LEN - 1)
    crc, crsa, crsb = _rope_tables(cmp_pos)
    kpa, kpb, kw1a, kw1b, kb1, kw2 = _cmp_weights(f32(cmp_k_pos[0]), f32(cmp_k_w1[0]),
                                                  f32(cmp_k_b1[0]), f32(cmp_k_w2[0]))
    vpa, vpb, vw1a, vw1b, vb1, vw2 = _cmp_weights(f32(cmp_v_pos[0]), f32(cmp_v_w1[0]),
                                                  f32(cmp_v_b1[0]), f32(cmp_v_w2[0]))
    flat = CMP_STRIDE * LANES
    h_blk = pl.BlockSpec((1, S, LANES), lambda b: (b, 0, 0))
    kc, vct = pl.pallas_call(
        functools.partial(_cmp_kernel, ncp=ncp),
        grid=(B,),
        in_specs=[h_blk, h_blk] + [_full((1, flat))] * 4
                 + [_full((flat, 256)), _full((flat, 256)), _full((1, 256)), _full((256, LANES))] * 2
                 + [_full((1, LANES)), _full((ncp, LANES)), _full((ncp, LANES)), _full((ncp, LANES)),
                    _full((256, 256))],
        out_specs=[pl.BlockSpec((1, ncp, LANES), lambda b: (b, 0, 0)),
                   pl.BlockSpec((1, LANES, ncp), lambda b: (b, 0, 0))],
        out_shape=[sds((B, ncp, LANES), BF16), sds((B, LANES, ncp), BF16)],
        compiler_params=pltpu.CompilerParams(dimension_semantics=("arbitrary",),
                                             vmem_limit_bytes=VMEM_LIMIT),
        name="nsa_compress",
    )(kc_raw, vc_raw, kpa, kpb, vpa, vpb,
      kw1a, kw1b, kb1, kw2, vw1a, vw1b, vb1, vw2, knw_c, crc, crsa, crsb, mbd)

    score_bound = (HEAD_DIM * Q_SCALE) * jnp.max(jnp.abs(f32(q_norm_w))) * jnp.max(jnp.abs(f32(k_norm_w)))
    tiles_per_range = TM_OUT // TQ
    n_ranges = S // TM_OUT

    def attention(bounded, r):
        last_pos = (r + 1) * TM_OUT
        tile0 = r * tiles_per_range
        return pl.pallas_call(
            functools.partial(_attn_kernel, ncp=ncp, bounded=bounded, tile0=tile0,
                              ncv=min(ncp, last_pos // CMP_STRIDE), nblk=min(SEL_BLOCK, last_pos // SEL_BLOCK)),
            grid=(B, tiles_per_range),
            in_specs=[pl.BlockSpec((1, 4, LANES, TQ), lambda b, i: (b, 0, 0, i + tile0)),
                      pl.BlockSpec((1, ncp, LANES), lambda b, i: (b, 0, 0)),
                      pl.BlockSpec((1, LANES, ncp), lambda b, i: (b, 0, 0)),
                      pl.BlockSpec((1, S, 2 * LANES), lambda b, i: (b, 0, 0)),
                      pl.BlockSpec((1, S // VCHUNK, LANES, VCHUNK), lambda b, i: (b, 0, 0, 0)),
                      pl.BlockSpec((1, S, LANES), lambda b, i: (b, 0, 0)),
                      pl.BlockSpec((1, S // VCHUNK, LANES, VCHUNK), lambda b, i: (b, 0, 0, 0)),
                      pl.BlockSpec((1, TQ, LANES), lambda b, i: (b, i + tile0, 0)),
                      pl.BlockSpec((1, TQ, D_NSA), lambda b, i: (b, i + tile0, 0)),
                      _full((SEL_BLOCK, ncp))],
            out_specs=pl.BlockSpec((1, TQ, D_NSA), lambda b, i: (b, i, 0)),
            out_shape=sds((B, TM_OUT, D_NSA), BF16),
            scratch_shapes=[pltpu.VMEM((KV_GROUPS, LANES, TQ), BF16),
                            pltpu.VMEM((2, KV_GROUPS, TK, GROUP_HEADS * TQ), F32),
                            pltpu.VMEM((2, KV_GROUPS, 1, GROUP_HEADS * TQ), F32)],
            compiler_params=cp,
            name=("nsa_attn_bounded_r%d" if bounded else "nsa_attn_online_r%d") % r,
        )

    attn_args = (qt4, kc, vct, ksf, vst, kw, vwt, gates, zs, _overlap_t(ncp))
    all_ranges = lambda bounded: (lambda *a: tuple(attention(bounded, r)(*a) for r in range(n_ranges)))
    o_nsa = lax.cond(score_bound <= MAX_SAFE_SCORE, all_ranges(True), all_ranges(False), *attn_args)

    out_blk = lambda n: pl.BlockSpec((1, TM_OUT, n), lambda b, s: (b, s, 0))
    on_blk = lambda k: pl.BlockSpec((1, TM_OUT, D_NSA),
                                    lambda b, s: (jnp.where(s >= k, b, jnp.maximum(b - 1, 0)), 0, 0))
    out = pl.pallas_call(
        _out_kernel,
        grid=(B, S // TM_OUT),
        in_specs=[out_blk(D)] + [on_blk(k) for k in range(n_ranges)]
                 + [out_blk(D_CONV), _full((D_NSA + D_CONV, D))],
        out_specs=out_blk(D),
        out_shape=sds((B, S, D), x.dtype),
        compiler_params=cp,
        name="nsa_out",
    )(x, *o_nsa, oconv, f32(w_out[0]).astype(BF16))
    return out
```

```python
import functools

import numpy as np
import jax
import jax.numpy as jnp
from jax import lax
from jax.experimental import pallas as pl
from jax.experimental.pallas import tpu as pltpu

LANES = 128
HEAD_DIM = 64
NSA_HEADS = 8
KV_GROUPS = 2
GROUP_HEADS = NSA_HEADS // KV_GROUPS
D_NSA = NSA_HEADS * HEAD_DIM
D_CONV = 512
N_BRANCH = 3
ROT_DIM = HEAD_DIM // 4
ROPE_THETA = 500000.0
CMP_LEN = 32
CMP_STRIDE = 16
CMP_HIDDEN = 2 * HEAD_DIM
SEL_BLOCK = 64
SEL_TOPK = 16
WINDOW = 512
EPS = 1e-6
NEG_INF = -1e30
N_FORCED = 3
TAKEN = -2.0
SCALE = HEAD_DIM ** -0.5
Q_SCALE = SCALE * float(np.log2(np.e))

TM = 512
TM_OUT = 1024
TQ = 256
TK = 512
VCHUNK = 128
VMEM_LIMIT = 56 * 1024 * 1024
MAX_SAFE_SCORE = 50.0

BF16 = jnp.bfloat16
F32 = jnp.float32


def _dot(a, b):
    return jnp.dot(a, b, preferred_element_type=F32)


def _split3(a):
    hi = a.astype(BF16)
    r1 = a - hi.astype(F32)
    mid = r1.astype(BF16)
    lo = (r1 - mid.astype(F32)).astype(BF16)
    return hi, mid, lo


def _group_mean(sq, mbd):
    return _dot(sq.astype(BF16), mbd)


def _rope(xn, c, sa, sb):
    return xn * c + pltpu.roll(xn, 8, 1) * sa + pltpu.roll(xn, LANES - 8, 1) * sb


def _silu(z):
    return z * (1.0 / (1.0 + jnp.exp(-z)))


def _proj_kernel(x_ref, nw_ref, wq_ref, wkv_ref, wr_ref, qnw_ref, knw_ref,
                 rc_ref, rsa_ref, rsb_ref, mbd_ref, cw_ref, cb_ref,
                 qt_ref, kc_ref, vc_ref, ksf_ref, vst_ref, kw_ref, vwt_ref,
                 gate_ref, zs_ref, oconv_ref, ubuf):
    si = pl.program_id(1)

    @pl.when(si == 0)
    def _():
        ubuf[0:8, :] = jnp.zeros((8, D_CONV), F32)

    x = x_ref[0]
    ms = jnp.mean(x * x, axis=-1, keepdims=True)
    h = (x * lax.rsqrt(ms + EPS) * nw_ref[...]).astype(BF16)

    rc, rsa, rsb = rc_ref[...], rsa_ref[...], rsb_ref[...]
    mbd = mbd_ref[...]

    pq = _dot(h, wq_ref[...])
    for pair in range(2):
        blk = pq[:, pair * 256:(pair + 1) * 256]
        msq = _group_mean(blk * blk, mbd)
        qn = blk * lax.rsqrt(msq + EPS) * qnw_ref[...]
        for half in range(2):
            t = qn[:, half * LANES:(half + 1) * LANES]
            qt_ref[0, pair * 2 + half] = (_rope(t, rc, rsa, rsb) * Q_SCALE).T.astype(BF16)

    pkv = _dot(h, wkv_ref[...])
    kc_ref[0] = pkv[:, 0:128]
    vc_ref[0] = pkv[:, 128:256]
    ksw = jnp.concatenate([pkv[:, 256:384], pkv[:, 512:640]], axis=1)
    msk = _group_mean(ksw * ksw, mbd)
    kn = ksw * lax.rsqrt(msk + EPS) * knw_ref[...]
    ks = _rope(kn[:, 0:128], rc, rsa, rsb)
    kw = _rope(kn[:, 128:256], rc, rsa, rsb)
    row = lax.broadcasted_iota(jnp.int32, (TM, LANES), 0) + si * TM
    lane = lax.broadcasted_iota(jnp.int32, (TM, LANES), 1)
    onehot = jnp.where(lane == (row >> 6), 1.0, 0.0)
    ksf_ref[0, :, 0:128] = ks.astype(BF16)
    ksf_ref[0, :, 128:256] = onehot.astype(BF16)
    kw_ref[0] = kw.astype(BF16)
    vs = pkv[:, 384:512]
    vw = pkv[:, 640:768]
    for j in range(TM // VCHUNK):
        vst_ref[0, j] = vs[j * VCHUNK:(j + 1) * VCHUNK, :].T.astype(BF16)
        vwt_ref[0, j] = vw[j * VCHUNK:(j + 1) * VCHUNK, :].T.astype(BF16)

    gate_ref[0] = 1.0 / (1.0 + jnp.exp(-pkv[:, 768:896]))
    zs_ref[0] = _silu(_dot(h, wr_ref[:, 0:D_NSA])).astype(BF16)

    pcv = _dot(h, wr_ref[:, D_NSA:D_NSA + 4 * D_CONV])
    u = pcv[:, 1024:1536] * pcv[:, 0:512]
    ubuf[8:8 + TM, :] = u
    u1 = ubuf[7:7 + TM, :]
    u2 = ubuf[6:6 + TM, :]
    conv = cw_ref[0:1, :] * u2 + cw_ref[1:2, :] * u1 + cw_ref[2:3, :] * u + cb_ref[...]
    oconv_ref[0] = (pcv[:, 512:1024] * conv * _silu(pcv[:, 1536:2048])).astype(BF16)
    ubuf[0:8, :] = ubuf[TM:TM + 8, :]


def _cmp_kernel(hk_ref, hv_ref, pka_ref, pkb_ref, pva_ref, pvb_ref,
                wk1a_ref, wk1b_ref, bk1_ref, wk2_ref,
                wv1a_ref, wv1b_ref, bv1_ref, wv2_ref,
                knw_ref, rc_ref, rsa_ref, rsb_ref, mbd_ref,
                kc_ref, vct_ref, *, ncp):
    def first_layer(h_ref, p_ref, w_ref):
        acc = None
        for l in range(0, CMP_STRIDE, 2):
            lhs = jnp.concatenate([h_ref[0, pl.ds(l, ncp, stride=CMP_STRIDE), :],
                                   h_ref[0, pl.ds(l + 1, ncp, stride=CMP_STRIDE), :]], axis=1)
            lhs = (lhs + p_ref[:, l * LANES:(l + 2) * LANES]).astype(BF16)
            part = _dot(lhs, w_ref[l * LANES:(l + 2) * LANES, :])
            acc = part if acc is None else acc + part
        return acc

    def mlp(h_ref, pa_ref, pb_ref, w1a_ref, w1b_ref, b1_ref, w2_ref):
        p = first_layer(h_ref, pa_ref, w1a_ref)
        q = first_layer(h_ref, pb_ref, w1b_ref)
        pre = p + pltpu.roll(q, ncp - 1, 0) + b1_ref[...]
        return _dot(_silu(pre).astype(BF16), w2_ref[...])

    kc = mlp(hk_ref, pka_ref, pkb_ref, wk1a_ref, wk1b_ref, bk1_ref, wk2_ref)
    msk = _group_mean(kc * kc, mbd_ref[0:LANES, 0:LANES])
    kn = kc * lax.rsqrt(msk + EPS) * knw_ref[...]
    kc_ref[0] = _rope(kn, rc_ref[...], rsa_ref[...], rsb_ref[...]).astype(BF16)
    vc = mlp(hv_ref, pva_ref, pvb_ref, wv1a_ref, wv1b_ref, bv1_ref, wv2_ref)
    vct_ref[0] = vc.T.astype(BF16)


def _attn_kernel(qt_ref, kc_ref, vct_ref, ksf_ref, vst_ref, kw_ref, vwt_ref,
                 gate_ref, zs_ref, ovt_ref, o_ref, pen_sc, s_sc, cm_sc, *, ncp, bounded, tile0, ncv, nblk):
    i = pl.program_id(1) + tile0
    t0 = i * TQ
    rows = GROUP_HEADS * TQ
    groups = range(KV_GROUPS)

    q_all = jnp.concatenate([qt_ref[0, j] for j in range(GROUP_HEADS)], axis=1)
    frow = lax.broadcasted_iota(jnp.int32, (LANES, rows), 0)
    qg = [jnp.where(frow < HEAD_DIM, q_all, 0), jnp.where(frow >= HEAD_DIM, q_all, 0)]
    t_q = t0 + lax.broadcasted_iota(jnp.int32, (1, TQ), 1)

    def fill_where(a, mask, fill):
        return jnp.concatenate([jnp.where(mask, a[:, r * TQ:(r + 1) * TQ], fill)
                                for r in range(GROUP_HEADS)], axis=1)

    vrows = lambda v, g: v[g * HEAD_DIM:(g + 1) * HEAD_DIM, :]

    s_cmp = [_dot(kc_ref[0, 0:ncv, :], qg[g]) for g in groups]

    c_idx = lax.broadcasted_iota(jnp.int32, (ncv, 1), 0)
    mask_c = ((c_idx * CMP_STRIDE + (CMP_LEN - 1)) <= t_q) & (c_idx < ncp - 1)
    p_c, o_c = [], []
    for g in groups:
        if bounded:
            e_c = fill_where(jnp.exp2(s_cmp[g]), mask_c, 0.0)
        else:
            sm = fill_where(s_cmp[g], mask_c, NEG_INF)
            e_c = fill_where(jnp.exp2(sm - jnp.max(sm, axis=0, keepdims=True)), mask_c, 0.0)
        l_c = jnp.sum(e_c, axis=0, keepdims=True)
        p = e_c * (1.0 / jnp.where(l_c > 0.0, l_c, 1.0))
        p_c.append(p)
        o_c.append(_dot(vrows(vct_ref[0], g)[:, 0:ncv], p.astype(BF16)))

    if nblk <= SEL_TOPK:
        pen_sc[...] = jnp.zeros(pen_sc.shape, BF16)
    else:
        n_idx = lax.broadcasted_iota(jnp.int32, (nblk, TQ), 0)
        tq = t0 + lax.broadcasted_iota(jnp.int32, (nblk, TQ), 1)
        cur = tq >> 6
        forced = (n_idx == 0) | (n_idx == cur) | (n_idx == cur - 1)
        visible = (n_idx << 6) <= tq
        ovt = ovt_ref[0:nblk, 0:ncv]
        left = []
        for g in groups:
            psum = p_c[g][:, 0:TQ]
            for r in range(1, GROUP_HEADS):
                psum = psum + p_c[g][:, r * TQ:(r + 1) * TQ]
            hi, mid, lo = _split3(psum)
            imp = _dot(ovt, hi) + _dot(ovt, mid) + _dot(ovt, lo)
            left.append(jnp.where(forced, TAKEN, jnp.where(visible, imp, -1.0)))
        for _ in range(SEL_TOPK - N_FORCED):
            for g in groups:
                top = jnp.max(left[g], axis=0, keepdims=True)
                first = jnp.min(jnp.where(left[g] == top, n_idx, nblk), axis=0, keepdims=True)
                left[g] = jnp.where(n_idx == first, TAKEN, left[g])
        for g in groups:
            pen_sc[g, 0:nblk, :] = jnp.where(left[g] == TAKEN, 0.0, NEG_INF).astype(BF16)
            pen_sc[g, nblk:LANES, :] = jnp.zeros((LANES - nblk, TQ), BF16)

    nwc = WINDOW // VCHUNK + TQ // VCHUNK
    cw = jnp.maximum(i * (TQ // VCHUNK) - WINDOW // VCHUNK, 0)
    w0 = pl.multiple_of(cw * VCHUNK, VCHUNK)
    kwin = kw_ref[0, pl.ds(w0, nwc * VCHUNK), :]
    s_win = [_dot(kwin, qg[g]) for g in groups]

    q_aug = [jnp.concatenate([qg[g], jnp.concatenate([pen_sc[g]] * GROUP_HEADS, axis=1)], axis=0)
             for g in groups]

    def window_branch():
        vwin = jnp.concatenate([vwt_ref[0, cw + j] for j in range(nwc)], axis=1)
        kpos = w0 + lax.broadcasted_iota(jnp.int32, (nwc * VCHUNK, 1), 0)
        mask_w = (kpos <= t_q) & (kpos > t_q - WINDOW)
        o_w, l_w = [], []
        for g in groups:
            if bounded:
                p_w = fill_where(jnp.exp2(s_win[g]), mask_w, 0.0)
            else:
                sw = fill_where(s_win[g], mask_w, NEG_INF)
                p_w = jnp.exp2(sw - jnp.max(sw, axis=0, keepdims=True))
            v_ext = jnp.concatenate([vrows(vwin, g), jnp.ones((16, nwc * VCHUNK), BF16)], axis=0)
            ol = _dot(v_ext, p_w.astype(BF16))
            o_w.append(ol[0:HEAD_DIM, :])
            l_w.append(ol[HEAD_DIM:HEAD_DIM + 1, :])
        return o_w, l_w

    def v_chunk(c, g):
        cv = c * (TK // VCHUNK)
        return jnp.concatenate([vrows(vst_ref[0, cv + j], g) for j in range(TK // VCHUNK)], axis=1)

    def qk(c, slot, g):
        k0 = pl.multiple_of(c * TK, TK)
        s = _dot(ksf_ref[0, pl.ds(k0, TK), :], q_aug[g])
        s_sc[slot, g] = s
        if not bounded:
            cm_sc[slot, g] = jnp.max(s, axis=0, keepdims=True)

    def sm_pv(c, slot, g, state, masked):
        m_p, l_p, acc = state
        s = s_sc[slot, g]
        if masked:
            kpos = c * TK + lax.broadcasted_iota(jnp.int32, (TK, 1), 0)
            s = fill_where(s, kpos <= t_q, NEG_INF)
        if bounded:
            p = jnp.exp2(s)
            return m_p, l_p + jnp.sum(p, axis=0, keepdims=True), acc + _dot(v_chunk(c, g), p.astype(BF16))
        cm = jnp.max(s, axis=0, keepdims=True) if masked else cm_sc[slot, g]
        m_n = jnp.maximum(m_p, cm)
        alpha = jnp.exp2(m_p - m_n)
        p = jnp.exp2(s - m_n)
        l_n = alpha * l_p + jnp.sum(p, axis=0, keepdims=True)
        acc = alpha * acc + _dot(v_chunk(c, g), p.astype(BF16))
        return m_n, l_n, acc

    def step(c, slot, carry):
        out = []
        for g in groups:
            qk(c + 1, 1 - slot, g)
            out.append(sm_pv(c, slot, g, carry[g], False))
        return tuple(out)

    def last(slot, carry):
        return tuple(sm_pv(n_full, slot, g, carry[g], True) for g in groups)

    n_full = tile0 * TQ // TK
    for g in groups:
        qk(0, 0, g)
    o_w, l_w = window_branch()
    carry = tuple((jnp.full((1, rows), 0.0 if bounded else NEG_INF, F32), jnp.zeros((1, rows), F32),
                   jnp.zeros((HEAD_DIM, rows), F32)) for _ in groups)
    for c in range(n_full):
        carry = step(c, c % 2, carry)
    carry = last(n_full % 2, carry)
    sel_out = [(carry[g][1], carry[g][2]) for g in groups]

    g_t = gate_ref[0].T
    slabs = []
    for g in groups:
        def gate_row(br):
            base = g * (N_BRANCH * GROUP_HEADS) + br * GROUP_HEADS
            return jnp.concatenate([g_t[base + r:base + r + 1, :] for r in range(GROUP_HEADS)], axis=1)
        l_s, o_s = sel_out[g]
        og = o_c[g] * gate_row(0) + o_s * (gate_row(1) / l_s) + o_w[g] * (gate_row(2) / l_w[g])
        slabs += [og[:, r * TQ:(r + 1) * TQ] for r in range(GROUP_HEADS)]
    out_t = jnp.concatenate(slabs, axis=0)
    o_ref[0] = (out_t.T * zs_ref[0].astype(F32)).astype(BF16)


def _out_kernel(x_ref, *refs):
    on_refs, (oc_ref, w_ref, o_ref) = refs[:-3], refs[-3:]
    per_tile = TM_OUT // TK
    for k in range(len(on_refs) // per_tile):
        @pl.when(pl.program_id(1) == k)
        def _(k=k):
            on = jnp.concatenate([r[0] for r in on_refs[k * per_tile:(k + 1) * per_tile]], axis=0)
            acc = _dot(on, w_ref[0:D_NSA, :]) + _dot(oc_ref[0], w_ref[D_NSA:D_NSA + D_CONV, :])
            o_ref[0] = x_ref[0] + acc


def _rope_tables(pos):
    inv_freq = (np.float32(ROPE_THETA) ** (-np.arange(0, ROT_DIM, 2, dtype=np.float32) / ROT_DIM)).astype(np.float32)
    ang = pos.astype(np.float32)[:, None] * inv_freq[None, :]
    cos, sin = np.cos(ang), np.sin(ang)
    n = pos.shape[0]
    ones = np.ones((n, HEAD_DIM - ROT_DIM), np.float32)
    zeros = np.zeros((n, HEAD_DIM - ROT_DIM), np.float32)
    z8 = np.zeros((n, ROT_DIM // 2), np.float32)
    c = np.concatenate([cos, cos, ones], axis=1)
    sa = np.concatenate([z8, sin, zeros], axis=1)
    sb = np.concatenate([-sin, z8, zeros], axis=1)
    return tuple(jnp.asarray(np.concatenate([t, t], axis=1), F32) for t in (c, sa, sb))


def _overlap_t(ncp):
    cs = np.arange(ncp) * CMP_STRIDE
    ce = cs + CMP_LEN
    ss = np.arange(SEL_BLOCK) * SEL_BLOCK
    se = ss + SEL_BLOCK
    ov = np.clip(np.minimum(ce[None, :], se[:, None]) - np.maximum(cs[None, :], ss[:, None]), 0, None)
    ov = ov.astype(np.float32) / CMP_LEN
    ov[:, ncp - 1] = 0.0
    return jnp.asarray(ov, BF16)


def _cmp_weights(pos, w1, b1, w2):
    half = CMP_STRIDE
    def big_w1(w):
        w = w.reshape(half, 1, HEAD_DIM, 1, CMP_HIDDEN)
        eye = jnp.eye(KV_GROUPS, dtype=F32).reshape(1, KV_GROUPS, 1, KV_GROUPS, 1)
        return (w * eye).reshape(half * KV_GROUPS * HEAD_DIM, KV_GROUPS * CMP_HIDDEN).astype(BF16)
    def big_pos(p):
        return jnp.broadcast_to(p[:, None, :], (half, KV_GROUPS, HEAD_DIM)).reshape(1, -1)
    w1a, w1b = w1[:half * HEAD_DIM], w1[half * HEAD_DIM:]
    eye2 = jnp.eye(KV_GROUPS, dtype=F32)
    w2b = (w2[None, :, None, :] * eye2[:, None, :, None]).reshape(KV_GROUPS * CMP_HIDDEN,
                                                                 KV_GROUPS * HEAD_DIM).astype(BF16)
    b1b = jnp.tile(b1, KV_GROUPS)[None, :]
    return big_pos(pos[:half]), big_pos(pos[half:]), big_w1(w1a), big_w1(w1b), b1b, w2b


def _full(shape):
    nd = len(shape)
    return pl.BlockSpec(shape, lambda *_: (0,) * nd)


def kernel(x, norm_w, w_in, q_norm_w, k_norm_w, cmp_k_pos, cmp_k_w1, cmp_k_b1, cmp_k_w2,
           cmp_v_pos, cmp_v_w1, cmp_v_b1, cmp_v_w2, conv_w, conv_b, w_out):
    B, S, D = x.shape
    assert norm_w.shape[0] == 1, "single layer"
    assert S % TM == 0 and S % TM_OUT == 0 and TK % TQ == 0 and TM_OUT % TK == 0 and S // SEL_BLOCK <= SEL_BLOCK and S >= WINDOW + TQ
    ncp = S // CMP_STRIDE
    nst = S // TM
    f32 = lambda a: a.astype(F32)

    w = f32(w_in[0])
    head_order = []
    for j in range(GROUP_HEADS):
        head_order += [j, GROUP_HEADS + j]
    wq = w[:, :D_NSA].reshape(D, NSA_HEADS, HEAD_DIM)[:, jnp.array(head_order), :].reshape(D, D_NSA)
    gate_cols = []
    for g in range(KV_GROUPS):
        for br in range(N_BRANCH):
            for r in range(GROUP_HEADS):
                gate_cols.append((g * GROUP_HEADS + r) * N_BRANCH + br)
    g0 = D_NSA + 6 * LANES
    wg = w[:, g0:g0 + NSA_HEADS * N_BRANCH][:, jnp.array(gate_cols)]
    wg = jnp.pad(wg, ((0, 0), (0, LANES - NSA_HEADS * N_BRANCH)))
    rest0 = g0 + NSA_HEADS * N_BRANCH
    wq, wkv, wr = (t.astype(BF16) for t in (wq, jnp.concatenate([w[:, D_NSA:g0], wg], axis=1), w[:, rest0:]))
    n_rest = wr.shape[1]

    rc, rsa, rsb = _rope_tables(np.arange(S))
    mbd = jnp.asarray(np.kron(np.eye(4), np.full((HEAD_DIM, HEAD_DIM), 1.0 / HEAD_DIM)), BF16)
    qnw = jnp.tile(f32(q_norm_w[0]), 4)[None, :]
    knw_sw = jnp.concatenate([jnp.tile(f32(k_norm_w[0, 1]), 2), jnp.tile(f32(k_norm_w[0, 2]), 2)])[None, :]
    knw_c = jnp.tile(f32(k_norm_w[0, 0]), 2)[None, :]

    cp = pltpu.CompilerParams(dimension_semantics=("arbitrary", "arbitrary"),
                              vmem_limit_bytes=VMEM_LIMIT)
    row_blk = lambda n: pl.BlockSpec((1, TM, n), lambda b, s: (b, s, 0))
    tab_blk = pl.BlockSpec((TM, LANES), lambda b, s: (s, 0))
    vt_blk = pl.BlockSpec((1, TM // VCHUNK, LANES, VCHUNK), lambda b, s: (b, s, 0, 0))
    sds = jax.ShapeDtypeStruct

    (qt4, kc_raw, vc_raw, ksf, vst, kw, vwt, gates, zs, oconv) = pl.pallas_call(
        _proj_kernel,
        grid=(B, nst),
        in_specs=[row_blk(D), _full((1, D)), _full((D, D_NSA)), _full((D, 7 * LANES)),
                  _full((D, n_rest)), _full((1, 256)), _full((1, 256)),
                  tab_blk, tab_blk, tab_blk, _full((256, 256)), _full((3, D_CONV)), _full((1, D_CONV))],
        out_specs=[pl.BlockSpec((1, 4, LANES, TM), lambda b, s: (b, 0, 0, s)),
                   row_blk(LANES), row_blk(LANES), row_blk(2 * LANES), vt_blk, row_blk(LANES), vt_blk,
                   row_blk(LANES), row_blk(D_NSA), row_blk(D_CONV)],
        out_shape=[sds((B, 4, LANES, S), BF16), sds((B, S, LANES), F32), sds((B, S, LANES), F32),
                   sds((B, S, 2 * LANES), BF16), sds((B, S // VCHUNK, LANES, VCHUNK), BF16),
                   sds((B, S, LANES), BF16), sds((B, S // VCHUNK, LANES, VCHUNK), BF16),
                   sds((B, S, LANES), F32), sds((B, S, D_NSA), BF16), sds((B, S, D_CONV), BF16)],
        scratch_shapes=[pltpu.VMEM((TM + 16, D_CONV), F32)],
        compiler_params=cp,
        name="nsa_proj",
    )(x, f32(norm_w), wq, wkv, wr, qnw, knw_sw, rc, rsa, rsb, mbd, f32(conv_w[0]), f32(conv_b))

    cmp_pos = np.arange(ncp) * CMP_STRIDE + (CMP_LEN - 1)
    crc, crsa, crsb = _rope_tables(cmp_pos)
    kpa, kpb, kw1a, kw1b, kb1, kw2 = _cmp_weights(f32(cmp_k_pos[0]), f32(cmp_k_w1[0]),
                                                  f32(cmp_k_b1[0]), f32(cmp_k_w2[0]))
    vpa, vpb, vw1a, vw1b, vb1, vw2 = _cmp_weights(f32(cmp_v_pos[0]), f32(cmp_v_w1[0]),
                                                  f32(cmp_v_b1[0]), f32(cmp_v_w2[0]))
    flat = CMP_STRIDE * LANES
    h_blk = pl.BlockSpec((1, S, LANES), lambda b: (b, 0, 0))
    kc, vct = pl.pallas_call(
        functools.partial(_cmp_kernel, ncp=ncp),
        grid=(B,),
        in_specs=[h_blk, h_blk] + [_full((1, flat))] * 4
                 + [_full((flat, 256)), _full((flat, 256)), _full((1, 256)), _full((256, LANES))] * 2
                 + [_full((1, LANES)), _full((ncp, LANES)), _full((ncp, LANES)), _full((ncp, LANES)),
                    _full((256, 256))],
        out_specs=[pl.BlockSpec((1, ncp, LANES), lambda b: (b, 0, 0)),
                   pl.BlockSpec((1, LANES, ncp), lambda b: (b, 0, 0))],
        out_shape=[sds((B, ncp, LANES), BF16), sds((B, LANES, ncp), BF16)],
        compiler_params=pltpu.CompilerParams(dimension_semantics=("arbitrary",),
                                             vmem_limit_bytes=VMEM_LIMIT),
        name="nsa_compress",
    )(kc_raw, vc_raw, kpa, kpb, vpa, vpb,
      kw1a, kw1b, kb1, kw2, vw1a, vw1b, vb1, vw2, knw_c, crc, crsa, crsb, mbd)

    score_bound = (HEAD_DIM * Q_SCALE) * jnp.max(jnp.abs(f32(q_norm_w))) * jnp.max(jnp.abs(f32(k_norm_w)))
    tiles_per_range = TK // TQ
    n_ranges = S // TK

    def attention(bounded, r):
        last_pos = (r + 1) * TK
        tile0 = r * tiles_per_range
        return pl.pallas_call(
            functools.partial(_attn_kernel, ncp=ncp, bounded=bounded, tile0=tile0,
                              ncv=min(ncp, last_pos // CMP_STRIDE), nblk=min(SEL_BLOCK, last_pos // SEL_BLOCK)),
            grid=(B, tiles_per_range),
            in_specs=[pl.BlockSpec((1, 4, LANES, TQ), lambda b, i: (b, 0, 0, i + tile0)),
                      pl.BlockSpec((1, ncp, LANES), lambda b, i: (b, 0, 0)),
                      pl.BlockSpec((1, LANES, ncp), lambda b, i: (b, 0, 0)),
                      pl.BlockSpec((1, S, 2 * LANES), lambda b, i: (b, 0, 0)),
                      pl.BlockSpec((1, S // VCHUNK, LANES, VCHUNK), lambda b, i: (b, 0, 0, 0)),
                      pl.BlockSpec((1, S, LANES), lambda b, i: (b, 0, 0)),
                      pl.BlockSpec((1, S // VCHUNK, LANES, VCHUNK), lambda b, i: (b, 0, 0, 0)),
                      pl.BlockSpec((1, TQ, LANES), lambda b, i: (b, i + tile0, 0)),
                      pl.BlockSpec((1, TQ, D_NSA), lambda b, i: (b, i + tile0, 0)),
                      _full((SEL_BLOCK, ncp))],
            out_specs=pl.BlockSpec((1, TQ, D_NSA), lambda b, i: (b, i, 0)),
            out_shape=sds((B, TK, D_NSA), BF16),
            scratch_shapes=[pltpu.VMEM((KV_GROUPS, LANES, TQ), BF16),
                            pltpu.VMEM((2, KV_GROUPS, TK, GROUP_HEADS * TQ), F32),
                            pltpu.VMEM((2, KV_GROUPS, 1, GROUP_HEADS * TQ), F32)],
            compiler_params=cp,
            name=("nsa_attn_bounded_r%d" if bounded else "nsa_attn_online_r%d") % r,
        )

    attn_args = (qt4, kc, vct, ksf, vst, kw, vwt, gates, zs, _overlap_t(ncp))
    all_ranges = lambda bounded: (lambda *a: tuple(attention(bounded, r)(*a) for r in range(n_ranges)))
    o_nsa = lax.cond(score_bound <= MAX_SAFE_SCORE, all_ranges(True), all_ranges(False), *attn_args)

    out_blk = lambda n: pl.BlockSpec((1, TM_OUT, n), lambda b, s: (b, s, 0))
    on_blk = lambda k: pl.BlockSpec((1, TK, D_NSA), lambda b, s: (
        jnp.where(s >= k // (TM_OUT // TK), b, jnp.maximum(b - 1, 0)), 0, 0))
    out = pl.pallas_call(
        _out_kernel,
        grid=(B, S // TM_OUT),
        in_specs=[out_blk(D)] + [on_blk(k) for k in range(n_ranges)]
                 + [out_blk(D_CONV), _full((D_NSA + D_CONV, D))],
        out_specs=out_blk(D),
        out_shape=sds((B, S, D), x.dtype),
        compiler_params=cp,
        name="nsa_out",
    )(x, *o_nsa, oconv, f32(w_out[0]).astype(BF16))
    return out
```

```python
import functools

import numpy as np
import jax
import jax.numpy as jnp
from jax import lax
from jax.experimental import pallas as pl
from jax.experimental.pallas import tpu as pltpu

LANES = 128
HEAD_DIM = 64
NSA_HEADS = 8
KV_GROUPS = 2
GROUP_HEADS = NSA_HEADS // KV_GROUPS
D_NSA = NSA_HEADS * HEAD_DIM
D_CONV = 512
N_BRANCH = 3
ROT_DIM = HEAD_DIM // 4
ROPE_THETA = 500000.0
CMP_LEN = 32
CMP_STRIDE = 16
CMP_HIDDEN = 2 * HEAD_DIM
SEL_BLOCK = 64
SEL_TOPK = 16
WINDOW = 512
EPS = 1e-6
NEG_INF = -1e30
N_FORCED = 3
TAKEN = -2.0
SCALE = HEAD_DIM ** -0.5
Q_SCALE = SCALE * float(np.log2(np.e))

TM = 512
TM_OUT = 1024
TQ = 256
TK = 512
VCHUNK = 128
VMEM_LIMIT = 56 * 1024 * 1024
MAX_SAFE_SCORE = 50.0

BF16 = jnp.bfloat16
F32 = jnp.float32


def _dot(a, b):
    return jnp.dot(a, b, preferred_element_type=F32)


def _split3(a):
    hi = a.astype(BF16)
    r1 = a - hi.astype(F32)
    mid = r1.astype(BF16)
    lo = (r1 - mid.astype(F32)).astype(BF16)
    return hi, mid, lo


def _group_mean(sq, mbd):
    return _dot(sq.astype(BF16), mbd)


def _rope(xn, c, sa, sb):
    return xn * c + pltpu.roll(xn, 8, 1) * sa + pltpu.roll(xn, LANES - 8, 1) * sb


def _silu(z):
    return z * (1.0 / (1.0 + jnp.exp(-z)))


def _proj_kernel(x_ref, nw_ref, wq_ref, wkv_ref, wr_ref, qnw_ref, knw_ref,
                 rc_ref, rsa_ref, rsb_ref, mbd_ref, cw_ref, cb_ref,
                 qt_ref, kc_ref, vc_ref, ksf_ref, vst_ref, kw_ref, vwt_ref,
                 gate_ref, zs_ref, oconv_ref, ubuf):
    si = pl.program_id(1)

    @pl.when(si == 0)
    def _():
        ubuf[0:8, :] = jnp.zeros((8, D_CONV), F32)

    x = x_ref[0]
    ms = jnp.mean(x * x, axis=-1, keepdims=True)
    h = (x * lax.rsqrt(ms + EPS) * nw_ref[...]).astype(BF16)

    rc, rsa, rsb = rc_ref[...], rsa_ref[...], rsb_ref[...]
    mbd = mbd_ref[...]

    pq = _dot(h, wq_ref[...])
    for pair in range(2):
        blk = pq[:, pair * 256:(pair + 1) * 256]
        msq = _group_mean(blk * blk, mbd)
        qn = blk * lax.rsqrt(msq + EPS) * qnw_ref[...]
        for half in range(2):
            t = qn[:, half * LANES:(half + 1) * LANES]
            qt_ref[0, pair * 2 + half] = (_rope(t, rc, rsa, rsb) * Q_SCALE).T.astype(BF16)

    pkv = _dot(h, wkv_ref[...])
    kc_ref[0] = pkv[:, 0:128]
    vc_ref[0] = pkv[:, 128:256]
    ksw = jnp.concatenate([pkv[:, 256:384], pkv[:, 512:640]], axis=1)
    msk = _group_mean(ksw * ksw, mbd)
    kn = ksw * lax.rsqrt(msk + EPS) * knw_ref[...]
    ks = _rope(kn[:, 0:128], rc, rsa, rsb)
    kw = _rope(kn[:, 128:256], rc, rsa, rsb)
    row = lax.broadcasted_iota(jnp.int32, (TM, LANES), 0) + si * TM
    lane = lax.broadcasted_iota(jnp.int32, (TM, LANES), 1)
    onehot = jnp.where(lane == (row >> 6), 1.0, 0.0)
    ksf_ref[0, :, 0:128] = ks.astype(BF16)
    ksf_ref[0, :, 128:256] = onehot.astype(BF16)
    kw_ref[0] = kw.astype(BF16)
    vs = pkv[:, 384:512]
    vw = pkv[:, 640:768]
    for j in range(TM // VCHUNK):
        vst_ref[0, j] = vs[j * VCHUNK:(j + 1) * VCHUNK, :].T.astype(BF16)
        vwt_ref[0, j] = vw[j * VCHUNK:(j + 1) * VCHUNK, :].T.astype(BF16)

    gate_ref[0] = 1.0 / (1.0 + jnp.exp(-pkv[:, 768:896]))
    zs_ref[0] = _silu(_dot(h, wr_ref[:, 0:D_NSA])).astype(BF16)

    pcv = _dot(h, wr_ref[:, D_NSA:D_NSA + 4 * D_CONV])
    u = pcv[:, 1024:1536] * pcv[:, 0:512]
    ubuf[8:8 + TM, :] = u
    u1 = ubuf[7:7 + TM, :]
    u2 = ubuf[6:6 + TM, :]
    conv = cw_ref[0:1, :] * u2 + cw_ref[1:2, :] * u1 + cw_ref[2:3, :] * u + cb_ref[...]
    oconv_ref[0] = (pcv[:, 512:1024] * conv * _silu(pcv[:, 1536:2048])).astype(BF16)
    ubuf[0:8, :] = ubuf[TM:TM + 8, :]


def _cmp_kernel(hk_ref, hv_ref, pka_ref, pkb_ref, pva_ref, pvb_ref,
                wk1a_ref, wk1b_ref, bk1_ref, wk2_ref,
                wv1a_ref, wv1b_ref, bv1_ref, wv2_ref,
                knw_ref, rc_ref, rsa_ref, rsb_ref, mbd_ref,
                kc_ref, vct_ref, *, ncp):
    def first_layer(h_ref, p_ref, w_ref):
        acc = None
        for l in range(0, CMP_STRIDE, 2):
            lhs = jnp.concatenate([h_ref[0, pl.ds(l, ncp, stride=CMP_STRIDE), :],
                                   h_ref[0, pl.ds(l + 1, ncp, stride=CMP_STRIDE), :]], axis=1)
            lhs = (lhs + p_ref[:, l * LANES:(l + 2) * LANES]).astype(BF16)
            part = _dot(lhs, w_ref[l * LANES:(l + 2) * LANES, :])
            acc = part if acc is None else acc + part
        return acc

    def mlp(h_ref, pa_ref, pb_ref, w1a_ref, w1b_ref, b1_ref, w2_ref):
        p = first_layer(h_ref, pa_ref, w1a_ref)
        q = first_layer(h_ref, pb_ref, w1b_ref)
        pre = p + pltpu.roll(q, ncp - 1, 0) + b1_ref[...]
        return _dot(_silu(pre).astype(BF16), w2_ref[...])

    kc = mlp(hk_ref, pka_ref, pkb_ref, wk1a_ref, wk1b_ref, bk1_ref, wk2_ref)
    msk = _group_mean(kc * kc, mbd_ref[0:LANES, 0:LANES])
    kn = kc * lax.rsqrt(msk + EPS) * knw_ref[...]
    kc_ref[0] = _rope(kn, rc_ref[...], rsa_ref[...], rsb_ref[...]).astype(BF16)
    vc = mlp(hv_ref, pva_ref, pvb_ref, wv1a_ref, wv1b_ref, bv1_ref, wv2_ref)
    vct_ref[0] = vc.T.astype(BF16)


def _attn_kernel(qt_ref, kc_ref, vct_ref, ksf_ref, vst_ref, kw_ref, vwt_ref,
                 gate_ref, zs_ref, ovt_ref, o_ref, pen_sc, s_sc, cm_sc, *, ncp, bounded, tile0, ncv, nblk):
    i = pl.program_id(1) + tile0
    t0 = i * TQ
    rows = GROUP_HEADS * TQ
    groups = range(KV_GROUPS)

    q_all = jnp.concatenate([qt_ref[0, j] for j in range(GROUP_HEADS)], axis=1)
    frow = lax.broadcasted_iota(jnp.int32, (LANES, rows), 0)
    qg = [jnp.where(frow < HEAD_DIM, q_all, 0), jnp.where(frow >= HEAD_DIM, q_all, 0)]
    t_q = t0 + lax.broadcasted_iota(jnp.int32, (1, TQ), 1)

    def fill_where(a, mask, fill):
        return jnp.concatenate([jnp.where(mask, a[:, r * TQ:(r + 1) * TQ], fill)
                                for r in range(GROUP_HEADS)], axis=1)

    vrows = lambda v, g: v[g * HEAD_DIM:(g + 1) * HEAD_DIM, :]

    s_cmp = [_dot(kc_ref[0, 0:ncv, :], qg[g]) for g in groups]

    c_idx = lax.broadcasted_iota(jnp.int32, (ncv, 1), 0)
    mask_c = ((c_idx * CMP_STRIDE + (CMP_LEN - 1)) <= t_q) & (c_idx < ncp - 1)
    p_c, o_c = [], []
    for g in groups:
        if bounded:
            e_c = fill_where(jnp.exp2(s_cmp[g]), mask_c, 0.0)
        else:
            sm = fill_where(s_cmp[g], mask_c, NEG_INF)
            e_c = fill_where(jnp.exp2(sm - jnp.max(sm, axis=0, keepdims=True)), mask_c, 0.0)
        l_c = jnp.sum(e_c, axis=0, keepdims=True)
        p = e_c * (1.0 / jnp.where(l_c > 0.0, l_c, 1.0))
        p_c.append(p)
        o_c.append(_dot(vrows(vct_ref[0], g)[:, 0:ncv], p.astype(BF16)))

    if nblk <= SEL_TOPK:
        pen_sc[...] = jnp.zeros(pen_sc.shape, BF16)
    else:
        n_idx = lax.broadcasted_iota(jnp.int32, (nblk, TQ), 0)
        tq = t0 + lax.broadcasted_iota(jnp.int32, (nblk, TQ), 1)
        cur = tq >> 6
        forced = (n_idx == 0) | (n_idx == cur) | (n_idx == cur - 1)
        visible = (n_idx << 6) <= tq
        ovt = ovt_ref[0:nblk, 0:ncv]
        left = []
        for g in groups:
            psum = p_c[g][:, 0:TQ]
            for r in range(1, GROUP_HEADS):
                psum = psum + p_c[g][:, r * TQ:(r + 1) * TQ]
            hi, mid, lo = _split3(psum)
            imp = _dot(ovt, hi) + _dot(ovt, mid) + _dot(ovt, lo)
            left.append(jnp.where(forced, TAKEN, jnp.where(visible, imp, -1.0)))
        for _ in range(SEL_TOPK - N_FORCED):
            for g in groups:
                top = jnp.max(left[g], axis=0, keepdims=True)
                first = jnp.min(jnp.where(left[g] == top, n_idx, nblk), axis=0, keepdims=True)
                left[g] = jnp.where(n_idx == first, TAKEN, left[g])
        for g in groups:
            pen_sc[g, 0:nblk, :] = jnp.where(left[g] == TAKEN, 0.0, NEG_INF).astype(BF16)
            pen_sc[g, nblk:LANES, :] = jnp.zeros((LANES - nblk, TQ), BF16)

    nwc = WINDOW // VCHUNK + TQ // VCHUNK
    cw = jnp.maximum(i * (TQ // VCHUNK) - WINDOW // VCHUNK, 0)
    w0 = pl.multiple_of(cw * VCHUNK, VCHUNK)
    kwin = kw_ref[0, pl.ds(w0, nwc * VCHUNK), :]
    s_win = [_dot(kwin, qg[g]) for g in groups]

    q_aug = [jnp.concatenate([qg[g], jnp.concatenate([pen_sc[g]] * GROUP_HEADS, axis=1)], axis=0)
             for g in groups]

    def window_branch():
        vwin = jnp.concatenate([vwt_ref[0, cw + j] for j in range(nwc)], axis=1)
        kpos = w0 + lax.broadcasted_iota(jnp.int32, (nwc * VCHUNK, 1), 0)

        def masked_w(a, fill):
            if tile0 * TQ < WINDOW:
                return fill_where(a, (kpos <= t_q) & (kpos > t_q - WINDOW), fill)
            lo, hi = slice(0, TQ), slice(WINDOW, WINDOW + TQ)
            return jnp.concatenate([fill_where(a[lo], kpos[lo] > t_q - WINDOW, fill), a[TQ:WINDOW],
                                    fill_where(a[hi], kpos[hi] <= t_q, fill)], axis=0)

        o_w, l_w = [], []
        for g in groups:
            if bounded:
                p_w = masked_w(jnp.exp2(s_win[g]), 0.0)
            else:
                sw = masked_w(s_win[g], NEG_INF)
                p_w = jnp.exp2(sw - jnp.max(sw, axis=0, keepdims=True))
            v_ext = jnp.concatenate([vrows(vwin, g), jnp.ones((16, nwc * VCHUNK), BF16)], axis=0)
            ol = _dot(v_ext, p_w.astype(BF16))
            o_w.append(ol[0:HEAD_DIM, :])
            l_w.append(ol[HEAD_DIM:HEAD_DIM + 1, :])
        return o_w, l_w

    def v_chunk(c, g):
        cv = c * (TK // VCHUNK)
        return jnp.concatenate([vrows(vst_ref[0, cv + j], g) for j in range(TK // VCHUNK)], axis=1)

    def qk(c, slot, g):
        k0 = pl.multiple_of(c * TK, TK)
        s = _dot(ksf_ref[0, pl.ds(k0, TK), :], q_aug[g])
        s_sc[slot, g] = s
        if not bounded:
            cm_sc[slot, g] = jnp.max(s, axis=0, keepdims=True)

    def sm_pv(c, slot, g, state, masked):
        m_p, l_p, acc = state
        s = s_sc[slot, g]
        if masked:
            kpos = c * TK + lax.broadcasted_iota(jnp.int32, (TK, 1), 0)
            s = fill_where(s, kpos <= t_q, NEG_INF)
        if bounded:
            p = jnp.exp2(s)
            return m_p, l_p + jnp.sum(p, axis=0, keepdims=True), acc + _dot(v_chunk(c, g), p.astype(BF16))
        cm = jnp.max(s, axis=0, keepdims=True) if masked else cm_sc[slot, g]
        m_n = jnp.maximum(m_p, cm)
        alpha = jnp.exp2(m_p - m_n)
        p = jnp.exp2(s - m_n)
        l_n = alpha * l_p + jnp.sum(p, axis=0, keepdims=True)
        acc = alpha * acc + _dot(v_chunk(c, g), p.astype(BF16))
        return m_n, l_n, acc

    def step(c, slot, carry):
        out = []
        for g in groups:
            qk(c + 1, 1 - slot, g)
            out.append(sm_pv(c, slot, g, carry[g], False))
        return tuple(out)

    def last(slot, carry):
        return tuple(sm_pv(n_full, slot, g, carry[g], True) for g in groups)

    n_full = tile0 * TQ // TK
    for g in groups:
        qk(0, 0, g)
    o_w, l_w = window_branch()
    carry = tuple((jnp.full((1, rows), 0.0 if bounded else NEG_INF, F32), jnp.zeros((1, rows), F32),
                   jnp.zeros((HEAD_DIM, rows), F32)) for _ in groups)
    for c in range(n_full):
        carry = step(c, c % 2, carry)
    carry = last(n_full % 2, carry)
    sel_out = [(carry[g][1], carry[g][2]) for g in groups]

    g_t = gate_ref[0].T
    slabs = []
    for g in groups:
        def gate_row(br):
            base = g * (N_BRANCH * GROUP_HEADS) + br * GROUP_HEADS
            return jnp.concatenate([g_t[base + r:base + r + 1, :] for r in range(GROUP_HEADS)], axis=1)
        l_s, o_s = sel_out[g]
        og = o_c[g] * gate_row(0) + o_s * (gate_row(1) / l_s) + o_w[g] * (gate_row(2) / l_w[g])
        slabs += [og[:, r * TQ:(r + 1) * TQ] for r in range(GROUP_HEADS)]
    out_t = jnp.concatenate(slabs, axis=0)
    o_ref[0] = (out_t.T * zs_ref[0].astype(F32)).astype(BF16)


def _out_kernel(x_ref, *refs):
    on_refs, (oc_ref, w_ref, o_ref) = refs[:-3], refs[-3:]
    per_tile = TM_OUT // TK
    for k in range(len(on_refs) // per_tile):
        @pl.when(pl.program_id(1) == k)
        def _(k=k):
            on = jnp.concatenate([r[0] for r in on_refs[k * per_tile:(k + 1) * per_tile]], axis=0)
            acc = _dot(on, w_ref[0:D_NSA, :]) + _dot(oc_ref[0], w_ref[D_NSA:D_NSA + D_CONV, :])
            o_ref[0] = x_ref[0] + acc


def _rope_tables(pos):
    inv_freq = (np.float32(ROPE_THETA) ** (-np.arange(0, ROT_DIM, 2, dtype=np.float32) / ROT_DIM)).astype(np.float32)
    ang = pos.astype(np.float32)[:, None] * inv_freq[None, :]
    cos, sin = np.cos(ang), np.sin(ang)
    n = pos.shape[0]
    ones = np.ones((n, HEAD_DIM - ROT_DIM), np.float32)
    zeros = np.zeros((n, HEAD_DIM - ROT_DIM), np.float32)
    z8 = np.zeros((n, ROT_DIM // 2), np.float32)
    c = np.concatenate([cos, cos, ones], axis=1)
    sa = np.concatenate([z8, sin, zeros], axis=1)
    sb = np.concatenate([-sin, z8, zeros], axis=1)
    return tuple(jnp.asarray(np.concatenate([t, t], axis=1), F32) for t in (c, sa, sb))


def _overlap_t(ncp):
    cs = np.arange(ncp) * CMP_STRIDE
    ce = cs + CMP_LEN
    ss = np.arange(SEL_BLOCK) * SEL_BLOCK
    se = ss + SEL_BLOCK
    ov = np.clip(np.minimum(ce[None, :], se[:, None]) - np.maximum(cs[None, :], ss[:, None]), 0, None)
    ov = ov.astype(np.float32) / CMP_LEN
    ov[:, ncp - 1] = 0.0
    return jnp.asarray(ov, BF16)


def _cmp_weights(pos, w1, b1, w2):
    half = CMP_STRIDE
    def big_w1(w):
        w = w.reshape(half, 1, HEAD_DIM, 1, CMP_HIDDEN)
        eye = jnp.eye(KV_GROUPS, dtype=F32).reshape(1, KV_GROUPS, 1, KV_GROUPS, 1)
        return (w * eye).reshape(half * KV_GROUPS * HEAD_DIM, KV_GROUPS * CMP_HIDDEN).astype(BF16)
    def big_pos(p):
        return jnp.broadcast_to(p[:, None, :], (half, KV_GROUPS, HEAD_DIM)).reshape(1, -1)
    w1a, w1b = w1[:half * HEAD_DIM], w1[half * HEAD_DIM:]
    eye2 = jnp.eye(KV_GROUPS, dtype=F32)
    w2b = (w2[None, :, None, :] * eye2[:, None, :, None]).reshape(KV_GROUPS * CMP_HIDDEN,
                                                                 KV_GROUPS * HEAD_DIM).astype(BF16)
    b1b = jnp.tile(b1, KV_GROUPS)[None, :]
    return big_pos(pos[:half]), big_pos(pos[half:]), big_w1(w1a), big_w1(w1b), b1b, w2b


def _full(shape):
    nd = len(shape)
    return pl.BlockSpec(shape, lambda *_: (0,) * nd)


def kernel(x, norm_w, w_in, q_norm_w, k_norm_w, cmp_k_pos, cmp_k_w1, cmp_k_b1, cmp_k_w2,
           cmp_v_pos, cmp_v_w1, cmp_v_b1, cmp_v_w2, conv_w, conv_b, w_out):
    B, S, D = x.shape
    assert norm_w.shape[0] == 1, "single layer"
    assert S % TM == 0 and S % TM_OUT == 0 and TK % TQ == 0 and TM_OUT % TK == 0 and S // SEL_BLOCK <= SEL_BLOCK and S >= WINDOW + TQ
    ncp = S // CMP_STRIDE
    nst = S // TM
    f32 = lambda a: a.astype(F32)

    w = f32(w_in[0])
    head_order = []
    for j in range(GROUP_HEADS):
        head_order += [j, GROUP_HEADS + j]
    wq = w[:, :D_NSA].reshape(D, NSA_HEADS, HEAD_DIM)[:, jnp.array(head_order), :].reshape(D, D_NSA)
    gate_cols = []
    for g in range(KV_GROUPS):
        for br in range(N_BRANCH):
            for r in range(GROUP_HEADS):
                gate_cols.append((g * GROUP_HEADS + r) * N_BRANCH + br)
    g0 = D_NSA + 6 * LANES
    wg = w[:, g0:g0 + NSA_HEADS * N_BRANCH][:, jnp.array(gate_cols)]
    wg = jnp.pad(wg, ((0, 0), (0, LANES - NSA_HEADS * N_BRANCH)))
    rest0 = g0 + NSA_HEADS * N_BRANCH
    wq, wkv, wr = (t.astype(BF16) for t in (wq, jnp.concatenate([w[:, D_NSA:g0], wg], axis=1), w[:, rest0:]))
    n_rest = wr.shape[1]

    rc, rsa, rsb = _rope_tables(np.arange(S))
    mbd = jnp.asarray(np.kron(np.eye(4), np.full((HEAD_DIM, HEAD_DIM), 1.0 / HEAD_DIM)), BF16)
    qnw = jnp.tile(f32(q_norm_w[0]), 4)[None, :]
    knw_sw = jnp.concatenate([jnp.tile(f32(k_norm_w[0, 1]), 2), jnp.tile(f32(k_norm_w[0, 2]), 2)])[None, :]
    knw_c = jnp.tile(f32(k_norm_w[0, 0]), 2)[None, :]

    cp = pltpu.CompilerParams(dimension_semantics=("arbitrary", "arbitrary"),
                              vmem_limit_bytes=VMEM_LIMIT)
    row_blk = lambda n: pl.BlockSpec((1, TM, n), lambda b, s: (b, s, 0))
    tab_blk = pl.BlockSpec((TM, LANES), lambda b, s: (s, 0))
    vt_blk = pl.BlockSpec((1, TM // VCHUNK, LANES, VCHUNK), lambda b, s: (b, s, 0, 0))
    sds = jax.ShapeDtypeStruct

    (qt4, kc_raw, vc_raw, ksf, vst, kw, vwt, gates, zs, oconv) = pl.pallas_call(
        _proj_kernel,
        grid=(B, nst),
        in_specs=[row_blk(D), _full((1, D)), _full((D, D_NSA)), _full((D, 7 * LANES)),
                  _full((D, n_rest)), _full((1, 256)), _full((1, 256)),
                  tab_blk, tab_blk, tab_blk, _full((256, 256)), _full((3, D_CONV)), _full((1, D_CONV))],
        out_specs=[pl.BlockSpec((1, 4, LANES, TM), lambda b, s: (b, 0, 0, s)),
                   row_blk(LANES), row_blk(LANES), row_blk(2 * LANES), vt_blk, row_blk(LANES), vt_blk,
                   row_blk(LANES), row_blk(D_NSA), row_blk(D_CONV)],
        out_shape=[sds((B, 4, LANES, S), BF16), sds((B, S, LANES), F32), sds((B, S, LANES), F32),
                   sds((B, S, 2 * LANES), BF16), sds((B, S // VCHUNK, LANES, VCHUNK), BF16),
                   sds((B, S, LANES), BF16), sds((B, S // VCHUNK, LANES, VCHUNK), BF16),
                   sds((B, S, LANES), F32), sds((B, S, D_NSA), BF16), sds((B, S, D_CONV), BF16)],
        scratch_shapes=[pltpu.VMEM((TM + 16, D_CONV), F32)],
        compiler_params=cp,
        name="nsa_proj",
    )(x, f32(norm_w), wq, wkv, wr, qnw, knw_sw, rc, rsa, rsb, mbd, f32(conv_w[0]), f32(conv_b))

    cmp_pos = np.arange(ncp) * CMP_STRIDE + (CMP_LEN - 1)
    crc, crsa, crsb = _rope_tables(cmp_pos)
    kpa, kpb, kw1a, kw1b, kb1, kw2 = _cmp_weights(f32(cmp_k_pos[0]), f32(cmp_k_w1[0]),
                                                  f32(cmp_k_b1[0]), f32(cmp_k_w2[0]))
    vpa, vpb, vw1a, vw1b, vb1, vw2 = _cmp_weights(f32(cmp_v_pos[0]), f32(cmp_v_w1[0]),
                                                  f32(cmp_v_b1[0]), f32(cmp_v_w2[0]))
    flat = CMP_STRIDE * LANES
    h_blk = pl.BlockSpec((1, S, LANES), lambda b: (b, 0, 0))
    kc, vct = pl.pallas_call(
        functools.partial(_cmp_kernel, ncp=ncp),
        grid=(B,),
        in_specs=[h_blk, h_blk] + [_full((1, flat))] * 4
                 + [_full((flat, 256)), _full((flat, 256)), _full((1, 256)), _full((256, LANES))] * 2
                 + [_full((1, LANES)), _full((ncp, LANES)), _full((ncp, LANES)), _full((ncp, LANES)),
                    _full((256, 256))],
        out_specs=[pl.BlockSpec((1, ncp, LANES), lambda b: (b, 0, 0)),
                   pl.BlockSpec((1, LANES, ncp), lambda b: (b, 0, 0))],
        out_shape=[sds((B, ncp, LANES), BF16), sds((B, LANES, ncp), BF16)],
        compiler_params=pltpu.CompilerParams(dimension_semantics=("arbitrary",),
                                             vmem_limit_bytes=VMEM_LIMIT),
        name="nsa_compress",
    )(kc_raw, vc_raw, kpa, kpb, vpa, vpb,
      kw1a, kw1b, kb1, kw2, vw1a, vw1b, vb1, vw2, knw_c, crc, crsa, crsb, mbd)

    score_bound = (HEAD_DIM * Q_SCALE) * jnp.max(jnp.abs(f32(q_norm_w))) * jnp.max(jnp.abs(f32(k_norm_w)))
    tiles_per_range = TK // TQ
    n_ranges = S // TK

    def attention(bounded, r):
        last_pos = (r + 1) * TK
        tile0 = r * tiles_per_range
        win_pos = max(last_pos, WINDOW + TQ)
        return pl.pallas_call(
            functools.partial(_attn_kernel, ncp=ncp, bounded=bounded, tile0=tile0,
                              ncv=min(ncp, last_pos // CMP_STRIDE), nblk=min(SEL_BLOCK, last_pos // SEL_BLOCK)),
            grid=(B, tiles_per_range),
            in_specs=[pl.BlockSpec((1, 4, LANES, TQ), lambda b, i: (b, 0, 0, i + tile0)),
                      pl.BlockSpec((1, ncp, LANES), lambda b, i: (b, 0, 0)),
                      pl.BlockSpec((1, LANES, ncp), lambda b, i: (b, 0, 0)),
                      pl.BlockSpec((1, last_pos, 2 * LANES), lambda b, i: (b, 0, 0)),
                      pl.BlockSpec((1, last_pos // VCHUNK, LANES, VCHUNK), lambda b, i: (b, 0, 0, 0)),
                      pl.BlockSpec((1, win_pos, LANES), lambda b, i: (b, 0, 0)),
                      pl.BlockSpec((1, win_pos // VCHUNK, LANES, VCHUNK), lambda b, i: (b, 0, 0, 0)),
                      pl.BlockSpec((1, TQ, LANES), lambda b, i: (b, i + tile0, 0)),
                      pl.BlockSpec((1, TQ, D_NSA), lambda b, i: (b, i + tile0, 0)),
                      _full((SEL_BLOCK, ncp))],
            out_specs=pl.BlockSpec((1, TQ, D_NSA), lambda b, i: (b, i, 0)),
            out_shape=sds((B, TK, D_NSA), BF16),
            scratch_shapes=[pltpu.VMEM((KV_GROUPS, LANES, TQ), BF16),
                            pltpu.VMEM((2, KV_GROUPS, TK, GROUP_HEADS * TQ), F32),
                            pltpu.VMEM((2, KV_GROUPS, 1, GROUP_HEADS * TQ), F32)],
            compiler_params=cp,
            name=("nsa_attn_bounded_r%d" if bounded else "nsa_attn_online_r%d") % r,
        )

    attn_args = (qt4, kc, vct, ksf, vst, kw, vwt, gates, zs, _overlap_t(ncp))
    all_ranges = lambda bounded: (lambda *a: tuple(attention(bounded, r)(*a) for r in range(n_ranges)))
    o_nsa = lax.cond(score_bound <= MAX_SAFE_SCORE, all_ranges(True), all_ranges(False), *attn_args)

    out_blk = lambda n: pl.BlockSpec((1, TM_OUT, n), lambda b, s: (b, s, 0))
    on_blk = lambda k: pl.BlockSpec((1, TK, D_NSA), lambda b, s: (
        jnp.where(s >= k // (TM_OUT // TK), b, jnp.maximum(b - 1, 0)), 0, 0))
    out = pl.pallas_call(
        _out_kernel,
        grid=(B, S // TM_OUT),
        in_specs=[out_blk(D)] + [on_blk(k) for k in range(n_ranges)]
                 + [out_blk(D_CONV), _full((D_NSA + D_CONV, D))],
        out_specs=out_blk(D),
        out_shape=sds((B, S, D), x.dtype),
        compiler_params=cp,
        name="nsa_out",
    )(x, *o_nsa, oconv, f32(w_out[0]).astype(BF16))
    return out
```

```python
import functools

import numpy as np
import jax
import jax.numpy as jnp
from jax import lax
from jax.experimental import pallas as pl
from jax.experimental.pallas import tpu as pltpu

LANES = 128
HEAD_DIM = 64
NSA_HEADS = 8
KV_GROUPS = 2
GROUP_HEADS = NSA_HEADS // KV_GROUPS
D_NSA = NSA_HEADS * HEAD_DIM
D_CONV = 512
N_BRANCH = 3
ROT_DIM = HEAD_DIM // 4
ROPE_THETA = 500000.0
CMP_LEN = 32
CMP_STRIDE = 16
CMP_HIDDEN = 2 * HEAD_DIM
SEL_BLOCK = 64
SEL_TOPK = 16
WINDOW = 512
EPS = 1e-6
NEG_INF = -1e30
N_FORCED = 3
TAKEN = -2.0
SCALE = HEAD_DIM ** -0.5
Q_SCALE = SCALE * float(np.log2(np.e))

TM = 1024
TM_OUT = 1024
TQ = 256
TK = 512
VCHUNK = 128
VMEM_LIMIT = 56 * 1024 * 1024
MAX_SAFE_SCORE = 50.0

BF16 = jnp.bfloat16
F32 = jnp.float32


def _dot(a, b):
    return jnp.dot(a, b, preferred_element_type=F32)


def _split3(a):
    hi = a.astype(BF16)
    r1 = a - hi.astype(F32)
    mid = r1.astype(BF16)
    lo = (r1 - mid.astype(F32)).astype(BF16)
    return hi, mid, lo


def _group_mean(sq, mbd):
    return _dot(sq.astype(BF16), mbd)


def _rope(xn, c, sa, sb):
    return xn * c + pltpu.roll(xn, 8, 1) * sa + pltpu.roll(xn, LANES - 8, 1) * sb


def _silu(z):
    return z * (1.0 / (1.0 + jnp.exp(-z)))


def _proj_kernel(x_ref, nw_ref, wq_ref, wkv_ref, wr_ref, qnw_ref, knw_ref,
                 rc_ref, rsa_ref, rsb_ref, mbd_ref, cw_ref, cb_ref,
                 qt_ref, kc_ref, vc_ref, ksf_ref, vst_ref, kw_ref, vwt_ref,
                 gate_ref, zs_ref, oconv_ref, ubuf):
    si = pl.program_id(1)

    @pl.when(si == 0)
    def _():
        ubuf[0:8, :] = jnp.zeros((8, D_CONV), F32)

    x = x_ref[0]
    ms = jnp.mean(x * x, axis=-1, keepdims=True)
    h = (x * lax.rsqrt(ms + EPS) * nw_ref[...]).astype(BF16)

    rc, rsa, rsb = rc_ref[...], rsa_ref[...], rsb_ref[...]
    mbd = mbd_ref[...]

    pq = _dot(h, wq_ref[...])
    for pair in range(2):
        blk = pq[:, pair * 256:(pair + 1) * 256]
        msq = _group_mean(blk * blk, mbd)
        qn = blk * lax.rsqrt(msq + EPS) * qnw_ref[...]
        for half in range(2):
            t = qn[:, half * LANES:(half + 1) * LANES]
            qt_ref[0, pair * 2 + half] = (_rope(t, rc, rsa, rsb) * Q_SCALE).T.astype(BF16)

    pkv = _dot(h, wkv_ref[...])
    kc_ref[0] = pkv[:, 0:128]
    vc_ref[0] = pkv[:, 128:256]
    ksw = jnp.concatenate([pkv[:, 256:384], pkv[:, 512:640]], axis=1)
    msk = _group_mean(ksw * ksw, mbd)
    kn = ksw * lax.rsqrt(msk + EPS) * knw_ref[...]
    ks = _rope(kn[:, 0:128], rc, rsa, rsb)
    kw = _rope(kn[:, 128:256], rc, rsa, rsb)
    row = lax.broadcasted_iota(jnp.int32, (TM, LANES), 0) + si * TM
    lane = lax.broadcasted_iota(jnp.int32, (TM, LANES), 1)
    onehot = jnp.where(lane == (row >> 6), 1.0, 0.0)
    ksf_ref[0, :, 0:128] = ks.astype(BF16)
    ksf_ref[0, :, 128:256] = onehot.astype(BF16)
    kw_ref[0] = kw.astype(BF16)
    vs = pkv[:, 384:512]
    vw = pkv[:, 640:768]
    for j in range(TM // VCHUNK):
        vst_ref[0, j] = vs[j * VCHUNK:(j + 1) * VCHUNK, :].T.astype(BF16)
        vwt_ref[0, j] = vw[j * VCHUNK:(j + 1) * VCHUNK, :].T.astype(BF16)

    gate_ref[0] = 1.0 / (1.0 + jnp.exp(-pkv[:, 768:896]))
    zs_ref[0] = _silu(_dot(h, wr_ref[:, 0:D_NSA])).astype(BF16)

    pcv = _dot(h, wr_ref[:, D_NSA:D_NSA + 4 * D_CONV])
    u = pcv[:, 1024:1536] * pcv[:, 0:512]
    ubuf[8:8 + TM, :] = u
    u1 = ubuf[7:7 + TM, :]
    u2 = ubuf[6:6 + TM, :]
    conv = cw_ref[0:1, :] * u2 + cw_ref[1:2, :] * u1 + cw_ref[2:3, :] * u + cb_ref[...]
    oconv_ref[0] = (pcv[:, 512:1024] * conv * _silu(pcv[:, 1536:2048])).astype(BF16)
    ubuf[0:8, :] = ubuf[TM:TM + 8, :]


def _cmp_kernel(hk_ref, hv_ref, pka_ref, pkb_ref, pva_ref, pvb_ref,
                wk1a_ref, wk1b_ref, bk1_ref, wk2_ref,
                wv1a_ref, wv1b_ref, bv1_ref, wv2_ref,
                knw_ref, rc_ref, rsa_ref, rsb_ref, mbd_ref,
                kc_ref, vct_ref, *, ncp):
    def first_layer(h_ref, p_ref, w_ref):
        acc = None
        for l in range(0, CMP_STRIDE, 2):
            lhs = jnp.concatenate([h_ref[0, pl.ds(l, ncp, stride=CMP_STRIDE), :],
                                   h_ref[0, pl.ds(l + 1, ncp, stride=CMP_STRIDE), :]], axis=1)
            lhs = (lhs + p_ref[:, l * LANES:(l + 2) * LANES]).astype(BF16)
            part = _dot(lhs, w_ref[l * LANES:(l + 2) * LANES, :])
            acc = part if acc is None else acc + part
        return acc

    def mlp(h_ref, pa_ref, pb_ref, w1a_ref, w1b_ref, b1_ref, w2_ref):
        p = first_layer(h_ref, pa_ref, w1a_ref)
        q = first_layer(h_ref, pb_ref, w1b_ref)
        pre = p + pltpu.roll(q, ncp - 1, 0) + b1_ref[...]
        return _dot(_silu(pre).astype(BF16), w2_ref[...])

    kc = mlp(hk_ref, pka_ref, pkb_ref, wk1a_ref, wk1b_ref, bk1_ref, wk2_ref)
    msk = _group_mean(kc * kc, mbd_ref[0:LANES, 0:LANES])
    kn = kc * lax.rsqrt(msk + EPS) * knw_ref[...]
    kc_ref[0] = _rope(kn, rc_ref[...], rsa_ref[...], rsb_ref[...]).astype(BF16)
    vc = mlp(hv_ref, pva_ref, pvb_ref, wv1a_ref, wv1b_ref, bv1_ref, wv2_ref)
    vct_ref[0] = vc.T.astype(BF16)


def _attn_kernel(qt_ref, kc_ref, vct_ref, ksf_ref, vst_ref, kw_ref, vwt_ref,
                 gate_ref, zs_ref, ovt_ref, o_ref, pen_sc, s_sc, cm_sc, *, ncp, bounded, tile0, ncv, nblk):
    i = pl.program_id(1) + tile0
    t0 = i * TQ
    rows = GROUP_HEADS * TQ
    groups = range(KV_GROUPS)

    q_all = jnp.concatenate([qt_ref[0, j] for j in range(GROUP_HEADS)], axis=1)
    frow = lax.broadcasted_iota(jnp.int32, (LANES, rows), 0)
    qg = [jnp.where(frow < HEAD_DIM, q_all, 0), jnp.where(frow >= HEAD_DIM, q_all, 0)]
    t_q = t0 + lax.broadcasted_iota(jnp.int32, (1, TQ), 1)

    def fill_where(a, mask, fill):
        return jnp.concatenate([jnp.where(mask, a[:, r * TQ:(r + 1) * TQ], fill)
                                for r in range(GROUP_HEADS)], axis=1)

    vrows = lambda v, g: v[g * HEAD_DIM:(g + 1) * HEAD_DIM, :]

    s_cmp = [_dot(kc_ref[0, 0:ncv, :], qg[g]) for g in groups]

    c_idx = lax.broadcasted_iota(jnp.int32, (ncv, 1), 0)
    mask_c = ((c_idx * CMP_STRIDE + (CMP_LEN - 1)) <= t_q) & (c_idx < ncp - 1)
    p_c, o_c = [], []
    for g in groups:
        if bounded:
            e_c = fill_where(jnp.exp2(s_cmp[g]), mask_c, 0.0)
        else:
            sm = fill_where(s_cmp[g], mask_c, NEG_INF)
            e_c = fill_where(jnp.exp2(sm - jnp.max(sm, axis=0, keepdims=True)), mask_c, 0.0)
        l_c = jnp.sum(e_c, axis=0, keepdims=True)
        p = e_c * (1.0 / jnp.where(l_c > 0.0, l_c, 1.0))
        p_c.append(p)
        o_c.append(_dot(vrows(vct_ref[0], g)[:, 0:ncv], p.astype(BF16)))

    if nblk <= SEL_TOPK:
        pen_sc[...] = jnp.zeros(pen_sc.shape, BF16)
    else:
        n_idx = lax.broadcasted_iota(jnp.int32, (nblk, TQ), 0)
        tq = t0 + lax.broadcasted_iota(jnp.int32, (nblk, TQ), 1)
        cur = tq >> 6
        forced = (n_idx == 0) | (n_idx == cur) | (n_idx == cur - 1)
        visible = (n_idx << 6) <= tq
        ovt = ovt_ref[0:nblk, 0:ncv]
        left = []
        for g in groups:
            psum = p_c[g][:, 0:TQ]
            for r in range(1, GROUP_HEADS):
                psum = psum + p_c[g][:, r * TQ:(r + 1) * TQ]
            hi, mid, lo = _split3(psum)
            imp = _dot(ovt, hi) + _dot(ovt, mid) + _dot(ovt, lo)
            left.append(jnp.where(forced, TAKEN, jnp.where(visible, imp, -1.0)))
        for _ in range(SEL_TOPK - N_FORCED):
            for g in groups:
                top = jnp.max(left[g], axis=0, keepdims=True)
                first = jnp.min(jnp.where(left[g] == top, n_idx, nblk), axis=0, keepdims=True)
                left[g] = jnp.where(n_idx == first, TAKEN, left[g])
        for g in groups:
            pen_sc[g, 0:nblk, :] = jnp.where(left[g] == TAKEN, 0.0, NEG_INF).astype(BF16)
            pen_sc[g, nblk:LANES, :] = jnp.zeros((LANES - nblk, TQ), BF16)

    nwc = WINDOW // VCHUNK + TQ // VCHUNK
    cw = jnp.maximum(i * (TQ // VCHUNK) - WINDOW // VCHUNK, 0)
    w0 = pl.multiple_of(cw * VCHUNK, VCHUNK)
    kwin = kw_ref[0, pl.ds(w0, nwc * VCHUNK), :]
    s_win = [_dot(kwin, qg[g]) for g in groups]

    q_aug = [jnp.concatenate([qg[g], jnp.concatenate([pen_sc[g]] * GROUP_HEADS, axis=1)], axis=0)
             for g in groups]

    def window_branch():
        vwin = jnp.concatenate([vwt_ref[0, cw + j] for j in range(nwc)], axis=1)
        kpos = w0 + lax.broadcasted_iota(jnp.int32, (nwc * VCHUNK, 1), 0)

        def masked_w(a, fill):
            if tile0 * TQ < WINDOW:
                return fill_where(a, (kpos <= t_q) & (kpos > t_q - WINDOW), fill)
            lo, hi = slice(0, TQ), slice(WINDOW, WINDOW + TQ)
            return jnp.concatenate([fill_where(a[lo], kpos[lo] > t_q - WINDOW, fill), a[TQ:WINDOW],
                                    fill_where(a[hi], kpos[hi] <= t_q, fill)], axis=0)

        o_w, l_w = [], []
        for g in groups:
            if bounded:
                p_w = masked_w(jnp.exp2(s_win[g]), 0.0)
            else:
                sw = masked_w(s_win[g], NEG_INF)
                p_w = jnp.exp2(sw - jnp.max(sw, axis=0, keepdims=True))
            v_ext = jnp.concatenate([vrows(vwin, g), jnp.ones((16, nwc * VCHUNK), BF16)], axis=0)
            ol = _dot(v_ext, p_w.astype(BF16))
            o_w.append(ol[0:HEAD_DIM, :])
            l_w.append(ol[HEAD_DIM:HEAD_DIM + 1, :])
        return o_w, l_w

    def v_chunk(c, g):
        cv = c * (TK // VCHUNK)
        return jnp.concatenate([vrows(vst_ref[0, cv + j], g) for j in range(TK // VCHUNK)], axis=1)

    def qk(c, slot, g):
        k0 = pl.multiple_of(c * TK, TK)
        s = _dot(ksf_ref[0, pl.ds(k0, TK), :], q_aug[g])
        s_sc[slot, g] = s
        if not bounded:
            cm_sc[slot, g] = jnp.max(s, axis=0, keepdims=True)

    def sm_pv(c, slot, g, state, masked):
        m_p, l_p, acc = state
        s = s_sc[slot, g]
        if masked:
            kpos = c * TK + lax.broadcasted_iota(jnp.int32, (TK, 1), 0)
            s = fill_where(s, kpos <= t_q, NEG_INF)
        if bounded:
            p = jnp.exp2(s)
            return m_p, l_p + jnp.sum(p, axis=0, keepdims=True), acc + _dot(v_chunk(c, g), p.astype(BF16))
        cm = jnp.max(s, axis=0, keepdims=True) if masked else cm_sc[slot, g]
        m_n = jnp.maximum(m_p, cm)
        alpha = jnp.exp2(m_p - m_n)
        p = jnp.exp2(s - m_n)
        l_n = alpha * l_p + jnp.sum(p, axis=0, keepdims=True)
        acc = alpha * acc + _dot(v_chunk(c, g), p.astype(BF16))
        return m_n, l_n, acc

    def step(c, slot, carry):
        out = []
        for g in groups:
            qk(c + 1, 1 - slot, g)
            out.append(sm_pv(c, slot, g, carry[g], False))
        return tuple(out)

    def last(slot, carry):
        return tuple(sm_pv(n_full, slot, g, carry[g], True) for g in groups)

    n_full = tile0 * TQ // TK
    for g in groups:
        qk(0, 0, g)
    o_w, l_w = window_branch()
    carry = tuple((jnp.full((1, rows), 0.0 if bounded else NEG_INF, F32), jnp.zeros((1, rows), F32),
                   jnp.zeros((HEAD_DIM, rows), F32)) for _ in groups)
    for c in range(n_full):
        carry = step(c, c % 2, carry)
    carry = last(n_full % 2, carry)
    sel_out = [(carry[g][1], carry[g][2]) for g in groups]

    g_t = gate_ref[0].T
    slabs = []
    for g in groups:
        def gate_row(br):
            base = g * (N_BRANCH * GROUP_HEADS) + br * GROUP_HEADS
            return jnp.concatenate([g_t[base + r:base + r + 1, :] for r in range(GROUP_HEADS)], axis=1)
        l_s, o_s = sel_out[g]
        og = o_c[g] * gate_row(0) + o_s * (gate_row(1) / l_s) + o_w[g] * (gate_row(2) / l_w[g])
        slabs += [og[:, r * TQ:(r + 1) * TQ] for r in range(GROUP_HEADS)]
    out_t = jnp.concatenate(slabs, axis=0)
    o_ref[0] = (out_t.T * zs_ref[0].astype(F32)).astype(BF16)


def _out_kernel(x_ref, *refs):
    on_refs, (oc_ref, w_ref, o_ref) = refs[:-3], refs[-3:]
    per_tile = TM_OUT // TK
    for k in range(len(on_refs) // per_tile):
        @pl.when(pl.program_id(1) == k)
        def _(k=k):
            on = jnp.concatenate([r[0] for r in on_refs[k * per_tile:(k + 1) * per_tile]], axis=0)
            acc = _dot(on, w_ref[0:D_NSA, :]) + _dot(oc_ref[0], w_ref[D_NSA:D_NSA + D_CONV, :])
            o_ref[0] = x_ref[0] + acc


def _rope_tables(pos):
    inv_freq = (np.float32(ROPE_THETA) ** (-np.arange(0, ROT_DIM, 2, dtype=np.float32) / ROT_DIM)).astype(np.float32)
    ang = pos.astype(np.float32)[:, None] * inv_freq[None, :]
    cos, sin = np.cos(ang), np.sin(ang)
    n = pos.shape[0]
    ones = np.ones((n, HEAD_DIM - ROT_DIM), np.float32)
    zeros = np.zeros((n, HEAD_DIM - ROT_DIM), np.float32)
    z8 = np.zeros((n, ROT_DIM // 2), np.float32)
    c = np.concatenate([cos, cos, ones], axis=1)
    sa = np.concatenate([z8, sin, zeros], axis=1)
    sb = np.concatenate([-sin, z8, zeros], axis=1)
    return tuple(jnp.asarray(np.concatenate([t, t], axis=1), F32) for t in (c, sa, sb))


def _overlap_t(ncp):
    cs = np.arange(ncp) * CMP_STRIDE
    ce = cs + CMP_LEN
    ss = np.arange(SEL_BLOCK) * SEL_BLOCK
    se = ss + SEL_BLOCK
    ov = np.clip(np.minimum(ce[None, :], se[:, None]) - np.maximum(cs[None, :], ss[:, None]), 0, None)
    ov = ov.astype(np.float32) / CMP_LEN
    ov[:, ncp - 1] = 0.0
    return jnp.asarray(ov, BF16)


def _cmp_weights(pos, w1, b1, w2):
    half = CMP_STRIDE
    def big_w1(w):
        w = w.reshape(half, 1, HEAD_DIM, 1, CMP_HIDDEN)
        eye = jnp.eye(KV_GROUPS, dtype=F32).reshape(1, KV_GROUPS, 1, KV_GROUPS, 1)
        return (w * eye).reshape(half * KV_GROUPS * HEAD_DIM, KV_GROUPS * CMP_HIDDEN).astype(BF16)
    def big_pos(p):
        return jnp.broadcast_to(p[:, None, :], (half, KV_GROUPS, HEAD_DIM)).reshape(1, -1)
    w1a, w1b = w1[:half * HEAD_DIM], w1[half * HEAD_DIM:]
    eye2 = jnp.eye(KV_GROUPS, dtype=F32)
    w2b = (w2[None, :, None, :] * eye2[:, None, :, None]).reshape(KV_GROUPS * CMP_HIDDEN,
                                                                 KV_GROUPS * HEAD_DIM).astype(BF16)
    b1b = jnp.tile(b1, KV_GROUPS)[None, :]
    return big_pos(pos[:half]), big_pos(pos[half:]), big_w1(w1a), big_w1(w1b), b1b, w2b


def _full(shape):
    nd = len(shape)
    return pl.BlockSpec(shape, lambda *_: (0,) * nd)


def _once(shape):
    nd = len(shape)
    return pl.BlockSpec(shape, lambda *_: (0,) * nd, pipeline_mode=pl.Buffered(1))


def kernel(x, norm_w, w_in, q_norm_w, k_norm_w, cmp_k_pos, cmp_k_w1, cmp_k_b1, cmp_k_w2,
           cmp_v_pos, cmp_v_w1, cmp_v_b1, cmp_v_w2, conv_w, conv_b, w_out):
    B, S, D = x.shape
    assert norm_w.shape[0] == 1, "single layer"
    assert S % TM == 0 and S % TM_OUT == 0 and TK % TQ == 0 and TM_OUT % TK == 0 and S // SEL_BLOCK <= SEL_BLOCK and S >= WINDOW + TQ
    ncp = S // CMP_STRIDE
    nst = S // TM
    f32 = lambda a: a.astype(F32)

    w = f32(w_in[0])
    head_order = []
    for j in range(GROUP_HEADS):
        head_order += [j, GROUP_HEADS + j]
    wq = w[:, :D_NSA].reshape(D, NSA_HEADS, HEAD_DIM)[:, jnp.array(head_order), :].reshape(D, D_NSA)
    gate_cols = []
    for g in range(KV_GROUPS):
        for br in range(N_BRANCH):
            for r in range(GROUP_HEADS):
                gate_cols.append((g * GROUP_HEADS + r) * N_BRANCH + br)
    g0 = D_NSA + 6 * LANES
    wg = w[:, g0:g0 + NSA_HEADS * N_BRANCH][:, jnp.array(gate_cols)]
    wg = jnp.pad(wg, ((0, 0), (0, LANES - NSA_HEADS * N_BRANCH)))
    rest0 = g0 + NSA_HEADS * N_BRANCH
    wq, wkv, wr = (t.astype(BF16) for t in (wq, jnp.concatenate([w[:, D_NSA:g0], wg], axis=1), w[:, rest0:]))
    n_rest = wr.shape[1]

    rc, rsa, rsb = _rope_tables(np.arange(S))
    mbd = jnp.asarray(np.kron(np.eye(4), np.full((HEAD_DIM, HEAD_DIM), 1.0 / HEAD_DIM)), BF16)
    qnw = jnp.tile(f32(q_norm_w[0]), 4)[None, :]
    knw_sw = jnp.concatenate([jnp.tile(f32(k_norm_w[0, 1]), 2), jnp.tile(f32(k_norm_w[0, 2]), 2)])[None, :]
    knw_c = jnp.tile(f32(k_norm_w[0, 0]), 2)[None, :]

    cp = pltpu.CompilerParams(dimension_semantics=("arbitrary", "arbitrary"),
                              vmem_limit_bytes=VMEM_LIMIT)
    row_blk = lambda n: pl.BlockSpec((1, TM, n), lambda b, s: (b, s, 0))
    tab_blk = pl.BlockSpec((TM, LANES), lambda b, s: (s, 0))
    vt_blk = pl.BlockSpec((1, TM // VCHUNK, LANES, VCHUNK), lambda b, s: (b, s, 0, 0))
    sds = jax.ShapeDtypeStruct

    (qt4, kc_raw, vc_raw, ksf, vst, kw, vwt, gates, zs, oconv) = pl.pallas_call(
        _proj_kernel,
        grid=(B, nst),
        in_specs=[row_blk(D), _full((1, D)), _once((D, D_NSA)), _once((D, 7 * LANES)),
                  _once((D, n_rest)), _full((1, 256)), _full((1, 256)),
                  tab_blk, tab_blk, tab_blk, _full((256, 256)), _full((3, D_CONV)), _full((1, D_CONV))],
        out_specs=[pl.BlockSpec((1, 4, LANES, TM), lambda b, s: (b, 0, 0, s)),
                   row_blk(LANES), row_blk(LANES), row_blk(2 * LANES), vt_blk, row_blk(LANES), vt_blk,
                   row_blk(LANES), row_blk(D_NSA), row_blk(D_CONV)],
        out_shape=[sds((B, 4, LANES, S), BF16), sds((B, S, LANES), F32), sds((B, S, LANES), F32),
                   sds((B, S, 2 * LANES), BF16), sds((B, S // VCHUNK, LANES, VCHUNK), BF16),
                   sds((B, S, LANES), BF16), sds((B, S // VCHUNK, LANES, VCHUNK), BF16),
                   sds((B, S, LANES), F32), sds((B, S, D_NSA), BF16), sds((B, S, D_CONV), BF16)],
        scratch_shapes=[pltpu.VMEM((TM + 16, D_CONV), F32)],
        compiler_params=cp,
        name="nsa_proj",
    )(x, f32(norm_w), wq, wkv, wr, qnw, knw_sw, rc, rsa, rsb, mbd, f32(conv_w[0]), f32(conv_b))

    cmp_pos = np.arange(ncp) * CMP_STRIDE + (CMP_LEN - 1)
    crc, crsa, crsb = _rope_tables(cmp_pos)
    kpa, kpb, kw1a, kw1b, kb1, kw2 = _cmp_weights(f32(cmp_k_pos[0]), f32(cmp_k_w1[0]),
                                                  f32(cmp_k_b1[0]), f32(cmp_k_w2[0]))
    vpa, vpb, vw1a, vw1b, vb1, vw2 = _cmp_weights(f32(cmp_v_pos[0]), f32(cmp_v_w1[0]),
                                                  f32(cmp_v_b1[0]), f32(cmp_v_w2[0]))
    flat = CMP_STRIDE * LANES
    h_blk = pl.BlockSpec((1, S, LANES), lambda b: (b, 0, 0))
    kc, vct = pl.pallas_call(
        functools.partial(_cmp_kernel, ncp=ncp),
        grid=(B,),
        in_specs=[h_blk, h_blk] + [_full((1, flat))] * 4
                 + [_full((flat, 256)), _full((flat, 256)), _full((1, 256)), _full((256, LANES))] * 2
                 + [_full((1, LANES)), _full((ncp, LANES)), _full((ncp, LANES)), _full((ncp, LANES)),
                    _full((256, 256))],
        out_specs=[pl.BlockSpec((1, ncp, LANES), lambda b: (b, 0, 0)),
                   pl.BlockSpec((1, LANES, ncp), lambda b: (b, 0, 0))],
        out_shape=[sds((B, ncp, LANES), BF16), sds((B, LANES, ncp), BF16)],
        compiler_params=pltpu.CompilerParams(dimension_semantics=("arbitrary",),
                                             vmem_limit_bytes=VMEM_LIMIT),
        name="nsa_compress",
    )(kc_raw, vc_raw, kpa, kpb, vpa, vpb,
      kw1a, kw1b, kb1, kw2, vw1a, vw1b, vb1, vw2, knw_c, crc, crsa, crsb, mbd)

    score_bound = (HEAD_DIM * Q_SCALE) * jnp.max(jnp.abs(f32(q_norm_w))) * jnp.max(jnp.abs(f32(k_norm_w)))
    tiles_per_range = TK // TQ
    n_ranges = S // TK

    def attention(bounded, r):
        last_pos = (r + 1) * TK
        tile0 = r * tiles_per_range
        win_pos = max(last_pos, WINDOW + TQ)
        return pl.pallas_call(
            functools.partial(_attn_kernel, ncp=ncp, bounded=bounded, tile0=tile0,
                              ncv=min(ncp, last_pos // CMP_STRIDE), nblk=min(SEL_BLOCK, last_pos // SEL_BLOCK)),
            grid=(B, tiles_per_range),
            in_specs=[pl.BlockSpec((1, 4, LANES, TQ), lambda b, i: (b, 0, 0, i + tile0)),
                      pl.BlockSpec((1, ncp, LANES), lambda b, i: (b, 0, 0)),
                      pl.BlockSpec((1, LANES, ncp), lambda b, i: (b, 0, 0)),
                      pl.BlockSpec((1, last_pos, 2 * LANES), lambda b, i: (b, 0, 0)),
                      pl.BlockSpec((1, last_pos // VCHUNK, LANES, VCHUNK), lambda b, i: (b, 0, 0, 0)),
                      pl.BlockSpec((1, win_pos, LANES), lambda b, i: (b, 0, 0)),
                      pl.BlockSpec((1, win_pos // VCHUNK, LANES, VCHUNK), lambda b, i: (b, 0, 0, 0)),
                      pl.BlockSpec((1, TQ, LANES), lambda b, i: (b, i + tile0, 0)),
                      pl.BlockSpec((1, TQ, D_NSA), lambda b, i: (b, i + tile0, 0)),
                      _full((SEL_BLOCK, ncp))],
            out_specs=pl.BlockSpec((1, TQ, D_NSA), lambda b, i: (b, i, 0)),
            out_shape=sds((B, TK, D_NSA), BF16),
            scratch_shapes=[pltpu.VMEM((KV_GROUPS, LANES, TQ), BF16),
                            pltpu.VMEM((2, KV_GROUPS, TK, GROUP_HEADS * TQ), F32),
                            pltpu.VMEM((2, KV_GROUPS, 1, GROUP_HEADS * TQ), F32)],
            compiler_params=cp,
            name=("nsa_attn_bounded_r%d" if bounded else "nsa_attn_online_r%d") % r,
        )

    attn_args = (qt4, kc, vct, ksf, vst, kw, vwt, gates, zs, _overlap_t(ncp))
    all_ranges = lambda bounded: (lambda *a: tuple(attention(bounded, r)(*a) for r in range(n_ranges)))
    o_nsa = lax.cond(score_bound <= MAX_SAFE_SCORE, all_ranges(True), all_ranges(False), *attn_args)

    out_blk = lambda n: pl.BlockSpec((1, TM_OUT, n), lambda b, s: (b, s, 0))
    on_blk = lambda k: pl.BlockSpec((1, TK, D_NSA), lambda b, s: (
        jnp.where(s >= k // (TM_OUT // TK), b, jnp.maximum(b - 1, 0)), 0, 0))
    out = pl.pallas_call(
        _out_kernel,
        grid=(B, S // TM_OUT),
        in_specs=[out_blk(D)] + [on_blk(k) for k in range(n_ranges)]
                 + [out_blk(D_CONV), _full((D_NSA + D_CONV, D))],
        out_specs=out_blk(D),
        out_shape=sds((B, S, D), x.dtype),
        compiler_params=cp,
        name="nsa_out",
    )(x, *o_nsa, oconv, f32(w_out[0]).astype(BF16))
    return out
```

```python
import functools

import numpy as np
import jax
import jax.numpy as jnp
from jax import lax
from jax.experimental import pallas as pl
from jax.experimental.pallas import tpu as pltpu

LANES = 128
HEAD_DIM = 64
NSA_HEADS = 8
KV_GROUPS = 2
GROUP_HEADS = NSA_HEADS // KV_GROUPS
D_NSA = NSA_HEADS * HEAD_DIM
D_CONV = 512
N_BRANCH = 3
ROT_DIM = HEAD_DIM // 4
ROPE_THETA = 500000.0
CMP_LEN = 32
CMP_STRIDE = 16
CMP_HIDDEN = 2 * HEAD_DIM
SEL_BLOCK = 64
SEL_TOPK = 16
WINDOW = 512
EPS = 1e-6
NEG_INF = -1e30
N_FORCED = 3
TAKEN = -2.0
SCALE = HEAD_DIM ** -0.5
Q_SCALE = SCALE * float(np.log2(np.e))

TM = 1024
TM_OUT = 1024
TQ = 256
TK = 512
VCHUNK = 128
VMEM_LIMIT = 56 * 1024 * 1024
MAX_SAFE_SCORE = 50.0

BF16 = jnp.bfloat16
F32 = jnp.float32


def _dot(a, b):
    return jnp.dot(a, b, preferred_element_type=F32)


def _split3(a):
    hi = a.astype(BF16)
    r1 = a - hi.astype(F32)
    mid = r1.astype(BF16)
    lo = (r1 - mid.astype(F32)).astype(BF16)
    return hi, mid, lo


def _group_mean(sq, mbd):
    return _dot(sq.astype(BF16), mbd)


def _rope(xn, c, sa, sb):
    return xn * c + pltpu.roll(xn, 8, 1) * sa + pltpu.roll(xn, LANES - 8, 1) * sb


def _silu(z):
    return z * (1.0 / (1.0 + jnp.exp(-z)))


def _proj_kernel(x_ref, nw_ref, wq_ref, wkv_ref, wr_ref, qnw_ref, knw_ref,
                 rc_ref, rsa_ref, rsb_ref, mbd_ref, cw_ref, cb_ref,
                 qt_ref, kc_ref, vc_ref, ksf_ref, vst_ref, kw_ref, vwt_ref,
                 gate_ref, zs_ref, oconv_ref, ubuf):
    si = pl.program_id(1)

    @pl.when(si == 0)
    def _():
        ubuf[0:8, :] = jnp.zeros((8, D_CONV), F32)

    x = x_ref[0]
    ms = jnp.mean(x * x, axis=-1, keepdims=True)
    h = (x * lax.rsqrt(ms + EPS) * nw_ref[...]).astype(BF16)

    rc, rsa, rsb = rc_ref[...], rsa_ref[...], rsb_ref[...]
    mbd = mbd_ref[...]

    pq = _dot(h, wq_ref[...])
    for pair in range(2):
        blk = pq[:, pair * 256:(pair + 1) * 256]
        msq = _group_mean(blk * blk, mbd)
        qn = blk * lax.rsqrt(msq + EPS) * qnw_ref[...]
        for half in range(2):
            t = qn[:, half * LANES:(half + 1) * LANES]
            qt_ref[0, pair * 2 + half] = (_rope(t, rc, rsa, rsb) * Q_SCALE).T.astype(BF16)

    pkv = _dot(h, wkv_ref[...])
    kc_ref[0] = pkv[:, 0:128]
    vc_ref[0] = pkv[:, 128:256]
    ksw = jnp.concatenate([pkv[:, 256:384], pkv[:, 512:640]], axis=1)
    msk = _group_mean(ksw * ksw, mbd)
    kn = ksw * lax.rsqrt(msk + EPS) * knw_ref[...]
    ks = _rope(kn[:, 0:128], rc, rsa, rsb)
    kw = _rope(kn[:, 128:256], rc, rsa, rsb)
    row = lax.broadcasted_iota(jnp.int32, (TM, LANES), 0) + si * TM
    lane = lax.broadcasted_iota(jnp.int32, (TM, LANES), 1)
    onehot = jnp.where(lane == (row >> 6), 1.0, 0.0)
    ksf_ref[0, :, 0:128] = ks.astype(BF16)
    ksf_ref[0, :, 128:256] = onehot.astype(BF16)
    kw_ref[0] = kw.astype(BF16)
    vs = pkv[:, 384:512]
    vw = pkv[:, 640:768]
    for j in range(TM // VCHUNK):
        vst_ref[0, j] = vs[j * VCHUNK:(j + 1) * VCHUNK, :].T.astype(BF16)
        vwt_ref[0, j] = vw[j * VCHUNK:(j + 1) * VCHUNK, :].T.astype(BF16)

    gate_ref[0] = 1.0 / (1.0 + jnp.exp(-pkv[:, 768:896]))
    zs_ref[0] = _silu(_dot(h, wr_ref[:, 0:D_NSA])).astype(BF16)

    pcv = _dot(h, wr_ref[:, D_NSA:D_NSA + 4 * D_CONV])
    u = pcv[:, 1024:1536] * pcv[:, 0:512]
    ubuf[8:8 + TM, :] = u
    u1 = ubuf[7:7 + TM, :]
    u2 = ubuf[6:6 + TM, :]
    conv = cw_ref[0:1, :] * u2 + cw_ref[1:2, :] * u1 + cw_ref[2:3, :] * u + cb_ref[...]
    oconv_ref[0] = (pcv[:, 512:1024] * conv * _silu(pcv[:, 1536:2048])).astype(BF16)
    ubuf[0:8, :] = ubuf[TM:TM + 8, :]


def _cmp_kernel(hk_ref, hv_ref, pka_ref, pkb_ref, pva_ref, pvb_ref,
                wk1a_ref, wk1b_ref, bk1_ref, wk2_ref,
                wv1a_ref, wv1b_ref, bv1_ref, wv2_ref,
                knw_ref, rc_ref, rsa_ref, rsb_ref, mbd_ref,
                kc_ref, vct_ref, *, ncp):
    def first_layer(h_ref, p_ref, w_ref):
        acc = None
        for l in range(0, CMP_STRIDE, 2):
            lhs = jnp.concatenate([h_ref[0, pl.ds(l, ncp, stride=CMP_STRIDE), :],
                                   h_ref[0, pl.ds(l + 1, ncp, stride=CMP_STRIDE), :]], axis=1)
            lhs = (lhs + p_ref[:, l * LANES:(l + 2) * LANES]).astype(BF16)
            part = _dot(lhs, w_ref[l * LANES:(l + 2) * LANES, :])
            acc = part if acc is None else acc + part
        return acc

    def mlp(h_ref, pa_ref, pb_ref, w1a_ref, w1b_ref, b1_ref, w2_ref):
        p = first_layer(h_ref, pa_ref, w1a_ref)
        q = first_layer(h_ref, pb_ref, w1b_ref)
        pre = p + pltpu.roll(q, ncp - 1, 0) + b1_ref[...]
        return _dot(_silu(pre).astype(BF16), w2_ref[...])

    kc = mlp(hk_ref, pka_ref, pkb_ref, wk1a_ref, wk1b_ref, bk1_ref, wk2_ref)
    msk = _group_mean(kc * kc, mbd_ref[0:LANES, 0:LANES])
    kn = kc * lax.rsqrt(msk + EPS) * knw_ref[...]
    kc_ref[0] = _rope(kn, rc_ref[...], rsa_ref[...], rsb_ref[...]).astype(BF16)
    vc = mlp(hv_ref, pva_ref, pvb_ref, wv1a_ref, wv1b_ref, bv1_ref, wv2_ref)
    vct_ref[0] = vc.T.astype(BF16)


def _attn_kernel(qt_ref, kc_ref, vct_ref, ksf_ref, vst_ref, kw_ref, vwt_ref,
                 gate_ref, zs_ref, ovt_ref, o_ref, pen_sc, s_sc, cm_sc, *, ncp, bounded, tile, ncv, nblk):
    i = tile
    t0 = i * TQ
    last_pos = t0 + TQ
    rows = GROUP_HEADS * TQ
    groups = range(KV_GROUPS)

    q_all = jnp.concatenate([qt_ref[0, j] for j in range(GROUP_HEADS)], axis=1)
    frow = lax.broadcasted_iota(jnp.int32, (LANES, rows), 0)
    qg = [jnp.where(frow < HEAD_DIM, q_all, 0), jnp.where(frow >= HEAD_DIM, q_all, 0)]
    t_q = t0 + lax.broadcasted_iota(jnp.int32, (1, TQ), 1)

    def fill_where(a, mask, fill):
        return jnp.concatenate([jnp.where(mask, a[:, r * TQ:(r + 1) * TQ], fill)
                                for r in range(GROUP_HEADS)], axis=1)

    vrows = lambda v, g: v[g * HEAD_DIM:(g + 1) * HEAD_DIM, :]

    s_cmp = [_dot(kc_ref[0, 0:ncv, :], qg[g]) for g in groups]

    c_idx = lax.broadcasted_iota(jnp.int32, (ncv, 1), 0)
    mask_c = ((c_idx * CMP_STRIDE + (CMP_LEN - 1)) <= t_q) & (c_idx < ncp - 1)
    p_c, o_c = [], []
    for g in groups:
        if bounded:
            e_c = fill_where(jnp.exp2(s_cmp[g]), mask_c, 0.0)
        else:
            sm = fill_where(s_cmp[g], mask_c, NEG_INF)
            e_c = fill_where(jnp.exp2(sm - jnp.max(sm, axis=0, keepdims=True)), mask_c, 0.0)
        l_c = jnp.sum(e_c, axis=0, keepdims=True)
        p = e_c * (1.0 / jnp.where(l_c > 0.0, l_c, 1.0))
        p_c.append(p)
        o_c.append(_dot(vrows(vct_ref[0], g)[:, 0:ncv], p.astype(BF16)))

    if last_pos // SEL_BLOCK <= SEL_TOPK:
        pen_sc[...] = jnp.zeros(pen_sc.shape, BF16)
    else:
        n_idx = lax.broadcasted_iota(jnp.int32, (nblk, TQ), 0)
        tq = t0 + lax.broadcasted_iota(jnp.int32, (nblk, TQ), 1)
        cur = tq >> 6
        forced = (n_idx == 0) | (n_idx == cur) | (n_idx == cur - 1)
        visible = (n_idx << 6) <= tq
        ovt = ovt_ref[0:nblk, 0:ncv]
        left = []
        for g in groups:
            psum = p_c[g][:, 0:TQ]
            for r in range(1, GROUP_HEADS):
                psum = psum + p_c[g][:, r * TQ:(r + 1) * TQ]
            hi, mid, lo = _split3(psum)
            imp = _dot(ovt, hi) + _dot(ovt, mid) + _dot(ovt, lo)
            left.append(jnp.where(forced, TAKEN, jnp.where(visible, imp, -1.0)))
        for _ in range(SEL_TOPK - N_FORCED):
            for g in groups:
                top = jnp.max(left[g], axis=0, keepdims=True)
                first = jnp.min(jnp.where(left[g] == top, n_idx, nblk), axis=0, keepdims=True)
                left[g] = jnp.where(n_idx == first, TAKEN, left[g])
        for g in groups:
            pen_sc[g, 0:nblk, :] = jnp.where(left[g] == TAKEN, 0.0, NEG_INF).astype(BF16)
            pen_sc[g, nblk:LANES, :] = jnp.zeros((LANES - nblk, TQ), BF16)

    nwc = WINDOW // VCHUNK + TQ // VCHUNK
    cw = max(i * (TQ // VCHUNK) - WINDOW // VCHUNK, 0)
    w0 = cw * VCHUNK
    kwin = kw_ref[0, w0:w0 + nwc * VCHUNK, :]
    s_win = [_dot(kwin, qg[g]) for g in groups]

    q_aug = [jnp.concatenate([qg[g], jnp.concatenate([pen_sc[g]] * GROUP_HEADS, axis=1)], axis=0)
             for g in groups]

    def window_branch():
        vwin = jnp.concatenate([vwt_ref[0, cw + j] for j in range(nwc)], axis=1)
        kpos = w0 + lax.broadcasted_iota(jnp.int32, (nwc * VCHUNK, 1), 0)

        def masked_w(a, fill):
            if t0 < WINDOW:
                return fill_where(a, (kpos <= t_q) & (kpos > t_q - WINDOW), fill)
            lo, hi = slice(0, TQ), slice(WINDOW, WINDOW + TQ)
            return jnp.concatenate([fill_where(a[lo], kpos[lo] > t_q - WINDOW, fill), a[TQ:WINDOW],
                                    fill_where(a[hi], kpos[hi] <= t_q, fill)], axis=0)

        o_w, l_w = [], []
        for g in groups:
            if bounded:
                p_w = masked_w(jnp.exp2(s_win[g]), 0.0)
            else:
                sw = masked_w(s_win[g], NEG_INF)
                p_w = jnp.exp2(sw - jnp.max(sw, axis=0, keepdims=True))
            v_ext = jnp.concatenate([vrows(vwin, g), jnp.ones((16, nwc * VCHUNK), BF16)], axis=0)
            ol = _dot(v_ext, p_w.astype(BF16))
            o_w.append(ol[0:HEAD_DIM, :])
            l_w.append(ol[HEAD_DIM:HEAD_DIM + 1, :])
        return o_w, l_w

    n_full = t0 // TK
    width = lambda c: TK if c < n_full else last_pos - n_full * TK

    def v_chunk(c, g):
        cv = c * (TK // VCHUNK)
        return jnp.concatenate([vrows(vst_ref[0, cv + j], g) for j in range(width(c) // VCHUNK)], axis=1)

    def qk(c, slot, g):
        s = _dot(ksf_ref[0, c * TK:c * TK + width(c), :], q_aug[g])
        s_sc[slot, g, 0:width(c)] = s
        if not bounded:
            cm_sc[slot, g] = jnp.max(s, axis=0, keepdims=True)

    def sm_pv(c, slot, g, state, masked):
        m_p, l_p, acc = state
        s = s_sc[slot, g, 0:width(c)]
        if masked:
            kpos = t0 + lax.broadcasted_iota(jnp.int32, (TQ, 1), 0)
            below = [s[0:width(c) - TQ]] if width(c) > TQ else []
            s = jnp.concatenate(below + [fill_where(s[width(c) - TQ:], kpos <= t_q, NEG_INF)], axis=0)
        if bounded:
            p = jnp.exp2(s)
            return m_p, l_p + jnp.sum(p, axis=0, keepdims=True), acc + _dot(v_chunk(c, g), p.astype(BF16))
        cm = jnp.max(s, axis=0, keepdims=True) if masked else cm_sc[slot, g]
        m_n = jnp.maximum(m_p, cm)
        alpha = jnp.exp2(m_p - m_n)
        p = jnp.exp2(s - m_n)
        l_n = alpha * l_p + jnp.sum(p, axis=0, keepdims=True)
        acc = alpha * acc + _dot(v_chunk(c, g), p.astype(BF16))
        return m_n, l_n, acc

    def step(c, slot, carry):
        out = []
        for g in groups:
            qk(c + 1, 1 - slot, g)
            out.append(sm_pv(c, slot, g, carry[g], False))
        return tuple(out)

    def last(slot, carry):
        return tuple(sm_pv(n_full, slot, g, carry[g], True) for g in groups)

    for g in groups:
        qk(0, 0, g)
    o_w, l_w = window_branch()
    carry = tuple((jnp.full((1, rows), 0.0 if bounded else NEG_INF, F32), jnp.zeros((1, rows), F32),
                   jnp.zeros((HEAD_DIM, rows), F32)) for _ in groups)
    for c in range(n_full):
        carry = step(c, c % 2, carry)
    carry = last(n_full % 2, carry)
    sel_out = [(carry[g][1], carry[g][2]) for g in groups]

    g_t = gate_ref[0].T
    slabs = []
    for g in groups:
        def gate_row(br):
            base = g * (N_BRANCH * GROUP_HEADS) + br * GROUP_HEADS
            return jnp.concatenate([g_t[base + r:base + r + 1, :] for r in range(GROUP_HEADS)], axis=1)
        l_s, o_s = sel_out[g]
        og = o_c[g] * gate_row(0) + o_s * (gate_row(1) / l_s) + o_w[g] * (gate_row(2) / l_w[g])
        slabs += [og[:, r * TQ:(r + 1) * TQ] for r in range(GROUP_HEADS)]
    out_t = jnp.concatenate(slabs, axis=0)
    o_ref[0] = (out_t.T * zs_ref[0].astype(F32)).astype(BF16)


def _out_kernel(x_ref, *refs):
    on_refs, (oc_ref, w_ref, o_ref) = refs[:-3], refs[-3:]
    per_tile = TM_OUT // TQ
    for k in range(len(on_refs) // per_tile):
        @pl.when(pl.program_id(1) == k)
        def _(k=k):
            on = jnp.concatenate([r[0] for r in on_refs[k * per_tile:(k + 1) * per_tile]], axis=0)
            acc = _dot(on, w_ref[0:D_NSA, :]) + _dot(oc_ref[0], w_ref[D_NSA:D_NSA + D_CONV, :])
            o_ref[0] = x_ref[0] + acc


def _rope_tables(pos):
    inv_freq = (np.float32(ROPE_THETA) ** (-np.arange(0, ROT_DIM, 2, dtype=np.float32) / ROT_DIM)).astype(np.float32)
    ang = pos.astype(np.float32)[:, None] * inv_freq[None, :]
    cos, sin = np.cos(ang), np.sin(ang)
    n = pos.shape[0]
    ones = np.ones((n, HEAD_DIM - ROT_DIM), np.float32)
    zeros = np.zeros((n, HEAD_DIM - ROT_DIM), np.float32)
    z8 = np.zeros((n, ROT_DIM // 2), np.float32)
    c = np.concatenate([cos, cos, ones], axis=1)
    sa = np.concatenate([z8, sin, zeros], axis=1)
    sb = np.concatenate([-sin, z8, zeros], axis=1)
    return tuple(jnp.asarray(np.concatenate([t, t], axis=1), F32) for t in (c, sa, sb))


def _overlap_t(ncp):
    cs = np.arange(ncp) * CMP_STRIDE
    ce = cs + CMP_LEN
    ss = np.arange(SEL_BLOCK) * SEL_BLOCK
    se = ss + SEL_BLOCK
    ov = np.clip(np.minimum(ce[None, :], se[:, None]) - np.maximum(cs[None, :], ss[:, None]), 0, None)
    ov = ov.astype(np.float32) / CMP_LEN
    ov[:, ncp - 1] = 0.0
    return jnp.asarray(ov, BF16)


def _cmp_weights(pos, w1, b1, w2):
    half = CMP_STRIDE
    def big_w1(w):
        w = w.reshape(half, 1, HEAD_DIM, 1, CMP_HIDDEN)
        eye = jnp.eye(KV_GROUPS, dtype=F32).reshape(1, KV_GROUPS, 1, KV_GROUPS, 1)
        return (w * eye).reshape(half * KV_GROUPS * HEAD_DIM, KV_GROUPS * CMP_HIDDEN).astype(BF16)
    def big_pos(p):
        return jnp.broadcast_to(p[:, None, :], (half, KV_GROUPS, HEAD_DIM)).reshape(1, -1)
    w1a, w1b = w1[:half * HEAD_DIM], w1[half * HEAD_DIM:]
    eye2 = jnp.eye(KV_GROUPS, dtype=F32)
    w2b = (w2[None, :, None, :] * eye2[:, None, :, None]).reshape(KV_GROUPS * CMP_HIDDEN,
                                                                 KV_GROUPS * HEAD_DIM).astype(BF16)
    b1b = jnp.tile(b1, KV_GROUPS)[None, :]
    return big_pos(pos[:half]), big_pos(pos[half:]), big_w1(w1a), big_w1(w1b), b1b, w2b


def _full(shape):
    nd = len(shape)
    return pl.BlockSpec(shape, lambda *_: (0,) * nd)


def _once(shape):
    nd = len(shape)
    return pl.BlockSpec(shape, lambda *_: (0,) * nd, pipeline_mode=pl.Buffered(1))


def kernel(x, norm_w, w_in, q_norm_w, k_norm_w, cmp_k_pos, cmp_k_w1, cmp_k_b1, cmp_k_w2,
           cmp_v_pos, cmp_v_w1, cmp_v_b1, cmp_v_w2, conv_w, conv_b, w_out):
    B, S, D = x.shape
    assert norm_w.shape[0] == 1, "single layer"
    assert S % TM == 0 and S % TM_OUT == 0 and TK % TQ == 0 and TM_OUT % TQ == 0 and S // SEL_BLOCK <= SEL_BLOCK and S >= WINDOW + TQ
    ncp = S // CMP_STRIDE
    nst = S // TM
    f32 = lambda a: a.astype(F32)

    w = f32(w_in[0])
    head_order = []
    for j in range(GROUP_HEADS):
        head_order += [j, GROUP_HEADS + j]
    wq = w[:, :D_NSA].reshape(D, NSA_HEADS, HEAD_DIM)[:, jnp.array(head_order), :].reshape(D, D_NSA)
    gate_cols = []
    for g in range(KV_GROUPS):
        for br in range(N_BRANCH):
            for r in range(GROUP_HEADS):
                gate_cols.append((g * GROUP_HEADS + r) * N_BRANCH + br)
    g0 = D_NSA + 6 * LANES
    wg = w[:, g0:g0 + NSA_HEADS * N_BRANCH][:, jnp.array(gate_cols)]
    wg = jnp.pad(wg, ((0, 0), (0, LANES - NSA_HEADS * N_BRANCH)))
    rest0 = g0 + NSA_HEADS * N_BRANCH
    wq, wkv, wr = (t.astype(BF16) for t in (wq, jnp.concatenate([w[:, D_NSA:g0], wg], axis=1), w[:, rest0:]))
    n_rest = wr.shape[1]

    rc, rsa, rsb = _rope_tables(np.arange(S))
    mbd = jnp.asarray(np.kron(np.eye(4), np.full((HEAD_DIM, HEAD_DIM), 1.0 / HEAD_DIM)), BF16)
    qnw = jnp.tile(f32(q_norm_w[0]), 4)[None, :]
    knw_sw = jnp.concatenate([jnp.tile(f32(k_norm_w[0, 1]), 2), jnp.tile(f32(k_norm_w[0, 2]), 2)])[None, :]
    knw_c = jnp.tile(f32(k_norm_w[0, 0]), 2)[None, :]

    cp = pltpu.CompilerParams(dimension_semantics=("arbitrary", "arbitrary"),
                              vmem_limit_bytes=VMEM_LIMIT)
    cp_seq = pltpu.CompilerParams(dimension_semantics=("arbitrary",), vmem_limit_bytes=VMEM_LIMIT)
    row_blk = lambda n: pl.BlockSpec((1, TM, n), lambda b, s: (b, s, 0))
    tab_blk = pl.BlockSpec((TM, LANES), lambda b, s: (s, 0))
    vt_blk = pl.BlockSpec((1, TM // VCHUNK, LANES, VCHUNK), lambda b, s: (b, s, 0, 0))
    sds = jax.ShapeDtypeStruct

    (qt4, kc_raw, vc_raw, ksf, vst, kw, vwt, gates, zs, oconv) = pl.pallas_call(
        _proj_kernel,
        grid=(B, nst),
        in_specs=[row_blk(D), _full((1, D)), _once((D, D_NSA)), _once((D, 7 * LANES)),
                  _once((D, n_rest)), _full((1, 256)), _full((1, 256)),
                  tab_blk, tab_blk, tab_blk, _full((256, 256)), _full((3, D_CONV)), _full((1, D_CONV))],
        out_specs=[pl.BlockSpec((1, 4, LANES, TM), lambda b, s: (b, 0, 0, s)),
                   row_blk(LANES), row_blk(LANES), row_blk(2 * LANES), vt_blk, row_blk(LANES), vt_blk,
                   row_blk(LANES), row_blk(D_NSA), row_blk(D_CONV)],
        out_shape=[sds((B, 4, LANES, S), BF16), sds((B, S, LANES), F32), sds((B, S, LANES), F32),
                   sds((B, S, 2 * LANES), BF16), sds((B, S // VCHUNK, LANES, VCHUNK), BF16),
                   sds((B, S, LANES), BF16), sds((B, S // VCHUNK, LANES, VCHUNK), BF16),
                   sds((B, S, LANES), F32), sds((B, S, D_NSA), BF16), sds((B, S, D_CONV), BF16)],
        scratch_shapes=[pltpu.VMEM((TM + 16, D_CONV), F32)],
        compiler_params=cp,
        name="nsa_proj",
    )(x, f32(norm_w), wq, wkv, wr, qnw, knw_sw, rc, rsa, rsb, mbd, f32(conv_w[0]), f32(conv_b))

    cmp_pos = np.arange(ncp) * CMP_STRIDE + (CMP_LEN - 1)
    crc, crsa, crsb = _rope_tables(cmp_pos)
    kpa, kpb, kw1a, kw1b, kb1, kw2 = _cmp_weights(f32(cmp_k_pos[0]), f32(cmp_k_w1[0]),
                                                  f32(cmp_k_b1[0]), f32(cmp_k_w2[0]))
    vpa, vpb, vw1a, vw1b, vb1, vw2 = _cmp_weights(f32(cmp_v_pos[0]), f32(cmp_v_w1[0]),
                                                  f32(cmp_v_b1[0]), f32(cmp_v_w2[0]))
    flat = CMP_STRIDE * LANES
    h_blk = pl.BlockSpec((1, S, LANES), lambda b: (b, 0, 0))
    kc, vct = pl.pallas_call(
        functools.partial(_cmp_kernel, ncp=ncp),
        grid=(B,),
        in_specs=[h_blk, h_blk] + [_full((1, flat))] * 4
                 + [_full((flat, 256)), _full((flat, 256)), _full((1, 256)), _full((256, LANES))] * 2
                 + [_full((1, LANES)), _full((ncp, LANES)), _full((ncp, LANES)), _full((ncp, LANES)),
                    _full((256, 256))],
        out_specs=[pl.BlockSpec((1, ncp, LANES), lambda b: (b, 0, 0)),
                   pl.BlockSpec((1, LANES, ncp), lambda b: (b, 0, 0))],
        out_shape=[sds((B, ncp, LANES), BF16), sds((B, LANES, ncp), BF16)],
        compiler_params=cp_seq,
        name="nsa_compress",
    )(kc_raw, vc_raw, kpa, kpb, vpa, vpb,
      kw1a, kw1b, kb1, kw2, vw1a, vw1b, vb1, vw2, knw_c, crc, crsa, crsb, mbd)

    score_bound = (HEAD_DIM * Q_SCALE) * jnp.max(jnp.abs(f32(q_norm_w))) * jnp.max(jnp.abs(f32(k_norm_w)))
    n_tiles = S // TQ

    def attention(bounded, r):
        last_pos = (r + 1) * TQ
        win_pos = max(last_pos, WINDOW + TQ)
        nblk = min(SEL_BLOCK, -(-last_pos // (8 * SEL_BLOCK)) * 8)
        return pl.pallas_call(
            functools.partial(_attn_kernel, ncp=ncp, bounded=bounded, tile=r,
                              ncv=min(ncp, last_pos // CMP_STRIDE), nblk=nblk),
            grid=(B,),
            in_specs=[pl.BlockSpec((1, 4, LANES, TQ), lambda b: (b, 0, 0, r)),
                      pl.BlockSpec((1, ncp, LANES), lambda b: (b, 0, 0)),
                      pl.BlockSpec((1, LANES, ncp), lambda b: (b, 0, 0)),
                      pl.BlockSpec((1, last_pos, 2 * LANES), lambda b: (b, 0, 0)),
                      pl.BlockSpec((1, last_pos // VCHUNK, LANES, VCHUNK), lambda b: (b, 0, 0, 0)),
                      pl.BlockSpec((1, win_pos, LANES), lambda b: (b, 0, 0)),
                      pl.BlockSpec((1, win_pos // VCHUNK, LANES, VCHUNK), lambda b: (b, 0, 0, 0)),
                      pl.BlockSpec((1, TQ, LANES), lambda b: (b, r, 0)),
                      pl.BlockSpec((1, TQ, D_NSA), lambda b: (b, r, 0)),
                      pl.BlockSpec((SEL_BLOCK, ncp), lambda b: (0, 0))],
            out_specs=pl.BlockSpec((1, TQ, D_NSA), lambda b: (b, 0, 0)),
            out_shape=sds((B, TQ, D_NSA), BF16),
            scratch_shapes=[pltpu.VMEM((KV_GROUPS, LANES, TQ), BF16),
                            pltpu.VMEM((2, KV_GROUPS, TK, GROUP_HEADS * TQ), F32),
                            pltpu.VMEM((2, KV_GROUPS, 1, GROUP_HEADS * TQ), F32)],
            compiler_params=cp_seq,
            name=("nsa_attn_bounded_r%d" if bounded else "nsa_attn_online_r%d") % r,
        )

    attn_args = (qt4, kc, vct, ksf, vst, kw, vwt, gates, zs, _overlap_t(ncp))
    all_tiles = lambda bounded: (lambda *a: tuple(attention(bounded, r)(*a) for r in range(n_tiles)))
    o_nsa = lax.cond(score_bound <= MAX_SAFE_SCORE, all_tiles(True), all_tiles(False), *attn_args)

    out_blk = lambda n: pl.BlockSpec((1, TM_OUT, n), lambda b, s: (b, s, 0))
    on_blk = lambda k: pl.BlockSpec((1, TQ, D_NSA), lambda b, s: (
        jnp.where(s >= k // (TM_OUT // TQ), b, jnp.maximum(b - 1, 0)), 0, 0))
    out = pl.pallas_call(
        _out_kernel,
        grid=(B, S // TM_OUT),
        in_specs=[out_blk(D)] + [on_blk(k) for k in range(n_tiles)]
                 + [out_blk(D_CONV), _full((D_NSA + D_CONV, D))],
        out_specs=out_blk(D),
        out_shape=sds((B, S, D), x.dtype),
        compiler_params=cp,
        name="nsa_out",
    )(x, *o_nsa, oconv, f32(w_out[0]).astype(BF16))
    return out
```

```python
import functools

import numpy as np
import jax
import jax.numpy as jnp
from jax import lax
from jax.experimental import pallas as pl
from jax.experimental.pallas import tpu as pltpu

LANES = 128
HEAD_DIM = 64
NSA_HEADS = 8
KV_GROUPS = 2
GROUP_HEADS = NSA_HEADS // KV_GROUPS
D_NSA = NSA_HEADS * HEAD_DIM
D_CONV = 512
N_BRANCH = 3
ROT_DIM = HEAD_DIM // 4
ROPE_THETA = 500000.0
CMP_LEN = 32
CMP_STRIDE = 16
CMP_HIDDEN = 2 * HEAD_DIM
SEL_BLOCK = 64
SEL_TOPK = 16
WINDOW = 512
EPS = 1e-6
NEG_INF = -1e30
N_FORCED = 3
TAKEN = -2.0
SCALE = HEAD_DIM ** -0.5
Q_SCALE = SCALE * float(np.log2(np.e))

TM = 1024
TM_OUT = 1024
TQ = 256
TK = 512
VCHUNK = 128
VMEM_LIMIT = 56 * 1024 * 1024
MAX_SAFE_SCORE = 50.0

BF16 = jnp.bfloat16
F32 = jnp.float32


def _dot(a, b):
    return jnp.dot(a, b, preferred_element_type=F32)


def _split3(a):
    hi = a.astype(BF16)
    r1 = a - hi.astype(F32)
    mid = r1.astype(BF16)
    lo = (r1 - mid.astype(F32)).astype(BF16)
    return hi, mid, lo


def _group_mean(sq, mbd):
    return _dot(sq.astype(BF16), mbd)


def _rope(xn, c, sa, sb):
    return xn * c + pltpu.roll(xn, 8, 1) * sa + pltpu.roll(xn, LANES - 8, 1) * sb


def _silu(z):
    return z * (1.0 / (1.0 + jnp.exp(-z)))


def _proj_kernel(x_ref, nw_ref, wq_ref, wkv_ref, wr_ref, qnw_ref, knw_ref,
                 rc_ref, rsa_ref, rsb_ref, mbd_ref, cw_ref, cb_ref,
                 qt_ref, kc_ref, vc_ref, ksf_ref, vst_ref, kw_ref, vwt_ref,
                 gate_ref, zs_ref, oconv_ref, ubuf):
    si = pl.program_id(1)

    @pl.when(si == 0)
    def _():
        ubuf[0:8, :] = jnp.zeros((8, D_CONV), F32)

    x = x_ref[0]
    ms = jnp.mean(x * x, axis=-1, keepdims=True)
    h = (x * lax.rsqrt(ms + EPS) * nw_ref[...]).astype(BF16)

    rc, rsa, rsb = rc_ref[...], rsa_ref[...], rsb_ref[...]
    mbd = mbd_ref[...]

    pq = _dot(h, wq_ref[...])
    for pair in range(2):
        blk = pq[:, pair * 256:(pair + 1) * 256]
        msq = _group_mean(blk * blk, mbd)
        qn = blk * lax.rsqrt(msq + EPS) * qnw_ref[...]
        for half in range(2):
            t = qn[:, half * LANES:(half + 1) * LANES]
            qt_ref[0, pair * 2 + half] = (_rope(t, rc, rsa, rsb) * Q_SCALE).T.astype(BF16)

    pkv = _dot(h, wkv_ref[...])
    kc_ref[0] = pkv[:, 0:128]
    vc_ref[0] = pkv[:, 128:256]
    ksw = jnp.concatenate([pkv[:, 256:384], pkv[:, 512:640]], axis=1)
    msk = _group_mean(ksw * ksw, mbd)
    kn = ksw * lax.rsqrt(msk + EPS) * knw_ref[...]
    ks = _rope(kn[:, 0:128], rc, rsa, rsb)
    kw = _rope(kn[:, 128:256], rc, rsa, rsb)
    row = lax.broadcasted_iota(jnp.int32, (TM, LANES), 0) + si * TM
    lane = lax.broadcasted_iota(jnp.int32, (TM, LANES), 1)
    onehot = jnp.where(lane == (row >> 6), 1.0, 0.0)
    ksf_ref[0, :, 0:128] = ks.astype(BF16)
    ksf_ref[0, :, 128:256] = onehot.astype(BF16)
    kw_ref[0] = kw.astype(BF16)
    vs = pkv[:, 384:512]
    vw = pkv[:, 640:768]
    for j in range(TM // VCHUNK):
        vst_ref[0, j] = vs[j * VCHUNK:(j + 1) * VCHUNK, :].T.astype(BF16)
        vwt_ref[0, j] = vw[j * VCHUNK:(j + 1) * VCHUNK, :].T.astype(BF16)

    gate_ref[0] = 1.0 / (1.0 + jnp.exp(-pkv[:, 768:896]))
    zs_ref[0] = _silu(_dot(h, wr_ref[:, 0:D_NSA])).astype(BF16)

    pcv = _dot(h, wr_ref[:, D_NSA:D_NSA + 4 * D_CONV])
    u = pcv[:, 1024:1536] * pcv[:, 0:512]
    ubuf[8:8 + TM, :] = u
    u1 = ubuf[7:7 + TM, :]
    u2 = ubuf[6:6 + TM, :]
    conv = cw_ref[0:1, :] * u2 + cw_ref[1:2, :] * u1 + cw_ref[2:3, :] * u + cb_ref[...]
    oconv_ref[0] = (pcv[:, 512:1024] * conv * _silu(pcv[:, 1536:2048])).astype(BF16)
    ubuf[0:8, :] = ubuf[TM:TM + 8, :]


def _cmp_kernel(hk_ref, hv_ref, pka_ref, pkb_ref, pva_ref, pvb_ref,
                wk1a_ref, wk1b_ref, bk1_ref, wk2_ref,
                wv1a_ref, wv1b_ref, bv1_ref, wv2_ref,
                knw_ref, rc_ref, rsa_ref, rsb_ref, mbd_ref,
                kc_ref, vct_ref, *, ncp):
    def first_layer(h_ref, p_ref, w_ref):
        acc = None
        for l in range(0, CMP_STRIDE, 2):
            lhs = jnp.concatenate([h_ref[0, pl.ds(l, ncp, stride=CMP_STRIDE), :],
                                   h_ref[0, pl.ds(l + 1, ncp, stride=CMP_STRIDE), :]], axis=1)
            lhs = (lhs + p_ref[:, l * LANES:(l + 2) * LANES]).astype(BF16)
            part = _dot(lhs, w_ref[l * LANES:(l + 2) * LANES, :])
            acc = part if acc is None else acc + part
        return acc

    def mlp(h_ref, pa_ref, pb_ref, w1a_ref, w1b_ref, b1_ref, w2_ref):
        p = first_layer(h_ref, pa_ref, w1a_ref)
        q = first_layer(h_ref, pb_ref, w1b_ref)
        pre = p + pltpu.roll(q, ncp - 1, 0) + b1_ref[...]
        return _dot(_silu(pre).astype(BF16), w2_ref[...])

    kc = mlp(hk_ref, pka_ref, pkb_ref, wk1a_ref, wk1b_ref, bk1_ref, wk2_ref)
    msk = _group_mean(kc * kc, mbd_ref[0:LANES, 0:LANES])
    kn = kc * lax.rsqrt(msk + EPS) * knw_ref[...]
    kc_ref[0] = _rope(kn, rc_ref[...], rsa_ref[...], rsb_ref[...]).astype(BF16)
    vc = mlp(hv_ref, pva_ref, pvb_ref, wv1a_ref, wv1b_ref, bv1_ref, wv2_ref)
    vct_ref[0] = vc.T.astype(BF16)


def _attn_kernel(qt_ref, kc_ref, vct_ref, ksf_ref, vst_ref, kw_ref, vwt_ref,
                 gate_ref, zs_ref, ovt_ref, o_ref, pen_sc, s_sc, cm_sc, *, ncp, bounded, tile, ncv, nblk):
    i = tile
    t0 = i * TQ
    last_pos = t0 + TQ
    rows = GROUP_HEADS * TQ
    groups = range(KV_GROUPS)

    q_all = jnp.concatenate([qt_ref[0, j] for j in range(GROUP_HEADS)], axis=1)
    frow = lax.broadcasted_iota(jnp.int32, (LANES, rows), 0)
    qg = [jnp.where(frow < HEAD_DIM, q_all, 0), jnp.where(frow >= HEAD_DIM, q_all, 0)]
    t_q = t0 + lax.broadcasted_iota(jnp.int32, (1, TQ), 1)

    def fill_where(a, mask, fill):
        return jnp.concatenate([jnp.where(mask, a[:, r * TQ:(r + 1) * TQ], fill)
                                for r in range(GROUP_HEADS)], axis=1)

    vrows = lambda v, g: v[g * HEAD_DIM:(g + 1) * HEAD_DIM, :]

    s_cmp = [_dot(kc_ref[0, 0:ncv, :], qg[g]) for g in groups]

    c_idx = lax.broadcasted_iota(jnp.int32, (ncv, 1), 0)
    mask_c = ((c_idx * CMP_STRIDE + (CMP_LEN - 1)) <= t_q) & (c_idx < ncp - 1)
    n_open = max(0, min((t0 - (CMP_LEN - 1)) // CMP_STRIDE + 1, ncp - 1)) // 8 * 8

    def masked_c(a, fill):
        tail = fill_where(a[n_open:], mask_c[n_open:], fill)
        return jnp.concatenate([a[0:n_open], tail], axis=0) if n_open else tail

    p_c, o_c = [], []
    for g in groups:
        if bounded:
            e_c = masked_c(jnp.exp2(s_cmp[g]), 0.0)
        else:
            sm = masked_c(s_cmp[g], NEG_INF)
            e_c = masked_c(jnp.exp2(sm - jnp.max(sm, axis=0, keepdims=True)), 0.0)
        l_c = jnp.sum(e_c, axis=0, keepdims=True)
        p = e_c * (1.0 / jnp.where(l_c > 0.0, l_c, 1.0))
        p_c.append(p)
        o_c.append(_dot(vrows(vct_ref[0], g)[:, 0:ncv], p.astype(BF16)))

    if last_pos // SEL_BLOCK <= SEL_TOPK:
        pen_sc[...] = jnp.zeros(pen_sc.shape, BF16)
    else:
        n_idx = lax.broadcasted_iota(jnp.int32, (nblk, TQ), 0)
        tq = t0 + lax.broadcasted_iota(jnp.int32, (nblk, TQ), 1)
        cur = tq >> 6
        forced = (n_idx == 0) | (n_idx == cur) | (n_idx == cur - 1)
        visible = (n_idx << 6) <= tq
        ovt = ovt_ref[0:nblk, 0:ncv]
        left = []
        for g in groups:
            psum = p_c[g][:, 0:TQ]
            for r in range(1, GROUP_HEADS):
                psum = psum + p_c[g][:, r * TQ:(r + 1) * TQ]
            hi, mid, lo = _split3(psum)
            imp = _dot(ovt, hi) + _dot(ovt, mid) + _dot(ovt, lo)
            left.append(jnp.where(forced, TAKEN, jnp.where(visible, imp, -1.0)))
        for _ in range(SEL_TOPK - N_FORCED):
            for g in groups:
                top = jnp.max(left[g], axis=0, keepdims=True)
                first = jnp.min(jnp.where(left[g] == top, n_idx, nblk), axis=0, keepdims=True)
                left[g] = jnp.where(n_idx == first, TAKEN, left[g])
        for g in groups:
            pen_sc[g, 0:nblk, :] = jnp.where(left[g] == TAKEN, 0.0, NEG_INF).astype(BF16)
            pen_sc[g, nblk:LANES, :] = jnp.zeros((LANES - nblk, TQ), BF16)

    nwc = WINDOW // VCHUNK + TQ // VCHUNK
    cw = max(i * (TQ // VCHUNK) - WINDOW // VCHUNK, 0)
    w0 = cw * VCHUNK
    kwin = kw_ref[0, w0:w0 + nwc * VCHUNK, :]
    s_win = [_dot(kwin, qg[g]) for g in groups]

    q_aug = [jnp.concatenate([qg[g], jnp.concatenate([pen_sc[g]] * GROUP_HEADS, axis=1)], axis=0)
             for g in groups]

    def window_branch():
        vwin = jnp.concatenate([vwt_ref[0, cw + j] for j in range(nwc)], axis=1)
        kpos = w0 + lax.broadcasted_iota(jnp.int32, (nwc * VCHUNK, 1), 0)

        def masked_w(a, fill):
            if t0 < WINDOW:
                return fill_where(a, (kpos <= t_q) & (kpos > t_q - WINDOW), fill)
            lo, hi = slice(0, TQ), slice(WINDOW, WINDOW + TQ)
            return jnp.concatenate([fill_where(a[lo], kpos[lo] > t_q - WINDOW, fill), a[TQ:WINDOW],
                                    fill_where(a[hi], kpos[hi] <= t_q, fill)], axis=0)

        o_w, l_w = [], []
        for g in groups:
            if bounded:
                p_w = masked_w(jnp.exp2(s_win[g]), 0.0)
            else:
                sw = masked_w(s_win[g], NEG_INF)
                p_w = jnp.exp2(sw - jnp.max(sw, axis=0, keepdims=True))
            v_ext = jnp.concatenate([vrows(vwin, g), jnp.ones((16, nwc * VCHUNK), BF16)], axis=0)
            ol = _dot(v_ext, p_w.astype(BF16))
            o_w.append(ol[0:HEAD_DIM, :])
            l_w.append(ol[HEAD_DIM:HEAD_DIM + 1, :])
        return o_w, l_w

    n_full = t0 // TK
    width = lambda c: TK if c < n_full else last_pos - n_full * TK

    def v_chunk(c, g):
        cv = c * (TK // VCHUNK)
        return jnp.concatenate([vrows(vst_ref[0, cv + j], g) for j in range(width(c) // VCHUNK)], axis=1)

    def qk(c, slot, g):
        s = _dot(ksf_ref[0, c * TK:c * TK + width(c), :], q_aug[g])
        s_sc[slot, g, 0:width(c)] = s
        if not bounded:
            cm_sc[slot, g] = jnp.max(s, axis=0, keepdims=True)

    def sm_pv(c, slot, g, state, masked):
        m_p, l_p, acc = state
        s = s_sc[slot, g, 0:width(c)]
        if masked:
            kpos = t0 + lax.broadcasted_iota(jnp.int32, (TQ, 1), 0)
            below = [s[0:width(c) - TQ]] if width(c) > TQ else []
            s = jnp.concatenate(below + [fill_where(s[width(c) - TQ:], kpos <= t_q, NEG_INF)], axis=0)
        if bounded:
            p = jnp.exp2(s)
            return m_p, l_p + jnp.sum(p, axis=0, keepdims=True), acc + _dot(v_chunk(c, g), p.astype(BF16))
        cm = jnp.max(s, axis=0, keepdims=True) if masked else cm_sc[slot, g]
        m_n = jnp.maximum(m_p, cm)
        alpha = jnp.exp2(m_p - m_n)
        p = jnp.exp2(s - m_n)
        l_n = alpha * l_p + jnp.sum(p, axis=0, keepdims=True)
        acc = alpha * acc + _dot(v_chunk(c, g), p.astype(BF16))
        return m_n, l_n, acc

    def step(c, slot, carry):
        out = []
        for g in groups:
            qk(c + 1, 1 - slot, g)
            out.append(sm_pv(c, slot, g, carry[g], False))
        return tuple(out)

    def last(slot, carry):
        return tuple(sm_pv(n_full, slot, g, carry[g], True) for g in groups)

    for g in groups:
        qk(0, 0, g)
    o_w, l_w = window_branch()
    carry = tuple((jnp.full((1, rows), 0.0 if bounded else NEG_INF, F32), jnp.zeros((1, rows), F32),
                   jnp.zeros((HEAD_DIM, rows), F32)) for _ in groups)
    for c in range(n_full):
        carry = step(c, c % 2, carry)
    carry = last(n_full % 2, carry)
    sel_out = [(carry[g][1], carry[g][2]) for g in groups]

    g_t = gate_ref[0].T
    slabs = []
    for g in groups:
        def gate_row(br):
            base = g * (N_BRANCH * GROUP_HEADS) + br * GROUP_HEADS
            return jnp.concatenate([g_t[base + r:base + r + 1, :] for r in range(GROUP_HEADS)], axis=1)
        l_s, o_s = sel_out[g]
        og = o_c[g] * gate_row(0) + o_s * (gate_row(1) / l_s) + o_w[g] * (gate_row(2) / l_w[g])
        slabs += [og[:, r * TQ:(r + 1) * TQ] for r in range(GROUP_HEADS)]
    out_t = jnp.concatenate(slabs, axis=0)
    o_ref[0] = (out_t.T * zs_ref[0].astype(F32)).astype(BF16)


def _out_kernel(x_ref, *refs):
    on_refs, (oc_ref, w_ref, o_ref) = refs[:-3], refs[-3:]
    per_tile = TM_OUT // TQ
    for k in range(len(on_refs) // per_tile):
        @pl.when(pl.program_id(1) == k)
        def _(k=k):
            on = jnp.concatenate([r[0] for r in on_refs[k * per_tile:(k + 1) * per_tile]], axis=0)
            acc = _dot(on, w_ref[0:D_NSA, :]) + _dot(oc_ref[0], w_ref[D_NSA:D_NSA + D_CONV, :])
            o_ref[0] = x_ref[0] + acc


def _rope_tables(pos):
    inv_freq = (np.float32(ROPE_THETA) ** (-np.arange(0, ROT_DIM, 2, dtype=np.float32) / ROT_DIM)).astype(np.float32)
    ang = pos.astype(np.float32)[:, None] * inv_freq[None, :]
    cos, sin = np.cos(ang), np.sin(ang)
    n = pos.shape[0]
    ones = np.ones((n, HEAD_DIM - ROT_DIM), np.float32)
    zeros = np.zeros((n, HEAD_DIM - ROT_DIM), np.float32)
    z8 = np.zeros((n, ROT_DIM // 2), np.float32)
    c = np.concatenate([cos, cos, ones], axis=1)
    sa = np.concatenate([z8, sin, zeros], axis=1)
    sb = np.concatenate([-sin, z8, zeros], axis=1)
    return tuple(jnp.asarray(np.concatenate([t, t], axis=1), F32) for t in (c, sa, sb))


def _overlap_t(ncp):
    cs = np.arange(ncp) * CMP_STRIDE
    ce = cs + CMP_LEN
    ss = np.arange(SEL_BLOCK) * SEL_BLOCK
    se = ss + SEL_BLOCK
    ov = np.clip(np.minimum(ce[None, :], se[:, None]) - np.maximum(cs[None, :], ss[:, None]), 0, None)
    ov = ov.astype(np.float32) / CMP_LEN
    ov[:, ncp - 1] = 0.0
    return jnp.asarray(ov, BF16)


def _cmp_weights(pos, w1, b1, w2):
    half = CMP_STRIDE
    def big_w1(w):
        w = w.reshape(half, 1, HEAD_DIM, 1, CMP_HIDDEN)
        eye = jnp.eye(KV_GROUPS, dtype=F32).reshape(1, KV_GROUPS, 1, KV_GROUPS, 1)
        return (w * eye).reshape(half * KV_GROUPS * HEAD_DIM, KV_GROUPS * CMP_HIDDEN).astype(BF16)
    def big_pos(p):
        return jnp.broadcast_to(p[:, None, :], (half, KV_GROUPS, HEAD_DIM)).reshape(1, -1)
    w1a, w1b = w1[:half * HEAD_DIM], w1[half * HEAD_DIM:]
    eye2 = jnp.eye(KV_GROUPS, dtype=F32)
    w2b = (w2[None, :, None, :] * eye2[:, None, :, None]).reshape(KV_GROUPS * CMP_HIDDEN,
                                                                 KV_GROUPS * HEAD_DIM).astype(BF16)
    b1b = jnp.tile(b1, KV_GROUPS)[None, :]
    return big_pos(pos[:half]), big_pos(pos[half:]), big_w1(w1a), big_w1(w1b), b1b, w2b


def _full(shape):
    nd = len(shape)
    return pl.BlockSpec(shape, lambda *_: (0,) * nd)


def _once(shape):
    nd = len(shape)
    return pl.BlockSpec(shape, lambda *_: (0,) * nd, pipeline_mode=pl.Buffered(1))


def kernel(x, norm_w, w_in, q_norm_w, k_norm_w, cmp_k_pos, cmp_k_w1, cmp_k_b1, cmp_k_w2,
           cmp_v_pos, cmp_v_w1, cmp_v_b1, cmp_v_w2, conv_w, conv_b, w_out):
    B, S, D = x.shape
    assert norm_w.shape[0] == 1, "single layer"
    assert S % TM == 0 and S % TM_OUT == 0 and TK % TQ == 0 and TM_OUT % TQ == 0 and S // SEL_BLOCK <= SEL_BLOCK and S >= WINDOW + TQ
    ncp = S // CMP_STRIDE
    nst = S // TM
    f32 = lambda a: a.astype(F32)

    w = w_in[0].astype(BF16)
    head_order = []
    for j in range(GROUP_HEADS):
        head_order += [j, GROUP_HEADS + j]
    wq = w[:, :D_NSA].reshape(D, NSA_HEADS, HEAD_DIM)[:, jnp.array(head_order), :].reshape(D, D_NSA)
    gate_cols = []
    for g in range(KV_GROUPS):
        for br in range(N_BRANCH):
            for r in range(GROUP_HEADS):
                gate_cols.append((g * GROUP_HEADS + r) * N_BRANCH + br)
    g0 = D_NSA + 6 * LANES
    wg = w[:, g0:g0 + NSA_HEADS * N_BRANCH][:, jnp.array(gate_cols)]
    wg = jnp.pad(wg, ((0, 0), (0, LANES - NSA_HEADS * N_BRANCH)))
    rest0 = g0 + NSA_HEADS * N_BRANCH
    wkv, wr = jnp.concatenate([w[:, D_NSA:g0], wg], axis=1), w[:, rest0:]
    n_rest = wr.shape[1]

    rc, rsa, rsb = _rope_tables(np.arange(S))
    mbd = jnp.asarray(np.kron(np.eye(4), np.full((HEAD_DIM, HEAD_DIM), 1.0 / HEAD_DIM)), BF16)
    qnw = jnp.tile(f32(q_norm_w[0]), 4)[None, :]
    knw_sw = jnp.concatenate([jnp.tile(f32(k_norm_w[0, 1]), 2), jnp.tile(f32(k_norm_w[0, 2]), 2)])[None, :]
    knw_c = jnp.tile(f32(k_norm_w[0, 0]), 2)[None, :]

    cp = pltpu.CompilerParams(dimension_semantics=("arbitrary", "arbitrary"),
                              vmem_limit_bytes=VMEM_LIMIT)
    cp_seq = pltpu.CompilerParams(dimension_semantics=("arbitrary",), vmem_limit_bytes=VMEM_LIMIT)
    row_blk = lambda n: pl.BlockSpec((1, TM, n), lambda b, s: (b, s, 0))
    tab_blk = pl.BlockSpec((TM, LANES), lambda b, s: (s, 0))
    vt_blk = pl.BlockSpec((1, TM // VCHUNK, LANES, VCHUNK), lambda b, s: (b, s, 0, 0))
    sds = jax.ShapeDtypeStruct

    (qt4, kc_raw, vc_raw, ksf, vst, kw, vwt, gates, zs, oconv) = pl.pallas_call(
        _proj_kernel,
        grid=(B, nst),
        in_specs=[row_blk(D), _full((1, D)), _once((D, D_NSA)), _once((D, 7 * LANES)),
                  _once((D, n_rest)), _full((1, 256)), _full((1, 256)),
                  tab_blk, tab_blk, tab_blk, _full((256, 256)), _full((3, D_CONV)), _full((1, D_CONV))],
        out_specs=[pl.BlockSpec((1, 4, LANES, TM), lambda b, s: (b, 0, 0, s)),
                   row_blk(LANES), row_blk(LANES), row_blk(2 * LANES), vt_blk, row_blk(LANES), vt_blk,
                   row_blk(LANES), row_blk(D_NSA), row_blk(D_CONV)],
        out_shape=[sds((B, 4, LANES, S), BF16), sds((B, S, LANES), F32), sds((B, S, LANES), F32),
                   sds((B, S, 2 * LANES), BF16), sds((B, S // VCHUNK, LANES, VCHUNK), BF16),
                   sds((B, S, LANES), BF16), sds((B, S // VCHUNK, LANES, VCHUNK), BF16),
                   sds((B, S, LANES), F32), sds((B, S, D_NSA), BF16), sds((B, S, D_CONV), BF16)],
        scratch_shapes=[pltpu.VMEM((TM + 16, D_CONV), F32)],
        compiler_params=cp,
        name="nsa_proj",
    )(x, f32(norm_w), wq, wkv, wr, qnw, knw_sw, rc, rsa, rsb, mbd, f32(conv_w[0]), f32(conv_b))

    cmp_pos = np.arange(ncp) * CMP_STRIDE + (CMP_LEN - 1)
    crc, crsa, crsb = _rope_tables(cmp_pos)
    kpa, kpb, kw1a, kw1b, kb1, kw2 = _cmp_weights(f32(cmp_k_pos[0]), f32(cmp_k_w1[0]),
                                                  f32(cmp_k_b1[0]), f32(cmp_k_w2[0]))
    vpa, vpb, vw1a, vw1b, vb1, vw2 = _cmp_weights(f32(cmp_v_pos[0]), f32(cmp_v_w1[0]),
                                                  f32(cmp_v_b1[0]), f32(cmp_v_w2[0]))
    flat = CMP_STRIDE * LANES
    h_blk = pl.BlockSpec((1, S, LANES), lambda b: (b, 0, 0))
    kc, vct = pl.pallas_call(
        functools.partial(_cmp_kernel, ncp=ncp),
        grid=(B,),
        in_specs=[h_blk, h_blk] + [_full((1, flat))] * 4
                 + [_full((flat, 256)), _full((flat, 256)), _full((1, 256)), _full((256, LANES))] * 2
                 + [_full((1, LANES)), _full((ncp, LANES)), _full((ncp, LANES)), _full((ncp, LANES)),
                    _full((256, 256))],
        out_specs=[pl.BlockSpec((1, ncp, LANES), lambda b: (b, 0, 0)),
                   pl.BlockSpec((1, LANES, ncp), lambda b: (b, 0, 0))],
        out_shape=[sds((B, ncp, LANES), BF16), sds((B, LANES, ncp), BF16)],
        compiler_params=cp_seq,
        name="nsa_compress",
    )(kc_raw, vc_raw, kpa, kpb, vpa, vpb,
      kw1a, kw1b, kb1, kw2, vw1a, vw1b, vb1, vw2, knw_c, crc, crsa, crsb, mbd)

    score_bound = (HEAD_DIM * Q_SCALE) * jnp.max(jnp.abs(f32(q_norm_w))) * jnp.max(jnp.abs(f32(k_norm_w)))
    n_tiles = S // TQ

    def attention(bounded, r):
        last_pos = (r + 1) * TQ
        win_pos = max(last_pos, WINDOW + TQ)
        nblk = min(SEL_BLOCK, -(-last_pos // (8 * SEL_BLOCK)) * 8)
        return pl.pallas_call(
            functools.partial(_attn_kernel, ncp=ncp, bounded=bounded, tile=r,
                              ncv=min(ncp, last_pos // CMP_STRIDE), nblk=nblk),
            grid=(B,),
            in_specs=[pl.BlockSpec((1, 4, LANES, TQ), lambda b: (b, 0, 0, r)),
                      pl.BlockSpec((1, ncp, LANES), lambda b: (b, 0, 0)),
                      pl.BlockSpec((1, LANES, ncp), lambda b: (b, 0, 0)),
                      pl.BlockSpec((1, last_pos, 2 * LANES), lambda b: (b, 0, 0)),
                      pl.BlockSpec((1, last_pos // VCHUNK, LANES, VCHUNK), lambda b: (b, 0, 0, 0)),
                      pl.BlockSpec((1, win_pos, LANES), lambda b: (b, 0, 0)),
                      pl.BlockSpec((1, win_pos // VCHUNK, LANES, VCHUNK), lambda b: (b, 0, 0, 0)),
                      pl.BlockSpec((1, TQ, LANES), lambda b: (b, r, 0)),
                      pl.BlockSpec((1, TQ, D_NSA), lambda b: (b, r, 0)),
                      pl.BlockSpec((SEL_BLOCK, ncp), lambda b: (0, 0))],
            out_specs=pl.BlockSpec((1, TQ, D_NSA), lambda b: (b, 0, 0)),
            out_shape=sds((B, TQ, D_NSA), BF16),
            scratch_shapes=[pltpu.VMEM((KV_GROUPS, LANES, TQ), BF16),
                            pltpu.VMEM((2, KV_GROUPS, TK, GROUP_HEADS * TQ), F32),
                            pltpu.VMEM((2, KV_GROUPS, 1, GROUP_HEADS * TQ), F32)],
            compiler_params=cp_seq,
            name=("nsa_attn_bounded_r%d" if bounded else "nsa_attn_online_r%d") % r,
        )

    attn_args = (qt4, kc, vct, ksf, vst, kw, vwt, gates, zs, _overlap_t(ncp))
    all_tiles = lambda bounded: (lambda *a: tuple(attention(bounded, r)(*a) for r in range(n_tiles)))
    o_nsa = lax.cond(score_bound <= MAX_SAFE_SCORE, all_tiles(True), all_tiles(False), *attn_args)

    out_blk = lambda n: pl.BlockSpec((1, TM_OUT, n), lambda b, s: (b, s, 0))
    on_blk = lambda k: pl.BlockSpec((1, TQ, D_NSA), lambda b, s: (
        jnp.where(s >= k // (TM_OUT // TQ), b, jnp.maximum(b - 1, 0)), 0, 0))
    out = pl.pallas_call(
        _out_kernel,
        grid=(B, S // TM_OUT),
        in_specs=[out_blk(D)] + [on_blk(k) for k in range(n_tiles)]
                 + [out_blk(D_CONV), _full((D_NSA + D_CONV, D))],
        out_specs=out_blk(D),
        out_shape=sds((B, S, D), x.dtype),
        compiler_params=cp,
        name="nsa_out",
    )(x, *o_nsa, oconv, f32(w_out[0]).astype(BF16))
    return out
```

```python
import functools

import numpy as np
import jax
import jax.numpy as jnp
from jax import lax
from jax.experimental import pallas as pl
from jax.experimental.pallas import tpu as pltpu

LANES = 128
HEAD_DIM = 64
NSA_HEADS = 8
KV_GROUPS = 2
GROUP_HEADS = NSA_HEADS // KV_GROUPS
D_NSA = NSA_HEADS * HEAD_DIM
D_CONV = 512
N_BRANCH = 3
ROT_DIM = HEAD_DIM // 4
ROPE_THETA = 500000.0
CMP_LEN = 32
CMP_STRIDE = 16
CMP_HIDDEN = 2 * HEAD_DIM
SEL_BLOCK = 64
SEL_TOPK = 16
WINDOW = 512
EPS = 1e-6
NEG_INF = -1e30
N_FORCED = 3
TAKEN = -2.0
SCALE = HEAD_DIM ** -0.5
Q_SCALE = SCALE * float(np.log2(np.e))

TM = 1024
TM_OUT = 1024
TQ = 256
TK = 256
VCHUNK = 128
VMEM_LIMIT = 56 * 1024 * 1024
MAX_SAFE_SCORE = 50.0

BF16 = jnp.bfloat16
F32 = jnp.float32


def _dot(a, b):
    return jnp.dot(a, b, preferred_element_type=F32)


def _split3(a):
    hi = a.astype(BF16)
    r1 = a - hi.astype(F32)
    mid = r1.astype(BF16)
    lo = (r1 - mid.astype(F32)).astype(BF16)
    return hi, mid, lo


def _group_mean(sq, mbd):
    return _dot(sq.astype(BF16), mbd)


def _rope(xn, c, sa, sb):
    return xn * c + pltpu.roll(xn, 8, 1) * sa + pltpu.roll(xn, LANES - 8, 1) * sb


def _silu(z):
    return z * (1.0 / (1.0 + jnp.exp(-z)))


def _proj_kernel(x_ref, nw_ref, wq_ref, wkv_ref, wr_ref, qnw_ref, knw_ref,
                 rc_ref, rsa_ref, rsb_ref, mbd_ref, cw_ref, cb_ref,
                 qt_ref, kc_ref, vc_ref, ksf_ref, vst_ref, kw_ref, vwt_ref,
                 gate_ref, zs_ref, oconv_ref, ubuf):
    si = pl.program_id(1)

    @pl.when(si == 0)
    def _():
        ubuf[0:8, :] = jnp.zeros((8, D_CONV), F32)

    x = x_ref[0]
    ms = jnp.mean(x * x, axis=-1, keepdims=True)
    h = (x * lax.rsqrt(ms + EPS) * nw_ref[...]).astype(BF16)

    rc, rsa, rsb = rc_ref[...], rsa_ref[...], rsb_ref[...]
    mbd = mbd_ref[...]

    pq = _dot(h, wq_ref[...])
    for pair in range(2):
        blk = pq[:, pair * 256:(pair + 1) * 256]
        msq = _group_mean(blk * blk, mbd)
        qn = blk * lax.rsqrt(msq + EPS) * qnw_ref[...]
        for half in range(2):
            t = qn[:, half * LANES:(half + 1) * LANES]
            qt_ref[0, pair * 2 + half] = (_rope(t, rc, rsa, rsb) * Q_SCALE).T.astype(BF16)

    pkv = _dot(h, wkv_ref[...])
    kc_ref[0] = pkv[:, 0:128]
    vc_ref[0] = pkv[:, 128:256]
    ksw = jnp.concatenate([pkv[:, 256:384], pkv[:, 512:640]], axis=1)
    msk = _group_mean(ksw * ksw, mbd)
    kn = ksw * lax.rsqrt(msk + EPS) * knw_ref[...]
    ks = _rope(kn[:, 0:128], rc, rsa, rsb)
    kw = _rope(kn[:, 128:256], rc, rsa, rsb)
    row = lax.broadcasted_iota(jnp.int32, (TM, LANES), 0) + si * TM
    lane = lax.broadcasted_iota(jnp.int32, (TM, LANES), 1)
    onehot = jnp.where(lane == (row >> 6), 1.0, 0.0)
    ksf_ref[0, :, 0:128] = ks.astype(BF16)
    ksf_ref[0, :, 128:256] = onehot.astype(BF16)
    kw_ref[0] = kw.astype(BF16)
    vs = pkv[:, 384:512]
    vw = pkv[:, 640:768]
    for j in range(TM // VCHUNK):
        vst_ref[0, j] = vs[j * VCHUNK:(j + 1) * VCHUNK, :].T.astype(BF16)
        vwt_ref[0, j] = vw[j * VCHUNK:(j + 1) * VCHUNK, :].T.astype(BF16)

    gate_ref[0] = 1.0 / (1.0 + jnp.exp(-pkv[:, 768:896]))
    zs_ref[0] = _silu(_dot(h, wr_ref[:, 0:D_NSA])).astype(BF16)

    pcv = _dot(h, wr_ref[:, D_NSA:D_NSA + 4 * D_CONV])
    u = pcv[:, 1024:1536] * pcv[:, 0:512]
    ubuf[8:8 + TM, :] = u
    u1 = ubuf[7:7 + TM, :]
    u2 = ubuf[6:6 + TM, :]
    conv = cw_ref[0:1, :] * u2 + cw_ref[1:2, :] * u1 + cw_ref[2:3, :] * u + cb_ref[...]
    oconv_ref[0] = (pcv[:, 512:1024] * conv * _silu(pcv[:, 1536:2048])).astype(BF16)
    ubuf[0:8, :] = ubuf[TM:TM + 8, :]


def _cmp_kernel(hk_ref, hv_ref, pka_ref, pkb_ref, pva_ref, pvb_ref,
                wk1a_ref, wk1b_ref, bk1_ref, wk2_ref,
                wv1a_ref, wv1b_ref, bv1_ref, wv2_ref,
                knw_ref, rc_ref, rsa_ref, rsb_ref, mbd_ref,
                kc_ref, vct_ref, *, ncp):
    def first_layer(h_ref, p_ref, w_ref):
        acc = None
        for l in range(0, CMP_STRIDE, 2):
            lhs = jnp.concatenate([h_ref[0, pl.ds(l, ncp, stride=CMP_STRIDE), :],
                                   h_ref[0, pl.ds(l + 1, ncp, stride=CMP_STRIDE), :]], axis=1)
            lhs = (lhs + p_ref[:, l * LANES:(l + 2) * LANES]).astype(BF16)
            part = _dot(lhs, w_ref[l * LANES:(l + 2) * LANES, :])
            acc = part if acc is None else acc + part
        return acc

    def mlp(h_ref, pa_ref, pb_ref, w1a_ref, w1b_ref, b1_ref, w2_ref):
        p = first_layer(h_ref, pa_ref, w1a_ref)
        q = first_layer(h_ref, pb_ref, w1b_ref)
        pre = p + pltpu.roll(q, ncp - 1, 0) + b1_ref[...]
        return _dot(_silu(pre).astype(BF16), w2_ref[...])

    kc = mlp(hk_ref, pka_ref, pkb_ref, wk1a_ref, wk1b_ref, bk1_ref, wk2_ref)
    msk = _group_mean(kc * kc, mbd_ref[0:LANES, 0:LANES])
    kn = kc * lax.rsqrt(msk + EPS) * knw_ref[...]
    kc_ref[0] = _rope(kn, rc_ref[...], rsa_ref[...], rsb_ref[...]).astype(BF16)
    vc = mlp(hv_ref, pva_ref, pvb_ref, wv1a_ref, wv1b_ref, bv1_ref, wv2_ref)
    vct_ref[0] = vc.T.astype(BF16)


def _attn_kernel(qt_ref, kc_ref, vct_ref, ksf_ref, vst_ref, kw_ref, vwt_ref,
                 gate_ref, zs_ref, ovt_ref, o_ref, pen_sc, s_sc, cm_sc, *, ncp, bounded, tile, ncv, nblk):
    i = tile
    t0 = i * TQ
    last_pos = t0 + TQ
    rows = GROUP_HEADS * TQ
    groups = range(KV_GROUPS)

    q_all = jnp.concatenate([qt_ref[0, j] for j in range(GROUP_HEADS)], axis=1)
    frow = lax.broadcasted_iota(jnp.int32, (LANES, rows), 0)
    qg = [jnp.where(frow < HEAD_DIM, q_all, 0), jnp.where(frow >= HEAD_DIM, q_all, 0)]
    t_q = t0 + lax.broadcasted_iota(jnp.int32, (1, TQ), 1)

    def fill_where(a, mask, fill):
        return jnp.concatenate([jnp.where(mask, a[:, r * TQ:(r + 1) * TQ], fill)
                                for r in range(GROUP_HEADS)], axis=1)

    vrows = lambda v, g: v[g * HEAD_DIM:(g + 1) * HEAD_DIM, :]

    s_cmp = [_dot(kc_ref[0, 0:ncv, :], qg[g]) for g in groups]

    c_idx = lax.broadcasted_iota(jnp.int32, (ncv, 1), 0)
    mask_c = ((c_idx * CMP_STRIDE + (CMP_LEN - 1)) <= t_q) & (c_idx < ncp - 1)
    n_open = max(0, min((t0 - (CMP_LEN - 1)) // CMP_STRIDE + 1, ncp - 1)) // 8 * 8

    def masked_c(a, fill):
        tail = fill_where(a[n_open:], mask_c[n_open:], fill)
        return jnp.concatenate([a[0:n_open], tail], axis=0) if n_open else tail

    p_c, o_c = [], []
    for g in groups:
        if bounded:
            e_c = masked_c(jnp.exp2(s_cmp[g]), 0.0)
        else:
            sm = masked_c(s_cmp[g], NEG_INF)
            e_c = masked_c(jnp.exp2(sm - jnp.max(sm, axis=0, keepdims=True)), 0.0)
        l_c = jnp.sum(e_c, axis=0, keepdims=True)
        p = e_c * (1.0 / jnp.where(l_c > 0.0, l_c, 1.0))
        p_c.append(p)
        o_c.append(_dot(vrows(vct_ref[0], g)[:, 0:ncv], p.astype(BF16)))

    if last_pos // SEL_BLOCK <= SEL_TOPK:
        pen_sc[...] = jnp.zeros(pen_sc.shape, BF16)
    else:
        n_idx = lax.broadcasted_iota(jnp.int32, (nblk, TQ), 0)
        tq = t0 + lax.broadcasted_iota(jnp.int32, (nblk, TQ), 1)
        cur = tq >> 6
        forced = (n_idx == 0) | (n_idx == cur) | (n_idx == cur - 1)
        visible = (n_idx << 6) <= tq
        ovt = ovt_ref[0:nblk, 0:ncv]
        left = []
        for g in groups:
            psum = p_c[g][:, 0:TQ]
            for r in range(1, GROUP_HEADS):
                psum = psum + p_c[g][:, r * TQ:(r + 1) * TQ]
            hi, mid, lo = _split3(psum)
            imp = _dot(ovt, hi) + _dot(ovt, mid) + _dot(ovt, lo)
            left.append(jnp.where(forced, TAKEN, jnp.where(visible, imp, -1.0)))
        for _ in range(SEL_TOPK - N_FORCED):
            for g in groups:
                top = jnp.max(left[g], axis=0, keepdims=True)
                first = jnp.min(jnp.where(left[g] == top, n_idx, nblk), axis=0, keepdims=True)
                left[g] = jnp.where(n_idx == first, TAKEN, left[g])
        for g in groups:
            pen_sc[g, 0:nblk, :] = jnp.where(left[g] == TAKEN, 0.0, NEG_INF).astype(BF16)
            pen_sc[g, nblk:LANES, :] = jnp.zeros((LANES - nblk, TQ), BF16)

    nwc = WINDOW // VCHUNK + TQ // VCHUNK
    cw = max(i * (TQ // VCHUNK) - WINDOW // VCHUNK, 0)
    w0 = cw * VCHUNK
    kwin = kw_ref[0, w0:w0 + nwc * VCHUNK, :]
    s_win = [_dot(kwin, qg[g]) for g in groups]

    q_aug = [jnp.concatenate([qg[g], jnp.concatenate([pen_sc[g]] * GROUP_HEADS, axis=1)], axis=0)
             for g in groups]

    def window_branch():
        vwin = jnp.concatenate([vwt_ref[0, cw + j] for j in range(nwc)], axis=1)
        kpos = w0 + lax.broadcasted_iota(jnp.int32, (nwc * VCHUNK, 1), 0)

        def masked_w(a, fill):
            if t0 < WINDOW:
                return fill_where(a, (kpos <= t_q) & (kpos > t_q - WINDOW), fill)
            lo, hi = slice(0, TQ), slice(WINDOW, WINDOW + TQ)
            return jnp.concatenate([fill_where(a[lo], kpos[lo] > t_q - WINDOW, fill), a[TQ:WINDOW],
                                    fill_where(a[hi], kpos[hi] <= t_q, fill)], axis=0)

        o_w, l_w = [], []
        for g in groups:
            if bounded:
                p_w = masked_w(jnp.exp2(s_win[g]), 0.0)
            else:
                sw = masked_w(s_win[g], NEG_INF)
                p_w = jnp.exp2(sw - jnp.max(sw, axis=0, keepdims=True))
            v_ext = jnp.concatenate([vrows(vwin, g), jnp.ones((16, nwc * VCHUNK), BF16)], axis=0)
            ol = _dot(v_ext, p_w.astype(BF16))
            o_w.append(ol[0:HEAD_DIM, :])
            l_w.append(ol[HEAD_DIM:HEAD_DIM + 1, :])
        return o_w, l_w

    n_full = t0 // TK
    width = lambda c: TK if c < n_full else last_pos - n_full * TK

    def v_chunk(c, g):
        cv = c * (TK // VCHUNK)
        return jnp.concatenate([vrows(vst_ref[0, cv + j], g) for j in range(width(c) // VCHUNK)], axis=1)

    def qk(c, slot, g):
        s = _dot(ksf_ref[0, c * TK:c * TK + width(c), :], q_aug[g])
        s_sc[slot, g, 0:width(c)] = s
        if not bounded:
            cm_sc[slot, g] = jnp.max(s, axis=0, keepdims=True)

    def sm_pv(c, slot, g, state, masked):
        m_p, l_p, acc = state
        s = s_sc[slot, g, 0:width(c)]
        if masked:
            kpos = t0 + lax.broadcasted_iota(jnp.int32, (TQ, 1), 0)
            below = [s[0:width(c) - TQ]] if width(c) > TQ else []
            s = jnp.concatenate(below + [fill_where(s[width(c) - TQ:], kpos <= t_q, NEG_INF)], axis=0)
        if bounded:
            p = jnp.exp2(s)
            return m_p, l_p + jnp.sum(p, axis=0, keepdims=True), acc + _dot(v_chunk(c, g), p.astype(BF16))
        cm = jnp.max(s, axis=0, keepdims=True) if masked else cm_sc[slot, g]
        m_n = jnp.maximum(m_p, cm)
        alpha = jnp.exp2(m_p - m_n)
        p = jnp.exp2(s - m_n)
        l_n = alpha * l_p + jnp.sum(p, axis=0, keepdims=True)
        acc = alpha * acc + _dot(v_chunk(c, g), p.astype(BF16))
        return m_n, l_n, acc

    def step(c, slot, carry):
        out = []
        for g in groups:
            qk(c + 1, 1 - slot, g)
            out.append(sm_pv(c, slot, g, carry[g], False))
        return tuple(out)

    def last(slot, carry):
        return tuple(sm_pv(n_full, slot, g, carry[g], True) for g in groups)

    for g in groups:
        qk(0, 0, g)
    o_w, l_w = window_branch()
    carry = tuple((jnp.full((1, rows), 0.0 if bounded else NEG_INF, F32), jnp.zeros((1, rows), F32),
                   jnp.zeros((HEAD_DIM, rows), F32)) for _ in groups)
    for c in range(n_full):
        carry = step(c, c % 2, carry)
    carry = last(n_full % 2, carry)
    sel_out = [(carry[g][1], carry[g][2]) for g in groups]

    g_t = gate_ref[0].T
    slabs = []
    for g in groups:
        def gate_row(br):
            base = g * (N_BRANCH * GROUP_HEADS) + br * GROUP_HEADS
            return jnp.concatenate([g_t[base + r:base + r + 1, :] for r in range(GROUP_HEADS)], axis=1)
        l_s, o_s = sel_out[g]
        og = o_c[g] * gate_row(0) + o_s * (gate_row(1) / l_s) + o_w[g] * (gate_row(2) / l_w[g])
        slabs += [og[:, r * TQ:(r + 1) * TQ] for r in range(GROUP_HEADS)]
    out_t = jnp.concatenate(slabs, axis=0)
    o_ref[0] = (out_t.T * zs_ref[0].astype(F32)).astype(BF16)


def _out_kernel(x_ref, *refs):
    on_refs, (oc_ref, w_ref, o_ref) = refs[:-3], refs[-3:]
    per_tile = TM_OUT // TQ
    for k in range(len(on_refs) // per_tile):
        @pl.when(pl.program_id(1) == k)
        def _(k=k):
            on = jnp.concatenate([r[0] for r in on_refs[k * per_tile:(k + 1) * per_tile]], axis=0)
            acc = _dot(on, w_ref[0:D_NSA, :]) + _dot(oc_ref[0], w_ref[D_NSA:D_NSA + D_CONV, :])
            o_ref[0] = x_ref[0] + acc


def _rope_tables(pos):
    inv_freq = (np.float32(ROPE_THETA) ** (-np.arange(0, ROT_DIM, 2, dtype=np.float32) / ROT_DIM)).astype(np.float32)
    ang = pos.astype(np.float32)[:, None] * inv_freq[None, :]
    cos, sin = np.cos(ang), np.sin(ang)
    n = pos.shape[0]
    ones = np.ones((n, HEAD_DIM - ROT_DIM), np.float32)
    zeros = np.zeros((n, HEAD_DIM - ROT_DIM), np.float32)
    z8 = np.zeros((n, ROT_DIM // 2), np.float32)
    c = np.concatenate([cos, cos, ones], axis=1)
    sa = np.concatenate([z8, sin, zeros], axis=1)
    sb = np.concatenate([-sin, z8, zeros], axis=1)
    return tuple(jnp.asarray(np.concatenate([t, t], axis=1), F32) for t in (c, sa, sb))


def _overlap_t(ncp):
    cs = np.arange(ncp) * CMP_STRIDE
    ce = cs + CMP_LEN
    ss = np.arange(SEL_BLOCK) * SEL_BLOCK
    se = ss + SEL_BLOCK
    ov = np.clip(np.minimum(ce[None, :], se[:, None]) - np.maximum(cs[None, :], ss[:, None]), 0, None)
    ov = ov.astype(np.float32) / CMP_LEN
    ov[:, ncp - 1] = 0.0
    return jnp.asarray(ov, BF16)


def _cmp_weights(pos, w1, b1, w2):
    half = CMP_STRIDE
    def big_w1(w):
        w = w.reshape(half, 1, HEAD_DIM, 1, CMP_HIDDEN)
        eye = jnp.eye(KV_GROUPS, dtype=F32).reshape(1, KV_GROUPS, 1, KV_GROUPS, 1)
        return (w * eye).reshape(half * KV_GROUPS * HEAD_DIM, KV_GROUPS * CMP_HIDDEN).astype(BF16)
    def big_pos(p):
        return jnp.broadcast_to(p[:, None, :], (half, KV_GROUPS, HEAD_DIM)).reshape(1, -1)
    w1a, w1b = w1[:half * HEAD_DIM], w1[half * HEAD_DIM:]
    eye2 = jnp.eye(KV_GROUPS, dtype=F32)
    w2b = (w2[None, :, None, :] * eye2[:, None, :, None]).reshape(KV_GROUPS * CMP_HIDDEN,
                                                                 KV_GROUPS * HEAD_DIM).astype(BF16)
    b1b = jnp.tile(b1, KV_GROUPS)[None, :]
    return big_pos(pos[:half]), big_pos(pos[half:]), big_w1(w1a), big_w1(w1b), b1b, w2b


def _full(shape):
    nd = len(shape)
    return pl.BlockSpec(shape, lambda *_: (0,) * nd)


def _once(shape):
    nd = len(shape)
    return pl.BlockSpec(shape, lambda *_: (0,) * nd, pipeline_mode=pl.Buffered(1))


def kernel(x, norm_w, w_in, q_norm_w, k_norm_w, cmp_k_pos, cmp_k_w1, cmp_k_b1, cmp_k_w2,
           cmp_v_pos, cmp_v_w1, cmp_v_b1, cmp_v_w2, conv_w, conv_b, w_out):
    B, S, D = x.shape
    assert norm_w.shape[0] == 1, "single layer"
    assert S % TM == 0 and S % TM_OUT == 0 and TK % TQ == 0 and TM_OUT % TQ == 0 and S // SEL_BLOCK <= SEL_BLOCK and S >= WINDOW + TQ
    ncp = S // CMP_STRIDE
    nst = S // TM
    f32 = lambda a: a.astype(F32)

    w = w_in[0].astype(BF16)
    head_order = []
    for j in range(GROUP_HEADS):
        head_order += [j, GROUP_HEADS + j]
    wq = w[:, :D_NSA].reshape(D, NSA_HEADS, HEAD_DIM)[:, jnp.array(head_order), :].reshape(D, D_NSA)
    gate_cols = []
    for g in range(KV_GROUPS):
        for br in range(N_BRANCH):
            for r in range(GROUP_HEADS):
                gate_cols.append((g * GROUP_HEADS + r) * N_BRANCH + br)
    g0 = D_NSA + 6 * LANES
    wg = w[:, g0:g0 + NSA_HEADS * N_BRANCH][:, jnp.array(gate_cols)]
    wg = jnp.pad(wg, ((0, 0), (0, LANES - NSA_HEADS * N_BRANCH)))
    rest0 = g0 + NSA_HEADS * N_BRANCH
    wkv, wr = jnp.concatenate([w[:, D_NSA:g0], wg], axis=1), w[:, rest0:]
    n_rest = wr.shape[1]

    rc, rsa, rsb = _rope_tables(np.arange(S))
    mbd = jnp.asarray(np.kron(np.eye(4), np.full((HEAD_DIM, HEAD_DIM), 1.0 / HEAD_DIM)), BF16)
    qnw = jnp.tile(f32(q_norm_w[0]), 4)[None, :]
    knw_sw = jnp.concatenate([jnp.tile(f32(k_norm_w[0, 1]), 2), jnp.tile(f32(k_norm_w[0, 2]), 2)])[None, :]
    knw_c = jnp.tile(f32(k_norm_w[0, 0]), 2)[None, :]

    cp = pltpu.CompilerParams(dimension_semantics=("arbitrary", "arbitrary"),
                              vmem_limit_bytes=VMEM_LIMIT)
    cp_seq = pltpu.CompilerParams(dimension_semantics=("arbitrary",), vmem_limit_bytes=VMEM_LIMIT)
    row_blk = lambda n: pl.BlockSpec((1, TM, n), lambda b, s: (b, s, 0))
    tab_blk = pl.BlockSpec((TM, LANES), lambda b, s: (s, 0))
    vt_blk = pl.BlockSpec((1, TM // VCHUNK, LANES, VCHUNK), lambda b, s: (b, s, 0, 0))
    sds = jax.ShapeDtypeStruct

    (qt4, kc_raw, vc_raw, ksf, vst, kw, vwt, gates, zs, oconv) = pl.pallas_call(
        _proj_kernel,
        grid=(B, nst),
        in_specs=[row_blk(D), _full((1, D)), _once((D, D_NSA)), _once((D, 7 * LANES)),
                  _once((D, n_rest)), _full((1, 256)), _full((1, 256)),
                  tab_blk, tab_blk, tab_blk, _full((256, 256)), _full((3, D_CONV)), _full((1, D_CONV))],
        out_specs=[pl.BlockSpec((1, 4, LANES, TM), lambda b, s: (b, 0, 0, s)),
                   row_blk(LANES), row_blk(LANES), row_blk(2 * LANES), vt_blk, row_blk(LANES), vt_blk,
                   row_blk(LANES), row_blk(D_NSA), row_blk(D_CONV)],
        out_shape=[sds((B, 4, LANES, S), BF16), sds((B, S, LANES), F32), sds((B, S, LANES), F32),
                   sds((B, S, 2 * LANES), BF16), sds((B, S // VCHUNK, LANES, VCHUNK), BF16),
                   sds((B, S, LANES), BF16), sds((B, S // VCHUNK, LANES, VCHUNK), BF16),
                   sds((B, S, LANES), F32), sds((B, S, D_NSA), BF16), sds((B, S, D_CONV), BF16)],
        scratch_shapes=[pltpu.VMEM((TM + 16, D_CONV), F32)],
        compiler_params=cp,
        name="nsa_proj",
    )(x, f32(norm_w), wq, wkv, wr, qnw, knw_sw, rc, rsa, rsb, mbd, f32(conv_w[0]), f32(conv_b))

    cmp_pos = np.arange(ncp) * CMP_STRIDE + (CMP_LEN - 1)
    crc, crsa, crsb = _rope_tables(cmp_pos)
    kpa, kpb, kw1a, kw1b, kb1, kw2 = _cmp_weights(f32(cmp_k_pos[0]), f32(cmp_k_w1[0]),
                                                  f32(cmp_k_b1[0]), f32(cmp_k_w2[0]))
    vpa, vpb, vw1a, vw1b, vb1, vw2 = _cmp_weights(f32(cmp_v_pos[0]), f32(cmp_v_w1[0]),
                                                  f32(cmp_v_b1[0]), f32(cmp_v_w2[0]))
    flat = CMP_STRIDE * LANES
    h_blk = pl.BlockSpec((1, S, LANES), lambda b: (b, 0, 0))
    kc, vct = pl.pallas_call(
        functools.partial(_cmp_kernel, ncp=ncp),
        grid=(B,),
        in_specs=[h_blk, h_blk] + [_full((1, flat))] * 4
                 + [_full((flat, 256)), _full((flat, 256)), _full((1, 256)), _full((256, LANES))] * 2
                 + [_full((1, LANES)), _full((ncp, LANES)), _full((ncp, LANES)), _full((ncp, LANES)),
                    _full((256, 256))],
        out_specs=[pl.BlockSpec((1, ncp, LANES), lambda b: (b, 0, 0)),
                   pl.BlockSpec((1, LANES, ncp), lambda b: (b, 0, 0))],
        out_shape=[sds((B, ncp, LANES), BF16), sds((B, LANES, ncp), BF16)],
        compiler_params=cp_seq,
        name="nsa_compress",
    )(kc_raw, vc_raw, kpa, kpb, vpa, vpb,
      kw1a, kw1b, kb1, kw2, vw1a, vw1b, vb1, vw2, knw_c, crc, crsa, crsb, mbd)

    score_bound = (HEAD_DIM * Q_SCALE) * jnp.max(jnp.abs(f32(q_norm_w))) * jnp.max(jnp.abs(f32(k_norm_w)))
    n_tiles = S // TQ

    def attention(bounded, r):
        last_pos = (r + 1) * TQ
        win_pos = max(last_pos, WINDOW + TQ)
        nblk = min(SEL_BLOCK, -(-last_pos // (8 * SEL_BLOCK)) * 8)
        return pl.pallas_call(
            functools.partial(_attn_kernel, ncp=ncp, bounded=bounded, tile=r,
                              ncv=min(ncp, last_pos // CMP_STRIDE), nblk=nblk),
            grid=(B,),
            in_specs=[pl.BlockSpec((1, 4, LANES, TQ), lambda b: (b, 0, 0, r)),
                      pl.BlockSpec((1, ncp, LANES), lambda b: (b, 0, 0)),
                      pl.BlockSpec((1, LANES, ncp), lambda b: (b, 0, 0)),
                      pl.BlockSpec((1, last_pos, 2 * LANES), lambda b: (b, 0, 0)),
                      pl.BlockSpec((1, last_pos // VCHUNK, LANES, VCHUNK), lambda b: (b, 0, 0, 0)),
                      pl.BlockSpec((1, win_pos, LANES), lambda b: (b, 0, 0)),
                      pl.BlockSpec((1, win_pos // VCHUNK, LANES, VCHUNK), lambda b: (b, 0, 0, 0)),
                      pl.BlockSpec((1, TQ, LANES), lambda b: (b, r, 0)),
                      pl.BlockSpec((1, TQ, D_NSA), lambda b: (b, r, 0)),
                      pl.BlockSpec((SEL_BLOCK, ncp), lambda b: (0, 0))],
            out_specs=pl.BlockSpec((1, TQ, D_NSA), lambda b: (b, 0, 0)),
            out_shape=sds((B, TQ, D_NSA), BF16),
            scratch_shapes=[pltpu.VMEM((KV_GROUPS, LANES, TQ), BF16),
                            pltpu.VMEM((2, KV_GROUPS, TK, GROUP_HEADS * TQ), F32),
                            pltpu.VMEM((2, KV_GROUPS, 1, GROUP_HEADS * TQ), F32)],
            compiler_params=cp_seq,
            name=("nsa_attn_bounded_r%d" if bounded else "nsa_attn_online_r%d") % r,
        )

    attn_args = (qt4, kc, vct, ksf, vst, kw, vwt, gates, zs, _overlap_t(ncp))
    all_tiles = lambda bounded: (lambda *a: tuple(attention(bounded, r)(*a) for r in range(n_tiles)))
    o_nsa = lax.cond(score_bound <= MAX_SAFE_SCORE, all_tiles(True), all_tiles(False), *attn_args)

    out_blk = lambda n: pl.BlockSpec((1, TM_OUT, n), lambda b, s: (b, s, 0))
    on_blk = lambda k: pl.BlockSpec((1, TQ, D_NSA), lambda b, s: (
        jnp.where(s >= k // (TM_OUT // TQ), b, jnp.maximum(b - 1, 0)), 0, 0))
    out = pl.pallas_call(
        _out_kernel,
        grid=(B, S // TM_OUT),
        in_specs=[out_blk(D)] + [on_blk(k) for k in range(n_tiles)]
                 + [out_blk(D_CONV), _full((D_NSA + D_CONV, D))],
        out_specs=out_blk(D),
        out_shape=sds((B, S, D), x.dtype),
        compiler_params=cp,
        name="nsa_out",
    )(x, *o_nsa, oconv, f32(w_out[0]).astype(BF16))
    return out
```

```python
import functools

import numpy as np
import jax
import jax.numpy as jnp
from jax import lax
from jax.experimental import pallas as pl
from jax.experimental.pallas import tpu as pltpu

LANES = 128
HEAD_DIM = 64
NSA_HEADS = 8
KV_GROUPS = 2
GROUP_HEADS = NSA_HEADS // KV_GROUPS
D_NSA = NSA_HEADS * HEAD_DIM
D_CONV = 512
N_BRANCH = 3
ROT_DIM = HEAD_DIM // 4
ROPE_THETA = 500000.0
CMP_LEN = 32
CMP_STRIDE = 16
CMP_HIDDEN = 2 * HEAD_DIM
SEL_BLOCK = 64
SEL_TOPK = 16
WINDOW = 512
EPS = 1e-6
NEG_INF = -1e30
N_FORCED = 3
TAKEN = -2.0
SCALE = HEAD_DIM ** -0.5
Q_SCALE = SCALE * float(np.log2(np.e))

TM = 1024
TM_OUT = 1024
TQ = 256
TK = 1024
VCHUNK = 128
VMEM_LIMIT = 56 * 1024 * 1024
MAX_SAFE_SCORE = 50.0

BF16 = jnp.bfloat16
F32 = jnp.float32


def _dot(a, b):
    return jnp.dot(a, b, preferred_element_type=F32)


def _split3(a):
    hi = a.astype(BF16)
    r1 = a - hi.astype(F32)
    mid = r1.astype(BF16)
    lo = (r1 - mid.astype(F32)).astype(BF16)
    return hi, mid, lo


def _group_mean(sq, mbd):
    return _dot(sq.astype(BF16), mbd)


def _rope(xn, c, sa, sb):
    return xn * c + pltpu.roll(xn, 8, 1) * sa + pltpu.roll(xn, LANES - 8, 1) * sb


def _silu(z):
    return z * (1.0 / (1.0 + jnp.exp(-z)))


def _proj_kernel(x_ref, nw_ref, wq_ref, wkv_ref, wr_ref, qnw_ref, knw_ref,
                 rc_ref, rsa_ref, rsb_ref, mbd_ref, cw_ref, cb_ref,
                 qt_ref, kc_ref, vc_ref, ksf_ref, vst_ref, kw_ref, vwt_ref,
                 gate_ref, zs_ref, oconv_ref, ubuf):
    si = pl.program_id(1)

    @pl.when(si == 0)
    def _():
        ubuf[0:8, :] = jnp.zeros((8, D_CONV), F32)

    x = x_ref[0]
    ms = jnp.mean(x * x, axis=-1, keepdims=True)
    h = (x * lax.rsqrt(ms + EPS) * nw_ref[...]).astype(BF16)

    rc, rsa, rsb = rc_ref[...], rsa_ref[...], rsb_ref[...]
    mbd = mbd_ref[...]

    pq = _dot(h, wq_ref[...])
    for pair in range(2):
        blk = pq[:, pair * 256:(pair + 1) * 256]
        msq = _group_mean(blk * blk, mbd)
        qn = blk * lax.rsqrt(msq + EPS) * qnw_ref[...]
        for half in range(2):
            t = qn[:, half * LANES:(half + 1) * LANES]
            qt_ref[0, pair * 2 + half] = (_rope(t, rc, rsa, rsb) * Q_SCALE).T.astype(BF16)

    pkv = _dot(h, wkv_ref[...])
    kc_ref[0] = pkv[:, 0:128]
    vc_ref[0] = pkv[:, 128:256]
    ksw = jnp.concatenate([pkv[:, 256:384], pkv[:, 512:640]], axis=1)
    msk = _group_mean(ksw * ksw, mbd)
    kn = ksw * lax.rsqrt(msk + EPS) * knw_ref[...]
    ks = _rope(kn[:, 0:128], rc, rsa, rsb)
    kw = _rope(kn[:, 128:256], rc, rsa, rsb)
    row = lax.broadcasted_iota(jnp.int32, (TM, LANES), 0) + si * TM
    lane = lax.broadcasted_iota(jnp.int32, (TM, LANES), 1)
    onehot = jnp.where(lane == (row >> 6), 1.0, 0.0)
    ksf_ref[0, :, 0:128] = ks.astype(BF16)
    ksf_ref[0, :, 128:256] = onehot.astype(BF16)
    kw_ref[0] = kw.astype(BF16)
    vs = pkv[:, 384:512]
    vw = pkv[:, 640:768]
    for j in range(TM // VCHUNK):
        vst_ref[0, j] = vs[j * VCHUNK:(j + 1) * VCHUNK, :].T.astype(BF16)
        vwt_ref[0, j] = vw[j * VCHUNK:(j + 1) * VCHUNK, :].T.astype(BF16)

    gate_ref[0] = 1.0 / (1.0 + jnp.exp(-pkv[:, 768:896]))
    zs_ref[0] = _silu(_dot(h, wr_ref[:, 0:D_NSA])).astype(BF16)

    pcv = _dot(h, wr_ref[:, D_NSA:D_NSA + 4 * D_CONV])
    u = pcv[:, 1024:1536] * pcv[:, 0:512]
    ubuf[8:8 + TM, :] = u
    u1 = ubuf[7:7 + TM, :]
    u2 = ubuf[6:6 + TM, :]
    conv = cw_ref[0:1, :] * u2 + cw_ref[1:2, :] * u1 + cw_ref[2:3, :] * u + cb_ref[...]
    oconv_ref[0] = (pcv[:, 512:1024] * conv * _silu(pcv[:, 1536:2048])).astype(BF16)
    ubuf[0:8, :] = ubuf[TM:TM + 8, :]


def _cmp_kernel(hk_ref, hv_ref, pka_ref, pkb_ref, pva_ref, pvb_ref,
                wk1a_ref, wk1b_ref, bk1_ref, wk2_ref,
                wv1a_ref, wv1b_ref, bv1_ref, wv2_ref,
                knw_ref, rc_ref, rsa_ref, rsb_ref, mbd_ref,
                kc_ref, vct_ref, *, ncp):
    def first_layer(h_ref, p_ref, w_ref):
        acc = None
        for l in range(0, CMP_STRIDE, 2):
            lhs = jnp.concatenate([h_ref[0, pl.ds(l, ncp, stride=CMP_STRIDE), :],
                                   h_ref[0, pl.ds(l + 1, ncp, stride=CMP_STRIDE), :]], axis=1)
            lhs = (lhs + p_ref[:, l * LANES:(l + 2) * LANES]).astype(BF16)
            part = _dot(lhs, w_ref[l * LANES:(l + 2) * LANES, :])
            acc = part if acc is None else acc + part
        return acc

    def mlp(h_ref, pa_ref, pb_ref, w1a_ref, w1b_ref, b1_ref, w2_ref):
        p = first_layer(h_ref, pa_ref, w1a_ref)
        q = first_layer(h_ref, pb_ref, w1b_ref)
        pre = p + pltpu.roll(q, ncp - 1, 0) + b1_ref[...]
        return _dot(_silu(pre).astype(BF16), w2_ref[...])

    kc = mlp(hk_ref, pka_ref, pkb_ref, wk1a_ref, wk1b_ref, bk1_ref, wk2_ref)
    msk = _group_mean(kc * kc, mbd_ref[0:LANES, 0:LANES])
    kn = kc * lax.rsqrt(msk + EPS) * knw_ref[...]
    kc_ref[0] = _rope(kn, rc_ref[...], rsa_ref[...], rsb_ref[...]).astype(BF16)
    vc = mlp(hv_ref, pva_ref, pvb_ref, wv1a_ref, wv1b_ref, bv1_ref, wv2_ref)
    vct_ref[0] = vc.T.astype(BF16)


def _attn_kernel(qt_ref, kc_ref, vct_ref, ksf_ref, vst_ref, kw_ref, vwt_ref,
                 gate_ref, zs_ref, ovt_ref, o_ref, pen_sc, s_sc, cm_sc, *, ncp, bounded, tile, ncv, nblk):
    i = tile
    t0 = i * TQ
    last_pos = t0 + TQ
    rows = GROUP_HEADS * TQ
    groups = range(KV_GROUPS)

    q_all = jnp.concatenate([qt_ref[0, j] for j in range(GROUP_HEADS)], axis=1)
    frow = lax.broadcasted_iota(jnp.int32, (LANES, rows), 0)
    qg = [jnp.where(frow < HEAD_DIM, q_all, 0), jnp.where(frow >= HEAD_DIM, q_all, 0)]
    t_q = t0 + lax.broadcasted_iota(jnp.int32, (1, TQ), 1)

    def fill_where(a, mask, fill):
        return jnp.concatenate([jnp.where(mask, a[:, r * TQ:(r + 1) * TQ], fill)
                                for r in range(GROUP_HEADS)], axis=1)

    vrows = lambda v, g: v[g * HEAD_DIM:(g + 1) * HEAD_DIM, :]

    s_cmp = [_dot(kc_ref[0, 0:ncv, :], qg[g]) for g in groups]

    c_idx = lax.broadcasted_iota(jnp.int32, (ncv, 1), 0)
    mask_c = ((c_idx * CMP_STRIDE + (CMP_LEN - 1)) <= t_q) & (c_idx < ncp - 1)
    n_open = max(0, min((t0 - (CMP_LEN - 1)) // CMP_STRIDE + 1, ncp - 1)) // 8 * 8

    def masked_c(a, fill):
        tail = fill_where(a[n_open:], mask_c[n_open:], fill)
        return jnp.concatenate([a[0:n_open], tail], axis=0) if n_open else tail

    p_c, o_c = [], []
    for g in groups:
        if bounded:
            e_c = masked_c(jnp.exp2(s_cmp[g]), 0.0)
        else:
            sm = masked_c(s_cmp[g], NEG_INF)
            e_c = masked_c(jnp.exp2(sm - jnp.max(sm, axis=0, keepdims=True)), 0.0)
        l_c = jnp.sum(e_c, axis=0, keepdims=True)
        p = e_c * (1.0 / jnp.where(l_c > 0.0, l_c, 1.0))
        p_c.append(p)
        o_c.append(_dot(vrows(vct_ref[0], g)[:, 0:ncv], p.astype(BF16)))

    if last_pos // SEL_BLOCK <= SEL_TOPK:
        pen_sc[...] = jnp.zeros(pen_sc.shape, BF16)
    else:
        n_idx = lax.broadcasted_iota(jnp.int32, (nblk, TQ), 0)
        tq = t0 + lax.broadcasted_iota(jnp.int32, (nblk, TQ), 1)
        cur = tq >> 6
        forced = (n_idx == 0) | (n_idx == cur) | (n_idx == cur - 1)
        visible = (n_idx << 6) <= tq
        ovt = ovt_ref[0:nblk, 0:ncv]
        left = []
        for g in groups:
            psum = p_c[g][:, 0:TQ]
            for r in range(1, GROUP_HEADS):
                psum = psum + p_c[g][:, r * TQ:(r + 1) * TQ]
            hi, mid, lo = _split3(psum)
            imp = _dot(ovt, hi) + _dot(ovt, mid) + _dot(ovt, lo)
            left.append(jnp.where(forced, TAKEN, jnp.where(visible, imp, -1.0)))
        for _ in range(SEL_TOPK - N_FORCED):
            for g in groups:
                top = jnp.max(left[g], axis=0, keepdims=True)
                first = jnp.min(jnp.where(left[g] == top, n_idx, nblk), axis=0, keepdims=True)
                left[g] = jnp.where(n_idx == first, TAKEN, left[g])
        for g in groups:
            pen_sc[g, 0:nblk, :] = jnp.where(left[g] == TAKEN, 0.0, NEG_INF).astype(BF16)
            pen_sc[g, nblk:LANES, :] = jnp.zeros((LANES - nblk, TQ), BF16)

    nwc = WINDOW // VCHUNK + TQ // VCHUNK
    cw = max(i * (TQ // VCHUNK) - WINDOW // VCHUNK, 0)
    w0 = cw * VCHUNK
    kwin = kw_ref[0, w0:w0 + nwc * VCHUNK, :]
    s_win = [_dot(kwin, qg[g]) for g in groups]

    q_aug = [jnp.concatenate([qg[g], jnp.concatenate([pen_sc[g]] * GROUP_HEADS, axis=1)], axis=0)
             for g in groups]

    def window_branch():
        vwin = jnp.concatenate([vwt_ref[0, cw + j] for j in range(nwc)], axis=1)
        kpos = w0 + lax.broadcasted_iota(jnp.int32, (nwc * VCHUNK, 1), 0)

        def masked_w(a, fill):
            if t0 < WINDOW:
                return fill_where(a, (kpos <= t_q) & (kpos > t_q - WINDOW), fill)
            lo, hi = slice(0, TQ), slice(WINDOW, WINDOW + TQ)
            return jnp.concatenate([fill_where(a[lo], kpos[lo] > t_q - WINDOW, fill), a[TQ:WINDOW],
                                    fill_where(a[hi], kpos[hi] <= t_q, fill)], axis=0)

        o_w, l_w = [], []
        for g in groups:
            if bounded:
                p_w = masked_w(jnp.exp2(s_win[g]), 0.0)
            else:
                sw = masked_w(s_win[g], NEG_INF)
                p_w = jnp.exp2(sw - jnp.max(sw, axis=0, keepdims=True))
            v_ext = jnp.concatenate([vrows(vwin, g), jnp.ones((16, nwc * VCHUNK), BF16)], axis=0)
            ol = _dot(v_ext, p_w.astype(BF16))
            o_w.append(ol[0:HEAD_DIM, :])
            l_w.append(ol[HEAD_DIM:HEAD_DIM + 1, :])
        return o_w, l_w

    n_full = t0 // TK
    width = lambda c: TK if c < n_full else last_pos - n_full * TK

    def v_chunk(c, g):
        cv = c * (TK // VCHUNK)
        return jnp.concatenate([vrows(vst_ref[0, cv + j], g) for j in range(width(c) // VCHUNK)], axis=1)

    def qk(c, slot, g):
        s = _dot(ksf_ref[0, c * TK:c * TK + width(c), :], q_aug[g])
        s_sc[slot, g, 0:width(c)] = s
        if not bounded:
            cm_sc[slot, g] = jnp.max(s, axis=0, keepdims=True)

    def sm_pv(c, slot, g, state, masked):
        m_p, l_p, acc = state
        s = s_sc[slot, g, 0:width(c)]
        if masked:
            kpos = t0 + lax.broadcasted_iota(jnp.int32, (TQ, 1), 0)
            below = [s[0:width(c) - TQ]] if width(c) > TQ else []
            s = jnp.concatenate(below + [fill_where(s[width(c) - TQ:], kpos <= t_q, NEG_INF)], axis=0)
        if bounded:
            p = jnp.exp2(s)
            return m_p, l_p + jnp.sum(p, axis=0, keepdims=True), acc + _dot(v_chunk(c, g), p.astype(BF16))
        cm = jnp.max(s, axis=0, keepdims=True) if masked else cm_sc[slot, g]
        m_n = jnp.maximum(m_p, cm)
        alpha = jnp.exp2(m_p - m_n)
        p = jnp.exp2(s - m_n)
        l_n = alpha * l_p + jnp.sum(p, axis=0, keepdims=True)
        acc = alpha * acc + _dot(v_chunk(c, g), p.astype(BF16))
        return m_n, l_n, acc

    def step(c, slot, carry):
        out = []
        for g in groups:
            qk(c + 1, 1 - slot, g)
            out.append(sm_pv(c, slot, g, carry[g], False))
        return tuple(out)

    def last(slot, carry):
        return tuple(sm_pv(n_full, slot, g, carry[g], True) for g in groups)

    for g in groups:
        qk(0, 0, g)
    o_w, l_w = window_branch()
    carry = tuple((jnp.full((1, rows), 0.0 if bounded else NEG_INF, F32), jnp.zeros((1, rows), F32),
                   jnp.zeros((HEAD_DIM, rows), F32)) for _ in groups)
    for c in range(n_full):
        carry = step(c, c % 2, carry)
    carry = last(n_full % 2, carry)
    sel_out = [(carry[g][1], carry[g][2]) for g in groups]

    g_t = gate_ref[0].T
    slabs = []
    for g in groups:
        def gate_row(br):
            base = g * (N_BRANCH * GROUP_HEADS) + br * GROUP_HEADS
            return jnp.concatenate([g_t[base + r:base + r + 1, :] for r in range(GROUP_HEADS)], axis=1)
        l_s, o_s = sel_out[g]
        og = o_c[g] * gate_row(0) + o_s * (gate_row(1) / l_s) + o_w[g] * (gate_row(2) / l_w[g])
        slabs += [og[:, r * TQ:(r + 1) * TQ] for r in range(GROUP_HEADS)]
    out_t = jnp.concatenate(slabs, axis=0)
    o_ref[0] = (out_t.T * zs_ref[0].astype(F32)).astype(BF16)


def _out_kernel(x_ref, *refs):
    on_refs, (oc_ref, w_ref, o_ref) = refs[:-3], refs[-3:]
    per_tile = TM_OUT // TQ
    for k in range(len(on_refs) // per_tile):
        @pl.when(pl.program_id(1) == k)
        def _(k=k):
            on = jnp.concatenate([r[0] for r in on_refs[k * per_tile:(k + 1) * per_tile]], axis=0)
            acc = _dot(on, w_ref[0:D_NSA, :]) + _dot(oc_ref[0], w_ref[D_NSA:D_NSA + D_CONV, :])
            o_ref[0] = x_ref[0] + acc


def _rope_tables(pos):
    inv_freq = (np.float32(ROPE_THETA) ** (-np.arange(0, ROT_DIM, 2, dtype=np.float32) / ROT_DIM)).astype(np.float32)
    ang = pos.astype(np.float32)[:, None] * inv_freq[None, :]
    cos, sin = np.cos(ang), np.sin(ang)
    n = pos.shape[0]
    ones = np.ones((n, HEAD_DIM - ROT_DIM), np.float32)
    zeros = np.zeros((n, HEAD_DIM - ROT_DIM), np.float32)
    z8 = np.zeros((n, ROT_DIM // 2), np.float32)
    c = np.concatenate([cos, cos, ones], axis=1)
    sa = np.concatenate([z8, sin, zeros], axis=1)
    sb = np.concatenate([-sin, z8, zeros], axis=1)
    return tuple(jnp.asarray(np.concatenate([t, t], axis=1), F32) for t in (c, sa, sb))


def _overlap_t(ncp):
    cs = np.arange(ncp) * CMP_STRIDE
    ce = cs + CMP_LEN
    ss = np.arange(SEL_BLOCK) * SEL_BLOCK
    se = ss + SEL_BLOCK
    ov = np.clip(np.minimum(ce[None, :], se[:, None]) - np.maximum(cs[None, :], ss[:, None]), 0, None)
    ov = ov.astype(np.float32) / CMP_LEN
    ov[:, ncp - 1] = 0.0
    return jnp.asarray(ov, BF16)


def _cmp_weights(pos, w1, b1, w2):
    half = CMP_STRIDE
    def big_w1(w):
        w = w.reshape(half, 1, HEAD_DIM, 1, CMP_HIDDEN)
        eye = jnp.eye(KV_GROUPS, dtype=F32).reshape(1, KV_GROUPS, 1, KV_GROUPS, 1)
        return (w * eye).reshape(half * KV_GROUPS * HEAD_DIM, KV_GROUPS * CMP_HIDDEN).astype(BF16)
    def big_pos(p):
        return jnp.broadcast_to(p[:, None, :], (half, KV_GROUPS, HEAD_DIM)).reshape(1, -1)
    w1a, w1b = w1[:half * HEAD_DIM], w1[half * HEAD_DIM:]
    eye2 = jnp.eye(KV_GROUPS, dtype=F32)
    w2b = (w2[None, :, None, :] * eye2[:, None, :, None]).reshape(KV_GROUPS * CMP_HIDDEN,
                                                                 KV_GROUPS * HEAD_DIM).astype(BF16)
    b1b = jnp.tile(b1, KV_GROUPS)[None, :]
    return big_pos(pos[:half]), big_pos(pos[half:]), big_w1(w1a), big_w1(w1b), b1b, w2b


def _full(shape):
    nd = len(shape)
    return pl.BlockSpec(shape, lambda *_: (0,) * nd)


def _once(shape):
    nd = len(shape)
    return pl.BlockSpec(shape, lambda *_: (0,) * nd, pipeline_mode=pl.Buffered(1))


def kernel(x, norm_w, w_in, q_norm_w, k_norm_w, cmp_k_pos, cmp_k_w1, cmp_k_b1, cmp_k_w2,
           cmp_v_pos, cmp_v_w1, cmp_v_b1, cmp_v_w2, conv_w, conv_b, w_out):
    B, S, D = x.shape
    assert norm_w.shape[0] == 1, "single layer"
    assert S % TM == 0 and S % TM_OUT == 0 and TK % TQ == 0 and TM_OUT % TQ == 0 and S // SEL_BLOCK <= SEL_BLOCK and S >= WINDOW + TQ
    ncp = S // CMP_STRIDE
    nst = S // TM
    f32 = lambda a: a.astype(F32)

    w = w_in[0].astype(BF16)
    head_order = []
    for j in range(GROUP_HEADS):
        head_order += [j, GROUP_HEADS + j]
    wq = w[:, :D_NSA].reshape(D, NSA_HEADS, HEAD_DIM)[:, jnp.array(head_order), :].reshape(D, D_NSA)
    gate_cols = []
    for g in range(KV_GROUPS):
        for br in range(N_BRANCH):
            for r in range(GROUP_HEADS):
                gate_cols.append((g * GROUP_HEADS + r) * N_BRANCH + br)
    g0 = D_NSA + 6 * LANES
    wg = w[:, g0:g0 + NSA_HEADS * N_BRANCH][:, jnp.array(gate_cols)]
    wg = jnp.pad(wg, ((0, 0), (0, LANES - NSA_HEADS * N_BRANCH)))
    rest0 = g0 + NSA_HEADS * N_BRANCH
    wkv, wr = jnp.concatenate([w[:, D_NSA:g0], wg], axis=1), w[:, rest0:]
    n_rest = wr.shape[1]

    rc, rsa, rsb = _rope_tables(np.arange(S))
    mbd = jnp.asarray(np.kron(np.eye(4), np.full((HEAD_DIM, HEAD_DIM), 1.0 / HEAD_DIM)), BF16)
    qnw = jnp.tile(f32(q_norm_w[0]), 4)[None, :]
    knw_sw = jnp.concatenate([jnp.tile(f32(k_norm_w[0, 1]), 2), jnp.tile(f32(k_norm_w[0, 2]), 2)])[None, :]
    knw_c = jnp.tile(f32(k_norm_w[0, 0]), 2)[None, :]

    cp = pltpu.CompilerParams(dimension_semantics=("arbitrary", "arbitrary"),
                              vmem_limit_bytes=VMEM_LIMIT)
    cp_seq = pltpu.CompilerParams(dimension_semantics=("arbitrary",), vmem_limit_bytes=VMEM_LIMIT)
    row_blk = lambda n: pl.BlockSpec((1, TM, n), lambda b, s: (b, s, 0))
    tab_blk = pl.BlockSpec((TM, LANES), lambda b, s: (s, 0))
    vt_blk = pl.BlockSpec((1, TM // VCHUNK, LANES, VCHUNK), lambda b, s: (b, s, 0, 0))
    sds = jax.ShapeDtypeStruct

    (qt4, kc_raw, vc_raw, ksf, vst, kw, vwt, gates, zs, oconv) = pl.pallas_call(
        _proj_kernel,
        grid=(B, nst),
        in_specs=[row_blk(D), _full((1, D)), _once((D, D_NSA)), _once((D, 7 * LANES)),
                  _once((D, n_rest)), _full((1, 256)), _full((1, 256)),
                  tab_blk, tab_blk, tab_blk, _full((256, 256)), _full((3, D_CONV)), _full((1, D_CONV))],
        out_specs=[pl.BlockSpec((1, 4, LANES, TM), lambda b, s: (b, 0, 0, s)),
                   row_blk(LANES), row_blk(LANES), row_blk(2 * LANES), vt_blk, row_blk(LANES), vt_blk,
                   row_blk(LANES), row_blk(D_NSA), row_blk(D_CONV)],
        out_shape=[sds((B, 4, LANES, S), BF16), sds((B, S, LANES), F32), sds((B, S, LANES), F32),
                   sds((B, S, 2 * LANES), BF16), sds((B, S // VCHUNK, LANES, VCHUNK), BF16),
                   sds((B, S, LANES), BF16), sds((B, S // VCHUNK, LANES, VCHUNK), BF16),
                   sds((B, S, LANES), F32), sds((B, S, D_NSA), BF16), sds((B, S, D_CONV), BF16)],
        scratch_shapes=[pltpu.VMEM((TM + 16, D_CONV), F32)],
        compiler_params=cp,
        name="nsa_proj",
    )(x, f32(norm_w), wq, wkv, wr, qnw, knw_sw, rc, rsa, rsb, mbd, f32(conv_w[0]), f32(conv_b))

    cmp_pos = np.arange(ncp) * CMP_STRIDE + (CMP_LEN - 1)
    crc, crsa, crsb = _rope_tables(cmp_pos)
    kpa, kpb, kw1a, kw1b, kb1, kw2 = _cmp_weights(f32(cmp_k_pos[0]), f32(cmp_k_w1[0]),
                                                  f32(cmp_k_b1[0]), f32(cmp_k_w2[0]))
    vpa, vpb, vw1a, vw1b, vb1, vw2 = _cmp_weights(f32(cmp_v_pos[0]), f32(cmp_v_w1[0]),
                                                  f32(cmp_v_b1[0]), f32(cmp_v_w2[0]))
    flat = CMP_STRIDE * LANES
    h_blk = pl.BlockSpec((1, S, LANES), lambda b: (b, 0, 0))
    kc, vct = pl.pallas_call(
        functools.partial(_cmp_kernel, ncp=ncp),
        grid=(B,),
        in_specs=[h_blk, h_blk] + [_full((1, flat))] * 4
                 + [_full((flat, 256)), _full((flat, 256)), _full((1, 256)), _full((256, LANES))] * 2
                 + [_full((1, LANES)), _full((ncp, LANES)), _full((ncp, LANES)), _full((ncp, LANES)),
                    _full((256, 256))],
        out_specs=[pl.BlockSpec((1, ncp, LANES), lambda b: (b, 0, 0)),
                   pl.BlockSpec((1, LANES, ncp), lambda b: (b, 0, 0))],
        out_shape=[sds((B, ncp, LANES), BF16), sds((B, LANES, ncp), BF16)],
        compiler_params=cp_seq,
        name="nsa_compress",
    )(kc_raw, vc_raw, kpa, kpb, vpa, vpb,
      kw1a, kw1b, kb1, kw2, vw1a, vw1b, vb1, vw2, knw_c, crc, crsa, crsb, mbd)

    score_bound = (HEAD_DIM * Q_SCALE) * jnp.max(jnp.abs(f32(q_norm_w))) * jnp.max(jnp.abs(f32(k_norm_w)))
    n_tiles = S // TQ

    def attention(bounded, r):
        last_pos = (r + 1) * TQ
        win_pos = max(last_pos, WINDOW + TQ)
        nblk = min(SEL_BLOCK, -(-last_pos // (8 * SEL_BLOCK)) * 8)
        return pl.pallas_call(
            functools.partial(_attn_kernel, ncp=ncp, bounded=bounded, tile=r,
                              ncv=min(ncp, last_pos // CMP_STRIDE), nblk=nblk),
            grid=(B,),
            in_specs=[pl.BlockSpec((1, 4, LANES, TQ), lambda b: (b, 0, 0, r)),
                      pl.BlockSpec((1, ncp, LANES), lambda b: (b, 0, 0)),
                      pl.BlockSpec((1, LANES, ncp), lambda b: (b, 0, 0)),
                      pl.BlockSpec((1, last_pos, 2 * LANES), lambda b: (b, 0, 0)),
                      pl.BlockSpec((1, last_pos // VCHUNK, LANES, VCHUNK), lambda b: (b, 0, 0, 0)),
                      pl.BlockSpec((1, win_pos, LANES), lambda b: (b, 0, 0)),
                      pl.BlockSpec((1, win_pos // VCHUNK, LANES, VCHUNK), lambda b: (b, 0, 0, 0)),
                      pl.BlockSpec((1, TQ, LANES), lambda b: (b, r, 0)),
                      pl.BlockSpec((1, TQ, D_NSA), lambda b: (b, r, 0)),
                      pl.BlockSpec((SEL_BLOCK, ncp), lambda b: (0, 0))],
            out_specs=pl.BlockSpec((1, TQ, D_NSA), lambda b: (b, 0, 0)),
            out_shape=sds((B, TQ, D_NSA), BF16),
            scratch_shapes=[pltpu.VMEM((KV_GROUPS, LANES, TQ), BF16),
                            pltpu.VMEM((2, KV_GROUPS, TK, GROUP_HEADS * TQ), F32),
                            pltpu.VMEM((2, KV_GROUPS, 1, GROUP_HEADS * TQ), F32)],
            compiler_params=cp_seq,
            name=("nsa_attn_bounded_r%d" if bounded else "nsa_attn_online_r%d") % r,
        )

    attn_args = (qt4, kc, vct, ksf, vst, kw, vwt, gates, zs, _overlap_t(ncp))
    all_tiles = lambda bounded: (lambda *a: tuple(attention(bounded, r)(*a) for r in range(n_tiles)))
    o_nsa = lax.cond(score_bound <= MAX_SAFE_SCORE, all_tiles(True), all_tiles(False), *attn_args)

    out_blk = lambda n: pl.BlockSpec((1, TM_OUT, n), lambda b, s: (b, s, 0))
    on_blk = lambda k: pl.BlockSpec((1, TQ, D_NSA), lambda b, s: (
        jnp.where(s >= k // (TM_OUT // TQ), b, jnp.maximum(b - 1, 0)), 0, 0))
    out = pl.pallas_call(
        _out_kernel,
        grid=(B, S // TM_OUT),
        in_specs=[out_blk(D)] + [on_blk(k) for k in range(n_tiles)]
                 + [out_blk(D_CONV), _full((D_NSA + D_CONV, D))],
        out_specs=out_blk(D),
        out_shape=sds((B, S, D), x.dtype),
        compiler_params=cp,
        name="nsa_out",
    )(x, *o_nsa, oconv, f32(w_out[0]).astype(BF16))
    return out
```

```python
import functools

import numpy as np
import jax
import jax.numpy as jnp
from jax import lax
from jax.experimental import pallas as pl
from jax.experimental.pallas import tpu as pltpu

LANES = 128
HEAD_DIM = 64
NSA_HEADS = 8
KV_GROUPS = 2
GROUP_HEADS = NSA_HEADS // KV_GROUPS
D_NSA = NSA_HEADS * HEAD_DIM
D_CONV = 512
N_BRANCH = 3
ROT_DIM = HEAD_DIM // 4
ROPE_THETA = 500000.0
CMP_LEN = 32
CMP_STRIDE = 16
CMP_HIDDEN = 2 * HEAD_DIM
SEL_BLOCK = 64
SEL_TOPK = 16
WINDOW = 512
EPS = 1e-6
NEG_INF = -1e30
N_FORCED = 3
TAKEN = -2.0
SCALE = HEAD_DIM ** -0.5
Q_SCALE = SCALE * float(np.log2(np.e))

TM = 1024
TM_OUT = 1024
TQ = 256
TK = 2048
VCHUNK = 128
VMEM_LIMIT = 56 * 1024 * 1024
MAX_SAFE_SCORE = 50.0

BF16 = jnp.bfloat16
F32 = jnp.float32


def _dot(a, b):
    return jnp.dot(a, b, preferred_element_type=F32)


def _split3(a):
    hi = a.astype(BF16)
    r1 = a - hi.astype(F32)
    mid = r1.astype(BF16)
    lo = (r1 - mid.astype(F32)).astype(BF16)
    return hi, mid, lo


def _group_mean(sq, mbd):
    return _dot(sq.astype(BF16), mbd)


def _rope(xn, c, sa, sb):
    return xn * c + pltpu.roll(xn, 8, 1) * sa + pltpu.roll(xn, LANES - 8, 1) * sb


def _silu(z):
    return z * (1.0 / (1.0 + jnp.exp(-z)))


def _proj_kernel(x_ref, nw_ref, wq_ref, wkv_ref, wr_ref, qnw_ref, knw_ref,
                 rc_ref, rsa_ref, rsb_ref, mbd_ref, cw_ref, cb_ref,
                 qt_ref, kc_ref, vc_ref, ksf_ref, vst_ref, kw_ref, vwt_ref,
                 gate_ref, zs_ref, oconv_ref, ubuf):
    si = pl.program_id(1)

    @pl.when(si == 0)
    def _():
        ubuf[0:8, :] = jnp.zeros((8, D_CONV), F32)

    x = x_ref[0]
    ms = jnp.mean(x * x, axis=-1, keepdims=True)
    h = (x * lax.rsqrt(ms + EPS) * nw_ref[...]).astype(BF16)

    rc, rsa, rsb = rc_ref[...], rsa_ref[...], rsb_ref[...]
    mbd = mbd_ref[...]

    pq = _dot(h, wq_ref[...])
    for pair in range(2):
        blk = pq[:, pair * 256:(pair + 1) * 256]
        msq = _group_mean(blk * blk, mbd)
        qn = blk * lax.rsqrt(msq + EPS) * qnw_ref[...]
        for half in range(2):
            t = qn[:, half * LANES:(half + 1) * LANES]
            qt_ref[0, pair * 2 + half] = (_rope(t, rc, rsa, rsb) * Q_SCALE).T.astype(BF16)

    pkv = _dot(h, wkv_ref[...])
    kc_ref[0] = pkv[:, 0:128]
    vc_ref[0] = pkv[:, 128:256]
    ksw = jnp.concatenate([pkv[:, 256:384], pkv[:, 512:640]], axis=1)
    msk = _group_mean(ksw * ksw, mbd)
    kn = ksw * lax.rsqrt(msk + EPS) * knw_ref[...]
    ks = _rope(kn[:, 0:128], rc, rsa, rsb)
    kw = _rope(kn[:, 128:256], rc, rsa, rsb)
    row = lax.broadcasted_iota(jnp.int32, (TM, LANES), 0) + si * TM
    lane = lax.broadcasted_iota(jnp.int32, (TM, LANES), 1)
    onehot = jnp.where(lane == (row >> 6), 1.0, 0.0)
    ksf_ref[0, :, 0:128] = ks.astype(BF16)
    ksf_ref[0, :, 128:256] = onehot.astype(BF16)
    kw_ref[0] = kw.astype(BF16)
    vs = pkv[:, 384:512]
    vw = pkv[:, 640:768]
    for j in range(TM // VCHUNK):
        vst_ref[0, j] = vs[j * VCHUNK:(j + 1) * VCHUNK, :].T.astype(BF16)
        vwt_ref[0, j] = vw[j * VCHUNK:(j + 1) * VCHUNK, :].T.astype(BF16)

    gate_ref[0] = 1.0 / (1.0 + jnp.exp(-pkv[:, 768:896]))
    zs_ref[0] = _silu(_dot(h, wr_ref[:, 0:D_NSA])).astype(BF16)

    pcv = _dot(h, wr_ref[:, D_NSA:D_NSA + 4 * D_CONV])
    u = pcv[:, 1024:1536] * pcv[:, 0:512]
    ubuf[8:8 + TM, :] = u
    u1 = ubuf[7:7 + TM, :]
    u2 = ubuf[6:6 + TM, :]
    conv = cw_ref[0:1, :] * u2 + cw_ref[1:2, :] * u1 + cw_ref[2:3, :] * u + cb_ref[...]
    oconv_ref[0] = (pcv[:, 512:1024] * conv * _silu(pcv[:, 1536:2048])).astype(BF16)
    ubuf[0:8, :] = ubuf[TM:TM + 8, :]


def _cmp_kernel(hk_ref, hv_ref, pka_ref, pkb_ref, pva_ref, pvb_ref,
                wk1a_ref, wk1b_ref, bk1_ref, wk2_ref,
                wv1a_ref, wv1b_ref, bv1_ref, wv2_ref,
                knw_ref, rc_ref, rsa_ref, rsb_ref, mbd_ref,
                kc_ref, vct_ref, *, ncp):
    def first_layer(h_ref, p_ref, w_ref):
        acc = None
        for l in range(0, CMP_STRIDE, 2):
            lhs = jnp.concatenate([h_ref[0, pl.ds(l, ncp, stride=CMP_STRIDE), :],
                                   h_ref[0, pl.ds(l + 1, ncp, stride=CMP_STRIDE), :]], axis=1)
            lhs = (lhs + p_ref[:, l * LANES:(l + 2) * LANES]).astype(BF16)
            part = _dot(lhs, w_ref[l * LANES:(l + 2) * LANES, :])
            acc = part if acc is None else acc + part
        return acc

    def mlp(h_ref, pa_ref, pb_ref, w1a_ref, w1b_ref, b1_ref, w2_ref):
        p = first_layer(h_ref, pa_ref, w1a_ref)
        q = first_layer(h_ref, pb_ref, w1b_ref)
        pre = p + pltpu.roll(q, ncp - 1, 0) + b1_ref[...]
        return _dot(_silu(pre).astype(BF16), w2_ref[...])

    kc = mlp(hk_ref, pka_ref, pkb_ref, wk1a_ref, wk1b_ref, bk1_ref, wk2_ref)
    msk = _group_mean(kc * kc, mbd_ref[0:LANES, 0:LANES])
    kn = kc * lax.rsqrt(msk + EPS) * knw_ref[...]
    kc_ref[0] = _rope(kn, rc_ref[...], rsa_ref[...], rsb_ref[...]).astype(BF16)
    vc = mlp(hv_ref, pva_ref, pvb_ref, wv1a_ref, wv1b_ref, bv1_ref, wv2_ref)
    vct_ref[0] = vc.T.astype(BF16)


def _attn_kernel(qt_ref, kc_ref, vct_ref, ksf_ref, vst_ref, kw_ref, vwt_ref,
                 gate_ref, zs_ref, ovt_ref, o_ref, pen_sc, s_sc, cm_sc, *, ncp, bounded, tile, ncv, nblk):
    i = tile
    t0 = i * TQ
    last_pos = t0 + TQ
    rows = GROUP_HEADS * TQ
    groups = range(KV_GROUPS)

    q_all = jnp.concatenate([qt_ref[0, j] for j in range(GROUP_HEADS)], axis=1)
    frow = lax.broadcasted_iota(jnp.int32, (LANES, rows), 0)
    qg = [jnp.where(frow < HEAD_DIM, q_all, 0), jnp.where(frow >= HEAD_DIM, q_all, 0)]
    t_q = t0 + lax.broadcasted_iota(jnp.int32, (1, TQ), 1)

    def fill_where(a, mask, fill):
        return jnp.concatenate([jnp.where(mask, a[:, r * TQ:(r + 1) * TQ], fill)
                                for r in range(GROUP_HEADS)], axis=1)

    vrows = lambda v, g: v[g * HEAD_DIM:(g + 1) * HEAD_DIM, :]

    s_cmp = [_dot(kc_ref[0, 0:ncv, :], qg[g]) for g in groups]

    c_idx = lax.broadcasted_iota(jnp.int32, (ncv, 1), 0)
    mask_c = ((c_idx * CMP_STRIDE + (CMP_LEN - 1)) <= t_q) & (c_idx < ncp - 1)
    n_open = max(0, min((t0 - (CMP_LEN - 1)) // CMP_STRIDE + 1, ncp - 1)) // 8 * 8

    def masked_c(a, fill):
        tail = fill_where(a[n_open:], mask_c[n_open:], fill)
        return jnp.concatenate([a[0:n_open], tail], axis=0) if n_open else tail

    p_c, o_c = [], []
    for g in groups:
        if bounded:
            e_c = masked_c(jnp.exp2(s_cmp[g]), 0.0)
        else:
            sm = masked_c(s_cmp[g], NEG_INF)
            e_c = masked_c(jnp.exp2(sm - jnp.max(sm, axis=0, keepdims=True)), 0.0)
        l_c = jnp.sum(e_c, axis=0, keepdims=True)
        p = e_c * (1.0 / jnp.where(l_c > 0.0, l_c, 1.0))
        p_c.append(p)
        o_c.append(_dot(vrows(vct_ref[0], g)[:, 0:ncv], p.astype(BF16)))

    if last_pos // SEL_BLOCK <= SEL_TOPK:
        pen_sc[...] = jnp.zeros(pen_sc.shape, BF16)
    else:
        n_idx = lax.broadcasted_iota(jnp.int32, (nblk, TQ), 0)
        tq = t0 + lax.broadcasted_iota(jnp.int32, (nblk, TQ), 1)
        cur = tq >> 6
        forced = (n_idx == 0) | (n_idx == cur) | (n_idx == cur - 1)
        visible = (n_idx << 6) <= tq
        ovt = ovt_ref[0:nblk, 0:ncv]
        left = []
        for g in groups:
            psum = p_c[g][:, 0:TQ]
            for r in range(1, GROUP_HEADS):
                psum = psum + p_c[g][:, r * TQ:(r + 1) * TQ]
            hi, mid, lo = _split3(psum)
            imp = _dot(ovt, hi) + _dot(ovt, mid) + _dot(ovt, lo)
            left.append(jnp.where(forced, TAKEN, jnp.where(visible, imp, -1.0)))
        for _ in range(SEL_TOPK - N_FORCED):
            for g in groups:
                top = jnp.max(left[g], axis=0, keepdims=True)
                first = jnp.min(jnp.where(left[g] == top, n_idx, nblk), axis=0, keepdims=True)
                left[g] = jnp.where(n_idx == first, TAKEN, left[g])
        for g in groups:
            pen_sc[g, 0:nblk, :] = jnp.where(left[g] == TAKEN, 0.0, NEG_INF).astype(BF16)
            pen_sc[g, nblk:LANES, :] = jnp.zeros((LANES - nblk, TQ), BF16)

    nwc = WINDOW // VCHUNK + TQ // VCHUNK
    cw = max(i * (TQ // VCHUNK) - WINDOW // VCHUNK, 0)
    w0 = cw * VCHUNK
    kwin = kw_ref[0, w0:w0 + nwc * VCHUNK, :]
    s_win = [_dot(kwin, qg[g]) for g in groups]

    q_aug = [jnp.concatenate([qg[g], jnp.concatenate([pen_sc[g]] * GROUP_HEADS, axis=1)], axis=0)
             for g in groups]

    def window_branch():
        vwin = jnp.concatenate([vwt_ref[0, cw + j] for j in range(nwc)], axis=1)
        kpos = w0 + lax.broadcasted_iota(jnp.int32, (nwc * VCHUNK, 1), 0)

        def masked_w(a, fill):
            if t0 < WINDOW:
                return fill_where(a, (kpos <= t_q) & (kpos > t_q - WINDOW), fill)
            lo, hi = slice(0, TQ), slice(WINDOW, WINDOW + TQ)
            return jnp.concatenate([fill_where(a[lo], kpos[lo] > t_q - WINDOW, fill), a[TQ:WINDOW],
                                    fill_where(a[hi], kpos[hi] <= t_q, fill)], axis=0)

        o_w, l_w = [], []
        for g in groups:
            if bounded:
                p_w = masked_w(jnp.exp2(s_win[g]), 0.0)
            else:
                sw = masked_w(s_win[g], NEG_INF)
                p_w = jnp.exp2(sw - jnp.max(sw, axis=0, keepdims=True))
            v_ext = jnp.concatenate([vrows(vwin, g), jnp.ones((16, nwc * VCHUNK), BF16)], axis=0)
            ol = _dot(v_ext, p_w.astype(BF16))
            o_w.append(ol[0:HEAD_DIM, :])
            l_w.append(ol[HEAD_DIM:HEAD_DIM + 1, :])
        return o_w, l_w

    n_full = t0 // TK
    width = lambda c: TK if c < n_full else last_pos - n_full * TK

    def v_chunk(c, g):
        cv = c * (TK // VCHUNK)
        return jnp.concatenate([vrows(vst_ref[0, cv + j], g) for j in range(width(c) // VCHUNK)], axis=1)

    def qk(c, slot, g):
        s = _dot(ksf_ref[0, c * TK:c * TK + width(c), :], q_aug[g])
        s_sc[slot, g, 0:width(c)] = s
        if not bounded:
            cm_sc[slot, g] = jnp.max(s, axis=0, keepdims=True)

    def sm_pv(c, slot, g, state, masked):
        m_p, l_p, acc = state
        s = s_sc[slot, g, 0:width(c)]
        if masked:
            kpos = t0 + lax.broadcasted_iota(jnp.int32, (TQ, 1), 0)
            below = [s[0:width(c) - TQ]] if width(c) > TQ else []
            s = jnp.concatenate(below + [fill_where(s[width(c) - TQ:], kpos <= t_q, NEG_INF)], axis=0)
        if bounded:
            p = jnp.exp2(s)
            return m_p, l_p + jnp.sum(p, axis=0, keepdims=True), acc + _dot(v_chunk(c, g), p.astype(BF16))
        cm = jnp.max(s, axis=0, keepdims=True) if masked else cm_sc[slot, g]
        m_n = jnp.maximum(m_p, cm)
        alpha = jnp.exp2(m_p - m_n)
        p = jnp.exp2(s - m_n)
        l_n = alpha * l_p + jnp.sum(p, axis=0, keepdims=True)
        acc = alpha * acc + _dot(v_chunk(c, g), p.astype(BF16))
        return m_n, l_n, acc

    def step(c, slot, carry):
        out = []
        for g in groups:
            qk(c + 1, 1 - slot, g)
            out.append(sm_pv(c, slot, g, carry[g], False))
        return tuple(out)

    def last(slot, carry):
        return tuple(sm_pv(n_full, slot, g, carry[g], True) for g in groups)

    for g in groups:
        qk(0, 0, g)
    o_w, l_w = window_branch()
    carry = tuple((jnp.full((1, rows), 0.0 if bounded else NEG_INF, F32), jnp.zeros((1, rows), F32),
                   jnp.zeros((HEAD_DIM, rows), F32)) for _ in groups)
    for c in range(n_full):
        carry = step(c, c % 2, carry)
    carry = last(n_full % 2, carry)
    sel_out = [(carry[g][1], carry[g][2]) for g in groups]

    g_t = gate_ref[0].T
    slabs = []
    for g in groups:
        def gate_row(br):
            base = g * (N_BRANCH * GROUP_HEADS) + br * GROUP_HEADS
            return jnp.concatenate([g_t[base + r:base + r + 1, :] for r in range(GROUP_HEADS)], axis=1)
        l_s, o_s = sel_out[g]
        og = o_c[g] * gate_row(0) + o_s * (gate_row(1) / l_s) + o_w[g] * (gate_row(2) / l_w[g])
        slabs += [og[:, r * TQ:(r + 1) * TQ] for r in range(GROUP_HEADS)]
    out_t = jnp.concatenate(slabs, axis=0)
    o_ref[0] = (out_t.T * zs_ref[0].astype(F32)).astype(BF16)


def _out_kernel(x_ref, *refs):
    on_refs, (oc_ref, w_ref, o_ref) = refs[:-3], refs[-3:]
    per_tile = TM_OUT // TQ
    for k in range(len(on_refs) // per_tile):
        @pl.when(pl.program_id(1) == k)
        def _(k=k):
            on = jnp.concatenate([r[0] for r in on_refs[k * per_tile:(k + 1) * per_tile]], axis=0)
            acc = _dot(on, w_ref[0:D_NSA, :]) + _dot(oc_ref[0], w_ref[D_NSA:D_NSA + D_CONV, :])
            o_ref[0] = x_ref[0] + acc


def _rope_tables(pos):
    inv_freq = (np.float32(ROPE_THETA) ** (-np.arange(0, ROT_DIM, 2, dtype=np.float32) / ROT_DIM)).astype(np.float32)
    ang = pos.astype(np.float32)[:, None] * inv_freq[None, :]
    cos, sin = np.cos(ang), np.sin(ang)
    n = pos.shape[0]
    ones = np.ones((n, HEAD_DIM - ROT_DIM), np.float32)
    zeros = np.zeros((n, HEAD_DIM - ROT_DIM), np.float32)
    z8 = np.zeros((n, ROT_DIM // 2), np.float32)
    c = np.concatenate([cos, cos, ones], axis=1)
    sa = np.concatenate([z8, sin, zeros], axis=1)
    sb = np.concatenate([-sin, z8, zeros], axis=1)
    return tuple(jnp.asarray(np.concatenate([t, t], axis=1), F32) for t in (c, sa, sb))


def _overlap_t(ncp):
    cs = np.arange(ncp) * CMP_STRIDE
    ce = cs + CMP_LEN
    ss = np.arange(SEL_BLOCK) * SEL_BLOCK
    se = ss + SEL_BLOCK
    ov = np.clip(np.minimum(ce[None, :], se[:, None]) - np.maximum(cs[None, :], ss[:, None]), 0, None)
    ov = ov.astype(np.float32) / CMP_LEN
    ov[:, ncp - 1] = 0.0
    return jnp.asarray(ov, BF16)


def _cmp_weights(pos, w1, b1, w2):
    half = CMP_STRIDE
    def big_w1(w):
        w = w.reshape(half, 1, HEAD_DIM, 1, CMP_HIDDEN)
        eye = jnp.eye(KV_GROUPS, dtype=F32).reshape(1, KV_GROUPS, 1, KV_GROUPS, 1)
        return (w * eye).reshape(half * KV_GROUPS * HEAD_DIM, KV_GROUPS * CMP_HIDDEN).astype(BF16)
    def big_pos(p):
        return jnp.broadcast_to(p[:, None, :], (half, KV_GROUPS, HEAD_DIM)).reshape(1, -1)
    w1a, w1b = w1[:half * HEAD_DIM], w1[half * HEAD_DIM:]
    eye2 = jnp.eye(KV_GROUPS, dtype=F32)
    w2b = (w2[None, :, None, :] * eye2[:, None, :, None]).reshape(KV_GROUPS * CMP_HIDDEN,
                                                                 KV_GROUPS * HEAD_DIM).astype(BF16)
    b1b = jnp.tile(b1, KV_GROUPS)[None, :]
    return big_pos(pos[:half]), big_pos(pos[half:]), big_w1(w1a), big_w1(w1b), b1b, w2b


def _full(shape):
    nd = len(shape)
    return pl.BlockSpec(shape, lambda *_: (0,) * nd)


def _once(shape):
    nd = len(shape)
    return pl.BlockSpec(shape, lambda *_: (0,) * nd, pipeline_mode=pl.Buffered(1))


def kernel(x, norm_w, w_in, q_norm_w, k_norm_w, cmp_k_pos, cmp_k_w1, cmp_k_b1, cmp_k_w2,
           cmp_v_pos, cmp_v_w1, cmp_v_b1, cmp_v_w2, conv_w, conv_b, w_out):
    B, S, D = x.shape
    assert norm_w.shape[0] == 1, "single layer"
    assert S % TM == 0 and S % TM_OUT == 0 and TK % TQ == 0 and TM_OUT % TQ == 0 and S // SEL_BLOCK <= SEL_BLOCK and S >= WINDOW + TQ
    ncp = S // CMP_STRIDE
    nst = S // TM
    f32 = lambda a: a.astype(F32)

    w = w_in[0].astype(BF16)
    head_order = []
    for j in range(GROUP_HEADS):
        head_order += [j, GROUP_HEADS + j]
    wq = w[:, :D_NSA].reshape(D, NSA_HEADS, HEAD_DIM)[:, jnp.array(head_order), :].reshape(D, D_NSA)
    gate_cols = []
    for g in range(KV_GROUPS):
        for br in range(N_BRANCH):
            for r in range(GROUP_HEADS):
                gate_cols.append((g * GROUP_HEADS + r) * N_BRANCH + br)
    g0 = D_NSA + 6 * LANES
    wg = w[:, g0:g0 + NSA_HEADS * N_BRANCH][:, jnp.array(gate_cols)]
    wg = jnp.pad(wg, ((0, 0), (0, LANES - NSA_HEADS * N_BRANCH)))
    rest0 = g0 + NSA_HEADS * N_BRANCH
    wkv, wr = jnp.concatenate([w[:, D_NSA:g0], wg], axis=1), w[:, rest0:]
    n_rest = wr.shape[1]

    rc, rsa, rsb = _rope_tables(np.arange(S))
    mbd = jnp.asarray(np.kron(np.eye(4), np.full((HEAD_DIM, HEAD_DIM), 1.0 / HEAD_DIM)), BF16)
    qnw = jnp.tile(f32(q_norm_w[0]), 4)[None, :]
    knw_sw = jnp.concatenate([jnp.tile(f32(k_norm_w[0, 1]), 2), jnp.tile(f32(k_norm_w[0, 2]), 2)])[None, :]
    knw_c = jnp.tile(f32(k_norm_w[0, 0]), 2)[None, :]

    cp = pltpu.CompilerParams(dimension_semantics=("arbitrary", "arbitrary"),
                              vmem_limit_bytes=VMEM_LIMIT)
    cp_seq = pltpu.CompilerParams(dimension_semantics=("arbitrary",), vmem_limit_bytes=VMEM_LIMIT)
    row_blk = lambda n: pl.BlockSpec((1, TM, n), lambda b, s: (b, s, 0))
    tab_blk = pl.BlockSpec((TM, LANES), lambda b, s: (s, 0))
    vt_blk = pl.BlockSpec((1, TM // VCHUNK, LANES, VCHUNK), lambda b, s: (b, s, 0, 0))
    sds = jax.ShapeDtypeStruct

    (qt4, kc_raw, vc_raw, ksf, vst, kw, vwt, gates, zs, oconv) = pl.pallas_call(
        _proj_kernel,
        grid=(B, nst),
        in_specs=[row_blk(D), _full((1, D)), _once((D, D_NSA)), _once((D, 7 * LANES)),
                  _once((D, n_rest)), _full((1, 256)), _full((1, 256)),
                  tab_blk, tab_blk, tab_blk, _full((256, 256)), _full((3, D_CONV)), _full((1, D_CONV))],
        out_specs=[pl.BlockSpec((1, 4, LANES, TM), lambda b, s: (b, 0, 0, s)),
                   row_blk(LANES), row_blk(LANES), row_blk(2 * LANES), vt_blk, row_blk(LANES), vt_blk,
                   row_blk(LANES), row_blk(D_NSA), row_blk(D_CONV)],
        out_shape=[sds((B, 4, LANES, S), BF16), sds((B, S, LANES), F32), sds((B, S, LANES), F32),
                   sds((B, S, 2 * LANES), BF16), sds((B, S // VCHUNK, LANES, VCHUNK), BF16),
                   sds((B, S, LANES), BF16), sds((B, S // VCHUNK, LANES, VCHUNK), BF16),
                   sds((B, S, LANES), F32), sds((B, S, D_NSA), BF16), sds((B, S, D_CONV), BF16)],
        scratch_shapes=[pltpu.VMEM((TM + 16, D_CONV), F32)],
        compiler_params=cp,
        name="nsa_proj",
    )(x, f32(norm_w), wq, wkv, wr, qnw, knw_sw, rc, rsa, rsb, mbd, f32(conv_w[0]), f32(conv_b))

    cmp_pos = np.arange(ncp) * CMP_STRIDE + (CMP_LEN - 1)
    crc, crsa, crsb = _rope_tables(cmp_pos)
    kpa, kpb, kw1a, kw1b, kb1, kw2 = _cmp_weights(f32(cmp_k_pos[0]), f32(cmp_k_w1[0]),
                                                  f32(cmp_k_b1[0]), f32(cmp_k_w2[0]))
    vpa, vpb, vw1a, vw1b, vb1, vw2 = _cmp_weights(f32(cmp_v_pos[0]), f32(cmp_v_w1[0]),
                                                  f32(cmp_v_b1[0]), f32(cmp_v_w2[0]))
    flat = CMP_STRIDE * LANES
    h_blk = pl.BlockSpec((1, S, LANES), lambda b: (b, 0, 0))
    kc, vct = pl.pallas_call(
        functools.partial(_cmp_kernel, ncp=ncp),
        grid=(B,),
        in_specs=[h_blk, h_blk] + [_full((1, flat))] * 4
                 + [_full((flat, 256)), _full((flat, 256)), _full((1, 256)), _full((256, LANES))] * 2
                 + [_full((1, LANES)), _full((ncp, LANES)), _full((ncp, LANES)), _full((ncp, LANES)),
                    _full((256, 256))],
        out_specs=[pl.BlockSpec((1, ncp, LANES), lambda b: (b, 0, 0)),
                   pl.BlockSpec((1, LANES, ncp), lambda b: (b, 0, 0))],
        out_shape=[sds((B, ncp, LANES), BF16), sds((B, LANES, ncp), BF16)],
        compiler_params=cp_seq,
        name="nsa_compress",
    )(kc_raw, vc_raw, kpa, kpb, vpa, vpb,
      kw1a, kw1b, kb1, kw2, vw1a, vw1b, vb1, vw2, knw_c, crc, crsa, crsb, mbd)

    score_bound = (HEAD_DIM * Q_SCALE) * jnp.max(jnp.abs(f32(q_norm_w))) * jnp.max(jnp.abs(f32(k_norm_w)))
    n_tiles = S // TQ

    def attention(bounded, r):
        last_pos = (r + 1) * TQ
        win_pos = max(last_pos, WINDOW + TQ)
        nblk = min(SEL_BLOCK, -(-last_pos // (8 * SEL_BLOCK)) * 8)
        return pl.pallas_call(
            functools.partial(_attn_kernel, ncp=ncp, bounded=bounded, tile=r,
                              ncv=min(ncp, last_pos // CMP_STRIDE), nblk=nblk),
            grid=(B,),
            in_specs=[pl.BlockSpec((1, 4, LANES, TQ), lambda b: (b, 0, 0, r)),
                      pl.BlockSpec((1, ncp, LANES), lambda b: (b, 0, 0)),
                      pl.BlockSpec((1, LANES, ncp), lambda b: (b, 0, 0)),
                      pl.BlockSpec((1, last_pos, 2 * LANES), lambda b: (b, 0, 0)),
                      pl.BlockSpec((1, last_pos // VCHUNK, LANES, VCHUNK), lambda b: (b, 0, 0, 0)),
                      pl.BlockSpec((1, win_pos, LANES), lambda b: (b, 0, 0)),
                      pl.BlockSpec((1, win_pos // VCHUNK, LANES, VCHUNK), lambda b: (b, 0, 0, 0)),
                      pl.BlockSpec((1, TQ, LANES), lambda b: (b, r, 0)),
                      pl.BlockSpec((1, TQ, D_NSA), lambda b: (b, r, 0)),
                      pl.BlockSpec((SEL_BLOCK, ncp), lambda b: (0, 0))],
            out_specs=pl.BlockSpec((1, TQ, D_NSA), lambda b: (b, 0, 0)),
            out_shape=sds((B, TQ, D_NSA), BF16),
            scratch_shapes=[pltpu.VMEM((KV_GROUPS, LANES, TQ), BF16),
                            pltpu.VMEM((2, KV_GROUPS, TK, GROUP_HEADS * TQ), F32),
                            pltpu.VMEM((2, KV_GROUPS, 1, GROUP_HEADS * TQ), F32)],
            compiler_params=cp_seq,
            name=("nsa_attn_bounded_r%d" if bounded else "nsa_attn_online_r%d") % r,
        )

    attn_args = (qt4, kc, vct, ksf, vst, kw, vwt, gates, zs, _overlap_t(ncp))
    all_tiles = lambda bounded: (lambda *a: tuple(attention(bounded, r)(*a) for r in range(n_tiles)))
    o_nsa = lax.cond(score_bound <= MAX_SAFE_SCORE, all_tiles(True), all_tiles(False), *attn_args)

    out_blk = lambda n: pl.BlockSpec((1, TM_OUT, n), lambda b, s: (b, s, 0))
    on_blk = lambda k: pl.BlockSpec((1, TQ, D_NSA), lambda b, s: (
        jnp.where(s >= k // (TM_OUT // TQ), b, jnp.maximum(b - 1, 0)), 0, 0))
    out = pl.pallas_call(
        _out_kernel,
        grid=(B, S // TM_OUT),
        in_specs=[out_blk(D)] + [on_blk(k) for k in range(n_tiles)]
                 + [out_blk(D_CONV), _full((D_NSA + D_CONV, D))],
        out_specs=out_blk(D),
        out_shape=sds((B, S, D), x.dtype),
        compiler_params=cp,
        name="nsa_out",
    )(x, *o_nsa, oconv, f32(w_out[0]).astype(BF16))
    return out
```

```python
import functools

import numpy as np
import jax
import jax.numpy as jnp
from jax import lax
from jax.experimental import pallas as pl
from jax.experimental.pallas import tpu as pltpu

LANES = 128
HEAD_DIM = 64
NSA_HEADS = 8
KV_GROUPS = 2
GROUP_HEADS = NSA_HEADS // KV_GROUPS
D_NSA = NSA_HEADS * HEAD_DIM
D_CONV = 512
N_BRANCH = 3
ROT_DIM = HEAD_DIM // 4
ROPE_THETA = 500000.0
CMP_LEN = 32
CMP_STRIDE = 16
CMP_HIDDEN = 2 * HEAD_DIM
SEL_BLOCK = 64
SEL_TOPK = 16
WINDOW = 512
EPS = 1e-6
NEG_INF = -1e30
N_FORCED = 3
TAKEN = -2.0
SCALE = HEAD_DIM ** -0.5
Q_SCALE = SCALE * float(np.log2(np.e))

TM = 1024
TM_OUT = 1024
TQ = 256
TQ_CALL = 512
TK = 1024
VCHUNK = 128
VMEM_LIMIT = 56 * 1024 * 1024
MAX_SAFE_SCORE = 50.0

BF16 = jnp.bfloat16
F32 = jnp.float32


def _dot(a, b):
    return jnp.dot(a, b, preferred_element_type=F32)


def _split3(a):
    hi = a.astype(BF16)
    r1 = a - hi.astype(F32)
    mid = r1.astype(BF16)
    lo = (r1 - mid.astype(F32)).astype(BF16)
    return hi, mid, lo


def _group_mean(sq, mbd):
    return _dot(sq.astype(BF16), mbd)


def _rope(xn, c, sa, sb):
    return xn * c + pltpu.roll(xn, 8, 1) * sa + pltpu.roll(xn, LANES - 8, 1) * sb


def _silu(z):
    return z * (1.0 / (1.0 + jnp.exp(-z)))


def _proj_kernel(x_ref, nw_ref, wq_ref, wkv_ref, wr_ref, qnw_ref, knw_ref,
                 rc_ref, rsa_ref, rsb_ref, mbd_ref, cw_ref, cb_ref,
                 qt_ref, kc_ref, vc_ref, ksf_ref, vst_ref, kw_ref, vwt_ref,
                 gate_ref, zs_ref, oconv_ref, ubuf):
    si = pl.program_id(1)

    @pl.when(si == 0)
    def _():
        ubuf[0:8, :] = jnp.zeros((8, D_CONV), F32)

    x = x_ref[0]
    ms = jnp.mean(x * x, axis=-1, keepdims=True)
    h = (x * lax.rsqrt(ms + EPS) * nw_ref[...]).astype(BF16)

    rc, rsa, rsb = rc_ref[...], rsa_ref[...], rsb_ref[...]
    mbd = mbd_ref[...]

    pq = _dot(h, wq_ref[...])
    for pair in range(2):
        blk = pq[:, pair * 256:(pair + 1) * 256]
        msq = _group_mean(blk * blk, mbd)
        qn = blk * lax.rsqrt(msq + EPS) * qnw_ref[...]
        for half in range(2):
            t = qn[:, half * LANES:(half + 1) * LANES]
            qt_ref[0, pair * 2 + half] = (_rope(t, rc, rsa, rsb) * Q_SCALE).T.astype(BF16)

    pkv = _dot(h, wkv_ref[...])
    kc_ref[0] = pkv[:, 0:128]
    vc_ref[0] = pkv[:, 128:256]
    ksw = jnp.concatenate([pkv[:, 256:384], pkv[:, 512:640]], axis=1)
    msk = _group_mean(ksw * ksw, mbd)
    kn = ksw * lax.rsqrt(msk + EPS) * knw_ref[...]
    ks = _rope(kn[:, 0:128], rc, rsa, rsb)
    kw = _rope(kn[:, 128:256], rc, rsa, rsb)
    row = lax.broadcasted_iota(jnp.int32, (TM, LANES), 0) + si * TM
    lane = lax.broadcasted_iota(jnp.int32, (TM, LANES), 1)
    onehot = jnp.where(lane == (row >> 6), 1.0, 0.0)
    ksf_ref[0, :, 0:128] = ks.astype(BF16)
    ksf_ref[0, :, 128:256] = onehot.astype(BF16)
    kw_ref[0] = kw.astype(BF16)
    vs = pkv[:, 384:512]
    vw = pkv[:, 640:768]
    for j in range(TM // VCHUNK):
        vst_ref[0, j] = vs[j * VCHUNK:(j + 1) * VCHUNK, :].T.astype(BF16)
        vwt_ref[0, j] = vw[j * VCHUNK:(j + 1) * VCHUNK, :].T.astype(BF16)

    gate_ref[0] = 1.0 / (1.0 + jnp.exp(-pkv[:, 768:896]))
    zs_ref[0] = _silu(_dot(h, wr_ref[:, 0:D_NSA])).astype(BF16)

    pcv = _dot(h, wr_ref[:, D_NSA:D_NSA + 4 * D_CONV])
    u = pcv[:, 1024:1536] * pcv[:, 0:512]
    ubuf[8:8 + TM, :] = u
    u1 = ubuf[7:7 + TM, :]
    u2 = ubuf[6:6 + TM, :]
    conv = cw_ref[0:1, :] * u2 + cw_ref[1:2, :] * u1 + cw_ref[2:3, :] * u + cb_ref[...]
    oconv_ref[0] = (pcv[:, 512:1024] * conv * _silu(pcv[:, 1536:2048])).astype(BF16)
    ubuf[0:8, :] = ubuf[TM:TM + 8, :]


def _cmp_kernel(hk_ref, hv_ref, pka_ref, pkb_ref, pva_ref, pvb_ref,
                wk1a_ref, wk1b_ref, bk1_ref, wk2_ref,
                wv1a_ref, wv1b_ref, bv1_ref, wv2_ref,
                knw_ref, rc_ref, rsa_ref, rsb_ref, mbd_ref,
                kc_ref, vct_ref, *, ncp):
    def first_layer(h_ref, p_ref, w_ref):
        acc = None
        for l in range(0, CMP_STRIDE, 2):
            lhs = jnp.concatenate([h_ref[0, pl.ds(l, ncp, stride=CMP_STRIDE), :],
                                   h_ref[0, pl.ds(l + 1, ncp, stride=CMP_STRIDE), :]], axis=1)
            lhs = (lhs + p_ref[:, l * LANES:(l + 2) * LANES]).astype(BF16)
            part = _dot(lhs, w_ref[l * LANES:(l + 2) * LANES, :])
            acc = part if acc is None else acc + part
        return acc

    def mlp(h_ref, pa_ref, pb_ref, w1a_ref, w1b_ref, b1_ref, w2_ref):
        p = first_layer(h_ref, pa_ref, w1a_ref)
        q = first_layer(h_ref, pb_ref, w1b_ref)
        pre = p + pltpu.roll(q, ncp - 1, 0) + b1_ref[...]
        return _dot(_silu(pre).astype(BF16), w2_ref[...])

    kc = mlp(hk_ref, pka_ref, pkb_ref, wk1a_ref, wk1b_ref, bk1_ref, wk2_ref)
    msk = _group_mean(kc * kc, mbd_ref[0:LANES, 0:LANES])
    kn = kc * lax.rsqrt(msk + EPS) * knw_ref[...]
    kc_ref[0] = _rope(kn, rc_ref[...], rsa_ref[...], rsb_ref[...]).astype(BF16)
    vc = mlp(hv_ref, pva_ref, pvb_ref, wv1a_ref, wv1b_ref, bv1_ref, wv2_ref)
    vct_ref[0] = vc.T.astype(BF16)


def _attn_kernel(*refs, ncp, bounded, tiles):
    for sub, tile in enumerate(tiles):
        _attn_tile(*refs, ncp=ncp, bounded=bounded, tile=tile, sub=sub)


def _attn_tile(qt_ref, kc_ref, vct_ref, ksf_ref, vst_ref, kw_ref, vwt_ref,
               gate_ref, zs_ref, ovt_ref, o_ref, pen_sc, s_sc, cm_sc, *, ncp, bounded, tile, sub):
    i = tile
    t0 = i * TQ
    last_pos = t0 + TQ
    ncv = min(ncp, last_pos // CMP_STRIDE)
    nblk = min(SEL_BLOCK, -(-last_pos // (8 * SEL_BLOCK)) * 8)
    rows = GROUP_HEADS * TQ
    groups = range(KV_GROUPS)
    sub_q = slice(sub * TQ, (sub + 1) * TQ)

    q_all = jnp.concatenate([qt_ref[0, j, :, sub_q] for j in range(GROUP_HEADS)], axis=1)
    frow = lax.broadcasted_iota(jnp.int32, (LANES, rows), 0)
    qg = [jnp.where(frow < HEAD_DIM, q_all, 0), jnp.where(frow >= HEAD_DIM, q_all, 0)]
    t_q = t0 + lax.broadcasted_iota(jnp.int32, (1, TQ), 1)

    def fill_where(a, mask, fill):
        return jnp.concatenate([jnp.where(mask, a[:, r * TQ:(r + 1) * TQ], fill)
                                for r in range(GROUP_HEADS)], axis=1)

    vrows = lambda v, g: v[g * HEAD_DIM:(g + 1) * HEAD_DIM, :]

    s_cmp = [_dot(kc_ref[0, 0:ncv, :], qg[g]) for g in groups]

    c_idx = lax.broadcasted_iota(jnp.int32, (ncv, 1), 0)
    mask_c = ((c_idx * CMP_STRIDE + (CMP_LEN - 1)) <= t_q) & (c_idx < ncp - 1)
    n_open = max(0, min((t0 - (CMP_LEN - 1)) // CMP_STRIDE + 1, ncp - 1)) // 8 * 8

    def masked_c(a, fill):
        tail = fill_where(a[n_open:], mask_c[n_open:], fill)
        return jnp.concatenate([a[0:n_open], tail], axis=0) if n_open else tail

    p_c, o_c = [], []
    for g in groups:
        if bounded:
            e_c = masked_c(jnp.exp2(s_cmp[g]), 0.0)
        else:
            sm = masked_c(s_cmp[g], NEG_INF)
            e_c = masked_c(jnp.exp2(sm - jnp.max(sm, axis=0, keepdims=True)), 0.0)
        l_c = jnp.sum(e_c, axis=0, keepdims=True)
        p = e_c * (1.0 / jnp.where(l_c > 0.0, l_c, 1.0))
        p_c.append(p)
        o_c.append(_dot(vrows(vct_ref[0], g)[:, 0:ncv], p.astype(BF16)))

    if last_pos // SEL_BLOCK <= SEL_TOPK:
        pen_sc[...] = jnp.zeros(pen_sc.shape, BF16)
    else:
        n_idx = lax.broadcasted_iota(jnp.int32, (nblk, TQ), 0)
        tq = t0 + lax.broadcasted_iota(jnp.int32, (nblk, TQ), 1)
        cur = tq >> 6
        forced = (n_idx == 0) | (n_idx == cur) | (n_idx == cur - 1)
        visible = (n_idx << 6) <= tq
        ovt = ovt_ref[0:nblk, 0:ncv]
        left = []
        for g in groups:
            psum = p_c[g][:, 0:TQ]
            for r in range(1, GROUP_HEADS):
                psum = psum + p_c[g][:, r * TQ:(r + 1) * TQ]
            hi, mid, lo = _split3(psum)
            imp = _dot(ovt, hi) + _dot(ovt, mid) + _dot(ovt, lo)
            left.append(jnp.where(forced, TAKEN, jnp.where(visible, imp, -1.0)))
        for _ in range(SEL_TOPK - N_FORCED):
            for g in groups:
                top = jnp.max(left[g], axis=0, keepdims=True)
                first = jnp.min(jnp.where(left[g] == top, n_idx, nblk), axis=0, keepdims=True)
                left[g] = jnp.where(n_idx == first, TAKEN, left[g])
        for g in groups:
            pen_sc[g, 0:nblk, :] = jnp.where(left[g] == TAKEN, 0.0, NEG_INF).astype(BF16)
            pen_sc[g, nblk:LANES, :] = jnp.zeros((LANES - nblk, TQ), BF16)

    nwc = WINDOW // VCHUNK + TQ // VCHUNK
    cw = max(i * (TQ // VCHUNK) - WINDOW // VCHUNK, 0)
    w0 = cw * VCHUNK
    kwin = kw_ref[0, w0:w0 + nwc * VCHUNK, :]
    s_win = [_dot(kwin, qg[g]) for g in groups]

    q_aug = [jnp.concatenate([qg[g], jnp.concatenate([pen_sc[g]] * GROUP_HEADS, axis=1)], axis=0)
             for g in groups]

    def window_branch():
        vwin = jnp.concatenate([vwt_ref[0, cw + j] for j in range(nwc)], axis=1)
        kpos = w0 + lax.broadcasted_iota(jnp.int32, (nwc * VCHUNK, 1), 0)

        def masked_w(a, fill):
            if t0 < WINDOW:
                return fill_where(a, (kpos <= t_q) & (kpos > t_q - WINDOW), fill)
            lo, hi = slice(0, TQ), slice(WINDOW, WINDOW + TQ)
            return jnp.concatenate([fill_where(a[lo], kpos[lo] > t_q - WINDOW, fill), a[TQ:WINDOW],
                                    fill_where(a[hi], kpos[hi] <= t_q, fill)], axis=0)

        o_w, l_w = [], []
        for g in groups:
            if bounded:
                p_w = masked_w(jnp.exp2(s_win[g]), 0.0)
            else:
                sw = masked_w(s_win[g], NEG_INF)
                p_w = jnp.exp2(sw - jnp.max(sw, axis=0, keepdims=True))
            v_ext = jnp.concatenate([vrows(vwin, g), jnp.ones((16, nwc * VCHUNK), BF16)], axis=0)
            ol = _dot(v_ext, p_w.astype(BF16))
            o_w.append(ol[0:HEAD_DIM, :])
            l_w.append(ol[HEAD_DIM:HEAD_DIM + 1, :])
        return o_w, l_w

    n_full = t0 // TK
    width = lambda c: TK if c < n_full else last_pos - n_full * TK

    def v_chunk(c, g):
        cv = c * (TK // VCHUNK)
        return jnp.concatenate([vrows(vst_ref[0, cv + j], g) for j in range(width(c) // VCHUNK)], axis=1)

    def qk(c, slot, g):
        s = _dot(ksf_ref[0, c * TK:c * TK + width(c), :], q_aug[g])
        s_sc[slot, g, 0:width(c)] = s
        if not bounded:
            cm_sc[slot, g] = jnp.max(s, axis=0, keepdims=True)

    def sm_pv(c, slot, g, state, masked):
        m_p, l_p, acc = state
        s = s_sc[slot, g, 0:width(c)]
        if masked:
            kpos = t0 + lax.broadcasted_iota(jnp.int32, (TQ, 1), 0)
            below = [s[0:width(c) - TQ]] if width(c) > TQ else []
            s = jnp.concatenate(below + [fill_where(s[width(c) - TQ:], kpos <= t_q, NEG_INF)], axis=0)
        if bounded:
            p = jnp.exp2(s)
            return m_p, l_p + jnp.sum(p, axis=0, keepdims=True), acc + _dot(v_chunk(c, g), p.astype(BF16))
        cm = jnp.max(s, axis=0, keepdims=True) if masked else cm_sc[slot, g]
        m_n = jnp.maximum(m_p, cm)
        alpha = jnp.exp2(m_p - m_n)
        p = jnp.exp2(s - m_n)
        l_n = alpha * l_p + jnp.sum(p, axis=0, keepdims=True)
        acc = alpha * acc + _dot(v_chunk(c, g), p.astype(BF16))
        return m_n, l_n, acc

    def step(c, slot, carry):
        out = []
        for g in groups:
            qk(c + 1, 1 - slot, g)
            out.append(sm_pv(c, slot, g, carry[g], False))
        return tuple(out)

    def last(slot, carry):
        return tuple(sm_pv(n_full, slot, g, carry[g], True) for g in groups)

    for g in groups:
        qk(0, 0, g)
    o_w, l_w = window_branch()
    carry = tuple((jnp.full((1, rows), 0.0 if bounded else NEG_INF, F32), jnp.zeros((1, rows), F32),
                   jnp.zeros((HEAD_DIM, rows), F32)) for _ in groups)
    for c in range(n_full):
        carry = step(c, c % 2, carry)
    carry = last(n_full % 2, carry)
    sel_out = [(carry[g][1], carry[g][2]) for g in groups]

    g_t = gate_ref[0, sub_q, :].T
    slabs = []
    for g in groups:
        def gate_row(br):
            base = g * (N_BRANCH * GROUP_HEADS) + br * GROUP_HEADS
            return jnp.concatenate([g_t[base + r:base + r + 1, :] for r in range(GROUP_HEADS)], axis=1)
        l_s, o_s = sel_out[g]
        og = o_c[g] * gate_row(0) + o_s * (gate_row(1) / l_s) + o_w[g] * (gate_row(2) / l_w[g])
        slabs += [og[:, r * TQ:(r + 1) * TQ] for r in range(GROUP_HEADS)]
    out_t = jnp.concatenate(slabs, axis=0)
    o_ref[0, sub_q, :] = (out_t.T * zs_ref[0, sub_q, :].astype(F32)).astype(BF16)


def _out_kernel(x_ref, *refs):
    on_refs, (oc_ref, w_ref, o_ref) = refs[:-3], refs[-3:]
    per_tile = TM_OUT // on_refs[0].shape[1]
    for k in range(len(on_refs) // per_tile):
        @pl.when(pl.program_id(1) == k)
        def _(k=k):
            on = jnp.concatenate([r[0] for r in on_refs[k * per_tile:(k + 1) * per_tile]], axis=0)
            acc = _dot(on, w_ref[0:D_NSA, :]) + _dot(oc_ref[0], w_ref[D_NSA:D_NSA + D_CONV, :])
            o_ref[0] = x_ref[0] + acc


def _rope_tables(pos):
    inv_freq = (np.float32(ROPE_THETA) ** (-np.arange(0, ROT_DIM, 2, dtype=np.float32) / ROT_DIM)).astype(np.float32)
    ang = pos.astype(np.float32)[:, None] * inv_freq[None, :]
    cos, sin = np.cos(ang), np.sin(ang)
    n = pos.shape[0]
    ones = np.ones((n, HEAD_DIM - ROT_DIM), np.float32)
    zeros = np.zeros((n, HEAD_DIM - ROT_DIM), np.float32)
    z8 = np.zeros((n, ROT_DIM // 2), np.float32)
    c = np.concatenate([cos, cos, ones], axis=1)
    sa = np.concatenate([z8, sin, zeros], axis=1)
    sb = np.concatenate([-sin, z8, zeros], axis=1)
    return tuple(jnp.asarray(np.concatenate([t, t], axis=1), F32) for t in (c, sa, sb))


def _overlap_t(ncp):
    cs = np.arange(ncp) * CMP_STRIDE
    ce = cs + CMP_LEN
    ss = np.arange(SEL_BLOCK) * SEL_BLOCK
    se = ss + SEL_BLOCK
    ov = np.clip(np.minimum(ce[None, :], se[:, None]) - np.maximum(cs[None, :], ss[:, None]), 0, None)
    ov = ov.astype(np.float32) / CMP_LEN
    ov[:, ncp - 1] = 0.0
    return jnp.asarray(ov, BF16)


def _cmp_weights(pos, w1, b1, w2):
    half = CMP_STRIDE
    def big_w1(w):
        w = w.reshape(half, 1, HEAD_DIM, 1, CMP_HIDDEN)
        eye = jnp.eye(KV_GROUPS, dtype=F32).reshape(1, KV_GROUPS, 1, KV_GROUPS, 1)
        return (w * eye).reshape(half * KV_GROUPS * HEAD_DIM, KV_GROUPS * CMP_HIDDEN).astype(BF16)
    def big_pos(p):
        return jnp.broadcast_to(p[:, None, :], (half, KV_GROUPS, HEAD_DIM)).reshape(1, -1)
    w1a, w1b = w1[:half * HEAD_DIM], w1[half * HEAD_DIM:]
    eye2 = jnp.eye(KV_GROUPS, dtype=F32)
    w2b = (w2[None, :, None, :] * eye2[:, None, :, None]).reshape(KV_GROUPS * CMP_HIDDEN,
                                                                 KV_GROUPS * HEAD_DIM).astype(BF16)
    b1b = jnp.tile(b1, KV_GROUPS)[None, :]
    return big_pos(pos[:half]), big_pos(pos[half:]), big_w1(w1a), big_w1(w1b), b1b, w2b


def _full(shape):
    nd = len(shape)
    return pl.BlockSpec(shape, lambda *_: (0,) * nd)


def _once(shape):
    nd = len(shape)
    return pl.BlockSpec(shape, lambda *_: (0,) * nd, pipeline_mode=pl.Buffered(1))


def kernel(x, norm_w, w_in, q_norm_w, k_norm_w, cmp_k_pos, cmp_k_w1, cmp_k_b1, cmp_k_w2,
           cmp_v_pos, cmp_v_w1, cmp_v_b1, cmp_v_w2, conv_w, conv_b, w_out):
    B, S, D = x.shape
    assert norm_w.shape[0] == 1, "single layer"
    assert S % TM == 0 and S % TM_OUT == 0 and TK % TQ == 0 and TQ_CALL % TQ == 0 and TM_OUT % TQ_CALL == 0 and S // SEL_BLOCK <= SEL_BLOCK and S >= WINDOW + TQ
    ncp = S // CMP_STRIDE
    nst = S // TM
    f32 = lambda a: a.astype(F32)

    w = w_in[0].astype(BF16)
    head_order = []
    for j in range(GROUP_HEADS):
        head_order += [j, GROUP_HEADS + j]
    wq = w[:, :D_NSA].reshape(D, NSA_HEADS, HEAD_DIM)[:, jnp.array(head_order), :].reshape(D, D_NSA)
    gate_cols = []
    for g in range(KV_GROUPS):
        for br in range(N_BRANCH):
            for r in range(GROUP_HEADS):
                gate_cols.append((g * GROUP_HEADS + r) * N_BRANCH + br)
    g0 = D_NSA + 6 * LANES
    wg = w[:, g0:g0 + NSA_HEADS * N_BRANCH][:, jnp.array(gate_cols)]
    wg = jnp.pad(wg, ((0, 0), (0, LANES - NSA_HEADS * N_BRANCH)))
    rest0 = g0 + NSA_HEADS * N_BRANCH
    wkv, wr = jnp.concatenate([w[:, D_NSA:g0], wg], axis=1), w[:, rest0:]
    n_rest = wr.shape[1]

    rc, rsa, rsb = _rope_tables(np.arange(S))
    mbd = jnp.asarray(np.kron(np.eye(4), np.full((HEAD_DIM, HEAD_DIM), 1.0 / HEAD_DIM)), BF16)
    qnw = jnp.tile(f32(q_norm_w[0]), 4)[None, :]
    knw_sw = jnp.concatenate([jnp.tile(f32(k_norm_w[0, 1]), 2), jnp.tile(f32(k_norm_w[0, 2]), 2)])[None, :]
    knw_c = jnp.tile(f32(k_norm_w[0, 0]), 2)[None, :]

    cp = pltpu.CompilerParams(dimension_semantics=("arbitrary", "arbitrary"),
                              vmem_limit_bytes=VMEM_LIMIT)
    cp_seq = pltpu.CompilerParams(dimension_semantics=("arbitrary",), vmem_limit_bytes=VMEM_LIMIT)
    row_blk = lambda n: pl.BlockSpec((1, TM, n), lambda b, s: (b, s, 0))
    tab_blk = pl.BlockSpec((TM, LANES), lambda b, s: (s, 0))
    vt_blk = pl.BlockSpec((1, TM // VCHUNK, LANES, VCHUNK), lambda b, s: (b, s, 0, 0))
    sds = jax.ShapeDtypeStruct

    (qt4, kc_raw, vc_raw, ksf, vst, kw, vwt, gates, zs, oconv) = pl.pallas_call(
        _proj_kernel,
        grid=(B, nst),
        in_specs=[row_blk(D), _full((1, D)), _once((D, D_NSA)), _once((D, 7 * LANES)),
                  _once((D, n_rest)), _full((1, 256)), _full((1, 256)),
                  tab_blk, tab_blk, tab_blk, _full((256, 256)), _full((3, D_CONV)), _full((1, D_CONV))],
        out_specs=[pl.BlockSpec((1, 4, LANES, TM), lambda b, s: (b, 0, 0, s)),
                   row_blk(LANES), row_blk(LANES), row_blk(2 * LANES), vt_blk, row_blk(LANES), vt_blk,
                   row_blk(LANES), row_blk(D_NSA), row_blk(D_CONV)],
        out_shape=[sds((B, 4, LANES, S), BF16), sds((B, S, LANES), F32), sds((B, S, LANES), F32),
                   sds((B, S, 2 * LANES), BF16), sds((B, S // VCHUNK, LANES, VCHUNK), BF16),
                   sds((B, S, LANES), BF16), sds((B, S // VCHUNK, LANES, VCHUNK), BF16),
                   sds((B, S, LANES), F32), sds((B, S, D_NSA), BF16), sds((B, S, D_CONV), BF16)],
        scratch_shapes=[pltpu.VMEM((TM + 16, D_CONV), F32)],
        compiler_params=cp,
        name="nsa_proj",
    )(x, f32(norm_w), wq, wkv, wr, qnw, knw_sw, rc, rsa, rsb, mbd, f32(conv_w[0]), f32(conv_b))

    cmp_pos = np.arange(ncp) * CMP_STRIDE + (CMP_LEN - 1)
    crc, crsa, crsb = _rope_tables(cmp_pos)
    kpa, kpb, kw1a, kw1b, kb1, kw2 = _cmp_weights(f32(cmp_k_pos[0]), f32(cmp_k_w1[0]),
                                                  f32(cmp_k_b1[0]), f32(cmp_k_w2[0]))
    vpa, vpb, vw1a, vw1b, vb1, vw2 = _cmp_weights(f32(cmp_v_pos[0]), f32(cmp_v_w1[0]),
                                                  f32(cmp_v_b1[0]), f32(cmp_v_w2[0]))
    flat = CMP_STRIDE * LANES
    h_blk = pl.BlockSpec((1, S, LANES), lambda b: (b, 0, 0))
    kc, vct = pl.pallas_call(
        functools.partial(_cmp_kernel, ncp=ncp),
        grid=(B,),
        in_specs=[h_blk, h_blk] + [_full((1, flat))] * 4
                 + [_full((flat, 256)), _full((flat, 256)), _full((1, 256)), _full((256, LANES))] * 2
                 + [_full((1, LANES)), _full((ncp, LANES)), _full((ncp, LANES)), _full((ncp, LANES)),
                    _full((256, 256))],
        out_specs=[pl.BlockSpec((1, ncp, LANES), lambda b: (b, 0, 0)),
                   pl.BlockSpec((1, LANES, ncp), lambda b: (b, 0, 0))],
        out_shape=[sds((B, ncp, LANES), BF16), sds((B, LANES, ncp), BF16)],
        compiler_params=cp_seq,
        name="nsa_compress",
    )(kc_raw, vc_raw, kpa, kpb, vpa, vpb,
      kw1a, kw1b, kb1, kw2, vw1a, vw1b, vb1, vw2, knw_c, crc, crsa, crsb, mbd)

    score_bound = (HEAD_DIM * Q_SCALE) * jnp.max(jnp.abs(f32(q_norm_w))) * jnp.max(jnp.abs(f32(k_norm_w)))
    n_calls = S // TQ_CALL
    per_call = TQ_CALL // TQ

    def attention(bounded, r):
        last_pos = (r + 1) * TQ_CALL
        win_pos = max(last_pos, WINDOW + TQ)
        return pl.pallas_call(
            functools.partial(_attn_kernel, ncp=ncp, bounded=bounded,
                              tiles=tuple(range(r * per_call, (r + 1) * per_call))),
            grid=(B,),
            in_specs=[pl.BlockSpec((1, 4, LANES, TQ_CALL), lambda b: (b, 0, 0, r)),
                      pl.BlockSpec((1, ncp, LANES), lambda b: (b, 0, 0)),
                      pl.BlockSpec((1, LANES, ncp), lambda b: (b, 0, 0)),
                      pl.BlockSpec((1, last_pos, 2 * LANES), lambda b: (b, 0, 0)),
                      pl.BlockSpec((1, last_pos // VCHUNK, LANES, VCHUNK), lambda b: (b, 0, 0, 0)),
                      pl.BlockSpec((1, win_pos, LANES), lambda b: (b, 0, 0)),
                      pl.BlockSpec((1, win_pos // VCHUNK, LANES, VCHUNK), lambda b: (b, 0, 0, 0)),
                      pl.BlockSpec((1, TQ_CALL, LANES), lambda b: (b, r, 0)),
                      pl.BlockSpec((1, TQ_CALL, D_NSA), lambda b: (b, r, 0)),
                      pl.BlockSpec((SEL_BLOCK, ncp), lambda b: (0, 0))],
            out_specs=pl.BlockSpec((1, TQ_CALL, D_NSA), lambda b: (b, 0, 0)),
            out_shape=sds((B, TQ_CALL, D_NSA), BF16),
            scratch_shapes=[pltpu.VMEM((KV_GROUPS, LANES, TQ), BF16),
                            pltpu.VMEM((2, KV_GROUPS, TK, GROUP_HEADS * TQ), F32),
                            pltpu.VMEM((2, KV_GROUPS, 1, GROUP_HEADS * TQ), F32)],
            compiler_params=cp_seq,
            name=("nsa_attn_bounded_r%d" if bounded else "nsa_attn_online_r%d") % r,
        )

    attn_args = (qt4, kc, vct, ksf, vst, kw, vwt, gates, zs, _overlap_t(ncp))
    all_tiles = lambda bounded: (lambda *a: tuple(attention(bounded, r)(*a) for r in range(n_calls)))
    o_nsa = lax.cond(score_bound <= MAX_SAFE_SCORE, all_tiles(True), all_tiles(False), *attn_args)

    out_blk = lambda n: pl.BlockSpec((1, TM_OUT, n), lambda b, s: (b, s, 0))
    on_blk = lambda k: pl.BlockSpec((1, TQ_CALL, D_NSA), lambda b, s: (
        jnp.where(s >= k // (TM_OUT // TQ_CALL), b, jnp.maximum(b - 1, 0)), 0, 0))
    out = pl.pallas_call(
        _out_kernel,
        grid=(B, S // TM_OUT),
        in_specs=[out_blk(D)] + [on_blk(k) for k in range(n_calls)]
                 + [out_blk(D_CONV), _full((D_NSA + D_CONV, D))],
        out_specs=out_blk(D),
        out_shape=sds((B, S, D), x.dtype),
        compiler_params=cp,
        name="nsa_out",
    )(x, *o_nsa, oconv, f32(w_out[0]).astype(BF16))
    return out
```

```python
import functools

import numpy as np
import jax
import jax.numpy as jnp
from jax import lax
from jax.experimental import pallas as pl
from jax.experimental.pallas import tpu as pltpu

LANES = 128
HEAD_DIM = 64
NSA_HEADS = 8
KV_GROUPS = 2
GROUP_HEADS = NSA_HEADS // KV_GROUPS
D_NSA = NSA_HEADS * HEAD_DIM
D_CONV = 512
N_BRANCH = 3
ROT_DIM = HEAD_DIM // 4
ROPE_THETA = 500000.0
CMP_LEN = 32
CMP_STRIDE = 16
CMP_HIDDEN = 2 * HEAD_DIM
SEL_BLOCK = 64
SEL_TOPK = 16
WINDOW = 512
EPS = 1e-6
NEG_INF = -1e30
N_FORCED = 3
TAKEN = -2.0
SCALE = HEAD_DIM ** -0.5
Q_SCALE = SCALE * float(np.log2(np.e))

TM = 1024
TM_OUT = 1024
TQ = 256
TQ_CALL = 1024
TK = 1024
VCHUNK = 128
VMEM_LIMIT = 56 * 1024 * 1024
MAX_SAFE_SCORE = 50.0

BF16 = jnp.bfloat16
F32 = jnp.float32


def _dot(a, b):
    return jnp.dot(a, b, preferred_element_type=F32)


def _split3(a):
    hi = a.astype(BF16)
    r1 = a - hi.astype(F32)
    mid = r1.astype(BF16)
    lo = (r1 - mid.astype(F32)).astype(BF16)
    return hi, mid, lo


def _group_mean(sq, mbd):
    return _dot(sq.astype(BF16), mbd)


def _rope(xn, c, sa, sb):
    return xn * c + pltpu.roll(xn, 8, 1) * sa + pltpu.roll(xn, LANES - 8, 1) * sb


def _silu(z):
    return z * (1.0 / (1.0 + jnp.exp(-z)))


def _proj_kernel(x_ref, nw_ref, wq_ref, wkv_ref, wr_ref, qnw_ref, knw_ref,
                 rc_ref, rsa_ref, rsb_ref, mbd_ref, cw_ref, cb_ref,
                 qt_ref, kc_ref, vc_ref, ksf_ref, vst_ref, kw_ref, vwt_ref,
                 gate_ref, zs_ref, oconv_ref, ubuf):
    si = pl.program_id(1)

    @pl.when(si == 0)
    def _():
        ubuf[0:8, :] = jnp.zeros((8, D_CONV), F32)

    x = x_ref[0]
    ms = jnp.mean(x * x, axis=-1, keepdims=True)
    h = (x * lax.rsqrt(ms + EPS) * nw_ref[...]).astype(BF16)

    rc, rsa, rsb = rc_ref[...], rsa_ref[...], rsb_ref[...]
    mbd = mbd_ref[...]

    pq = _dot(h, wq_ref[...])
    for pair in range(2):
        blk = pq[:, pair * 256:(pair + 1) * 256]
        msq = _group_mean(blk * blk, mbd)
        qn = blk * lax.rsqrt(msq + EPS) * qnw_ref[...]
        for half in range(2):
            t = qn[:, half * LANES:(half + 1) * LANES]
            qt_ref[0, pair * 2 + half] = (_rope(t, rc, rsa, rsb) * Q_SCALE).T.astype(BF16)

    pkv = _dot(h, wkv_ref[...])
    kc_ref[0] = pkv[:, 0:128]
    vc_ref[0] = pkv[:, 128:256]
    ksw = jnp.concatenate([pkv[:, 256:384], pkv[:, 512:640]], axis=1)
    msk = _group_mean(ksw * ksw, mbd)
    kn = ksw * lax.rsqrt(msk + EPS) * knw_ref[...]
    ks = _rope(kn[:, 0:128], rc, rsa, rsb)
    kw = _rope(kn[:, 128:256], rc, rsa, rsb)
    row = lax.broadcasted_iota(jnp.int32, (TM, LANES), 0) + si * TM
    lane = lax.broadcasted_iota(jnp.int32, (TM, LANES), 1)
    onehot = jnp.where(lane == (row >> 6), 1.0, 0.0)
    ksf_ref[0, :, 0:128] = ks.astype(BF16)
    ksf_ref[0, :, 128:256] = onehot.astype(BF16)
    kw_ref[0] = kw.astype(BF16)
    vs = pkv[:, 384:512]
    vw = pkv[:, 640:768]
    for j in range(TM // VCHUNK):
        vst_ref[0, j] = vs[j * VCHUNK:(j + 1) * VCHUNK, :].T.astype(BF16)
        vwt_ref[0, j] = vw[j * VCHUNK:(j + 1) * VCHUNK, :].T.astype(BF16)

    gate_ref[0] = 1.0 / (1.0 + jnp.exp(-pkv[:, 768:896]))
    zs_ref[0] = _silu(_dot(h, wr_ref[:, 0:D_NSA])).astype(BF16)

    pcv = _dot(h, wr_ref[:, D_NSA:D_NSA + 4 * D_CONV])
    u = pcv[:, 1024:1536] * pcv[:, 0:512]
    ubuf[8:8 + TM, :] = u
    u1 = ubuf[7:7 + TM, :]
    u2 = ubuf[6:6 + TM, :]
    conv = cw_ref[0:1, :] * u2 + cw_ref[1:2, :] * u1 + cw_ref[2:3, :] * u + cb_ref[...]
    oconv_ref[0] = (pcv[:, 512:1024] * conv * _silu(pcv[:, 1536:2048])).astype(BF16)
    ubuf[0:8, :] = ubuf[TM:TM + 8, :]


def _cmp_kernel(hk_ref, hv_ref, pka_ref, pkb_ref, pva_ref, pvb_ref,
                wk1a_ref, wk1b_ref, bk1_ref, wk2_ref,
                wv1a_ref, wv1b_ref, bv1_ref, wv2_ref,
                knw_ref, rc_ref, rsa_ref, rsb_ref, mbd_ref,
                kc_ref, vct_ref, *, ncp):
    def first_layer(h_ref, p_ref, w_ref):
        acc = None
        for l in range(0, CMP_STRIDE, 2):
            lhs = jnp.concatenate([h_ref[0, pl.ds(l, ncp, stride=CMP_STRIDE), :],
                                   h_ref[0, pl.ds(l + 1, ncp, stride=CMP_STRIDE), :]], axis=1)
            lhs = (lhs + p_ref[:, l * LANES:(l + 2) * LANES]).astype(BF16)
            part = _dot(lhs, w_ref[l * LANES:(l + 2) * LANES, :])
            acc = part if acc is None else acc + part
        return acc

    def mlp(h_ref, pa_ref, pb_ref, w1a_ref, w1b_ref, b1_ref, w2_ref):
        p = first_layer(h_ref, pa_ref, w1a_ref)
        q = first_layer(h_ref, pb_ref, w1b_ref)
        pre = p + pltpu.roll(q, ncp - 1, 0) + b1_ref[...]
        return _dot(_silu(pre).astype(BF16), w2_ref[...])

    kc = mlp(hk_ref, pka_ref, pkb_ref, wk1a_ref, wk1b_ref, bk1_ref, wk2_ref)
    msk = _group_mean(kc * kc, mbd_ref[0:LANES, 0:LANES])
    kn = kc * lax.rsqrt(msk + EPS) * knw_ref[...]
    kc_ref[0] = _rope(kn, rc_ref[...], rsa_ref[...], rsb_ref[...]).astype(BF16)
    vc = mlp(hv_ref, pva_ref, pvb_ref, wv1a_ref, wv1b_ref, bv1_ref, wv2_ref)
    vct_ref[0] = vc.T.astype(BF16)


def _attn_kernel(*refs, ncp, bounded, tiles):
    for sub, tile in enumerate(tiles):
        _attn_tile(*refs, ncp=ncp, bounded=bounded, tile=tile, sub=sub)


def _attn_tile(qt_ref, kc_ref, vct_ref, ksf_ref, vst_ref, kw_ref, vwt_ref,
               gate_ref, zs_ref, ovt_ref, o_ref, pen_sc, s_sc, cm_sc, *, ncp, bounded, tile, sub):
    i = tile
    t0 = i * TQ
    last_pos = t0 + TQ
    ncv = min(ncp, last_pos // CMP_STRIDE)
    nblk = min(SEL_BLOCK, -(-last_pos // (8 * SEL_BLOCK)) * 8)
    rows = GROUP_HEADS * TQ
    groups = range(KV_GROUPS)
    sub_q = slice(sub * TQ, (sub + 1) * TQ)

    q_all = jnp.concatenate([qt_ref[0, j, :, sub_q] for j in range(GROUP_HEADS)], axis=1)
    frow = lax.broadcasted_iota(jnp.int32, (LANES, rows), 0)
    qg = [jnp.where(frow < HEAD_DIM, q_all, 0), jnp.where(frow >= HEAD_DIM, q_all, 0)]
    t_q = t0 + lax.broadcasted_iota(jnp.int32, (1, TQ), 1)

    def fill_where(a, mask, fill):
        return jnp.concatenate([jnp.where(mask, a[:, r * TQ:(r + 1) * TQ], fill)
                                for r in range(GROUP_HEADS)], axis=1)

    vrows = lambda v, g: v[g * HEAD_DIM:(g + 1) * HEAD_DIM, :]

    s_cmp = [_dot(kc_ref[0, 0:ncv, :], qg[g]) for g in groups]

    c_idx = lax.broadcasted_iota(jnp.int32, (ncv, 1), 0)
    mask_c = ((c_idx * CMP_STRIDE + (CMP_LEN - 1)) <= t_q) & (c_idx < ncp - 1)
    n_open = max(0, min((t0 - (CMP_LEN - 1)) // CMP_STRIDE + 1, ncp - 1)) // 8 * 8

    def masked_c(a, fill):
        tail = fill_where(a[n_open:], mask_c[n_open:], fill)
        return jnp.concatenate([a[0:n_open], tail], axis=0) if n_open else tail

    p_c, o_c = [], []
    for g in groups:
        if bounded:
            e_c = masked_c(jnp.exp2(s_cmp[g]), 0.0)
        else:
            sm = masked_c(s_cmp[g], NEG_INF)
            e_c = masked_c(jnp.exp2(sm - jnp.max(sm, axis=0, keepdims=True)), 0.0)
        l_c = jnp.sum(e_c, axis=0, keepdims=True)
        p = e_c * (1.0 / jnp.where(l_c > 0.0, l_c, 1.0))
        p_c.append(p)
        o_c.append(_dot(vrows(vct_ref[0], g)[:, 0:ncv], p.astype(BF16)))

    if last_pos // SEL_BLOCK <= SEL_TOPK:
        pen_sc[...] = jnp.zeros(pen_sc.shape, BF16)
    else:
        n_idx = lax.broadcasted_iota(jnp.int32, (nblk, TQ), 0)
        tq = t0 + lax.broadcasted_iota(jnp.int32, (nblk, TQ), 1)
        cur = tq >> 6
        forced = (n_idx == 0) | (n_idx == cur) | (n_idx == cur - 1)
        visible = (n_idx << 6) <= tq
        ovt = ovt_ref[0:nblk, 0:ncv]
        left = []
        for g in groups:
            psum = p_c[g][:, 0:TQ]
            for r in range(1, GROUP_HEADS):
                psum = psum + p_c[g][:, r * TQ:(r + 1) * TQ]
            hi, mid, lo = _split3(psum)
            imp = _dot(ovt, hi) + _dot(ovt, mid) + _dot(ovt, lo)
            left.append(jnp.where(forced, TAKEN, jnp.where(visible, imp, -1.0)))
        for _ in range(SEL_TOPK - N_FORCED):
            for g in groups:
                top = jnp.max(left[g], axis=0, keepdims=True)
                first = jnp.min(jnp.where(left[g] == top, n_idx, nblk), axis=0, keepdims=True)
                left[g] = jnp.where(n_idx == first, TAKEN, left[g])
        for g in groups:
            pen_sc[g, 0:nblk, :] = jnp.where(left[g] == TAKEN, 0.0, NEG_INF).astype(BF16)
            pen_sc[g, nblk:LANES, :] = jnp.zeros((LANES - nblk, TQ), BF16)

    nwc = WINDOW // VCHUNK + TQ // VCHUNK
    cw = max(i * (TQ // VCHUNK) - WINDOW // VCHUNK, 0)
    w0 = cw * VCHUNK
    kwin = kw_ref[0, w0:w0 + nwc * VCHUNK, :]
    s_win = [_dot(kwin, qg[g]) for g in groups]

    q_aug = [jnp.concatenate([qg[g], jnp.concatenate([pen_sc[g]] * GROUP_HEADS, axis=1)], axis=0)
             for g in groups]

    def window_branch():
        vwin = jnp.concatenate([vwt_ref[0, cw + j] for j in range(nwc)], axis=1)
        kpos = w0 + lax.broadcasted_iota(jnp.int32, (nwc * VCHUNK, 1), 0)

        def masked_w(a, fill):
            if t0 < WINDOW:
                return fill_where(a, (kpos <= t_q) & (kpos > t_q - WINDOW), fill)
            lo, hi = slice(0, TQ), slice(WINDOW, WINDOW + TQ)
            return jnp.concatenate([fill_where(a[lo], kpos[lo] > t_q - WINDOW, fill), a[TQ:WINDOW],
                                    fill_where(a[hi], kpos[hi] <= t_q, fill)], axis=0)

        o_w, l_w = [], []
        for g in groups:
            if bounded:
                p_w = masked_w(jnp.exp2(s_win[g]), 0.0)
            else:
                sw = masked_w(s_win[g], NEG_INF)
                p_w = jnp.exp2(sw - jnp.max(sw, axis=0, keepdims=True))
            v_ext = jnp.concatenate([vrows(vwin, g), jnp.ones((16, nwc * VCHUNK), BF16)], axis=0)
            ol = _dot(v_ext, p_w.astype(BF16))
            o_w.append(ol[0:HEAD_DIM, :])
            l_w.append(ol[HEAD_DIM:HEAD_DIM + 1, :])
        return o_w, l_w

    n_full = t0 // TK
    width = lambda c: TK if c < n_full else last_pos - n_full * TK

    def v_chunk(c, g):
        cv = c * (TK // VCHUNK)
        return jnp.concatenate([vrows(vst_ref[0, cv + j], g) for j in range(width(c) // VCHUNK)], axis=1)

    def qk(c, slot, g):
        s = _dot(ksf_ref[0, c * TK:c * TK + width(c), :], q_aug[g])
        s_sc[slot, g, 0:width(c)] = s
        if not bounded:
            cm_sc[slot, g] = jnp.max(s, axis=0, keepdims=True)

    def sm_pv(c, slot, g, state, masked):
        m_p, l_p, acc = state
        s = s_sc[slot, g, 0:width(c)]
        if masked:
            kpos = t0 + lax.broadcasted_iota(jnp.int32, (TQ, 1), 0)
            below = [s[0:width(c) - TQ]] if width(c) > TQ else []
            s = jnp.concatenate(below + [fill_where(s[width(c) - TQ:], kpos <= t_q, NEG_INF)], axis=0)
        if bounded:
            p = jnp.exp2(s)
            return m_p, l_p + jnp.sum(p, axis=0, keepdims=True), acc + _dot(v_chunk(c, g), p.astype(BF16))
        cm = jnp.max(s, axis=0, keepdims=True) if masked else cm_sc[slot, g]
        m_n = jnp.maximum(m_p, cm)
        alpha = jnp.exp2(m_p - m_n)
        p = jnp.exp2(s - m_n)
        l_n = alpha * l_p + jnp.sum(p, axis=0, keepdims=True)
        acc = alpha * acc + _dot(v_chunk(c, g), p.astype(BF16))
        return m_n, l_n, acc

    def step(c, slot, carry):
        out = []
        for g in groups:
            qk(c + 1, 1 - slot, g)
            out.append(sm_pv(c, slot, g, carry[g], False))
        return tuple(out)

    def last(slot, carry):
        return tuple(sm_pv(n_full, slot, g, carry[g], True) for g in groups)

    for g in groups:
        qk(0, 0, g)
    o_w, l_w = window_branch()
    carry = tuple((jnp.full((1, rows), 0.0 if bounded else NEG_INF, F32), jnp.zeros((1, rows), F32),
                   jnp.zeros((HEAD_DIM, rows), F32)) for _ in groups)
    for c in range(n_full):
        carry = step(c, c % 2, carry)
    carry = last(n_full % 2, carry)
    sel_out = [(carry[g][1], carry[g][2]) for g in groups]

    g_t = gate_ref[0, sub_q, :].T
    slabs = []
    for g in groups:
        def gate_row(br):
            base = g * (N_BRANCH * GROUP_HEADS) + br * GROUP_HEADS
            return jnp.concatenate([g_t[base + r:base + r + 1, :] for r in range(GROUP_HEADS)], axis=1)
        l_s, o_s = sel_out[g]
        og = o_c[g] * gate_row(0) + o_s * (gate_row(1) / l_s) + o_w[g] * (gate_row(2) / l_w[g])
        slabs += [og[:, r * TQ:(r + 1) * TQ] for r in range(GROUP_HEADS)]
    out_t = jnp.concatenate(slabs, axis=0)
    o_ref[0, sub_q, :] = (out_t.T * zs_ref[0, sub_q, :].astype(F32)).astype(BF16)


def _out_kernel(x_ref, *refs):
    on_refs, (oc_ref, w_ref, o_ref) = refs[:-3], refs[-3:]
    per_tile = TM_OUT // on_refs[0].shape[1]
    for k in range(len(on_refs) // per_tile):
        @pl.when(pl.program_id(1) == k)
        def _(k=k):
            on = jnp.concatenate([r[0] for r in on_refs[k * per_tile:(k + 1) * per_tile]], axis=0)
            acc = _dot(on, w_ref[0:D_NSA, :]) + _dot(oc_ref[0], w_ref[D_NSA:D_NSA + D_CONV, :])
            o_ref[0] = x_ref[0] + acc


def _rope_tables(pos):
    inv_freq = (np.float32(ROPE_THETA) ** (-np.arange(0, ROT_DIM, 2, dtype=np.float32) / ROT_DIM)).astype(np.float32)
    ang = pos.astype(np.float32)[:, None] * inv_freq[None, :]
    cos, sin = np.cos(ang), np.sin(ang)
    n = pos.shape[0]
    ones = np.ones((n, HEAD_DIM - ROT_DIM), np.float32)
    zeros = np.zeros((n, HEAD_DIM - ROT_DIM), np.float32)
    z8 = np.zeros((n, ROT_DIM // 2), np.float32)
    c = np.concatenate([cos, cos, ones], axis=1)
    sa = np.concatenate([z8, sin, zeros], axis=1)
    sb = np.concatenate([-sin, z8, zeros], axis=1)
    return tuple(jnp.asarray(np.concatenate([t, t], axis=1), F32) for t in (c, sa, sb))


def _overlap_t(ncp):
    cs = np.arange(ncp) * CMP_STRIDE
    ce = cs + CMP_LEN
    ss = np.arange(SEL_BLOCK) * SEL_BLOCK
    se = ss + SEL_BLOCK
    ov = np.clip(np.minimum(ce[None, :], se[:, None]) - np.maximum(cs[None, :], ss[:, None]), 0, None)
    ov = ov.astype(np.float32) / CMP_LEN
    ov[:, ncp - 1] = 0.0
    return jnp.asarray(ov, BF16)


def _cmp_weights(pos, w1, b1, w2):
    half = CMP_STRIDE
    def big_w1(w):
        w = w.reshape(half, 1, HEAD_DIM, 1, CMP_HIDDEN)
        eye = jnp.eye(KV_GROUPS, dtype=F32).reshape(1, KV_GROUPS, 1, KV_GROUPS, 1)
        return (w * eye).reshape(half * KV_GROUPS * HEAD_DIM, KV_GROUPS * CMP_HIDDEN).astype(BF16)
    def big_pos(p):
        return jnp.broadcast_to(p[:, None, :], (half, KV_GROUPS, HEAD_DIM)).reshape(1, -1)
    w1a, w1b = w1[:half * HEAD_DIM], w1[half * HEAD_DIM:]
    eye2 = jnp.eye(KV_GROUPS, dtype=F32)
    w2b = (w2[None, :, None, :] * eye2[:, None, :, None]).reshape(KV_GROUPS * CMP_HIDDEN,
                                                                 KV_GROUPS * HEAD_DIM).astype(BF16)
    b1b = jnp.tile(b1, KV_GROUPS)[None, :]
    return big_pos(pos[:half]), big_pos(pos[half:]), big_w1(w1a), big_w1(w1b), b1b, w2b


def _full(shape):
    nd = len(shape)
    return pl.BlockSpec(shape, lambda *_: (0,) * nd)


def _once(shape):
    nd = len(shape)
    return pl.BlockSpec(shape, lambda *_: (0,) * nd, pipeline_mode=pl.Buffered(1))


def kernel(x, norm_w, w_in, q_norm_w, k_norm_w, cmp_k_pos, cmp_k_w1, cmp_k_b1, cmp_k_w2,
           cmp_v_pos, cmp_v_w1, cmp_v_b1, cmp_v_w2, conv_w, conv_b, w_out):
    B, S, D = x.shape
    assert norm_w.shape[0] == 1, "single layer"
    assert S % TM == 0 and S % TM_OUT == 0 and TK % TQ == 0 and TQ_CALL % TQ == 0 and TM_OUT % TQ_CALL == 0 and S // SEL_BLOCK <= SEL_BLOCK and S >= WINDOW + TQ
    ncp = S // CMP_STRIDE
    nst = S // TM
    f32 = lambda a: a.astype(F32)

    w = w_in[0].astype(BF16)
    head_order = []
    for j in range(GROUP_HEADS):
        head_order += [j, GROUP_HEADS + j]
    wq = w[:, :D_NSA].reshape(D, NSA_HEADS, HEAD_DIM)[:, jnp.array(head_order), :].reshape(D, D_NSA)
    gate_cols = []
    for g in range(KV_GROUPS):
        for br in range(N_BRANCH):
            for r in range(GROUP_HEADS):
                gate_cols.append((g * GROUP_HEADS + r) * N_BRANCH + br)
    g0 = D_NSA + 6 * LANES
    wg = w[:, g0:g0 + NSA_HEADS * N_BRANCH][:, jnp.array(gate_cols)]
    wg = jnp.pad(wg, ((0, 0), (0, LANES - NSA_HEADS * N_BRANCH)))
    rest0 = g0 + NSA_HEADS * N_BRANCH
    wkv, wr = jnp.concatenate([w[:, D_NSA:g0], wg], axis=1), w[:, rest0:]
    n_rest = wr.shape[1]

    rc, rsa, rsb = _rope_tables(np.arange(S))
    mbd = jnp.asarray(np.kron(np.eye(4), np.full((HEAD_DIM, HEAD_DIM), 1.0 / HEAD_DIM)), BF16)
    qnw = jnp.tile(f32(q_norm_w[0]), 4)[None, :]
    knw_sw = jnp.concatenate([jnp.tile(f32(k_norm_w[0, 1]), 2), jnp.tile(f32(k_norm_w[0, 2]), 2)])[None, :]
    knw_c = jnp.tile(f32(k_norm_w[0, 0]), 2)[None, :]

    cp = pltpu.CompilerParams(dimension_semantics=("arbitrary", "arbitrary"),
                              vmem_limit_bytes=VMEM_LIMIT)
    cp_seq = pltpu.CompilerParams(dimension_semantics=("arbitrary",), vmem_limit_bytes=VMEM_LIMIT)
    row_blk = lambda n: pl.BlockSpec((1, TM, n), lambda b, s: (b, s, 0))
    tab_blk = pl.BlockSpec((TM, LANES), lambda b, s: (s, 0))
    vt_blk = pl.BlockSpec((1, TM // VCHUNK, LANES, VCHUNK), lambda b, s: (b, s, 0, 0))
    sds = jax.ShapeDtypeStruct

    (qt4, kc_raw, vc_raw, ksf, vst, kw, vwt, gates, zs, oconv) = pl.pallas_call(
        _proj_kernel,
        grid=(B, nst),
        in_specs=[row_blk(D), _full((1, D)), _once((D, D_NSA)), _once((D, 7 * LANES)),
                  _once((D, n_rest)), _full((1, 256)), _full((1, 256)),
                  tab_blk, tab_blk, tab_blk, _full((256, 256)), _full((3, D_CONV)), _full((1, D_CONV))],
        out_specs=[pl.BlockSpec((1, 4, LANES, TM), lambda b, s: (b, 0, 0, s)),
                   row_blk(LANES), row_blk(LANES), row_blk(2 * LANES), vt_blk, row_blk(LANES), vt_blk,
                   row_blk(LANES), row_blk(D_NSA), row_blk(D_CONV)],
        out_shape=[sds((B, 4, LANES, S), BF16), sds((B, S, LANES), F32), sds((B, S, LANES), F32),
                   sds((B, S, 2 * LANES), BF16), sds((B, S // VCHUNK, LANES, VCHUNK), BF16),
                   sds((B, S, LANES), BF16), sds((B, S // VCHUNK, LANES, VCHUNK), BF16),
                   sds((B, S, LANES), F32), sds((B, S, D_NSA), BF16), sds((B, S, D_CONV), BF16)],
        scratch_shapes=[pltpu.VMEM((TM + 16, D_CONV), F32)],
        compiler_params=cp,
        name="nsa_proj",
    )(x, f32(norm_w), wq, wkv, wr, qnw, knw_sw, rc, rsa, rsb, mbd, f32(conv_w[0]), f32(conv_b))

    cmp_pos = np.arange(ncp) * CMP_STRIDE + (CMP_LEN - 1)
    crc, crsa, crsb = _rope_tables(cmp_pos)
    kpa, kpb, kw1a, kw1b, kb1, kw2 = _cmp_weights(f32(cmp_k_pos[0]), f32(cmp_k_w1[0]),
                                                  f32(cmp_k_b1[0]), f32(cmp_k_w2[0]))
    vpa, vpb, vw1a, vw1b, vb1, vw2 = _cmp_weights(f32(cmp_v_pos[0]), f32(cmp_v_w1[0]),
                                                  f32(cmp_v_b1[0]), f32(cmp_v_w2[0]))
    flat = CMP_STRIDE * LANES
    h_blk = pl.BlockSpec((1, S, LANES), lambda b: (b, 0, 0))
    kc, vct = pl.pallas_call(
        functools.partial(_cmp_kernel, ncp=ncp),
        grid=(B,),
        in_specs=[h_blk, h_blk] + [_full((1, flat))] * 4
                 + [_full((flat, 256)), _full((flat, 256)), _full((1, 256)), _full((256, LANES))] * 2
                 + [_full((1, LANES)), _full((ncp, LANES)), _full((ncp, LANES)), _full((ncp, LANES)),
                    _full((256, 256))],
        out_specs=[pl.BlockSpec((1, ncp, LANES), lambda b: (b, 0, 0)),
                   pl.BlockSpec((1, LANES, ncp), lambda b: (b, 0, 0))],
        out_shape=[sds((B, ncp, LANES), BF16), sds((B, LANES, ncp), BF16)],
        compiler_params=cp_seq,
        name="nsa_compress",
    )(kc_raw, vc_raw, kpa, kpb, vpa, vpb,
      kw1a, kw1b, kb1, kw2, vw1a, vw1b, vb1, vw2, knw_c, crc, crsa, crsb, mbd)

    score_bound = (HEAD_DIM * Q_SCALE) * jnp.max(jnp.abs(f32(q_norm_w))) * jnp.max(jnp.abs(f32(k_norm_w)))
    n_calls = S // TQ_CALL
    per_call = TQ_CALL // TQ

    def attention(bounded, r):
        last_pos = (r + 1) * TQ_CALL
        win_pos = max(last_pos, WINDOW + TQ)
        return pl.pallas_call(
            functools.partial(_attn_kernel, ncp=ncp, bounded=bounded,
                              tiles=tuple(range(r * per_call, (r + 1) * per_call))),
            grid=(B,),
            in_specs=[pl.BlockSpec((1, 4, LANES, TQ_CALL), lambda b: (b, 0, 0, r)),
                      pl.BlockSpec((1, ncp, LANES), lambda b: (b, 0, 0)),
                      pl.BlockSpec((1, LANES, ncp), lambda b: (b, 0, 0)),
                      pl.BlockSpec((1, last_pos, 2 * LANES), lambda b: (b, 0, 0)),
                      pl.BlockSpec((1, last_pos // VCHUNK, LANES, VCHUNK), lambda b: (b, 0, 0, 0)),
                      pl.BlockSpec((1, win_pos, LANES), lambda b: (b, 0, 0)),
                      pl.BlockSpec((1, win_pos // VCHUNK, LANES, VCHUNK), lambda b: (b, 0, 0, 0)),
                      pl.BlockSpec((1, TQ_CALL, LANES), lambda b: (b, r, 0)),
                      pl.BlockSpec((1, TQ_CALL, D_NSA), lambda b: (b, r, 0)),
                      pl.BlockSpec((SEL_BLOCK, ncp), lambda b: (0, 0))],
            out_specs=pl.BlockSpec((1, TQ_CALL, D_NSA), lambda b: (b, 0, 0)),
            out_shape=sds((B, TQ_CALL, D_NSA), BF16),
            scratch_shapes=[pltpu.VMEM((KV_GROUPS, LANES, TQ), BF16),
                            pltpu.VMEM((2, KV_GROUPS, TK, GROUP_HEADS * TQ), F32),
                            pltpu.VMEM((2, KV_GROUPS, 1, GROUP_HEADS * TQ), F32)],
            compiler_params=cp_seq,
            name=("nsa_attn_bounded_r%d" if bounded else "nsa_attn_online_r%d") % r,
        )

    attn_args = (qt4, kc, vct, ksf, vst, kw, vwt, gates, zs, _overlap_t(ncp))
    all_tiles = lambda bounded: (lambda *a: tuple(attention(bounded, r)(*a) for r in range(n_calls)))
    o_nsa = lax.cond(score_bound <= MAX_SAFE_SCORE, all_tiles(True), all_tiles(False), *attn_args)

    out_blk = lambda n: pl.BlockSpec((1, TM_OUT, n), lambda b, s: (b, s, 0))
    on_blk = lambda k: pl.BlockSpec((1, TQ_CALL, D_NSA), lambda b, s: (
        jnp.where(s >= k // (TM_OUT // TQ_CALL), b, jnp.maximum(b - 1, 0)), 0, 0))
    out = pl.pallas_call(
        _out_kernel,
        grid=(B, S // TM_OUT),
        in_specs=[out_blk(D)] + [on_blk(k) for k in range(n_calls)]
                 + [out_blk(D_CONV), _full((D_NSA + D_CONV, D))],
        out_specs=out_blk(D),
        out_shape=sds((B, S, D), x.dtype),
        compiler_params=cp,
        name="nsa_out",
    )(x, *o_nsa, oconv, f32(w_out[0]).astype(BF16))
    return out
```
